```python
import math
import jax, jax.numpy as jnp
from jax import lax
import numpy as np

D_MODEL = 1024
BATCH = 8
SEQ = 16384
DEPTH = 4

N_EVEN = (DEPTH + 1) // 2
N_ODD = DEPTH // 2
A_GROUPS = 4
A_CHUNK = 128
A_WIDTH = D_MODEL // 2
A_GROUP_CH = A_WIDTH // A_GROUPS
B_HEADS = 8
B_HEAD_DIM = 64
B_WIDTH = B_HEADS * B_HEAD_DIM
Q_BLOCK = 128
MIX_WIDTH = A_WIDTH + B_WIDTH
IN_COLS = 2 * A_WIDTH + 3 * B_WIDTH + B_HEADS
S5_GROUP_CH = 16
S5_GROUPS = D_MODEL // S5_GROUP_CH
S5_STATE = 64
D_FF = 2816
CONV_W = 3
PLE_DIM = 256
EPS = 1e-6
NEG_INF = -1e30

kernel_name = "hybrid_gmlp_fox_s5_convffn_ple"


def rms_norm(x, g=None):
    xf = x.astype(jnp.float32)
    y = xf * lax.rsqrt(jnp.mean(xf * xf, axis=-1, keepdims=True) + EPS)
    if g is not None:
        y = y * g.astype(jnp.float32)
    return y.astype(x.dtype)


def gmlp_mixer(u, v, v_gain, w_s, b_s):
    bsz, seq = u.shape[0], u.shape[1]
    u = jax.nn.gelu(u)
    v = jax.nn.gelu(v).reshape(bsz, seq, A_GROUPS, A_GROUP_CH)
    v = rms_norm(v, v_gain.reshape(A_GROUPS, A_GROUP_CH))
    v = v.reshape(bsz, seq // A_CHUNK, A_CHUNK, A_GROUPS, A_GROUP_CH)
    tri = jnp.tril(jnp.ones((A_CHUNK, A_CHUNK), dtype=bool))
    w = jnp.where(tri[None], w_s, jnp.zeros_like(w_s))
    sv = jnp.einsum('gts,bnsgc->bntgc', w, v) + b_s.T[None, None, :, :, None]
    return u * sv.reshape(bsz, seq, A_WIDTH)


def fox_attention(q, k, v, f_logit, q_gain, k_gain):
    bsz, seq = q.shape[0], q.shape[1]
    q = rms_norm(q, q_gain)
    k = rms_norm(k, k_gain)
    c = jnp.cumsum(jax.nn.log_sigmoid(f_logit.astype(jnp.float32)), axis=1)
    nb = seq // Q_BLOCK
    qb = q.reshape(bsz, nb, Q_BLOCK, B_HEADS, B_HEAD_DIM).transpose(1, 0, 2, 3, 4)
    cb = c.reshape(bsz, nb, Q_BLOCK, B_HEADS).transpose(1, 0, 2, 3)
    pos_b = jnp.arange(seq, dtype=jnp.int32).reshape(nb, Q_BLOCK)
    kpos = jnp.arange(seq, dtype=jnp.int32)
    ck = c.transpose(0, 2, 1)
    scale = B_HEAD_DIM ** -0.5

    def block(args):
        qi, ci, pi = args
        s = jnp.einsum('bqhd,bkhd->bhqk', qi, k).astype(jnp.float32) * scale
        s = s + ci.transpose(0, 2, 1)[..., None] - ck[:, :, None, :]
        s = jnp.where(kpos[None, :] <= pi[:, None], s, NEG_INF)
        pr = jax.nn.softmax(s, axis=-1)
        return jnp.einsum('bhqk,bkhd->bqhd', pr.astype(v.dtype), v)

    o = lax.map(block, (qb, cb, pos_b))
    return o.transpose(1, 0, 2, 3, 4).reshape(bsz, seq, B_WIDTH)


def _complex_affine_combine(e1, e2):
    a1r, a1i, b1r, b1i = e1
    a2r, a2i, b2r, b2i = e2
    ar = a2r * a1r - a2i * a1i
    ai = a2r * a1i + a2i * a1r
    br = a2r * b1r - a2i * b1i + b2r
    bi = a2r * b1i + a2i * b1r + b2i
    return (ar, ai, br, bi)


def s5_mixer(u, a_re, a_im, log_dt, b_re, b_im, c_re, c_im, d):
    bsz, seq, _ = u.shape
    f32 = jnp.float32
    uf = u.astype(f32).reshape(bsz, seq, S5_GROUPS, S5_GROUP_CH)
    dt = jnp.exp(log_dt.astype(f32))[:, None]
    lr, li = a_re.astype(f32), a_im.astype(f32)
    mag = jnp.exp(lr * dt)
    ab_re, ab_im = mag * jnp.cos(li * dt), mag * jnp.sin(li * dt)
    den = lr * lr + li * li
    nr, ni = ab_re - 1.0, ab_im
    cr = (nr * lr + ni * li) / den
    ci = (ni * lr - nr * li) / den
    br, bi = b_re.astype(f32), b_im.astype(f32)
    bb_re = cr[..., None] * br - ci[..., None] * bi
    bb_im = cr[..., None] * bi + ci[..., None] * br
    bu_re = jnp.einsum('gpc,bsgc->bsgp', bb_re, uf)
    bu_im = jnp.einsum('gpc,bsgc->bsgp', bb_im, uf)
    a_r = jnp.broadcast_to(ab_re, bu_re.shape)
    a_i = jnp.broadcast_to(ab_im, bu_im.shape)
    _, _, xr, xi = lax.associative_scan(_complex_affine_combine, (a_r, a_i, bu_re, bu_im), axis=1)
    y = (jnp.einsum('gcp,bsgp->bsgc', c_re.astype(f32), xr)
         - jnp.einsum('gcp,bsgp->bsgc', c_im.astype(f32), xi)
         + d.astype(f32).reshape(S5_GROUPS, S5_GROUP_CH) * uf)
    return y.reshape(bsz, seq, D_MODEL).astype(u.dtype)


def conv_ffn(x, w_up, conv_w, conv_b, w_down):
    seq = x.shape[1]
    h = x @ w_up
    hp = jnp.pad(h, ((0, 0), (CONV_W - 1, 0), (0, 0)))
    hc = conv_b + conv_w[0] * hp[:, 0:seq]
    for j in range(1, CONV_W):
        hc = hc + conv_w[j] * hp[:, j:j + seq]
    g, up = jnp.split(hc, 2, axis=-1)
    return (jax.nn.silu(g) * up) @ w_down


def _fwd_setup_inputs(seed: int = 0) -> dict:
    key = jax.random.key(seed)
    ks = jax.random.split(key, 32)
    f32 = jnp.float32

    def nrm(k, shape, scale):
        return scale * jax.random.normal(k, shape, f32)

    a_im_base = jnp.broadcast_to(math.pi * jnp.arange(S5_STATE, dtype=f32), (N_ODD, S5_GROUPS, S5_STATE))
    return {
        "x": nrm(ks[0], (BATCH, SEQ, D_MODEL), 1.0),
        "p": nrm(ks[1], (DEPTH, BATCH, SEQ, PLE_DIM), 1.0),
        "norm_mix": 1.0 + nrm(ks[2], (DEPTH, D_MODEL), 0.05),
        "norm_ffn": 1.0 + nrm(ks[3], (DEPTH, D_MODEL), 0.05),
        "ev_w_in": nrm(ks[4], (N_EVEN, D_MODEL, IN_COLS), D_MODEL ** -0.5),
        "ev_b_fgate": 4.0 + nrm(ks[5], (N_EVEN, B_HEADS), 0.5),
        "ev_q_norm": 1.0 + nrm(ks[6], (N_EVEN, B_HEAD_DIM), 0.05),
        "ev_k_norm": 1.0 + nrm(ks[7], (N_EVEN, B_HEAD_DIM), 0.05),
        "ev_v_norm": 1.0 + nrm(ks[8], (N_EVEN, A_WIDTH), 0.05),
        "ev_w_spatial": nrm(ks[9], (N_EVEN, A_GROUPS, A_CHUNK, A_CHUNK), A_CHUNK ** -0.5),
        "ev_b_spatial": 1.0 + nrm(ks[10], (N_EVEN, A_GROUPS, A_CHUNK), 0.1),
        "ev_w_out": nrm(ks[11], (N_EVEN, MIX_WIDTH, D_MODEL), MIX_WIDTH ** -0.5),
        "od_a_re": -0.5 + nrm(ks[12], (N_ODD, S5_GROUPS, S5_STATE), 0.01),
        "od_a_im": a_im_base + nrm(ks[13], (N_ODD, S5_GROUPS, S5_STATE), 0.01),
        "od_log_dt": jax.random.uniform(ks[14], (N_ODD, S5_GROUPS), f32, math.log(1e-3), math.log(1e-1)),
        "od_b_re": nrm(ks[15], (N_ODD, S5_GROUPS, S5_STATE, S5_GROUP_CH), (2 * S5_GROUP_CH) ** -0.5),
        "od_b_im": nrm(ks[16], (N_ODD, S5_GROUPS, S5_STATE, S5_GROUP_CH), (2 * S5_GROUP_CH) ** -0.5),
        "od_c_re": nrm(ks[17], (N_ODD, S5_GROUPS, S5_GROUP_CH, S5_STATE), S5_STATE ** -0.5),
        "od_c_im": nrm(ks[18], (N_ODD, S5_GROUPS, S5_GROUP_CH, S5_STATE), S5_STATE ** -0.5),
        "od_d": nrm(ks[19], (N_ODD, D_MODEL), 1.0),
        "od_w_glu": nrm(ks[20], (N_ODD, D_MODEL, 2 * D_MODEL), D_MODEL ** -0.5),
        "ffn_w_up": nrm(ks[21], (DEPTH, D_MODEL, 2 * D_FF), D_MODEL ** -0.5),
        "ffn_conv_w": nrm(ks[22], (DEPTH, CONV_W, 2 * D_FF), CONV_W ** -0.5),
        "ffn_conv_b": nrm(ks[23], (DEPTH, 2 * D_FF), 0.02),
        "ffn_w_down": nrm(ks[24], (DEPTH, D_FF, D_MODEL), D_FF ** -0.5),
        "ple_w_proj": nrm(ks[25], (DEPTH, PLE_DIM, D_MODEL), PLE_DIM ** -0.5),
        "ple_w_gate": nrm(ks[26], (DEPTH, D_MODEL, D_MODEL), D_MODEL ** -0.5),
    }


def _fwd_reference(x, p, norm_mix, norm_ffn, ev_w_in, ev_b_fgate, ev_q_norm, ev_k_norm, ev_v_norm,
              ev_w_spatial, ev_b_spatial, ev_w_out, od_a_re, od_a_im, od_log_dt, od_b_re, od_b_im,
              od_c_re, od_c_im, od_d, od_w_glu, ffn_w_up, ffn_conv_w, ffn_conv_b, ffn_w_down,
              ple_w_proj, ple_w_gate):
    bsz, seq = x.shape[0], x.shape[1]
    splits = [A_WIDTH, 2 * A_WIDTH, 2 * A_WIDTH + B_WIDTH, 2 * A_WIDTH + 2 * B_WIDTH,
              2 * A_WIDTH + 3 * B_WIDTH]
    for i in range(DEPTH):
        h = rms_norm(x, norm_mix[i])
        if i % 2 == 0:
            e = i // 2
            z = h @ ev_w_in[e]
            u_a, v_a, q, k, v_b, f = jnp.split(z, splits, axis=-1)
            y_a = gmlp_mixer(u_a, v_a, ev_v_norm[e], ev_w_spatial[e], ev_b_spatial[e])
            shp = (bsz, seq, B_HEADS, B_HEAD_DIM)
            y_b = fox_attention(q.reshape(shp), k.reshape(shp), v_b.reshape(shp),
                                f + ev_b_fgate[e], ev_q_norm[e], ev_k_norm[e])
            x = x + jnp.concatenate([y_a, y_b], axis=-1) @ ev_w_out[e]
        else:
            o = i // 2
            y = s5_mixer(h, od_a_re[o], od_a_im[o], od_log_dt[o], od_b_re[o], od_b_im[o],
                         od_c_re[o], od_c_im[o], od_d[o])
            g_a, g_b = jnp.split(jax.nn.gelu(y) @ od_w_glu[o], 2, axis=-1)
            x = x + g_a * jax.nn.sigmoid(g_b)
        x = x + conv_ffn(rms_norm(x, norm_ffn[i]), ffn_w_up[i], ffn_conv_w[i], ffn_conv_b[i], ffn_w_down[i])
        gate = jax.nn.sigmoid(rms_norm(x) @ ple_w_gate[i])
        x = x + gate * (p[i] @ ple_w_proj[i])
    return x


import jax as _jax
import jax.numpy as _jnp

TWIN_FORMAT = 'train_step'
FWD_PARAMS = ['x', 'p', 'norm_mix', 'norm_ffn', 'ev_w_in', 'ev_b_fgate', 'ev_q_norm', 'ev_k_norm', 'ev_v_norm', 'ev_w_spatial', 'ev_b_spatial', 'ev_w_out', 'od_a_re', 'od_a_im', 'od_log_dt', 'od_b_re', 'od_b_im', 'od_c_re', 'od_c_im', 'od_d', 'od_w_glu', 'ffn_w_up', 'ffn_conv_w', 'ffn_conv_b', 'ffn_w_down', 'ple_w_proj', 'ple_w_gate']
TWIN_WEIGHTS = ['norm_mix', 'norm_ffn', 'ev_w_in', 'ev_b_fgate', 'ev_q_norm', 'ev_k_norm', 'ev_v_norm', 'ev_w_spatial', 'ev_b_spatial', 'ev_w_out', 'od_a_re', 'od_a_im', 'od_log_dt', 'od_b_re', 'od_b_im', 'od_c_re', 'od_c_im', 'od_d', 'od_w_glu', 'ffn_w_up', 'ffn_conv_w', 'ffn_conv_b', 'ffn_w_down', 'ple_w_proj', 'ple_w_gate']
TWIN_DIFF_INPUT = 'x'
TWIN_INPUTS = ['x', 'p', 'norm_mix', 'norm_ffn', 'ev_w_in', 'ev_b_fgate', 'ev_q_norm', 'ev_k_norm', 'ev_v_norm', 'ev_w_spatial', 'ev_b_spatial', 'ev_w_out', 'od_a_re', 'od_a_im', 'od_log_dt', 'od_b_re', 'od_b_im', 'od_c_re', 'od_c_im', 'od_d', 'od_w_glu', 'ffn_w_up', 'ffn_conv_w', 'ffn_conv_b', 'ffn_w_down', 'ple_w_proj', 'ple_w_gate', 'loss_target', 'm_norm_mix', 'm_norm_ffn', 'm_ev_w_in', 'm_ev_b_fgate', 'm_ev_q_norm', 'm_ev_k_norm', 'm_ev_v_norm', 'm_ev_w_spatial', 'm_ev_b_spatial', 'm_ev_w_out', 'm_od_a_re', 'm_od_a_im', 'm_od_log_dt', 'm_od_b_re', 'm_od_b_im', 'm_od_c_re', 'm_od_c_im', 'm_od_d', 'm_od_w_glu', 'm_ffn_w_up', 'm_ffn_conv_w', 'm_ffn_conv_b', 'm_ffn_w_down', 'm_ple_w_proj', 'm_ple_w_gate', 'v_norm_mix', 'v_norm_ffn', 'v_ev_w_in', 'v_ev_b_fgate', 'v_ev_q_norm', 'v_ev_k_norm', 'v_ev_v_norm', 'v_ev_w_spatial', 'v_ev_b_spatial', 'v_ev_w_out', 'v_od_a_re', 'v_od_a_im', 'v_od_log_dt', 'v_od_b_re', 'v_od_b_im', 'v_od_c_re', 'v_od_c_im', 'v_od_d', 'v_od_w_glu', 'v_ffn_w_up', 'v_ffn_conv_w', 'v_ffn_conv_b', 'v_ffn_w_down', 'v_ple_w_proj', 'v_ple_w_gate']
TWIN_OUTPUTS = ['loss', 'grad_x', 'grad_norm_mix', 'grad_norm_ffn', 'grad_ev_w_in', 'grad_ev_b_fgate', 'grad_ev_q_norm', 'grad_ev_k_norm', 'grad_ev_v_norm', 'grad_ev_w_spatial', 'grad_ev_b_spatial', 'grad_ev_w_out', 'grad_od_a_re', 'grad_od_a_im', 'grad_od_log_dt', 'grad_od_b_re', 'grad_od_b_im', 'grad_od_c_re', 'grad_od_c_im', 'grad_od_d', 'grad_od_w_glu', 'grad_ffn_w_up', 'grad_ffn_conv_w', 'grad_ffn_conv_b', 'grad_ffn_w_down', 'grad_ple_w_proj', 'grad_ple_w_gate', 'delta_norm_mix', 'delta_norm_ffn', 'delta_ev_w_in', 'delta_ev_b_fgate', 'delta_ev_q_norm', 'delta_ev_k_norm', 'delta_ev_v_norm', 'delta_ev_w_spatial', 'delta_ev_b_spatial', 'delta_ev_w_out', 'delta_od_a_re', 'delta_od_a_im', 'delta_od_log_dt', 'delta_od_b_re', 'delta_od_b_im', 'delta_od_c_re', 'delta_od_c_im', 'delta_od_d', 'delta_od_w_glu', 'delta_ffn_w_up', 'delta_ffn_conv_w', 'delta_ffn_conv_b', 'delta_ffn_w_down', 'delta_ple_w_proj', 'delta_ple_w_gate', 'new_m_norm_mix', 'new_m_norm_ffn', 'new_m_ev_w_in', 'new_m_ev_b_fgate', 'new_m_ev_q_norm', 'new_m_ev_k_norm', 'new_m_ev_v_norm', 'new_m_ev_w_spatial', 'new_m_ev_b_spatial', 'new_m_ev_w_out', 'new_m_od_a_re', 'new_m_od_a_im', 'new_m_od_log_dt', 'new_m_od_b_re', 'new_m_od_b_im', 'new_m_od_c_re', 'new_m_od_c_im', 'new_m_od_d', 'new_m_od_w_glu', 'new_m_ffn_w_up', 'new_m_ffn_conv_w', 'new_m_ffn_conv_b', 'new_m_ffn_w_down', 'new_m_ple_w_proj', 'new_m_ple_w_gate', 'new_v_norm_mix', 'new_v_norm_ffn', 'new_v_ev_w_in', 'new_v_ev_b_fgate', 'new_v_ev_q_norm', 'new_v_ev_k_norm', 'new_v_ev_v_norm', 'new_v_ev_w_spatial', 'new_v_ev_b_spatial', 'new_v_ev_w_out', 'new_v_od_a_re', 'new_v_od_a_im', 'new_v_od_log_dt', 'new_v_od_b_re', 'new_v_od_b_im', 'new_v_od_c_re', 'new_v_od_c_im', 'new_v_od_d', 'new_v_od_w_glu', 'new_v_ffn_w_up', 'new_v_ffn_conv_w', 'new_v_ffn_conv_b', 'new_v_ffn_w_down', 'new_v_ple_w_proj', 'new_v_ple_w_gate']
TWIN_LEAF_KINDS = {'loss': 'loss', 'grad_x': 'grad_x', 'grad_norm_mix': 'grad_w', 'grad_norm_ffn': 'grad_w', 'grad_ev_w_in': 'grad_w', 'grad_ev_b_fgate': 'grad_w', 'grad_ev_q_norm': 'grad_w', 'grad_ev_k_norm': 'grad_w', 'grad_ev_v_norm': 'grad_w', 'grad_ev_w_spatial': 'grad_w', 'grad_ev_b_spatial': 'grad_w', 'grad_ev_w_out': 'grad_w', 'grad_od_a_re': 'grad_w', 'grad_od_a_im': 'grad_w', 'grad_od_log_dt': 'grad_w', 'grad_od_b_re': 'grad_w', 'grad_od_b_im': 'grad_w', 'grad_od_c_re': 'grad_w', 'grad_od_c_im': 'grad_w', 'grad_od_d': 'grad_w', 'grad_od_w_glu': 'grad_w', 'grad_ffn_w_up': 'grad_w', 'grad_ffn_conv_w': 'grad_w', 'grad_ffn_conv_b': 'grad_w', 'grad_ffn_w_down': 'grad_w', 'grad_ple_w_proj': 'grad_w', 'grad_ple_w_gate': 'grad_w', 'delta_norm_mix': 'delta_w', 'delta_norm_ffn': 'delta_w', 'delta_ev_w_in': 'delta_w', 'delta_ev_b_fgate': 'delta_w', 'delta_ev_q_norm': 'delta_w', 'delta_ev_k_norm': 'delta_w', 'delta_ev_v_norm': 'delta_w', 'delta_ev_w_spatial': 'delta_w', 'delta_ev_b_spatial': 'delta_w', 'delta_ev_w_out': 'delta_w', 'delta_od_a_re': 'delta_w', 'delta_od_a_im': 'delta_w', 'delta_od_log_dt': 'delta_w', 'delta_od_b_re': 'delta_w', 'delta_od_b_im': 'delta_w', 'delta_od_c_re': 'delta_w', 'delta_od_c_im': 'delta_w', 'delta_od_d': 'delta_w', 'delta_od_w_glu': 'delta_w', 'delta_ffn_w_up': 'delta_w', 'delta_ffn_conv_w': 'delta_w', 'delta_ffn_conv_b': 'delta_w', 'delta_ffn_w_down': 'delta_w', 'delta_ple_w_proj': 'delta_w', 'delta_ple_w_gate': 'delta_w', 'new_m_norm_mix': 'new_m', 'new_m_norm_ffn': 'new_m', 'new_m_ev_w_in': 'new_m', 'new_m_ev_b_fgate': 'new_m', 'new_m_ev_q_norm': 'new_m', 'new_m_ev_k_norm': 'new_m', 'new_m_ev_v_norm': 'new_m', 'new_m_ev_w_spatial': 'new_m', 'new_m_ev_b_spatial': 'new_m', 'new_m_ev_w_out': 'new_m', 'new_m_od_a_re': 'new_m', 'new_m_od_a_im': 'new_m', 'new_m_od_log_dt': 'new_m', 'new_m_od_b_re': 'new_m', 'new_m_od_b_im': 'new_m', 'new_m_od_c_re': 'new_m', 'new_m_od_c_im': 'new_m', 'new_m_od_d': 'new_m', 'new_m_od_w_glu': 'new_m', 'new_m_ffn_w_up': 'new_m', 'new_m_ffn_conv_w': 'new_m', 'new_m_ffn_conv_b': 'new_m', 'new_m_ffn_w_down': 'new_m', 'new_m_ple_w_proj': 'new_m', 'new_m_ple_w_gate': 'new_m', 'new_v_norm_mix': 'new_v', 'new_v_norm_ffn': 'new_v', 'new_v_ev_w_in': 'new_v', 'new_v_ev_b_fgate': 'new_v', 'new_v_ev_q_norm': 'new_v', 'new_v_ev_k_norm': 'new_v', 'new_v_ev_v_norm': 'new_v', 'new_v_ev_w_spatial': 'new_v', 'new_v_ev_b_spatial': 'new_v', 'new_v_ev_w_out': 'new_v', 'new_v_od_a_re': 'new_v', 'new_v_od_a_im': 'new_v', 'new_v_od_log_dt': 'new_v', 'new_v_od_b_re': 'new_v', 'new_v_od_b_im': 'new_v', 'new_v_od_c_re': 'new_v', 'new_v_od_c_im': 'new_v', 'new_v_od_d': 'new_v', 'new_v_od_w_glu': 'new_v', 'new_v_ffn_w_up': 'new_v', 'new_v_ffn_conv_w': 'new_v', 'new_v_ffn_conv_b': 'new_v', 'new_v_ffn_w_down': 'new_v', 'new_v_ple_w_proj': 'new_v', 'new_v_ple_w_gate': 'new_v'}


def _forward(args):
    return _fwd_reference(*[args[k] for k in FWD_PARAMS])


def _output_shape():
    def fwd():
        inp = _fwd_setup_inputs(0)
        return _fwd_reference(*[inp[k] for k in FWD_PARAMS])
    out = _jax.eval_shape(fwd)
    return out.shape, out.dtype

N_MICROBATCH = 1
ADAM_LR = 0.001
ADAM_B1 = 0.9
ADAM_B2 = 0.999
ADAM_EPS = 1e-08
ADAM_WD = 0.01
ADAM_STEP = 10
PER_EXAMPLE_BATCH_AXIS = {'x': 0, 'p': 1, 'loss_target': 0}
SHARED_INPUTS = []
_WEIGHT_DTYPES = {'norm_mix': _jnp.float32, 'norm_ffn': _jnp.float32, 'ev_w_in': _jnp.float32, 'ev_b_fgate': _jnp.float32, 'ev_q_norm': _jnp.float32, 'ev_k_norm': _jnp.float32, 'ev_v_norm': _jnp.float32, 'ev_w_spatial': _jnp.float32, 'ev_b_spatial': _jnp.float32, 'ev_w_out': _jnp.float32, 'od_a_re': _jnp.float32, 'od_a_im': _jnp.float32, 'od_log_dt': _jnp.float32, 'od_b_re': _jnp.float32, 'od_b_im': _jnp.float32, 'od_c_re': _jnp.float32, 'od_c_im': _jnp.float32, 'od_d': _jnp.float32, 'od_w_glu': _jnp.float32, 'ffn_w_up': _jnp.float32, 'ffn_conv_w': _jnp.float32, 'ffn_conv_b': _jnp.float32, 'ffn_w_down': _jnp.float32, 'ple_w_proj': _jnp.float32, 'ple_w_gate': _jnp.float32}
MOMENT_SCALE = {'norm_mix': 4.378157e+01, 'norm_ffn': 1.047608e+02, 'ev_w_in': 3.630204e+00, 'ev_b_fgate': 9.390007e+01, 'ev_q_norm': 1.669755e+01, 'ev_k_norm': 1.696420e+01, 'ev_v_norm': 2.798079e+01, 'ev_w_spatial': 1.755878e+01, 'ev_b_spatial': 5.809237e+01, 'ev_w_out': 1.700380e+01, 'od_a_re': 3.439019e-01, 'od_a_im': 4.010155e-01, 'od_log_dt': 2.871555e+01, 'od_b_re': 3.415967e-01, 'od_b_im': 3.343259e-01, 'od_c_re': 4.602548e-01, 'od_c_im': 4.427530e-01, 'od_d': 2.324269e+01, 'od_w_glu': 8.866751e+00, 'ffn_w_up': 2.978497e+00, 'ffn_conv_w': 1.499722e+01, 'ffn_conv_b': 1.556012e+01, 'ffn_w_down': 2.517454e+00, 'ple_w_proj': 1.443724e+00, 'ple_w_gate': 1.714383e+00}


def _to_microbatches(a, axis):
    t = _jnp.moveaxis(a, axis, 0)
    t = t.reshape((N_MICROBATCH, t.shape[0] // N_MICROBATCH) + t.shape[1:])
    return _jnp.moveaxis(t, 1, axis + 1)


def setup_inputs(seed: int = 0) -> dict:
    inp = _fwd_setup_inputs(seed)
    key = _jax.random.fold_in(_jax.random.key(seed), 7919)
    shape, _ = _output_shape()
    out = dict(inp)
    out["loss_target"] = _jax.random.normal(_jax.random.fold_in(key, 0), shape, _jnp.float32)
    for i, name in enumerate(TWIN_WEIGHTS):
        w = inp[name].astype(_jnp.float32)
        if MOMENT_SCALE is None:
            s = _jnp.sqrt(_jnp.mean(_jnp.square(w)) + 1e-30)
        else:
            s = MOMENT_SCALE[name]
        km, kv = _jax.random.split(_jax.random.fold_in(key, i + 1))
        out[name] = w
        out["m_" + name] = s * _jax.random.normal(km, w.shape, _jnp.float32)
        out["v_" + name] = (s * s) * _jax.random.uniform(kv, w.shape, _jnp.float32, 0.5, 1.5)
    if N_MICROBATCH > 1:
        for name, axis in PER_EXAMPLE_BATCH_AXIS.items():
            out[name] = _to_microbatches(out[name], axis)
    return {'x': out['x'], 'p': out['p'], 'norm_mix': out['norm_mix'], 'norm_ffn': out['norm_ffn'], 'ev_w_in': out['ev_w_in'], 'ev_b_fgate': out['ev_b_fgate'], 'ev_q_norm': out['ev_q_norm'], 'ev_k_norm': out['ev_k_norm'], 'ev_v_norm': out['ev_v_norm'], 'ev_w_spatial': out['ev_w_spatial'], 'ev_b_spatial': out['ev_b_spatial'], 'ev_w_out': out['ev_w_out'], 'od_a_re': out['od_a_re'], 'od_a_im': out['od_a_im'], 'od_log_dt': out['od_log_dt'], 'od_b_re': out['od_b_re'], 'od_b_im': out['od_b_im'], 'od_c_re': out['od_c_re'], 'od_c_im': out['od_c_im'], 'od_d': out['od_d'], 'od_w_glu': out['od_w_glu'], 'ffn_w_up': out['ffn_w_up'], 'ffn_conv_w': out['ffn_conv_w'], 'ffn_conv_b': out['ffn_conv_b'], 'ffn_w_down': out['ffn_w_down'], 'ple_w_proj': out['ple_w_proj'], 'ple_w_gate': out['ple_w_gate'], 'loss_target': out['loss_target'], 'm_norm_mix': out['m_norm_mix'], 'm_norm_ffn': out['m_norm_ffn'], 'm_ev_w_in': out['m_ev_w_in'], 'm_ev_b_fgate': out['m_ev_b_fgate'], 'm_ev_q_norm': out['m_ev_q_norm'], 'm_ev_k_norm': out['m_ev_k_norm'], 'm_ev_v_norm': out['m_ev_v_norm'], 'm_ev_w_spatial': out['m_ev_w_spatial'], 'm_ev_b_spatial': out['m_ev_b_spatial'], 'm_ev_w_out': out['m_ev_w_out'], 'm_od_a_re': out['m_od_a_re'], 'm_od_a_im': out['m_od_a_im'], 'm_od_log_dt': out['m_od_log_dt'], 'm_od_b_re': out['m_od_b_re'], 'm_od_b_im': out['m_od_b_im'], 'm_od_c_re': out['m_od_c_re'], 'm_od_c_im': out['m_od_c_im'], 'm_od_d': out['m_od_d'], 'm_od_w_glu': out['m_od_w_glu'], 'm_ffn_w_up': out['m_ffn_w_up'], 'm_ffn_conv_w': out['m_ffn_conv_w'], 'm_ffn_conv_b': out['m_ffn_conv_b'], 'm_ffn_w_down': out['m_ffn_w_down'], 'm_ple_w_proj': out['m_ple_w_proj'], 'm_ple_w_gate': out['m_ple_w_gate'], 'v_norm_mix': out['v_norm_mix'], 'v_norm_ffn': out['v_norm_ffn'], 'v_ev_w_in': out['v_ev_w_in'], 'v_ev_b_fgate': out['v_ev_b_fgate'], 'v_ev_q_norm': out['v_ev_q_norm'], 'v_ev_k_norm': out['v_ev_k_norm'], 'v_ev_v_norm': out['v_ev_v_norm'], 'v_ev_w_spatial': out['v_ev_w_spatial'], 'v_ev_b_spatial': out['v_ev_b_spatial'], 'v_ev_w_out': out['v_ev_w_out'], 'v_od_a_re': out['v_od_a_re'], 'v_od_a_im': out['v_od_a_im'], 'v_od_log_dt': out['v_od_log_dt'], 'v_od_b_re': out['v_od_b_re'], 'v_od_b_im': out['v_od_b_im'], 'v_od_c_re': out['v_od_c_re'], 'v_od_c_im': out['v_od_c_im'], 'v_od_d': out['v_od_d'], 'v_od_w_glu': out['v_od_w_glu'], 'v_ffn_w_up': out['v_ffn_w_up'], 'v_ffn_conv_w': out['v_ffn_conv_w'], 'v_ffn_conv_b': out['v_ffn_conv_b'], 'v_ffn_w_down': out['v_ffn_w_down'], 'v_ple_w_proj': out['v_ple_w_proj'], 'v_ple_w_gate': out['v_ple_w_gate']}


def _loss(weights, diff, rest, loss_target):
    with _jax.named_scope("forward"):
        args = {**rest, TWIN_DIFF_INPUT: diff, **{k: w.astype(_WEIGHT_DTYPES[k]) for k, w in weights.items()}}
        y = _forward(args)
    with _jax.named_scope("loss_head"):
        err = _jnp.square(y.astype(_jnp.float32) - loss_target)
        return 0.5 * _jnp.sum(_jnp.mean(err, axis=-1)) if err.ndim else 0.5 * err


def _adamw(w, g, m, v):
    m = ADAM_B1 * m + (1.0 - ADAM_B1) * g
    v = ADAM_B2 * v + (1.0 - ADAM_B2) * _jnp.square(g)
    m_hat = m / (1.0 - ADAM_B1 ** ADAM_STEP)
    v_hat = v / (1.0 - ADAM_B2 ** ADAM_STEP)
    delta = -ADAM_LR * (m_hat / (_jnp.sqrt(v_hat) + ADAM_EPS) + ADAM_WD * w)
    return delta, m, v


def reference(x, p, norm_mix, norm_ffn, ev_w_in, ev_b_fgate, ev_q_norm, ev_k_norm, ev_v_norm, ev_w_spatial, ev_b_spatial, ev_w_out, od_a_re, od_a_im, od_log_dt, od_b_re, od_b_im, od_c_re, od_c_im, od_d, od_w_glu, ffn_w_up, ffn_conv_w, ffn_conv_b, ffn_w_down, ple_w_proj, ple_w_gate, loss_target, m_norm_mix, m_norm_ffn, m_ev_w_in, m_ev_b_fgate, m_ev_q_norm, m_ev_k_norm, m_ev_v_norm, m_ev_w_spatial, m_ev_b_spatial, m_ev_w_out, m_od_a_re, m_od_a_im, m_od_log_dt, m_od_b_re, m_od_b_im, m_od_c_re, m_od_c_im, m_od_d, m_od_w_glu, m_ffn_w_up, m_ffn_conv_w, m_ffn_conv_b, m_ffn_w_down, m_ple_w_proj, m_ple_w_gate, v_norm_mix, v_norm_ffn, v_ev_w_in, v_ev_b_fgate, v_ev_q_norm, v_ev_k_norm, v_ev_v_norm, v_ev_w_spatial, v_ev_b_spatial, v_ev_w_out, v_od_a_re, v_od_a_im, v_od_log_dt, v_od_b_re, v_od_b_im, v_od_c_re, v_od_c_im, v_od_d, v_od_w_glu, v_ffn_w_up, v_ffn_conv_w, v_ffn_conv_b, v_ffn_w_down, v_ple_w_proj, v_ple_w_gate):
    given = dict(x=x, p=p, norm_mix=norm_mix, norm_ffn=norm_ffn, ev_w_in=ev_w_in, ev_b_fgate=ev_b_fgate, ev_q_norm=ev_q_norm, ev_k_norm=ev_k_norm, ev_v_norm=ev_v_norm, ev_w_spatial=ev_w_spatial, ev_b_spatial=ev_b_spatial, ev_w_out=ev_w_out, od_a_re=od_a_re, od_a_im=od_a_im, od_log_dt=od_log_dt, od_b_re=od_b_re, od_b_im=od_b_im, od_c_re=od_c_re, od_c_im=od_c_im, od_d=od_d, od_w_glu=od_w_glu, ffn_w_up=ffn_w_up, ffn_conv_w=ffn_conv_w, ffn_conv_b=ffn_conv_b, ffn_w_down=ffn_w_down, ple_w_proj=ple_w_proj, ple_w_gate=ple_w_gate, loss_target=loss_target, m_norm_mix=m_norm_mix, m_norm_ffn=m_norm_ffn, m_ev_w_in=m_ev_w_in, m_ev_b_fgate=m_ev_b_fgate, m_ev_q_norm=m_ev_q_norm, m_ev_k_norm=m_ev_k_norm, m_ev_v_norm=m_ev_v_norm, m_ev_w_spatial=m_ev_w_spatial, m_ev_b_spatial=m_ev_b_spatial, m_ev_w_out=m_ev_w_out, m_od_a_re=m_od_a_re, m_od_a_im=m_od_a_im, m_od_log_dt=m_od_log_dt, m_od_b_re=m_od_b_re, m_od_b_im=m_od_b_im, m_od_c_re=m_od_c_re, m_od_c_im=m_od_c_im, m_od_d=m_od_d, m_od_w_glu=m_od_w_glu, m_ffn_w_up=m_ffn_w_up, m_ffn_conv_w=m_ffn_conv_w, m_ffn_conv_b=m_ffn_conv_b, m_ffn_w_down=m_ffn_w_down, m_ple_w_proj=m_ple_w_proj, m_ple_w_gate=m_ple_w_gate, v_norm_mix=v_norm_mix, v_norm_ffn=v_norm_ffn, v_ev_w_in=v_ev_w_in, v_ev_b_fgate=v_ev_b_fgate, v_ev_q_norm=v_ev_q_norm, v_ev_k_norm=v_ev_k_norm, v_ev_v_norm=v_ev_v_norm, v_ev_w_spatial=v_ev_w_spatial, v_ev_b_spatial=v_ev_b_spatial, v_ev_w_out=v_ev_w_out, v_od_a_re=v_od_a_re, v_od_a_im=v_od_a_im, v_od_log_dt=v_od_log_dt, v_od_b_re=v_od_b_re, v_od_b_im=v_od_b_im, v_od_c_re=v_od_c_re, v_od_c_im=v_od_c_im, v_od_d=v_od_d, v_od_w_glu=v_od_w_glu, v_ffn_w_up=v_ffn_w_up, v_ffn_conv_w=v_ffn_conv_w, v_ffn_conv_b=v_ffn_conv_b, v_ffn_w_down=v_ffn_w_down, v_ple_w_proj=v_ple_w_proj, v_ple_w_gate=v_ple_w_gate)
    weights = {n: given[n] for n in TWIN_WEIGHTS}
    shared = {n: given[n] for n in SHARED_INPUTS}
    per_example = {n: given[n] for n in ['x', 'p']}
    grad_fn = _jax.value_and_grad(_loss, argnums=(0, 1))

    def one_microbatch(ex, loss_target):
        ex = dict(ex)
        diff = ex.pop(TWIN_DIFF_INPUT)
        return grad_fn(weights, diff, {**shared, **ex}, loss_target)

    if N_MICROBATCH == 1:
        loss, (grad_w, grad_x) = one_microbatch(per_example, given["loss_target"])
    else:
        def body(carry, xs):
            loss_sum, grad_sum = carry
            l_k, (gw_k, gx_k) = one_microbatch(xs[0], xs[1])
            with _jax.named_scope("update"):
                return (loss_sum + l_k, _jax.tree.map(_jnp.add, grad_sum, gw_k)), gx_k

        init = (_jnp.zeros((), _jnp.float32), _jax.tree.map(_jnp.zeros_like, weights))
        (loss, grad_w), grad_x = _jax.lax.scan(body, init, (per_example, given["loss_target"]))
    with _jax.named_scope("update"):
        delta_w, new_m, new_v = {}, {}, {}
        for n in TWIN_WEIGHTS:
            delta_w[n], new_m[n], new_v[n] = _adamw(weights[n], grad_w[n], given["m_" + n], given["v_" + n])
    return (loss, grad_x, *[grad_w[n] for n in TWIN_WEIGHTS], *[delta_w[n] for n in TWIN_WEIGHTS],
            *[new_m[n] for n in TWIN_WEIGHTS], *[new_v[n] for n in TWIN_WEIGHTS])
```

```python
import functools
import math

import jax
import jax.numpy as jnp
import numpy as np
from jax import lax
from jax.experimental import pallas as pl
from jax.experimental.pallas import tpu as pltpu

F32 = jnp.float32
BF16 = jnp.bfloat16
MESH = pl.DeviceIdType.MESH

V7X_VMEM_LIMIT_BYTES = 56 * 1024 * 1024
LANES = 128

D_MODEL = 1024
DEPTH = 4
A_GROUPS = 4
A_CHUNK = 128
A_WIDTH = 512
B_HEADS = 8
B_HEAD_DIM = 64
B_WIDTH = 512
S5_GROUP_CH = 16
S5_GROUPS = 64
S5_STATE = 64
S5_BLOCKS = 8
S5_LANES = 512
D_FF = 2816
PLE_DIM = 256
EPS = 1e-6
NEG_INF = -1e30

ADAM_LR = 0.001
ADAM_B1 = 0.9
ADAM_B2 = 0.999
ADAM_EPS = 1e-08
ADAM_WD = 0.01
ADAM_STEP = 10

N_CHIPS = 4
PACK_W = 1024


def _cparams(sem):
    return pltpu.CompilerParams(dimension_semantics=sem, vmem_limit_bytes=V7X_VMEM_LIMIT_BYTES)


def _pick(n, target):
    if n <= target:
        return n
    t = (target // LANES) * LANES
    while t >= LANES:
        if n % t == 0:
            return t
        t -= LANES
    return n


_GELU_K = 0.7978845608028654
_GELU_C = 0.044715


def _gelu(x):
    return x * (0.5 * (1.0 + jnp.tanh(_GELU_K * (x + _GELU_C * (x * x * x)))))


def _gelu_grad(x):
    x2 = x * x
    t = jnp.tanh(_GELU_K * (x + _GELU_C * (x * x2)))
    return 0.5 * (1.0 + t) + (0.5 * x) * (1.0 - t * t) * (_GELU_K * (1.0 + (3.0 * _GELU_C) * x2))


def _sigmoid(x):
    return 1.0 / (1.0 + jnp.exp(-x))


def _log_sigmoid(x):
    return -(jnp.maximum(-x, 0.0) + jnp.log(1.0 + jnp.exp(-jnp.abs(x))))


def _rstd(x):
    return lax.rsqrt(jnp.mean(x * x, axis=-1, keepdims=True) + EPS)


def _rows(name, fn, row_ins, full_ins, outs, accs=(), tile=256):
    rows = row_ins[0].shape[0]
    r = min(tile, rows)
    n = rows // r
    n_in = len(row_ins) + len(full_ins)
    n_out = len(outs)

    def body(*refs):
        res = fn(*[ref[...] for ref in refs[:n_in]])
        for ref, v in zip(refs[n_in:n_in + n_out], res[:n_out]):
            ref[...] = v.astype(ref.dtype)
        acc_refs = refs[n_in + n_out:]
        if acc_refs:
            @pl.when(pl.program_id(0) == 0)
            def _():
                for ref in acc_refs:
                    ref[...] = jnp.zeros(ref.shape, ref.dtype)

            for ref, v in zip(acc_refs, res[n_out:]):
                ref[...] += v

    in_specs = [pl.BlockSpec((r, a.shape[1]), lambda i: (i, 0)) for a in row_ins]
    in_specs += [pl.BlockSpec(a.shape, lambda i, nd=a.ndim: (0,) * nd) for a in full_ins]
    out_shape = [jax.ShapeDtypeStruct((rows, w), dt) for (w, dt) in outs]
    out_shape += [jax.ShapeDtypeStruct(s, F32) for s in accs]
    out_specs = [pl.BlockSpec((r, w), lambda i: (i, 0)) for (w, dt) in outs]
    out_specs += [pl.BlockSpec(s, lambda i, nd=len(s): (0,) * nd) for s in accs]
    return pl.pallas_call(
        body, grid=(n,), in_specs=in_specs, out_specs=out_specs, out_shape=out_shape, name=name,
        compiler_params=_cparams(("arbitrary",) if accs else ("parallel",)),
    )(*row_ins, *full_ins)


_DOT_DIMS = {"nn": (((1,), (0,)), ((), ())), "nt": (((1,), (1,)), ((), ())), "tn": (((0,), (0,)), ((), ()))}


def _mm(name, a, b, mode="nn", out_dtype=F32, res=None, tm=1024, tn=1024, tk=1024):
    if mode == "nn":
        (m, k), (k2, n) = a.shape, b.shape
    elif mode == "nt":
        (m, k), (n, k2) = a.shape, b.shape
    else:
        (k, m), (k2, n) = a.shape, b.shape
    assert k == k2, (name, a.shape, b.shape, mode)
    tm, tn, tk = _pick(m, tm), _pick(n, tn), _pick(k, tk)
    nk = k // tk
    dims = _DOT_DIMS[mode]
    has_res = res is not None

    def body(*refs):
        a_ref, b_ref = refs[0], refs[1]
        res_ref = refs[2] if has_res else None
        o_ref = refs[3] if has_res else refs[2]
        prod = lax.dot_general(a_ref[...].astype(BF16), b_ref[...].astype(BF16), dims, preferred_element_type=F32)
        if nk == 1:
            if has_res:
                prod = res_ref[...] + prod
            o_ref[...] = prod.astype(o_ref.dtype)
            return
        acc = refs[-1]
        kk = pl.program_id(2)

        @pl.when(kk == 0)
        def _():
            acc[...] = prod

        @pl.when(kk > 0)
        def _():
            acc[...] += prod

        @pl.when(kk == nk - 1)
        def _():
            tot = acc[...]
            if has_res:
                tot = res_ref[...] + tot
            o_ref[...] = tot.astype(o_ref.dtype)

    if mode == "nn":
        a_spec = pl.BlockSpec((tm, tk), lambda i, j, kk: (i, kk))
        b_spec = pl.BlockSpec((tk, tn), lambda i, j, kk: (kk, j))
    elif mode == "nt":
        a_spec = pl.BlockSpec((tm, tk), lambda i, j, kk: (i, kk))
        b_spec = pl.BlockSpec((tn, tk), lambda i, j, kk: (j, kk))
    else:
        a_spec = pl.BlockSpec((tk, tm), lambda i, j, kk: (kk, i))
        b_spec = pl.BlockSpec((tk, tn), lambda i, j, kk: (kk, j))
    o_spec = pl.BlockSpec((tm, tn), lambda i, j, kk: (i, j))
    in_specs = [a_spec, b_spec] + ([o_spec] if has_res else [])
    args = (a, b) + ((res,) if has_res else ())
    return pl.pallas_call(
        body, grid=(m // tm, n // tn, nk), in_specs=in_specs, out_specs=o_spec,
        out_shape=jax.ShapeDtypeStruct((m, n), out_dtype), name=name,
        scratch_shapes=[pltpu.VMEM((tm, tn), F32)] if nk > 1 else [],
        compiler_params=_cparams(("parallel", "parallel", "arbitrary")),
    )(*args)


def _rmsnorm_fwd(name, x, g, outs):
    def fn(xv, gv):
        y = (xv * _rstd(xv)) * gv
        return tuple(y for _ in outs)

    return _rows(name, fn, [x], [g], [(x.shape[1], dt) for dt in outs])


def _rmsnorm_bwd(name, x, dy, dres, g):
    def fn(xv, dyv, drv, gv):
        r = _rstd(xv)
        xh = xv * r
        dyg = dyv * gv
        dx = drv + r * (dyg - xh * jnp.mean(dyg * xh, axis=-1, keepdims=True))
        return dx, jnp.sum(dyv * xh, axis=0, keepdims=True)

    w = x.shape[1]
    return _rows(name, fn, [x, dy, dres], [g], [(w, F32)], accs=[(1, w)])


_CONV_ROWS = 256
_CONV_COLS = 1408


def _conv_taps(h_ref, halo_ref, first):
    h = h_ref[...]
    rows = h.shape[0]
    row = lax.broadcasted_iota(jnp.int32, (rows, 1), 0)
    keep = jnp.where(first, 0.0, 1.0)
    m1 = halo_ref[7:8, :] * keep
    m2 = halo_ref[6:7, :] * keep
    p1 = jnp.where(row == 0, m1, pltpu.roll(h, 1, 0))
    p2 = jnp.where(row == 0, m2, jnp.where(row == 1, m1, pltpu.roll(h, 2, 0)))
    return h, p1, p2


def _conv_specs(rows, r, cw):
    tile = pl.BlockSpec((r, cw), lambda j, i: (i, j))
    halo = pl.BlockSpec((8, cw), lambda j, i: (jnp.maximum(i * (r // 8) - 1, 0), j))
    vec3 = pl.BlockSpec((3, cw), lambda j, i: (0, j))
    vec1 = pl.BlockSpec((1, cw), lambda j, i: (0, j))
    return tile, halo, vec3, vec1


def _convffn_fwd(name, hg, hu, wg, wu, bg, bu):
    rows, f = hg.shape
    r, cw = min(_CONV_ROWS, rows), _pick(f, _CONV_COLS)

    def body(hg_ref, hgh_ref, hu_ref, huh_ref, wg_ref, wu_ref, bg_ref, bu_ref, o_ref):
        first = pl.program_id(1) == 0
        h, p1, p2 = _conv_taps(hg_ref, hgh_ref, first)
        g = bg_ref[...] + wg_ref[0:1, :] * p2 + wg_ref[1:2, :] * p1 + wg_ref[2:3, :] * h
        h, p1, p2 = _conv_taps(hu_ref, huh_ref, first)
        u = bu_ref[...] + wu_ref[0:1, :] * p2 + wu_ref[1:2, :] * p1 + wu_ref[2:3, :] * h
        o_ref[...] = ((g * _sigmoid(g)) * u).astype(o_ref.dtype)

    tile, halo, vec3, vec1 = _conv_specs(rows, r, cw)
    return pl.pallas_call(
        body, grid=(f // cw, rows // r), in_specs=[tile, halo, tile, halo, vec3, vec3, vec1, vec1], out_specs=tile,
        out_shape=jax.ShapeDtypeStruct((rows, f), BF16), name=name, compiler_params=_cparams(("parallel", "parallel")),
    )(hg, hg, hu, hu, wg, wu, bg, bu)


def _convffn_bwd_gate(name, da, hg, hu, wg, wu, bg, bu):
    rows, f = hg.shape
    r, cw = min(_CONV_ROWS, rows), _pick(f, _CONV_COLS)

    def body(da_ref, hg_ref, hgh_ref, hu_ref, huh_ref, wg_ref, wu_ref, bg_ref, bu_ref,
             dcg_ref, dcu_ref, dwg_ref, dwu_ref, dbg_ref, dbu_ref):
        first = pl.program_id(1) == 0
        hgv, g1, g2 = _conv_taps(hg_ref, hgh_ref, first)
        g = bg_ref[...] + wg_ref[0:1, :] * g2 + wg_ref[1:2, :] * g1 + wg_ref[2:3, :] * hgv
        huv, u1, u2 = _conv_taps(hu_ref, huh_ref, first)
        u = bu_ref[...] + wu_ref[0:1, :] * u2 + wu_ref[1:2, :] * u1 + wu_ref[2:3, :] * huv
        da_v = da_ref[...]
        sg = _sigmoid(g)
        dcg = da_v * u * (sg * (1.0 + g * (1.0 - sg)))
        dcu = da_v * (g * sg)
        dcg_ref[...] = dcg
        dcu_ref[...] = dcu

        @pl.when(first)
        def _():
            for ref in (dwg_ref, dwu_ref, dbg_ref, dbu_ref):
                ref[...] = jnp.zeros(ref.shape, ref.dtype)

        def colsum(v):
            return jnp.sum(v, axis=0, keepdims=True)

        dwg_ref[0:1, :] += colsum(dcg * g2)
        dwg_ref[1:2, :] += colsum(dcg * g1)
        dwg_ref[2:3, :] += colsum(dcg * hgv)
        dwu_ref[0:1, :] += colsum(dcu * u2)
        dwu_ref[1:2, :] += colsum(dcu * u1)
        dwu_ref[2:3, :] += colsum(dcu * huv)
        dbg_ref[...] += colsum(dcg)
        dbu_ref[...] += colsum(dcu)

    tile, halo, vec3, vec1 = _conv_specs(rows, r, cw)
    big = jax.ShapeDtypeStruct((rows, f), F32)
    return pl.pallas_call(
        body, grid=(f // cw, rows // r),
        in_specs=[tile, tile, halo, tile, halo, vec3, vec3, vec1, vec1],
        out_specs=[tile, tile, vec3, vec3, vec1, vec1],
        out_shape=[big, big, jax.ShapeDtypeStruct((3, f), F32), jax.ShapeDtypeStruct((3, f), F32),
                   jax.ShapeDtypeStruct((1, f), F32), jax.ShapeDtypeStruct((1, f), F32)],
        name=name, compiler_params=_cparams(("parallel", "arbitrary")),
    )(da, hg, hg, hu, hu, wg, wu, bg, bu)


def _conv_transpose(name, dc, w):
    rows, f = dc.shape
    r, cw = min(_CONV_ROWS, rows), _pick(f, _CONV_COLS)
    nrt = rows // r

    def body(d_ref, halo_ref, w_ref, o_ref):
        d = d_ref[...]
        row = lax.broadcasted_iota(jnp.int32, (r, 1), 0)
        keep = jnp.where(pl.program_id(1) == nrt - 1, 0.0, 1.0)
        n0 = halo_ref[0:1, :] * keep
        n1 = halo_ref[1:2, :] * keep
        f1 = jnp.where(row == r - 1, n0, pltpu.roll(d, r - 1, 0))
        f2 = jnp.where(row == r - 1, n1, jnp.where(row == r - 2, n0, pltpu.roll(d, r - 2, 0)))
        o_ref[...] = (w_ref[2:3, :] * d + w_ref[1:2, :] * f1 + w_ref[0:1, :] * f2).astype(o_ref.dtype)

    tile = pl.BlockSpec((r, cw), lambda j, i: (i, j))
    halo = pl.BlockSpec((8, cw), lambda j, i: (jnp.minimum((i + 1) * (r // 8), rows // 8 - 1), j))
    vec3 = pl.BlockSpec((3, cw), lambda j, i: (0, j))
    return pl.pallas_call(
        body, grid=(f // cw, nrt), in_specs=[tile, halo, vec3], out_specs=tile,
        out_shape=jax.ShapeDtypeStruct((rows, f), BF16), name=name, compiler_params=_cparams(("parallel", "parallel")),
    )(dc, dc, w)


_GMLP_ROWS = 256


def _gmlp_group_norm(vg, gain):
    r = lax.rsqrt(jnp.mean(vg * vg, axis=-1, keepdims=True) + EPS)
    vh = vg * r
    return vh, r, vh * gain


def _gmlp_fwd(name, zuv, v_gain, w_tril, b_exp):
    rows = zuv.shape[0]
    r = min(_GMLP_ROWS, rows)

    def body(z_ref, gain_ref, w_ref, b_ref, o_ref):
        for ch in range(r // A_CHUNK):
            lo = ch * A_CHUNK
            for g in range(A_GROUPS):
                c0 = g * LANES
                u = _gelu(z_ref[lo:lo + A_CHUNK, c0:c0 + LANES])
                v = _gelu(z_ref[lo:lo + A_CHUNK, A_WIDTH + c0:A_WIDTH + c0 + LANES])
                _, _, vn = _gmlp_group_norm(v, gain_ref[:, c0:c0 + LANES])
                sv = jnp.dot(w_ref[g], vn.astype(BF16), preferred_element_type=F32) + b_ref[g]
                o_ref[lo:lo + A_CHUNK, c0:c0 + LANES] = (u * sv).astype(o_ref.dtype)

    return pl.pallas_call(
        body, grid=(rows // r,),
        in_specs=[pl.BlockSpec((r, 2 * A_WIDTH), lambda i: (i, 0)), pl.BlockSpec((1, A_WIDTH), lambda i: (0, 0)),
                  pl.BlockSpec((A_GROUPS, A_CHUNK, A_CHUNK), lambda i: (0, 0, 0)),
                  pl.BlockSpec((A_GROUPS, A_CHUNK, LANES), lambda i: (0, 0, 0))],
        out_specs=pl.BlockSpec((r, A_WIDTH), lambda i: (i, 0)),
        out_shape=jax.ShapeDtypeStruct((rows, A_WIDTH), BF16), name=name, compiler_params=_cparams(("parallel",)),
    )(zuv, v_gain, w_tril, b_exp)


def _gmlp_bwd(name, zuv, dya, v_gain, w_tril, w_tril_t, b_exp):
    rows = zuv.shape[0]
    r = min(_GMLP_ROWS, rows)

    def body(z_ref, dy_ref, gain_ref, w_ref, wt_ref, b_ref, dz_ref, dw_ref, db_ref, dgain_ref):
        @pl.when(pl.program_id(0) == 0)
        def _():
            for ref in (dw_ref, db_ref, dgain_ref):
                ref[...] = jnp.zeros(ref.shape, ref.dtype)

        for ch in range(r // A_CHUNK):
            lo = ch * A_CHUNK
            for g in range(A_GROUPS):
                c0 = g * LANES
                zu = z_ref[lo:lo + A_CHUNK, c0:c0 + LANES]
                zv = z_ref[lo:lo + A_CHUNK, A_WIDTH + c0:A_WIDTH + c0 + LANES]
                gain = gain_ref[:, c0:c0 + LANES]
                u = _gelu(zu)
                v = _gelu(zv)
                vh, rr, vn = _gmlp_group_norm(v, gain)
                vn_b = vn.astype(BF16)
                sv = jnp.dot(w_ref[g], vn_b, preferred_element_type=F32) + b_ref[g]
                dy = dy_ref[lo:lo + A_CHUNK, c0:c0 + LANES]
                dsv = dy * u
                dsv_b = dsv.astype(BF16)
                dz_ref[lo:lo + A_CHUNK, c0:c0 + LANES] = ((dy * sv) * _gelu_grad(zu)).astype(dz_ref.dtype)
                dw_ref[g] += lax.dot_general(dsv_b, vn_b, _DOT_DIMS["nt"], preferred_element_type=F32)
                db_ref[g] += dsv
                dvn = jnp.dot(wt_ref[g], dsv_b, preferred_element_type=F32)
                dgain_ref[:, c0:c0 + LANES] += jnp.sum(dvn * vh, axis=0, keepdims=True)
                dvh = dvn * gain
                dv = rr * (dvh - vh * jnp.mean(dvh * vh, axis=-1, keepdims=True))
                dz_ref[lo:lo + A_CHUNK, A_WIDTH + c0:A_WIDTH + c0 + LANES] = (dv * _gelu_grad(zv)).astype(dz_ref.dtype)

    wspec = pl.BlockSpec((A_GROUPS, A_CHUNK, A_CHUNK), lambda i: (0, 0, 0))
    bspec = pl.BlockSpec((A_GROUPS, A_CHUNK, LANES), lambda i: (0, 0, 0))
    gspec = pl.BlockSpec((1, A_WIDTH), lambda i: (0, 0))
    return pl.pallas_call(
        body, grid=(rows // r,),
        in_specs=[pl.BlockSpec((r, 2 * A_WIDTH), lambda i: (i, 0)), pl.BlockSpec((r, A_WIDTH), lambda i: (i, 0)),
                  gspec, wspec, wspec, bspec],
        out_specs=[pl.BlockSpec((r, 2 * A_WIDTH), lambda i: (i, 0)), wspec, bspec, gspec],
        out_shape=[jax.ShapeDtypeStruct((rows, 2 * A_WIDTH), BF16),
                   jax.ShapeDtypeStruct((A_GROUPS, A_CHUNK, A_CHUNK), F32),
                   jax.ShapeDtypeStruct((A_GROUPS, A_CHUNK, LANES), F32), jax.ShapeDtypeStruct((1, A_WIDTH), F32)],
        name=name, compiler_params=_cparams(("arbitrary",)),
    )(zuv, dya, v_gain, w_tril, w_tril_t, b_exp)


_ATT_T = 512
_Q_SCALE = B_HEAD_DIM ** -0.5


def _head_mean(v, bd):
    return jnp.dot(v, bd, preferred_element_type=F32, precision=lax.Precision.HIGHEST)


def _qkv_prep_fwd(name, zqkv, zf, qg, kg, bf, bd):
    def fn(z, f, qg_v, kg_v, bf_v, bd_v):
        zq, zk, zv = z[:, :B_WIDTH], z[:, B_WIDTH:2 * B_WIDTH], z[:, 2 * B_WIDTH:]
        q = (zq * lax.rsqrt(_head_mean(zq * zq, bd_v) + EPS)) * qg_v * _Q_SCALE
        k = (zk * lax.rsqrt(_head_mean(zk * zk, bd_v) + EPS)) * kg_v
        return q, k, zv, _log_sigmoid(f + bf_v)

    return _rows(name, fn, [zqkv, zf], [qg, kg, bf, bd],
                 [(B_WIDTH, BF16), (B_WIDTH, BF16), (B_WIDTH, BF16), (LANES, F32)])


def _qkv_prep_bwd(name, zqkv, zf, dq, dk, dv, dls, qg, kg, bf, bd):
    def fn(z, f, dq_v, dk_v, dv_v, dls_v, qg_v, kg_v, bf_v, bd_v):
        zq, zk = z[:, :B_WIDTH], z[:, B_WIDTH:2 * B_WIDTH]

        def norm_bwd(x, dy, gain):
            r = lax.rsqrt(_head_mean(x * x, bd_v) + EPS)
            xh = x * r
            dxh = dy * gain
            dx = r * (dxh - xh * _head_mean(dxh * xh, bd_v))
            return dx, jnp.sum(dy * xh, axis=0, keepdims=True)

        dzq, dqg = norm_bwd(zq, dq_v * _Q_SCALE, qg_v)
        dzk, dkg = norm_bwd(zk, dk_v, kg_v)
        dzf = dls_v * (1.0 - _sigmoid(f + bf_v))
        return jnp.concatenate([dzq, dzk, dv_v], axis=1), dzf, dqg, dkg, jnp.sum(dzf, axis=0, keepdims=True)

    return _rows(name, fn, [zqkv, zf, dq, dk, dv, dls], [qg, kg, bf, bd],
                 [(3 * B_WIDTH, BF16), (LANES, BF16)], accs=[(1, B_WIDTH), (1, B_WIDTH), (1, LANES)])


def _cumsum_rows(name, a, reverse=False, tile=512):
    rows, w = a.shape
    r = min(tile, rows)
    n = rows // r

    def body(a_ref, o_ref, carry):
        @pl.when(pl.program_id(0) == 0)
        def _():
            carry[...] = jnp.zeros(carry.shape, carry.dtype)

        x = a_ref[...]
        row = lax.broadcasted_iota(jnp.int32, (r, 1), 0)
        s = 1
        while s < r:
            if reverse:
                x = x + jnp.where(row < r - s, pltpu.roll(x, r - s, 0), 0.0)
            else:
                x = x + jnp.where(row >= s, pltpu.roll(x, s, 0), 0.0)
            s *= 2
        x = x + carry[0:1, :]
        o_ref[...] = x
        edge = x[0:1, :] if reverse else x[r - 1:r, :]
        carry[...] = jnp.broadcast_to(edge, carry.shape)

    idx = (lambda i: (n - 1 - i, 0)) if reverse else (lambda i: (i, 0))
    return pl.pallas_call(
        body, grid=(n,), in_specs=[pl.BlockSpec((r, w), idx)], out_specs=pl.BlockSpec((r, w), idx),
        out_shape=jax.ShapeDtypeStruct((rows, w), F32), scratch_shapes=[pltpu.VMEM((8, w), F32)], name=name,
        compiler_params=_cparams(("arbitrary",)),
    )(a)


def _head_masks():
    lane = lax.broadcasted_iota(jnp.int32, (1, LANES), 1)
    return [lane < B_HEAD_DIM, lane >= B_HEAD_DIM]


def _causal(t):
    row = lax.broadcasted_iota(jnp.int32, (t, t), 0)
    col = lax.broadcasted_iota(jnp.int32, (t, t), 1)
    return row, col


def _flash_fwd(name, q, k, v, nck_rows):
    rows = q.shape[0]
    t = min(_ATT_T, rows)
    nb = rows // t

    def body(q_ref, k_ref, v_ref, nck_ref, o_ref, lse_ref):
        pair, i = pl.program_id(0), pl.program_id(1)
        q2 = q_ref[...]
        row, col = _causal(t)
        o_tot = jnp.zeros((t, LANES), F32)
        for hh, hmask in enumerate(_head_masks()):
            head = 2 * pair + hh
            qh = jnp.where(hmask, q2, jnp.zeros_like(q2))

            def step(j, carry, diag, qh=qh, hmask=hmask, head=head):
                m, l, acc = carry
                start = pl.multiple_of(j * t, t)
                kb = k_ref[pl.ds(start, t), :]
                vb = v_ref[pl.ds(start, t), :]
                vb = jnp.where(hmask, vb, jnp.zeros_like(vb))
                s = lax.dot_general(qh, kb, _DOT_DIMS["nt"], preferred_element_type=F32)
                s = s + nck_ref[head, pl.ds(j, 1), :]
                if diag:
                    s = jnp.where(col <= row, s, NEG_INF)
                m_new = jnp.maximum(m, jnp.max(s, axis=1, keepdims=True))
                p = jnp.exp(s - m_new)
                alpha = jnp.exp(m - m_new)
                l = alpha * l + jnp.sum(p, axis=1, keepdims=True)
                acc = alpha * acc + jnp.dot(p.astype(BF16), vb, preferred_element_type=F32)
                return m_new, l, acc

            init = (jnp.full((t, 1), NEG_INF, F32), jnp.zeros((t, 1), F32), jnp.zeros((t, LANES), F32))
            carry = lax.fori_loop(0, i, lambda j, c: step(j, c, False), init)
            m, l, acc = step(i, carry, True)
            o_tot = o_tot + acc / l
            lse_ref[hh] = m + jnp.log(l)
        o_ref[...] = o_tot

    return pl.pallas_call(
        body, grid=(B_HEADS // 2, nb),
        in_specs=[pl.BlockSpec((t, LANES), lambda p, i: (i, p)), pl.BlockSpec((rows, LANES), lambda p, i: (0, p)),
                  pl.BlockSpec((rows, LANES), lambda p, i: (0, p)),
                  pl.BlockSpec((B_HEADS, nb, t), lambda p, i: (0, 0, 0))],
        out_specs=[pl.BlockSpec((t, LANES), lambda p, i: (i, p)), pl.BlockSpec((2, t, 1), lambda p, i: (p, i, 0))],
        out_shape=[jax.ShapeDtypeStruct((rows, B_WIDTH), F32), jax.ShapeDtypeStruct((B_HEADS, rows, 1), F32)],
        name=name, compiler_params=_cparams(("parallel", "parallel")),
    )(q, k, v, nck_rows)


def _flash_bwd_dq(name, q, k, v, nck_rows, o, do, lse_col):
    rows = q.shape[0]
    t = min(_ATT_T, rows)
    nb = rows // t

    def body(q_ref, k_ref, v_ref, nck_ref, o_ref, do_ref, lse_ref, dq_ref, delta_ref):
        pair, i = pl.program_id(0), pl.program_id(1)
        q2 = q_ref[...]
        do2 = do_ref[...]
        od = o_ref[...] * do2
        do_b = do2.astype(BF16)
        row, col = _causal(t)
        dq_tot = jnp.zeros((t, LANES), F32)
        for hh, hmask in enumerate(_head_masks()):
            head = 2 * pair + hh
            qh = jnp.where(hmask, q2, jnp.zeros_like(q2))
            doh = jnp.where(hmask, do_b, jnp.zeros_like(do_b))
            delta = jnp.sum(jnp.where(hmask, od, 0.0), axis=1, keepdims=True)
            delta_ref[hh] = delta
            lse = lse_ref[hh]

            def step(j, acc, diag, qh=qh, doh=doh, delta=delta, lse=lse, hmask=hmask, head=head):
                start = pl.multiple_of(j * t, t)
                kb = k_ref[pl.ds(start, t), :]
                vb = v_ref[pl.ds(start, t), :]
                s = lax.dot_general(qh, kb, _DOT_DIMS["nt"], preferred_element_type=F32)
                s = s + nck_ref[head, pl.ds(j, 1), :]
                p = jnp.exp(s - lse)
                if diag:
                    p = jnp.where(col <= row, p, 0.0)
                dp = lax.dot_general(doh, vb, _DOT_DIMS["nt"], preferred_element_type=F32)
                ds = p * (dp - delta)
                kh = jnp.where(hmask, kb, jnp.zeros_like(kb))
                return acc + jnp.dot(ds.astype(BF16), kh, preferred_element_type=F32)

            acc = lax.fori_loop(0, i, lambda j, c: step(j, c, False), jnp.zeros((t, LANES), F32))
            dq_tot = dq_tot + step(i, acc, True)
        dq_ref[...] = dq_tot

    tile = pl.BlockSpec((t, LANES), lambda p, i: (i, p))
    full = pl.BlockSpec((rows, LANES), lambda p, i: (0, p))
    colspec = pl.BlockSpec((2, t, 1), lambda p, i: (p, i, 0))
    return pl.pallas_call(
        body, grid=(B_HEADS // 2, nb),
        in_specs=[tile, full, full, pl.BlockSpec((B_HEADS, nb, t), lambda p, i: (0, 0, 0)), tile, tile, colspec],
        out_specs=[tile, colspec],
        out_shape=[jax.ShapeDtypeStruct((rows, B_WIDTH), F32), jax.ShapeDtypeStruct((B_HEADS, rows, 1), F32)],
        name=name, compiler_params=_cparams(("parallel", "parallel")),
    )(q, k, v, nck_rows, o, do, lse_col)


def _flash_bwd_dkv(name, q, k, v, nck_col, do, lse_rows, delta_rows):
    rows = q.shape[0]
    t = min(_ATT_T, rows)
    nb = rows // t

    def body(k_ref, v_ref, q_ref, do_ref, nck_ref, lse_ref, delta_ref, dk_ref, dv_ref, dn_ref):
        pair, j = pl.program_id(0), pl.program_id(1)
        k2 = k_ref[...]
        v2 = v_ref[...]
        row, col = _causal(t)
        dk_tot = jnp.zeros((t, LANES), F32)
        dv_tot = jnp.zeros((t, LANES), F32)
        for hh, hmask in enumerate(_head_masks()):
            head = 2 * pair + hh
            kh = jnp.where(hmask, k2, jnp.zeros_like(k2))
            vh = jnp.where(hmask, v2, jnp.zeros_like(v2))
            nck = nck_ref[hh]

            def step(i, carry, diag, kh=kh, vh=vh, nck=nck, hmask=hmask, head=head):
                dk, dv, dn = carry
                start = pl.multiple_of(i * t, t)
                qb = q_ref[pl.ds(start, t), :]
                dob = do_ref[pl.ds(start, t), :].astype(BF16)
                st = lax.dot_general(kh, qb, _DOT_DIMS["nt"], preferred_element_type=F32) + nck
                pt = jnp.exp(st - lse_ref[head, pl.ds(i, 1), :])
                if diag:
                    pt = jnp.where(row <= col, pt, 0.0)
                dpt = lax.dot_general(vh, dob, _DOT_DIMS["nt"], preferred_element_type=F32)
                dst = pt * (dpt - delta_ref[head, pl.ds(i, 1), :])
                dohm = jnp.where(hmask, dob, jnp.zeros_like(dob))
                qhm = jnp.where(hmask, qb, jnp.zeros_like(qb))
                dv = dv + jnp.dot(pt.astype(BF16), dohm, preferred_element_type=F32)
                dk = dk + jnp.dot(dst.astype(BF16), qhm, preferred_element_type=F32)
                dn = dn + jnp.sum(dst, axis=1, keepdims=True)
                return dk, dv, dn

            zero = jnp.zeros((t, LANES), F32)
            carry = step(j, (zero, zero, jnp.zeros((t, 1), F32)), True)
            dk, dv, dn = lax.fori_loop(j + 1, nb, lambda i, c: step(i, c, False), carry)
            dk_tot = dk_tot + dk
            dv_tot = dv_tot + dv
            dn_ref[hh] = dn
        dk_ref[...] = dk_tot
        dv_ref[...] = dv_tot

    tile = pl.BlockSpec((t, LANES), lambda p, j: (j, p))
    full = pl.BlockSpec((rows, LANES), lambda p, j: (0, p))
    colspec = pl.BlockSpec((2, t, 1), lambda p, j: (p, j, 0))
    rowspec = pl.BlockSpec((B_HEADS, nb, t), lambda p, j: (0, 0, 0))
    big = jax.ShapeDtypeStruct((rows, B_WIDTH), F32)
    return pl.pallas_call(
        body, grid=(B_HEADS // 2, nb),
        in_specs=[tile, tile, full, full, colspec, rowspec, rowspec], out_specs=[tile, tile, colspec],
        out_shape=[big, big, jax.ShapeDtypeStruct((B_HEADS, rows, 1), F32)],
        name=name, compiler_params=_cparams(("parallel", "parallel")),
    )(k, v, q, do, nck_col, lse_rows, delta_rows)


_S5_ROWS = 256


def _s5_discretize(a_re, a_im, log_dt, b_re, b_im):
    dt = jnp.exp(log_dt)[:, None]
    mag = jnp.exp(a_re * dt)
    ab_re, ab_im = mag * jnp.cos(a_im * dt), mag * jnp.sin(a_im * dt)
    den = a_re * a_re + a_im * a_im
    nr, ni = ab_re - 1.0, ab_im
    cr = (nr * a_re + ni * a_im) / den
    ci = (ni * a_re - nr * a_im) / den
    bb_re = cr[..., None] * b_re - ci[..., None] * b_im
    bb_im = cr[..., None] * b_im + ci[..., None] * b_re
    return ab_re, ab_im, bb_re, bb_im


def _s5_block_diag(m):
    g, r, c = m.shape
    mb = m.reshape(S5_BLOCKS, 8, r, c)
    eye = jnp.eye(8, dtype=m.dtype)
    return jnp.einsum("bgrc,gh->bgrhc", mb, eye).reshape(S5_BLOCKS, 8 * r, 8 * c)


def _s5_block_diag_extract(m, r, c):
    mb = m.reshape(S5_BLOCKS, 8, r, 8, c)
    return jnp.einsum("bgrhc,gh->bgrc", mb, jnp.eye(8, dtype=m.dtype)).reshape(S5_GROUPS, r, c)


def _s5_tables(ab_re, ab_im, r):
    ar = jnp.broadcast_to(ab_re.reshape(1, -1), (r, S5_GROUPS * S5_STATE))
    ai = jnp.broadcast_to(ab_im.reshape(1, -1), (r, S5_GROUPS * S5_STATE))

    def mul(x, y):
        return x[0] * y[0] - x[1] * y[1], x[0] * y[1] + x[1] * y[0]

    return lax.associative_scan(mul, (ar, ai), axis=0)


def _shift_down(x, s, row):
    return jnp.where(row >= s, pltpu.roll(x, s, 0), 0.0)


def _shift_up(x, s, row):
    r = x.shape[0]
    return jnp.where(row < r - s, pltpu.roll(x, r - s, 0), 0.0)


def _s5_scan_tile(u_ref, bcat_ref, pr_ref, pi_ref, cin_r, cin_i, row):
    r = u_ref.shape[0]
    bu = jnp.dot(u_ref[...], bcat_ref[...], preferred_element_type=F32)
    xr, xi = bu[:, :S5_LANES], bu[:, S5_LANES:]
    s = 1
    while s < r:
        ar, ai = pr_ref[s - 1:s, :], pi_ref[s - 1:s, :]
        sr, si = _shift_down(xr, s, row), _shift_down(xi, s, row)
        xr, xi = xr + (ar * sr - ai * si), xi + (ar * si + ai * sr)
        s *= 2
    pr, pi = pr_ref[...], pi_ref[...]
    xr, xi = xr + (pr * cin_r - pi * cin_i), xi + (pr * cin_i + pi * cin_r)
    return xr, xi


def _s5_fwd(name, u, bcat, ccat, pw_re, pw_im):
    rows = u.shape[0]
    r = pw_re.shape[0]
    nt = rows // r

    def body(u_ref, bcat_ref, ccat_ref, pr_ref, pi_ref, y_ref, xin_ref, carry):
        @pl.when(pl.program_id(1) == 0)
        def _():
            carry[...] = jnp.zeros(carry.shape, carry.dtype)

        row = lax.broadcasted_iota(jnp.int32, (r, 1), 0)
        xin_ref[...] = carry[...]
        xr, xi = _s5_scan_tile(u_ref, bcat_ref, pr_ref, pi_ref, carry[0:1, :S5_LANES], carry[0:1, S5_LANES:], row)
        xcat = jnp.concatenate([xr, xi], axis=1)
        carry[...] = jnp.broadcast_to(xcat[r - 1:r, :], carry.shape)
        y_ref[...] = jnp.dot(xcat.astype(BF16), ccat_ref[...], preferred_element_type=F32)

    return pl.pallas_call(
        body, grid=(S5_BLOCKS, nt),
        in_specs=[pl.BlockSpec((r, LANES), lambda b, i: (i, b)),
                  pl.BlockSpec((None, LANES, 2 * S5_LANES), lambda b, i: (b, 0, 0)),
                  pl.BlockSpec((None, 2 * S5_LANES, LANES), lambda b, i: (b, 0, 0)),
                  pl.BlockSpec((r, S5_LANES), lambda b, i: (0, b)), pl.BlockSpec((r, S5_LANES), lambda b, i: (0, b))],
        out_specs=[pl.BlockSpec((r, LANES), lambda b, i: (i, b)),
                   pl.BlockSpec((None, 8, 2 * S5_LANES), lambda b, i: (b, i, 0))],
        out_shape=[jax.ShapeDtypeStruct((rows, D_MODEL), F32),
                   jax.ShapeDtypeStruct((S5_BLOCKS, 8 * nt, 2 * S5_LANES), F32)],
        scratch_shapes=[pltpu.VMEM((8, 2 * S5_LANES), F32)], name=name,
        compiler_params=_cparams(("parallel", "arbitrary")),
    )(u, bcat, ccat, pw_re, pw_im)


def _s5_bwd(name, u, dy, xin, bcat, ccat, pw_re, pw_im, pwf_re, pwf_im):
    rows = u.shape[0]
    r = pw_re.shape[0]
    nt = rows // r

    def body(u_ref, dy_ref, xin_ref, bcat_ref, ccat_ref, pr_ref, pi_ref, fr_ref, fi_ref,
             du_ref, db_ref, dc_ref, dar_ref, dai_ref, carry):
        @pl.when(pl.program_id(1) == 0)
        def _():
            carry[...] = jnp.zeros(carry.shape, carry.dtype)
            for ref in (db_ref, dc_ref, dar_ref, dai_ref):
                ref[...] = jnp.zeros(ref.shape, ref.dtype)

        row = lax.broadcasted_iota(jnp.int32, (r, 1), 0)
        cin_r, cin_i = xin_ref[0:1, :S5_LANES], xin_ref[0:1, S5_LANES:]
        xr, xi = _s5_scan_tile(u_ref, bcat_ref, pr_ref, pi_ref, cin_r, cin_i, row)
        dy_b = dy_ref[...].astype(BF16)
        xcat = jnp.concatenate([xr, xi], axis=1).astype(BF16)
        dc_ref[...] += lax.dot_general(xcat, dy_b, _DOT_DIMS["tn"], preferred_element_type=F32)
        g = lax.dot_general(dy_b, ccat_ref[...], _DOT_DIMS["nt"], preferred_element_type=F32)
        lr, li = g[:, :S5_LANES], g[:, S5_LANES:]
        s = 1
        while s < r:
            ar, ai = pr_ref[s - 1:s, :], pi_ref[s - 1:s, :]
            sr, si = _shift_up(lr, s, row), _shift_up(li, s, row)
            lr, li = lr + (ar * sr + ai * si), li + (ar * si - ai * sr)
            s *= 2
        nr, ni = carry[0:1, :S5_LANES], carry[0:1, S5_LANES:]
        fr, fi = fr_ref[...], fi_ref[...]
        lr, li = lr + (fr * nr + fi * ni), li + (fr * ni - fi * nr)
        carry[...] = jnp.broadcast_to(jnp.concatenate([lr[0:1, :], li[0:1, :]], axis=1), carry.shape)
        lcat = jnp.concatenate([lr, li], axis=1).astype(BF16)
        du_ref[...] = lax.dot_general(lcat, bcat_ref[...], _DOT_DIMS["nt"], preferred_element_type=F32)
        db_ref[...] += lax.dot_general(u_ref[...], lcat, _DOT_DIMS["tn"], preferred_element_type=F32)
        pxr = jnp.where(row == 0, cin_r, pltpu.roll(xr, 1, 0))
        pxi = jnp.where(row == 0, cin_i, pltpu.roll(xi, 1, 0))
        dar_ref[...] += jnp.sum((lr * pxr + li * pxi).reshape(r // 8, 8, S5_LANES), axis=0)
        dai_ref[...] += jnp.sum((li * pxr - lr * pxi).reshape(r // 8, 8, S5_LANES), axis=0)

    rev = lambda b, i: (nt - 1 - i, b)
    tab = pl.BlockSpec((r, S5_LANES), lambda b, i: (0, b))
    return pl.pallas_call(
        body, grid=(S5_BLOCKS, nt),
        in_specs=[pl.BlockSpec((r, LANES), rev), pl.BlockSpec((r, LANES), rev),
                  pl.BlockSpec((None, 8, 2 * S5_LANES), lambda b, i: (b, nt - 1 - i, 0)),
                  pl.BlockSpec((None, LANES, 2 * S5_LANES), lambda b, i: (b, 0, 0)),
                  pl.BlockSpec((None, 2 * S5_LANES, LANES), lambda b, i: (b, 0, 0)), tab, tab, tab, tab],
        out_specs=[pl.BlockSpec((r, LANES), rev),
                   pl.BlockSpec((None, LANES, 2 * S5_LANES), lambda b, i: (b, 0, 0)),
                   pl.BlockSpec((None, 2 * S5_LANES, LANES), lambda b, i: (b, 0, 0)),
                   pl.BlockSpec((None, 8, S5_LANES), lambda b, i: (b, 0, 0)),
                   pl.BlockSpec((None, 8, S5_LANES), lambda b, i: (b, 0, 0))],
        out_shape=[jax.ShapeDtypeStruct((rows, D_MODEL), F32),
                   jax.ShapeDtypeStruct((S5_BLOCKS, LANES, 2 * S5_LANES), F32),
                   jax.ShapeDtypeStruct((S5_BLOCKS, 2 * S5_LANES, LANES), F32),
                   jax.ShapeDtypeStruct((S5_BLOCKS, 8, S5_LANES), F32),
                   jax.ShapeDtypeStruct((S5_BLOCKS, 8, S5_LANES), F32)],
        scratch_shapes=[pltpu.VMEM((8, 2 * S5_LANES), F32)], name=name,
        compiler_params=_cparams(("parallel", "arbitrary")),
    )(u, dy, xin, bcat, ccat, pw_re, pw_im, pwf_re, pwf_im)


def _ones_gain():
    return jnp.ones((1, D_MODEL), F32)


def _channel_fwd(i, x1, p_i, w, rp):
    hn, = _rmsnorm_fwd(f"ffn_norm_{i}", x1, rp["norm_ffn"][i][None], [BF16])
    hg = _mm(f"ffn_up_g_{i}", hn, w["up_g"], tn=1408)
    hu = _mm(f"ffn_up_u_{i}", hn, w["up_u"], tn=1408)
    a = _convffn_fwd(f"ffn_conv_{i}", hg, hu, w["cw_g"], w["cw_u"], w["cb_g"], w["cb_u"])
    x2 = _mm(f"ffn_down_{i}", a, w["down"], res=x1, tk=1408)
    r, = _rmsnorm_fwd(f"ple_norm_{i}", x2, _ones_gain(), [BF16])
    zg = _mm(f"ple_gate_{i}", r, w["ple_gate"])
    pp = _mm(f"ple_proj_{i}", p_i, w["ple_proj"])
    x3, = _rows(f"ple_out_{i}", lambda xv, zv, pv: (xv + _sigmoid(zv) * pv,), [x2, zg, pp], [], [(D_MODEL, F32)])
    return x3, dict(x1=x1, hn=hn, hg=hg, hu=hu, a=a, x2=x2, r=r, zg=zg, pp=pp, p_i=p_i)


def _channel_bwd(i, dx3, sv, w, rp):
    def ple_bwd(dv, zv, pv):
        gate = _sigmoid(zv)
        return dv * gate, (dv * pv) * (gate * (1.0 - gate))

    dpp, dzg = _rows(f"ple_out_bwd_{i}", ple_bwd, [dx3, sv["zg"], sv["pp"]], [], [(D_MODEL, BF16), (D_MODEL, BF16)])
    g = {}
    g["ple_proj"] = _mm(f"ple_proj_dw_{i}", sv["p_i"], dpp, "tn")
    g["ple_gate"] = _mm(f"ple_gate_dw_{i}", sv["r"], dzg, "tn")
    dr = _mm(f"ple_gate_dx_{i}", dzg, w["ple_gate"], "nt")
    dx2, _ = _rmsnorm_bwd(f"ple_norm_bwd_{i}", sv["x2"], dr, dx3, _ones_gain())
    da = _mm(f"ffn_down_dx_{i}", dx2, w["down"], "nt", tn=1408)
    g["down"] = _mm(f"ffn_down_dw_{i}", sv["a"], dx2, "tn", tm=1408)
    dcg, dcu, g["cw_g"], g["cw_u"], dbg, dbu = _convffn_bwd_gate(
        f"ffn_conv_bwd_{i}", da, sv["hg"], sv["hu"], w["cw_g"], w["cw_u"], w["cb_g"], w["cb_u"])
    g["conv_b"] = jnp.concatenate([dbg, dbu], axis=1)[0]
    dhg = _conv_transpose(f"ffn_conv_t_g_{i}", dcg, w["cw_g"])
    dhu = _conv_transpose(f"ffn_conv_t_u_{i}", dcu, w["cw_u"])
    g["up_g"] = _mm(f"ffn_up_g_dw_{i}", sv["hn"], dhg, "tn", tn=1408)
    g["up_u"] = _mm(f"ffn_up_u_dw_{i}", sv["hn"], dhu, "tn", tn=1408)
    dhn = _mm(f"ffn_up_g_dx_{i}", dhg, w["up_g"], "nt", tk=1408)
    dhn = _mm(f"ffn_up_u_dx_{i}", dhu, w["up_u"], "nt", res=dhn, tk=1408)
    dx1, dgf = _rmsnorm_bwd(f"ffn_norm_bwd_{i}", sv["x1"], dhn, dx2, rp["norm_ffn"][i][None])
    g["norm_ffn"] = dgf[0]
    return dx1, g


def _even_consts(e, rp):
    tri = jnp.tril(jnp.ones((A_CHUNK, A_CHUNK), dtype=bool))
    w_tril = jnp.where(tri[None], rp["ev_w_spatial"][e], 0.0).astype(BF16)
    b_exp = jnp.broadcast_to(rp["ev_b_spatial"][e][:, :, None], (A_GROUPS, A_CHUNK, LANES))
    seg = np.arange(B_WIDTH) // B_HEAD_DIM
    bd = jnp.asarray((seg[:, None] == seg[None, :]).astype(np.float32) / B_HEAD_DIM)
    return dict(
        tri=tri, w_tril=w_tril, w_tril_t=jnp.swapaxes(w_tril, 1, 2), b_exp=b_exp, bd=bd,
        v_gain=rp["ev_v_norm"][e][None], qg=jnp.tile(rp["ev_q_norm"][e], B_HEADS)[None],
        kg=jnp.tile(rp["ev_k_norm"][e], B_HEADS)[None],
        bf=jnp.pad(rp["ev_b_fgate"][e], (0, LANES - B_HEADS))[None])


def _even_fwd(i, x, w, rp):
    e = i // 2
    c = _even_consts(e, rp)
    rows = x.shape[0]
    t = min(_ATT_T, rows)
    h, = _rmsnorm_fwd(f"mix_norm_{i}", x, rp["norm_mix"][i][None], [BF16])
    zuv = _mm(f"in_uv_{i}", h, w["in_uv"])
    zqkv = _mm(f"in_qkv_{i}", h, w["in_qkv"], tn=768)
    zf = _mm(f"in_f_{i}", h, w["in_f"])
    ya = _gmlp_fwd(f"gmlp_{i}", zuv, c["v_gain"], c["w_tril"], c["b_exp"])
    q, k, v, ls = _qkv_prep_fwd(f"qkv_prep_{i}", zqkv, zf, c["qg"], c["kg"], c["bf"], c["bd"])
    csum = _cumsum_rows(f"forget_cumsum_{i}", ls)
    nck = -csum[:, :B_HEADS].T
    nck_rows = nck.reshape(B_HEADS, rows // t, t)
    nck_col = nck.reshape(B_HEADS, rows, 1)
    o, lse = _flash_fwd(f"attn_{i}", q, k, v, nck_rows)
    x1 = _mm(f"out_a_{i}", ya, w["out_a"], res=x)
    x1 = _mm(f"out_b_{i}", o, w["out_b"], res=x1)
    return x1, dict(x=x, h=h, zuv=zuv, zqkv=zqkv, zf=zf, ya=ya, q=q, k=k, v=v, nck_rows=nck_rows, nck_col=nck_col,
                    o=o, lse=lse)


def _even_bwd(i, dx1, sv, w, rp):
    e = i // 2
    c = _even_consts(e, rp)
    rows = dx1.shape[0]
    t = min(_ATT_T, rows)
    nb = rows // t
    g = {}
    dya = _mm(f"out_a_dx_{i}", dx1, w["out_a"], "nt")
    do = _mm(f"out_b_dx_{i}", dx1, w["out_b"], "nt")
    g["out_a"] = _mm(f"out_a_dw_{i}", sv["ya"], dx1, "tn")
    g["out_b"] = _mm(f"out_b_dw_{i}", sv["o"], dx1, "tn")
    dq, delta = _flash_bwd_dq(f"attn_dq_{i}", sv["q"], sv["k"], sv["v"], sv["nck_rows"], sv["o"], do, sv["lse"])
    dk, dv, dn = _flash_bwd_dkv(f"attn_dkv_{i}", sv["q"], sv["k"], sv["v"], sv["nck_col"], do,
                                sv["lse"].reshape(B_HEADS, nb, t), delta.reshape(B_HEADS, nb, t))
    dcs = jnp.pad(-dn.reshape(B_HEADS, rows).T, ((0, 0), (0, LANES - B_HEADS)))
    dls = _cumsum_rows(f"forget_cumsum_bwd_{i}", dcs, reverse=True)
    dzqkv, dzf, dqg, dkg, dbf = _qkv_prep_bwd(f"qkv_prep_bwd_{i}", sv["zqkv"], sv["zf"], dq, dk, dv, dls,
                                              c["qg"], c["kg"], c["bf"], c["bd"])
    dzuv, dws, dbs, dvg = _gmlp_bwd(f"gmlp_bwd_{i}", sv["zuv"], dya, c["v_gain"], c["w_tril"], c["w_tril_t"],
                                    c["b_exp"])
    g["in_uv"] = _mm(f"in_uv_dw_{i}", sv["h"], dzuv, "tn")
    g["in_qkv"] = _mm(f"in_qkv_dw_{i}", sv["h"], dzqkv, "tn", tn=768)
    g["in_f"] = _mm(f"in_f_dw_{i}", sv["h"], dzf, "tn")
    dh = _mm(f"in_uv_dx_{i}", dzuv, w["in_uv"], "nt")
    dh = _mm(f"in_qkv_dx_{i}", dzqkv, w["in_qkv"], "nt", res=dh, tk=768)
    dh = _mm(f"in_f_dx_{i}", dzf, w["in_f"], "nt", res=dh)
    dx, dgm = _rmsnorm_bwd(f"mix_norm_bwd_{i}", sv["x"], dh, dx1, rp["norm_mix"][i][None])
    g["norm_mix"] = dgm[0]
    g["ev_b_fgate"] = dbf[0, :B_HEADS]
    g["ev_q_norm"] = dqg.reshape(B_HEADS, B_HEAD_DIM).sum(axis=0)
    g["ev_k_norm"] = dkg.reshape(B_HEADS, B_HEAD_DIM).sum(axis=0)
    g["ev_v_norm"] = dvg[0]
    g["ev_w_spatial"] = jnp.where(c["tri"][None], dws, 0.0)
    g["ev_b_spatial"] = dbs.sum(axis=-1)
    return dx, g


def _s5_consts(o, rp, r):
    prm = (rp["od_a_re"][o], rp["od_a_im"][o], rp["od_log_dt"][o], rp["od_b_re"][o], rp["od_b_im"][o])
    (ab_re, ab_im, bb_re, bb_im), vjp = jax.vjp(_s5_discretize, *prm)
    bcat = jnp.concatenate([_s5_block_diag(bb_re.transpose(0, 2, 1)), _s5_block_diag(bb_im.transpose(0, 2, 1))], axis=2)
    c_re, c_im = rp["od_c_re"][o], rp["od_c_im"][o]
    ccat = jnp.concatenate([_s5_block_diag(c_re.transpose(0, 2, 1)), _s5_block_diag(-c_im.transpose(0, 2, 1))], axis=1)
    pw_re, pw_im = _s5_tables(ab_re, ab_im, r)
    return dict(vjp=vjp, bcat=bcat.astype(BF16), ccat=ccat.astype(BF16), pw_re=pw_re, pw_im=pw_im,
                pwf_re=jnp.flip(pw_re, axis=0), pwf_im=jnp.flip(pw_im, axis=0))


def _odd_fwd(i, x, w, rp):
    o = i // 2
    rows = x.shape[0]
    c = _s5_consts(o, rp, min(_S5_ROWS, rows))
    hb, hf = _rmsnorm_fwd(f"mix_norm_{i}", x, rp["norm_mix"][i][None], [BF16, F32])
    ys, xin = _s5_fwd(f"s5_{i}", hb, c["bcat"], c["ccat"], c["pw_re"], c["pw_im"])

    def skip_gelu(yv, hv, dv):
        y = yv + dv * hv
        return y, _gelu(y)

    y, ge = _rows(f"s5_skip_gelu_{i}", skip_gelu, [ys, hf], [w["od_d"]], [(D_MODEL, F32), (D_MODEL, BF16)])
    gl = _mm(f"glu_{i}", ge, w["glu"])

    def glu_out(xv, gv):
        return (xv + gv[:, :D_MODEL] * _sigmoid(gv[:, D_MODEL:]),)

    x1, = _rows(f"glu_out_{i}", glu_out, [x, gl], [], [(D_MODEL, F32)])
    return x1, dict(x=x, hb=hb, hf=hf, xin=xin, y=y, ge=ge, gl=gl, c=c)


def _odd_bwd(i, dx1, sv, w, rp):
    o = i // 2
    c = sv["c"]
    g = {}

    def glu_bwd(dv, gv):
        ga, gb = gv[:, :D_MODEL], gv[:, D_MODEL:]
        sg = _sigmoid(gb)
        return (jnp.concatenate([dv * sg, (dv * ga) * (sg * (1.0 - sg))], axis=1),)

    dgl, = _rows(f"glu_out_bwd_{i}", glu_bwd, [dx1, sv["gl"]], [], [(2 * D_MODEL, BF16)])
    g["glu"] = _mm(f"glu_dw_{i}", sv["ge"], dgl, "tn")
    dge = _mm(f"glu_dx_{i}", dgl, w["glu"], "nt")

    def gelu_bwd(dv, yv, hv):
        dy = dv * _gelu_grad(yv)
        return dy, jnp.sum(dy * hv, axis=0, keepdims=True)

    dy, dd = _rows(f"s5_skip_gelu_bwd_{i}", gelu_bwd, [dge, sv["y"], sv["hf"]], [], [(D_MODEL, F32)],
                   accs=[(1, D_MODEL)])
    g["od_d"] = dd[0]
    du, db, dc, dar, dai = _s5_bwd(f"s5_bwd_{i}", sv["hb"], dy, sv["xin"], c["bcat"], c["ccat"], c["pw_re"],
                                   c["pw_im"], c["pwf_re"], c["pwf_im"])
    dab_re = dar.sum(axis=1).reshape(S5_GROUPS, S5_STATE)
    dab_im = dai.sum(axis=1).reshape(S5_GROUPS, S5_STATE)
    dbb_re = _s5_block_diag_extract(db[:, :, :S5_LANES], S5_GROUP_CH, S5_STATE).transpose(0, 2, 1)
    dbb_im = _s5_block_diag_extract(db[:, :, S5_LANES:], S5_GROUP_CH, S5_STATE).transpose(0, 2, 1)
    g["od_a_re"], g["od_a_im"], g["od_log_dt"], g["od_b_re"], g["od_b_im"] = c["vjp"]((dab_re, dab_im, dbb_re, dbb_im))
    g["od_c_re"] = _s5_block_diag_extract(dc[:, :S5_LANES, :], S5_STATE, S5_GROUP_CH).transpose(0, 2, 1)
    g["od_c_im"] = -_s5_block_diag_extract(dc[:, S5_LANES:, :], S5_STATE, S5_GROUP_CH).transpose(0, 2, 1)

    def norm_bwd(xv, duv, dyv, drv, gv, dv):
        dh = duv + dv * dyv
        r = _rstd(xv)
        xh = xv * r
        dhg = dh * gv
        dx = drv + r * (dhg - xh * jnp.mean(dhg * xh, axis=-1, keepdims=True))
        return dx, jnp.sum(dh * xh, axis=0, keepdims=True)

    dx, dgm = _rows(f"mix_norm_bwd_{i}", norm_bwd, [sv["x"], du, dy, dx1], [rp["norm_mix"][i][None], w["od_d"]],
                    [(D_MODEL, F32)], accs=[(1, D_MODEL)])
    g["norm_mix"] = dgm[0]
    return dx, g


def _local_step(x, p, target, lw, rp):
    saved = []
    for i in range(DEPTH):
        x, s_mix = (_even_fwd if i % 2 == 0 else _odd_fwd)(i, x, lw[i], rp)
        x, s_ch = _channel_fwd(i, x, p[i], lw[i], rp)
        saved.append((s_mix, s_ch))

    def loss_fn(yv, tv):
        diff = yv - tv
        return diff * (1.0 / D_MODEL), jnp.sum(diff * diff, axis=0, keepdims=True)

    dx, sq = _rows("loss", loss_fn, [x, target], [], [(D_MODEL, F32)], accs=[(1, D_MODEL)])
    loss = 0.5 * jnp.sum(sq) / D_MODEL
    grads = [None] * DEPTH
    for i in reversed(range(DEPTH)):
        s_mix, s_ch = saved[i]
        dx, g_ch = _channel_bwd(i, dx, s_ch, lw[i], rp)
        dx, g_mix = (_even_bwd if i % 2 == 0 else _odd_bwd)(i, dx, s_mix, lw[i], rp)
        grads[i] = {**g_ch, **g_mix}
    return loss, dx, grads


WEIGHT_ORDER = ["norm_mix", "norm_ffn", "ev_w_in", "ev_b_fgate", "ev_q_norm", "ev_k_norm", "ev_v_norm", "ev_w_spatial",
                "ev_b_spatial", "ev_w_out", "od_a_re", "od_a_im", "od_log_dt", "od_b_re", "od_b_im", "od_c_re",
                "od_c_im", "od_d", "od_w_glu", "ffn_w_up", "ffn_conv_w", "ffn_conv_b", "ffn_w_down", "ple_w_proj",
                "ple_w_gate"]
SHARD_AXIS = {"ev_w_in": 2, "ev_w_out": 1, "od_d": 1, "od_w_glu": 2, "ffn_w_up": 2, "ffn_conv_w": 2, "ffn_w_down": 1,
              "ple_w_proj": 2, "ple_w_gate": 1}
BIG_WEIGHTS = [n for n in WEIGHT_ORDER if n in SHARD_AXIS]
SMALL_WEIGHTS = [n for n in WEIGHT_ORDER if n not in SHARD_AXIS]
KEPT_F32 = ("od_d", "ffn_conv_w")
IN_UV, IN_QKV_END, IN_COLS = 2 * A_WIDTH, 2 * A_WIDTH + 3 * B_WIDTH, 2 * A_WIDTH + 3 * B_WIDTH + B_HEADS


def _layer_weights(i, full, rp):
    w = {}
    up, cw, cb = full["ffn_w_up"][i], full["ffn_conv_w"][i], rp["ffn_conv_b"][i][None]
    w["up_g"], w["up_u"] = up[:, :D_FF], up[:, D_FF:]
    w["cw_g"], w["cw_u"] = cw[:, :D_FF], cw[:, D_FF:]
    w["cb_g"], w["cb_u"] = cb[:, :D_FF], cb[:, D_FF:]
    w["down"], w["ple_proj"], w["ple_gate"] = full["ffn_w_down"][i], full["ple_w_proj"][i], full["ple_w_gate"][i]
    if i % 2 == 0:
        win, wout = full["ev_w_in"][i // 2], full["ev_w_out"][i // 2]
        w["in_uv"], w["in_qkv"] = win[:, :IN_UV], win[:, IN_UV:IN_QKV_END]
        w["in_f"] = jnp.pad(win[:, IN_QKV_END:], ((0, 0), (0, LANES - B_HEADS)))
        w["out_a"], w["out_b"] = wout[:A_WIDTH], wout[A_WIDTH:]
    else:
        w["od_d"], w["glu"] = full["od_d"][i // 2][None], full["od_w_glu"][i // 2]
    return w


def _full_grads(grads):
    ev, od = [grads[i] for i in range(0, DEPTH, 2)], [grads[i] for i in range(1, DEPTH, 2)]
    out = {
        "norm_mix": jnp.stack([g["norm_mix"] for g in grads]), "norm_ffn": jnp.stack([g["norm_ffn"] for g in grads]),
        "ev_w_in": jnp.stack([jnp.concatenate([g["in_uv"], g["in_qkv"], g["in_f"][:, :B_HEADS]], axis=1) for g in ev]),
        "ev_w_out": jnp.stack([jnp.concatenate([g["out_a"], g["out_b"]], axis=0) for g in ev]),
        "od_w_glu": jnp.stack([g["glu"] for g in od]),
        "ffn_w_up": jnp.stack([jnp.concatenate([g["up_g"], g["up_u"]], axis=1) for g in grads]),
        "ffn_conv_w": jnp.stack([jnp.concatenate([g["cw_g"], g["cw_u"]], axis=1) for g in grads]),
        "ffn_conv_b": jnp.stack([g["conv_b"] for g in grads]),
        "ffn_w_down": jnp.stack([g["down"] for g in grads]),
        "ple_w_proj": jnp.stack([g["ple_proj"] for g in grads]),
        "ple_w_gate": jnp.stack([g["ple_gate"] for g in grads]),
    }
    for n in ("ev_b_fgate", "ev_q_norm", "ev_k_norm", "ev_v_norm", "ev_w_spatial", "ev_b_spatial"):
        out[n] = jnp.stack([g[n] for g in ev])
    for n in ("od_a_re", "od_a_im", "od_log_dt", "od_b_re", "od_b_im", "od_c_re", "od_c_im", "od_d"):
        out[n] = jnp.stack([g[n] for g in od])
    return out


def _pack(arrs, row_multiple):
    flat = jnp.concatenate([a.reshape(-1) for a in arrs])
    rows = -(-flat.shape[0] // (PACK_W * row_multiple)) * row_multiple
    return jnp.pad(flat, (0, rows * PACK_W - flat.shape[0])).reshape(rows, PACK_W)


def _unpack(buf, shapes):
    flat = buf.reshape(-1)
    out, at = [], 0
    for s in shapes:
        n = int(np.prod(s))
        out.append(flat[at:at + n].reshape(s))
        at += n
    return out


def _shard(name, a, k):
    ax = SHARD_AXIS[name]
    n = a.shape[ax] // N_CHIPS
    return lax.slice_in_dim(a, k * n, (k + 1) * n, axis=ax)


_ANY = pl.BlockSpec(memory_space=pl.ANY)


def _mesh_pos():
    return lax.axis_index("x"), lax.axis_index("y"), lax.axis_index("c")


def _other_chips(x, y):
    return [(1 - x, y), (x, 1 - y), (1 - x, 1 - y)]


def _gather_weight_shards(name, shard):
    rows, w = shard.shape
    half = rows // 2

    def body(in_ref, out_ref, send_sems, recv_sems, local_sem):
        x, y, c = _mesh_pos()
        sibling = (x, y, 1 - c)
        chips = _other_chips(x, y)

        def part(k, hc):
            return out_ref.at[k, pl.ds(hc * half, half), :]

        def copy(sem, src, dst, to):
            return pltpu.make_async_remote_copy(src_ref=src, dst_ref=dst, send_sem=send_sems.at[sem],
                                                recv_sem=recv_sems.at[sem], device_id=to, device_id_type=MESH)

        mine = pltpu.make_async_copy(in_ref, out_ref.at[2 * x + y], local_sem)
        mine.start()
        sent = [copy(j, in_ref.at[pl.ds(c * half, half), :], part(2 * x + y, c), (cx, cy, c))
                for j, (cx, cy) in enumerate(chips)]
        for cp in sent:
            cp.start()
        passed = []
        for j, (cx, cy) in enumerate(chips):
            blk = part(2 * cx + cy, c)
            copy(j, blk, blk, (cx, cy, c)).wait_recv()
            cp = copy(3 + j, blk, blk, sibling)
            cp.start()
            passed.append(cp)
        for j, (cx, cy) in enumerate(chips):
            blk = part(2 * cx + cy, 1 - c)
            copy(3 + j, blk, blk, sibling).wait_recv()
        for cp in sent + passed:
            cp.wait_send()
        mine.wait()

    return pl.pallas_call(
        body, out_shape=jax.ShapeDtypeStruct((N_CHIPS, rows, w), shard.dtype), in_specs=[_ANY], out_specs=_ANY,
        scratch_shapes=[pltpu.SemaphoreType.DMA((6,)), pltpu.SemaphoreType.DMA((6,)), pltpu.SemaphoreType.DMA],
        name=name,
    )(shard)


def _swap_with_sibling(name, a):
    def body(a_ref, o_ref, send_sem, recv_sem):
        x, y, c = _mesh_pos()
        cp = pltpu.make_async_remote_copy(src_ref=a_ref, dst_ref=o_ref, send_sem=send_sem, recv_sem=recv_sem,
                                          device_id=(x, y, 1 - c), device_id_type=MESH)
        cp.start()
        cp.wait()

    return pl.pallas_call(
        body, out_shape=jax.ShapeDtypeStruct(a.shape, a.dtype), in_specs=[_ANY], out_specs=_ANY,
        scratch_shapes=[pltpu.SemaphoreType.DMA, pltpu.SemaphoreType.DMA], name=name,
    )(a)


def _send_to_owner_chips(name, h):
    _, rows, w = h.shape

    def body(h_ref, o_ref, send_sems, recv_sems):
        x, y, c = _mesh_pos()
        cps = []
        for j, (cx, cy) in enumerate(_other_chips(x, y)):
            cp = pltpu.make_async_remote_copy(src_ref=h_ref.at[2 * cx + cy], dst_ref=o_ref.at[j],
                                              send_sem=send_sems.at[j], recv_sem=recv_sems.at[j],
                                              device_id=(cx, cy, c), device_id_type=MESH)
            cp.start()
            cps.append(cp)
        for cp in cps:
            cp.wait()

    return pl.pallas_call(
        body, out_shape=jax.ShapeDtypeStruct((3, rows, w), h.dtype), in_specs=[_ANY], out_specs=_ANY,
        scratch_shapes=[pltpu.SemaphoreType.DMA((3,)), pltpu.SemaphoreType.DMA((3,))], name=name,
    )(h)


def _all_gather_devices(name, a):
    rows, w = a.shape

    def body(a_ref, out_ref, send_sems, recv_sems, local_sem):
        x, y, c = _mesh_pos()
        me, sibling = (x, y, c), (x, y, 1 - c)
        chips = _other_chips(x, y)

        def slot(px, py, pc):
            return out_ref.at[4 * px + 2 * py + pc]

        def copy(sem, block, to, src=None):
            return pltpu.make_async_remote_copy(src_ref=slot(*block) if src is None else src, dst_ref=slot(*block),
                                                send_sem=send_sems.at[sem], recv_sem=recv_sems.at[sem], device_id=to,
                                                device_id_type=MESH)

        mine = pltpu.make_async_copy(a_ref, slot(*me), local_sem)
        mine.start()
        first = [copy(0, me, sibling, src=a_ref)]
        first += [copy(1 + j, me, (*chip, c), src=a_ref) for j, chip in enumerate(chips)]
        for cp in first:
            cp.start()
        passed = [copy(4 + j, (*chip, c), sibling) for j, chip in enumerate(chips)]
        for j, chip in enumerate(chips):
            copy(1 + j, (*chip, c), me).wait_recv()
            passed[j].start()
        copy(0, sibling, me).wait_recv()
        for j, chip in enumerate(chips):
            copy(4 + j, (*chip, 1 - c), me).wait_recv()
        for cp in first + passed:
            cp.wait_send()
        mine.wait()

    return pl.pallas_call(
        body, out_shape=jax.ShapeDtypeStruct((8, rows, w), a.dtype), in_specs=[_ANY], out_specs=_ANY,
        scratch_shapes=[pltpu.SemaphoreType.DMA((7,)), pltpu.SemaphoreType.DMA((7,)), pltpu.SemaphoreType.DMA],
        name=name,
    )(a)


_PACK_TILE = 256


def _sum_rows(name, arrs):
    def fn(*vals):
        tot = vals[0]
        for v in vals[1:]:
            tot = tot + v
        return (tot,)

    return _rows(name, fn, list(arrs), [], [(arrs[0].shape[1], F32)], tile=_PACK_TILE)[0]


def _adamw(name, w, g, m, v):
    def fn(wv, gv, mv, vv):
        m2 = ADAM_B1 * mv + (1.0 - ADAM_B1) * gv
        v2 = ADAM_B2 * vv + (1.0 - ADAM_B2) * (gv * gv)
        m_hat = m2 / (1.0 - ADAM_B1 ** ADAM_STEP)
        v_hat = v2 / (1.0 - ADAM_B2 ** ADAM_STEP)
        delta = -ADAM_LR * (m_hat / (jnp.sqrt(v_hat) + ADAM_EPS) + ADAM_WD * wv)
        return delta, m2, v2

    return _rows(name, fn, [w, g, m, v], [], [(PACK_W, F32)] * 3, tile=_PACK_TILE)


def _bf16_payload(name, a):
    if name in KEPT_F32:
        return lax.bitcast_convert_type(a, BF16)
    return a.astype(BF16)


_INPUT_ORDER = (["x", "p"] + WEIGHT_ORDER + ["loss_target"] + ["m_" + n for n in WEIGHT_ORDER]
                + ["v_" + n for n in WEIGHT_ORDER])


def _step(a):
    xi, yi, ci = _mesh_pos()
    chip = 2 * xi + yi
    rp = {n: a[n] for n in SMALL_WEIGHTS}

    payload = [_bf16_payload(n, a[n]) for n in BIG_WEIGHTS]
    gathered = _gather_weight_shards("gather_weights", _pack(payload, 32))
    parts = [_unpack(gathered[k], [p.shape for p in payload]) for k in range(N_CHIPS)]
    full = {}
    for idx, n in enumerate(BIG_WEIGHTS):
        pieces = [parts[k][idx] for k in range(N_CHIPS)]
        if n in KEPT_F32:
            pieces = [lax.bitcast_convert_type(p, F32) for p in pieces]
        full[n] = jnp.concatenate(pieces, axis=SHARD_AXIS[n])
    lw = [_layer_weights(i, full, rp) for i in range(DEPTH)]

    loss_local, grad_x, grads = _local_step(a["x"][0], a["p"][:, 0], a["loss_target"][0], lw, rp)
    loss = lax.psum(loss_local, ("x", "y", "c"))
    gfull = _full_grads(grads)

    big_shapes = [a[n].shape for n in BIG_WEIGHTS]
    gpack = jnp.stack([_pack([_shard(n, gfull[n], k) for n in BIG_WEIGHTS], 2 * _PACK_TILE) for k in range(N_CHIPS)])
    rows = gpack.shape[1]
    half = rows // 2
    gsplit = gpack.reshape(N_CHIPS, 2, half, PACK_W)
    keep = lax.dynamic_index_in_dim(gsplit, ci, axis=1, keepdims=False)
    give = lax.dynamic_index_in_dim(gsplit, 1 - ci, axis=1, keepdims=False)
    got = _swap_with_sibling("grad_pair_swap", give)
    pair = _sum_rows("grad_pair_sum", [keep.reshape(N_CHIPS * half, PACK_W), got.reshape(N_CHIPS * half, PACK_W)])
    pair = pair.reshape(N_CHIPS, half, PACK_W)
    owed = _send_to_owner_chips("grad_to_owner", pair)
    mine = lax.dynamic_index_in_dim(pair, chip, axis=0, keepdims=False)
    my_half = _sum_rows("grad_owner_sum", [mine, owed[0], owed[1], owed[2]])
    other_half = _swap_with_sibling("grad_half_swap", my_half)
    g_big = jnp.where(ci == 0, jnp.concatenate([my_half, other_half]), jnp.concatenate([other_half, my_half]))

    small_shapes = [a[n].shape for n in SMALL_WEIGHTS]
    everyone = _all_gather_devices("small_grad_gather", _pack([gfull[n] for n in SMALL_WEIGHTS], _PACK_TILE))
    g_small = _sum_rows("small_grad_sum", [everyone[d] for d in range(8)])

    out = {}
    for tag, names, shapes, g, mult in (("big", BIG_WEIGHTS, big_shapes, g_big, 2 * _PACK_TILE),
                                        ("small", SMALL_WEIGHTS, small_shapes, g_small, _PACK_TILE)):
        w, m, v = (_pack([a[pre + n] for n in names], mult) for pre in ("", "m_", "v_"))
        delta, m2, v2 = _adamw(f"adamw_{tag}", w, g, m, v)
        for kind, buf in (("grad", g), ("delta", delta), ("new_m", m2), ("new_v", v2)):
            for n, val in zip(names, _unpack(buf, shapes)):
                out[kind + "_" + n] = val
    res = [loss, grad_x[None]]
    for kind in ("grad", "delta", "new_m", "new_v"):
        res += [out[kind + "_" + n] for n in WEIGHT_ORDER]
    return tuple(res)


def kernel(x, p, norm_mix, norm_ffn, ev_w_in, ev_b_fgate, ev_q_norm, ev_k_norm, ev_v_norm, ev_w_spatial, ev_b_spatial, ev_w_out, od_a_re, od_a_im, od_log_dt, od_b_re, od_b_im, od_c_re, od_c_im, od_d, od_w_glu, ffn_w_up, ffn_conv_w, ffn_conv_b, ffn_w_down, ple_w_proj, ple_w_gate, loss_target, m_norm_mix, m_norm_ffn, m_ev_w_in, m_ev_b_fgate, m_ev_q_norm, m_ev_k_norm, m_ev_v_norm, m_ev_w_spatial, m_ev_b_spatial, m_ev_w_out, m_od_a_re, m_od_a_im, m_od_log_dt, m_od_b_re, m_od_b_im, m_od_c_re, m_od_c_im, m_od_d, m_od_w_glu, m_ffn_w_up, m_ffn_conv_w, m_ffn_conv_b, m_ffn_w_down, m_ple_w_proj, m_ple_w_gate, v_norm_mix, v_norm_ffn, v_ev_w_in, v_ev_b_fgate, v_ev_q_norm, v_ev_k_norm, v_ev_v_norm, v_ev_w_spatial, v_ev_b_spatial, v_ev_w_out, v_od_a_re, v_od_a_im, v_od_log_dt, v_od_b_re, v_od_b_im, v_od_c_re, v_od_c_im, v_od_d, v_od_w_glu, v_ffn_w_up, v_ffn_conv_w, v_ffn_conv_b, v_ffn_w_down, v_ple_w_proj, v_ple_w_gate):
    args = (x, p, norm_mix, norm_ffn, ev_w_in, ev_b_fgate, ev_q_norm, ev_k_norm, ev_v_norm, ev_w_spatial, ev_b_spatial, ev_w_out, od_a_re, od_a_im, od_log_dt, od_b_re, od_b_im, od_c_re, od_c_im, od_d, od_w_glu, ffn_w_up, ffn_conv_w, ffn_conv_b, ffn_w_down, ple_w_proj, ple_w_gate, loss_target, m_norm_mix, m_norm_ffn, m_ev_w_in, m_ev_b_fgate, m_ev_q_norm, m_ev_k_norm, m_ev_v_norm, m_ev_w_spatial, m_ev_b_spatial, m_ev_w_out, m_od_a_re, m_od_a_im, m_od_log_dt, m_od_b_re, m_od_b_im, m_od_c_re, m_od_c_im, m_od_d, m_od_w_glu, m_ffn_w_up, m_ffn_conv_w, m_ffn_conv_b, m_ffn_w_down, m_ple_w_proj, m_ple_w_gate, v_norm_mix, v_norm_ffn, v_ev_w_in, v_ev_b_fgate, v_ev_q_norm, v_ev_k_norm, v_ev_v_norm, v_ev_w_spatial, v_ev_b_spatial, v_ev_w_out, v_od_a_re, v_od_a_im, v_od_log_dt, v_od_b_re, v_od_b_im, v_od_c_re, v_od_c_im, v_od_d, v_od_w_glu, v_ffn_w_up, v_ffn_conv_w, v_ffn_conv_b, v_ffn_w_down, v_ple_w_proj, v_ple_w_gate)
    return _step(dict(zip(_INPUT_ORDER, args)))
```

```python
import functools
import math

import jax
import jax.numpy as jnp
import numpy as np
from jax import lax
from jax.experimental import pallas as pl
from jax.experimental.pallas import tpu as pltpu

F32 = jnp.float32
BF16 = jnp.bfloat16
MESH = pl.DeviceIdType.MESH

V7X_VMEM_LIMIT_BYTES = 56 * 1024 * 1024
LANES = 128

D_MODEL = 1024
DEPTH = 4
A_GROUPS = 4
A_CHUNK = 128
A_WIDTH = 512
B_HEADS = 8
B_HEAD_DIM = 64
B_WIDTH = 512
S5_GROUP_CH = 16
S5_GROUPS = 64
S5_STATE = 64
S5_BLOCKS = 8
S5_LANES = 512
D_FF = 2816
PLE_DIM = 256
EPS = 1e-6
NEG_INF = -1e30

ADAM_LR = 0.001
ADAM_B1 = 0.9
ADAM_B2 = 0.999
ADAM_EPS = 1e-08
ADAM_WD = 0.01
ADAM_STEP = 10

N_CHIPS = 4
PACK_W = 1024


def _cparams(sem):
    return pltpu.CompilerParams(dimension_semantics=sem, vmem_limit_bytes=V7X_VMEM_LIMIT_BYTES)


def _pick(n, target):
    if n <= target:
        return n
    t = (target // LANES) * LANES
    while t >= LANES:
        if n % t == 0:
            return t
        t -= LANES
    return n


_GELU_K = 0.7978845608028654
_GELU_C = 0.044715


def _gelu(x):
    return x * (0.5 * (1.0 + jnp.tanh(_GELU_K * (x + _GELU_C * (x * x * x)))))


def _gelu_grad(x):
    x2 = x * x
    t = jnp.tanh(_GELU_K * (x + _GELU_C * (x * x2)))
    return 0.5 * (1.0 + t) + (0.5 * x) * (1.0 - t * t) * (_GELU_K * (1.0 + (3.0 * _GELU_C) * x2))


def _sigmoid(x):
    return 1.0 / (1.0 + jnp.exp(-x))


def _log_sigmoid(x):
    return -(jnp.maximum(-x, 0.0) + jnp.log(1.0 + jnp.exp(-jnp.abs(x))))


def _rstd(x):
    return lax.rsqrt(jnp.mean(x * x, axis=-1, keepdims=True) + EPS)


def _rows(name, fn, row_ins, full_ins, outs, accs=(), tile=256):
    rows = row_ins[0].shape[0]
    r = min(tile, rows)
    n = rows // r
    n_in = len(row_ins) + len(full_ins)
    n_out = len(outs)

    def body(*refs):
        res = fn(*[ref[...] for ref in refs[:n_in]])
        for ref, v in zip(refs[n_in:n_in + n_out], res[:n_out]):
            ref[...] = v.astype(ref.dtype)
        acc_refs = refs[n_in + n_out:]
        if acc_refs:
            @pl.when(pl.program_id(0) == 0)
            def _():
                for ref in acc_refs:
                    ref[...] = jnp.zeros(ref.shape, ref.dtype)

            for ref, v in zip(acc_refs, res[n_out:]):
                ref[...] += v

    in_specs = [pl.BlockSpec((r, a.shape[1]), lambda i: (i, 0)) for a in row_ins]
    in_specs += [pl.BlockSpec(a.shape, lambda i, nd=a.ndim: (0,) * nd) for a in full_ins]
    out_shape = [jax.ShapeDtypeStruct((rows, w), dt) for (w, dt) in outs]
    out_shape += [jax.ShapeDtypeStruct(s, F32) for s in accs]
    out_specs = [pl.BlockSpec((r, w), lambda i: (i, 0)) for (w, dt) in outs]
    out_specs += [pl.BlockSpec(s, lambda i, nd=len(s): (0,) * nd) for s in accs]
    return pl.pallas_call(
        body, grid=(n,), in_specs=in_specs, out_specs=out_specs, out_shape=out_shape, name=name,
        compiler_params=_cparams(("arbitrary",) if accs else ("parallel",)),
    )(*row_ins, *full_ins)


_DOT_DIMS = {"nn": (((1,), (0,)), ((), ())), "nt": (((1,), (1,)), ((), ())), "tn": (((0,), (0,)), ((), ()))}


def _mm(name, a, b, mode="nn", out_dtype=F32, res=None, tm=1024, tn=1024, tk=1024):
    if mode == "nn":
        (m, k), (k2, n) = a.shape, b.shape
    elif mode == "nt":
        (m, k), (n, k2) = a.shape, b.shape
    else:
        (k, m), (k2, n) = a.shape, b.shape
    assert k == k2, (name, a.shape, b.shape, mode)
    tm, tn, tk = _pick(m, tm), _pick(n, tn), _pick(k, tk)
    nk = k // tk
    dims = _DOT_DIMS[mode]
    has_res = res is not None

    def body(*refs):
        a_ref, b_ref = refs[0], refs[1]
        res_ref = refs[2] if has_res else None
        o_ref = refs[3] if has_res else refs[2]
        prod = lax.dot_general(a_ref[...].astype(BF16), b_ref[...].astype(BF16), dims, preferred_element_type=F32)
        if nk == 1:
            if has_res:
                prod = res_ref[...] + prod
            o_ref[...] = prod.astype(o_ref.dtype)
            return
        acc = refs[-1]
        kk = pl.program_id(2)

        @pl.when(kk == 0)
        def _():
            acc[...] = prod

        @pl.when(kk > 0)
        def _():
            acc[...] += prod

        @pl.when(kk == nk - 1)
        def _():
            tot = acc[...]
            if has_res:
                tot = res_ref[...] + tot
            o_ref[...] = tot.astype(o_ref.dtype)

    if mode == "nn":
        a_spec = pl.BlockSpec((tm, tk), lambda i, j, kk: (i, kk))
        b_spec = pl.BlockSpec((tk, tn), lambda i, j, kk: (kk, j))
    elif mode == "nt":
        a_spec = pl.BlockSpec((tm, tk), lambda i, j, kk: (i, kk))
        b_spec = pl.BlockSpec((tn, tk), lambda i, j, kk: (j, kk))
    else:
        a_spec = pl.BlockSpec((tk, tm), lambda i, j, kk: (kk, i))
        b_spec = pl.BlockSpec((tk, tn), lambda i, j, kk: (kk, j))
    o_spec = pl.BlockSpec((tm, tn), lambda i, j, kk: (i, j))
    in_specs = [a_spec, b_spec] + ([o_spec] if has_res else [])
    args = (a, b) + ((res,) if has_res else ())
    return pl.pallas_call(
        body, grid=(m // tm, n // tn, nk), in_specs=in_specs, out_specs=o_spec,
        out_shape=jax.ShapeDtypeStruct((m, n), out_dtype), name=name,
        scratch_shapes=[pltpu.VMEM((tm, tn), F32)] if nk > 1 else [],
        compiler_params=_cparams(("parallel", "parallel", "arbitrary")),
    )(*args)


def _rmsnorm_fwd(name, x, g, outs):
    def fn(xv, gv):
        y = (xv * _rstd(xv)) * gv
        return tuple(y for _ in outs)

    return _rows(name, fn, [x], [g], [(x.shape[1], dt) for dt in outs])


def _rmsnorm_bwd(name, x, dy, dres, g):
    def fn(xv, dyv, drv, gv):
        r = _rstd(xv)
        xh = xv * r
        dyg = dyv * gv
        dx = drv + r * (dyg - xh * jnp.mean(dyg * xh, axis=-1, keepdims=True))
        return dx, jnp.sum(dyv * xh, axis=0, keepdims=True)

    w = x.shape[1]
    return _rows(name, fn, [x, dy, dres], [g], [(w, F32)], accs=[(1, w)])


_CONV_ROWS = 256
_CONV_COLS = 1408


def _conv_taps(h_ref, halo_ref, first):
    h = h_ref[...]
    rows = h.shape[0]
    row = lax.broadcasted_iota(jnp.int32, (rows, 1), 0)
    keep = jnp.where(first, 0.0, 1.0)
    m1 = halo_ref[7:8, :] * keep
    m2 = halo_ref[6:7, :] * keep
    p1 = jnp.where(row == 0, m1, pltpu.roll(h, 1, 0))
    p2 = jnp.where(row == 0, m2, jnp.where(row == 1, m1, pltpu.roll(h, 2, 0)))
    return h, p1, p2


def _conv_specs(rows, r, cw):
    tile = pl.BlockSpec((r, cw), lambda j, i: (i, j))
    halo = pl.BlockSpec((8, cw), lambda j, i: (jnp.maximum(i * (r // 8) - 1, 0), j))
    vec3 = pl.BlockSpec((3, cw), lambda j, i: (0, j))
    vec1 = pl.BlockSpec((1, cw), lambda j, i: (0, j))
    return tile, halo, vec3, vec1


def _convffn_fwd(name, hg, hu, wg, wu, bg, bu):
    rows, f = hg.shape
    r, cw = min(_CONV_ROWS, rows), _pick(f, _CONV_COLS)

    def body(hg_ref, hgh_ref, hu_ref, huh_ref, wg_ref, wu_ref, bg_ref, bu_ref, o_ref):
        first = pl.program_id(1) == 0
        h, p1, p2 = _conv_taps(hg_ref, hgh_ref, first)
        g = bg_ref[...] + wg_ref[0:1, :] * p2 + wg_ref[1:2, :] * p1 + wg_ref[2:3, :] * h
        h, p1, p2 = _conv_taps(hu_ref, huh_ref, first)
        u = bu_ref[...] + wu_ref[0:1, :] * p2 + wu_ref[1:2, :] * p1 + wu_ref[2:3, :] * h
        o_ref[...] = ((g * _sigmoid(g)) * u).astype(o_ref.dtype)

    tile, halo, vec3, vec1 = _conv_specs(rows, r, cw)
    return pl.pallas_call(
        body, grid=(f // cw, rows // r), in_specs=[tile, halo, tile, halo, vec3, vec3, vec1, vec1], out_specs=tile,
        out_shape=jax.ShapeDtypeStruct((rows, f), BF16), name=name, compiler_params=_cparams(("parallel", "parallel")),
    )(hg, hg, hu, hu, wg, wu, bg, bu)


def _convffn_bwd_gate(name, da, hg, hu, wg, wu, bg, bu):
    rows, f = hg.shape
    r, cw = min(_CONV_ROWS, rows), _pick(f, _CONV_COLS)

    def body(da_ref, hg_ref, hgh_ref, hu_ref, huh_ref, wg_ref, wu_ref, bg_ref, bu_ref,
             dcg_ref, dcu_ref, dwg_ref, dwu_ref, dbg_ref, dbu_ref):
        first = pl.program_id(1) == 0
        hgv, g1, g2 = _conv_taps(hg_ref, hgh_ref, first)
        g = bg_ref[...] + wg_ref[0:1, :] * g2 + wg_ref[1:2, :] * g1 + wg_ref[2:3, :] * hgv
        huv, u1, u2 = _conv_taps(hu_ref, huh_ref, first)
        u = bu_ref[...] + wu_ref[0:1, :] * u2 + wu_ref[1:2, :] * u1 + wu_ref[2:3, :] * huv
        da_v = da_ref[...]
        sg = _sigmoid(g)
        dcg = da_v * u * (sg * (1.0 + g * (1.0 - sg)))
        dcu = da_v * (g * sg)
        dcg_ref[...] = dcg
        dcu_ref[...] = dcu

        @pl.when(first)
        def _():
            for ref in (dwg_ref, dwu_ref, dbg_ref, dbu_ref):
                ref[...] = jnp.zeros(ref.shape, ref.dtype)

        def colsum(v):
            return jnp.sum(v, axis=0, keepdims=True)

        dwg_ref[0:1, :] += colsum(dcg * g2)
        dwg_ref[1:2, :] += colsum(dcg * g1)
        dwg_ref[2:3, :] += colsum(dcg * hgv)
        dwu_ref[0:1, :] += colsum(dcu * u2)
        dwu_ref[1:2, :] += colsum(dcu * u1)
        dwu_ref[2:3, :] += colsum(dcu * huv)
        dbg_ref[...] += colsum(dcg)
        dbu_ref[...] += colsum(dcu)

    tile, halo, vec3, vec1 = _conv_specs(rows, r, cw)
    big = jax.ShapeDtypeStruct((rows, f), F32)
    return pl.pallas_call(
        body, grid=(f // cw, rows // r),
        in_specs=[tile, tile, halo, tile, halo, vec3, vec3, vec1, vec1],
        out_specs=[tile, tile, vec3, vec3, vec1, vec1],
        out_shape=[big, big, jax.ShapeDtypeStruct((3, f), F32), jax.ShapeDtypeStruct((3, f), F32),
                   jax.ShapeDtypeStruct((1, f), F32), jax.ShapeDtypeStruct((1, f), F32)],
        name=name, compiler_params=_cparams(("parallel", "arbitrary")),
    )(da, hg, hg, hu, hu, wg, wu, bg, bu)


def _conv_transpose(name, dc, w):
    rows, f = dc.shape
    r, cw = min(_CONV_ROWS, rows), _pick(f, _CONV_COLS)
    nrt = rows // r

    def body(d_ref, halo_ref, w_ref, o_ref):
        d = d_ref[...]
        row = lax.broadcasted_iota(jnp.int32, (r, 1), 0)
        keep = jnp.where(pl.program_id(1) == nrt - 1, 0.0, 1.0)
        n0 = halo_ref[0:1, :] * keep
        n1 = halo_ref[1:2, :] * keep
        f1 = jnp.where(row == r - 1, n0, pltpu.roll(d, r - 1, 0))
        f2 = jnp.where(row == r - 1, n1, jnp.where(row == r - 2, n0, pltpu.roll(d, r - 2, 0)))
        o_ref[...] = (w_ref[2:3, :] * d + w_ref[1:2, :] * f1 + w_ref[0:1, :] * f2).astype(o_ref.dtype)

    tile = pl.BlockSpec((r, cw), lambda j, i: (i, j))
    halo = pl.BlockSpec((8, cw), lambda j, i: (jnp.minimum((i + 1) * (r // 8), rows // 8 - 1), j))
    vec3 = pl.BlockSpec((3, cw), lambda j, i: (0, j))
    return pl.pallas_call(
        body, grid=(f // cw, nrt), in_specs=[tile, halo, vec3], out_specs=tile,
        out_shape=jax.ShapeDtypeStruct((rows, f), BF16), name=name, compiler_params=_cparams(("parallel", "parallel")),
    )(dc, dc, w)


_GMLP_ROWS = 256


def _gmlp_group_norm(vg, gain):
    r = lax.rsqrt(jnp.mean(vg * vg, axis=-1, keepdims=True) + EPS)
    vh = vg * r
    return vh, r, vh * gain


def _gmlp_fwd(name, zuv, v_gain, w_tril, b_exp):
    rows = zuv.shape[0]
    r = min(_GMLP_ROWS, rows)

    def body(z_ref, gain_ref, w_ref, b_ref, o_ref):
        for ch in range(r // A_CHUNK):
            lo = ch * A_CHUNK
            for g in range(A_GROUPS):
                c0 = g * LANES
                u = _gelu(z_ref[lo:lo + A_CHUNK, c0:c0 + LANES])
                v = _gelu(z_ref[lo:lo + A_CHUNK, A_WIDTH + c0:A_WIDTH + c0 + LANES])
                _, _, vn = _gmlp_group_norm(v, gain_ref[:, c0:c0 + LANES])
                sv = jnp.dot(w_ref[g], vn.astype(BF16), preferred_element_type=F32) + b_ref[g]
                o_ref[lo:lo + A_CHUNK, c0:c0 + LANES] = (u * sv).astype(o_ref.dtype)

    return pl.pallas_call(
        body, grid=(rows // r,),
        in_specs=[pl.BlockSpec((r, 2 * A_WIDTH), lambda i: (i, 0)), pl.BlockSpec((1, A_WIDTH), lambda i: (0, 0)),
                  pl.BlockSpec((A_GROUPS, A_CHUNK, A_CHUNK), lambda i: (0, 0, 0)),
                  pl.BlockSpec((A_GROUPS, A_CHUNK, LANES), lambda i: (0, 0, 0))],
        out_specs=pl.BlockSpec((r, A_WIDTH), lambda i: (i, 0)),
        out_shape=jax.ShapeDtypeStruct((rows, A_WIDTH), BF16), name=name, compiler_params=_cparams(("parallel",)),
    )(zuv, v_gain, w_tril, b_exp)


def _gmlp_bwd(name, zuv, dya, v_gain, w_tril, w_tril_t, b_exp):
    rows = zuv.shape[0]
    r = min(_GMLP_ROWS, rows)

    def body(z_ref, dy_ref, gain_ref, w_ref, wt_ref, b_ref, dz_ref, dw_ref, db_ref, dgain_ref):
        @pl.when(pl.program_id(0) == 0)
        def _():
            for ref in (dw_ref, db_ref, dgain_ref):
                ref[...] = jnp.zeros(ref.shape, ref.dtype)

        for ch in range(r // A_CHUNK):
            lo = ch * A_CHUNK
            for g in range(A_GROUPS):
                c0 = g * LANES
                zu = z_ref[lo:lo + A_CHUNK, c0:c0 + LANES]
                zv = z_ref[lo:lo + A_CHUNK, A_WIDTH + c0:A_WIDTH + c0 + LANES]
                gain = gain_ref[:, c0:c0 + LANES]
                u = _gelu(zu)
                v = _gelu(zv)
                vh, rr, vn = _gmlp_group_norm(v, gain)
                vn_b = vn.astype(BF16)
                sv = jnp.dot(w_ref[g], vn_b, preferred_element_type=F32) + b_ref[g]
                dy = dy_ref[lo:lo + A_CHUNK, c0:c0 + LANES]
                dsv = dy * u
                dsv_b = dsv.astype(BF16)
                dz_ref[lo:lo + A_CHUNK, c0:c0 + LANES] = ((dy * sv) * _gelu_grad(zu)).astype(dz_ref.dtype)
                dw_ref[g] += lax.dot_general(dsv_b, vn_b, _DOT_DIMS["nt"], preferred_element_type=F32)
                db_ref[g] += dsv
                dvn = jnp.dot(wt_ref[g], dsv_b, preferred_element_type=F32)
                dgain_ref[:, c0:c0 + LANES] += jnp.sum(dvn * vh, axis=0, keepdims=True)
                dvh = dvn * gain
                dv = rr * (dvh - vh * jnp.mean(dvh * vh, axis=-1, keepdims=True))
                dz_ref[lo:lo + A_CHUNK, A_WIDTH + c0:A_WIDTH + c0 + LANES] = (dv * _gelu_grad(zv)).astype(dz_ref.dtype)

    wspec = pl.BlockSpec((A_GROUPS, A_CHUNK, A_CHUNK), lambda i: (0, 0, 0))
    bspec = pl.BlockSpec((A_GROUPS, A_CHUNK, LANES), lambda i: (0, 0, 0))
    gspec = pl.BlockSpec((1, A_WIDTH), lambda i: (0, 0))
    return pl.pallas_call(
        body, grid=(rows // r,),
        in_specs=[pl.BlockSpec((r, 2 * A_WIDTH), lambda i: (i, 0)), pl.BlockSpec((r, A_WIDTH), lambda i: (i, 0)),
                  gspec, wspec, wspec, bspec],
        out_specs=[pl.BlockSpec((r, 2 * A_WIDTH), lambda i: (i, 0)), wspec, bspec, gspec],
        out_shape=[jax.ShapeDtypeStruct((rows, 2 * A_WIDTH), BF16),
                   jax.ShapeDtypeStruct((A_GROUPS, A_CHUNK, A_CHUNK), F32),
                   jax.ShapeDtypeStruct((A_GROUPS, A_CHUNK, LANES), F32), jax.ShapeDtypeStruct((1, A_WIDTH), F32)],
        name=name, compiler_params=_cparams(("arbitrary",)),
    )(zuv, dya, v_gain, w_tril, w_tril_t, b_exp)


_ATT_T = 512
_Q_SCALE = B_HEAD_DIM ** -0.5


def _head_mean(v, bd):
    return jnp.dot(v, bd, preferred_element_type=F32, precision=lax.Precision.HIGHEST)


def _qkv_prep_fwd(name, zqkv, zf, qg, kg, bf, bd):
    def fn(z, f, qg_v, kg_v, bf_v, bd_v):
        zq, zk, zv = z[:, :B_WIDTH], z[:, B_WIDTH:2 * B_WIDTH], z[:, 2 * B_WIDTH:]
        q = (zq * lax.rsqrt(_head_mean(zq * zq, bd_v) + EPS)) * qg_v * _Q_SCALE
        k = (zk * lax.rsqrt(_head_mean(zk * zk, bd_v) + EPS)) * kg_v
        return q, k, zv, _log_sigmoid(f + bf_v)

    return _rows(name, fn, [zqkv, zf], [qg, kg, bf, bd],
                 [(B_WIDTH, BF16), (B_WIDTH, BF16), (B_WIDTH, BF16), (LANES, F32)])


def _qkv_prep_bwd(name, zqkv, zf, dq, dk, dv, dls, qg, kg, bf, bd):
    def fn(z, f, dq_v, dk_v, dv_v, dls_v, qg_v, kg_v, bf_v, bd_v):
        zq, zk = z[:, :B_WIDTH], z[:, B_WIDTH:2 * B_WIDTH]

        def norm_bwd(x, dy, gain):
            r = lax.rsqrt(_head_mean(x * x, bd_v) + EPS)
            xh = x * r
            dxh = dy * gain
            dx = r * (dxh - xh * _head_mean(dxh * xh, bd_v))
            return dx, jnp.sum(dy * xh, axis=0, keepdims=True)

        dzq, dqg = norm_bwd(zq, dq_v * _Q_SCALE, qg_v)
        dzk, dkg = norm_bwd(zk, dk_v, kg_v)
        dzf = dls_v * (1.0 - _sigmoid(f + bf_v))
        return jnp.concatenate([dzq, dzk, dv_v], axis=1), dzf, dqg, dkg, jnp.sum(dzf, axis=0, keepdims=True)

    return _rows(name, fn, [zqkv, zf, dq, dk, dv, dls], [qg, kg, bf, bd],
                 [(3 * B_WIDTH, BF16), (LANES, BF16)], accs=[(1, B_WIDTH), (1, B_WIDTH), (1, LANES)])


def _cumsum_rows(name, a, reverse=False, tile=512):
    rows, w = a.shape
    r = min(tile, rows)
    n = rows // r

    def body(a_ref, o_ref, carry):
        @pl.when(pl.program_id(0) == 0)
        def _():
            carry[...] = jnp.zeros(carry.shape, carry.dtype)

        x = a_ref[...]
        row = lax.broadcasted_iota(jnp.int32, (r, 1), 0)
        s = 1
        while s < r:
            if reverse:
                x = x + jnp.where(row < r - s, pltpu.roll(x, r - s, 0), 0.0)
            else:
                x = x + jnp.where(row >= s, pltpu.roll(x, s, 0), 0.0)
            s *= 2
        x = x + carry[0:1, :]
        o_ref[...] = x
        edge = x[0:1, :] if reverse else x[r - 1:r, :]
        carry[...] = jnp.broadcast_to(edge, carry.shape)

    idx = (lambda i: (n - 1 - i, 0)) if reverse else (lambda i: (i, 0))
    return pl.pallas_call(
        body, grid=(n,), in_specs=[pl.BlockSpec((r, w), idx)], out_specs=pl.BlockSpec((r, w), idx),
        out_shape=jax.ShapeDtypeStruct((rows, w), F32), scratch_shapes=[pltpu.VMEM((8, w), F32)], name=name,
        compiler_params=_cparams(("arbitrary",)),
    )(a)


def _head_masks():
    lane = lax.broadcasted_iota(jnp.int32, (1, LANES), 1)
    return [lane < B_HEAD_DIM, lane >= B_HEAD_DIM]


def _causal(t):
    row = lax.broadcasted_iota(jnp.int32, (t, t), 0)
    col = lax.broadcasted_iota(jnp.int32, (t, t), 1)
    return row, col


def _flash_fwd(name, q, k, v, nck_rows):
    rows = q.shape[0]
    t = min(_ATT_T, rows)
    nb = rows // t

    def body(q_ref, k_ref, v_ref, nck_ref, o_ref, lse_ref):
        pair, i = pl.program_id(0), pl.program_id(1)
        q2 = q_ref[...]
        row, col = _causal(t)
        masks = _head_masks()
        qh = [jnp.where(hm, q2, jnp.zeros_like(q2)) for hm in masks]

        def step(j, carry, diag):
            ml, acc = carry
            start = pl.multiple_of(j * t, t)
            kb = k_ref[pl.ds(start, t), :]
            vb = v_ref[pl.ds(start, t), :]
            new_ml = []
            for hh, hm in enumerate(masks):
                m, l = ml[hh]
                s = lax.dot_general(qh[hh], kb, _DOT_DIMS["nt"], preferred_element_type=F32)
                s = s + nck_ref[2 * pair + hh, pl.ds(j, 1), :]
                if diag:
                    s = jnp.where(col <= row, s, NEG_INF)
                m_new = jnp.maximum(m, jnp.max(s, axis=1, keepdims=True))
                p = jnp.exp(s - m_new)
                alpha = jnp.exp(m - m_new)
                new_ml.append((m_new, alpha * l + jnp.sum(p, axis=1, keepdims=True)))
                pv = jnp.dot(p.astype(BF16), jnp.where(hm, vb, jnp.zeros_like(vb)), preferred_element_type=F32)
                acc = acc * jnp.where(hm, alpha, 1.0) + pv
            return tuple(new_ml), acc

        def init_ml():
            return (jnp.full((t, 1), NEG_INF, F32), jnp.zeros((t, 1), F32))

        init = ((init_ml(), init_ml()), jnp.zeros((t, LANES), F32))
        carry = lax.fori_loop(0, i, lambda j, c: step(j, c, False), init)
        ml, acc = step(i, carry, True)
        o_ref[...] = acc / jnp.where(masks[0], ml[0][1], ml[1][1])
        for hh in range(2):
            lse_ref[hh] = ml[hh][0] + jnp.log(ml[hh][1])

    return pl.pallas_call(
        body, grid=(B_HEADS // 2, nb),
        in_specs=[pl.BlockSpec((t, LANES), lambda p, i: (i, p)), pl.BlockSpec((rows, LANES), lambda p, i: (0, p)),
                  pl.BlockSpec((rows, LANES), lambda p, i: (0, p)),
                  pl.BlockSpec((B_HEADS, nb, t), lambda p, i: (0, 0, 0))],
        out_specs=[pl.BlockSpec((t, LANES), lambda p, i: (i, p)), pl.BlockSpec((2, t, 1), lambda p, i: (p, i, 0))],
        out_shape=[jax.ShapeDtypeStruct((rows, B_WIDTH), F32), jax.ShapeDtypeStruct((B_HEADS, rows, 1), F32)],
        name=name, compiler_params=_cparams(("parallel", "parallel")),
    )(q, k, v, nck_rows)


def _flash_bwd_dq(name, q, k, v, nck_rows, o, do, lse_col):
    rows = q.shape[0]
    t = min(_ATT_T, rows)
    nb = rows // t

    def body(q_ref, k_ref, v_ref, nck_ref, o_ref, do_ref, lse_ref, dq_ref, delta_ref):
        pair, i = pl.program_id(0), pl.program_id(1)
        q2 = q_ref[...]
        do2 = do_ref[...]
        od = o_ref[...] * do2
        do_b = do2.astype(BF16)
        row, col = _causal(t)
        masks = _head_masks()
        qh = [jnp.where(hm, q2, jnp.zeros_like(q2)) for hm in masks]
        doh = [jnp.where(hm, do_b, jnp.zeros_like(do_b)) for hm in masks]
        delta = [jnp.sum(jnp.where(hm, od, 0.0), axis=1, keepdims=True) for hm in masks]
        lse = [lse_ref[hh] for hh in range(2)]

        def step(j, carry, diag):
            acc, rowsum = carry
            start = pl.multiple_of(j * t, t)
            kb = k_ref[pl.ds(start, t), :]
            vb = v_ref[pl.ds(start, t), :]
            new_rowsum = []
            for hh, hm in enumerate(masks):
                s = lax.dot_general(qh[hh], kb, _DOT_DIMS["nt"], preferred_element_type=F32)
                s = s + nck_ref[2 * pair + hh, pl.ds(j, 1), :]
                p = jnp.exp(s - lse[hh])
                if diag:
                    p = jnp.where(col <= row, p, 0.0)
                dp = lax.dot_general(doh[hh], vb, _DOT_DIMS["nt"], preferred_element_type=F32)
                ds = p * (dp - delta[hh])
                new_rowsum.append(rowsum[hh] + jnp.sum(ds, axis=1, keepdims=True))
                acc = acc + jnp.dot(ds.astype(BF16), jnp.where(hm, kb, jnp.zeros_like(kb)),
                                    preferred_element_type=F32)
            return acc, tuple(new_rowsum)

        zcol = jnp.zeros((t, 1), F32)
        carry = lax.fori_loop(0, i, lambda j, c: step(j, c, False), (jnp.zeros((t, LANES), F32), (zcol, zcol)))
        acc, rowsum = step(i, carry, True)
        dq_ref[...] = acc
        for hh in range(2):
            delta_ref[hh] = delta[hh] + rowsum[hh]

    tile = pl.BlockSpec((t, LANES), lambda p, i: (i, p))
    full = pl.BlockSpec((rows, LANES), lambda p, i: (0, p))
    colspec = pl.BlockSpec((2, t, 1), lambda p, i: (p, i, 0))
    return pl.pallas_call(
        body, grid=(B_HEADS // 2, nb),
        in_specs=[tile, full, full, pl.BlockSpec((B_HEADS, nb, t), lambda p, i: (0, 0, 0)), tile, tile, colspec],
        out_specs=[tile, colspec],
        out_shape=[jax.ShapeDtypeStruct((rows, B_WIDTH), F32), jax.ShapeDtypeStruct((B_HEADS, rows, 1), F32)],
        name=name, compiler_params=_cparams(("parallel", "parallel")),
    )(q, k, v, nck_rows, o, do, lse_col)


def _flash_bwd_dkv(name, q, k, v, nck_col, do, lse_rows, delta_rows):
    rows = q.shape[0]
    t = min(_ATT_T, rows)
    nb = rows // t

    def body(k_ref, v_ref, q_ref, do_ref, nck_ref, lse_ref, delta_ref, dk_ref, dv_ref, dn_ref):
        pair, j = pl.program_id(0), pl.program_id(1)
        k2 = k_ref[...]
        v2 = v_ref[...]
        row, col = _causal(t)
        masks = _head_masks()
        kh = [jnp.where(hm, k2, jnp.zeros_like(k2)) for hm in masks]
        vh = [jnp.where(hm, v2, jnp.zeros_like(v2)) for hm in masks]
        nck = [nck_ref[hh] for hh in range(2)]

        def step(i, carry, diag):
            dk, dv, dn = carry
            start = pl.multiple_of(i * t, t)
            qb = q_ref[pl.ds(start, t), :]
            dob = do_ref[pl.ds(start, t), :].astype(BF16)
            dn_new = []
            for hh, hm in enumerate(masks):
                head = 2 * pair + hh
                st = lax.dot_general(kh[hh], qb, _DOT_DIMS["nt"], preferred_element_type=F32) + nck[hh]
                pt = jnp.exp(st - lse_ref[head, pl.ds(i, 1), :])
                if diag:
                    pt = jnp.where(row <= col, pt, 0.0)
                dpt = lax.dot_general(vh[hh], dob, _DOT_DIMS["nt"], preferred_element_type=F32)
                dst = pt * (dpt - delta_ref[head, pl.ds(i, 1), :])
                dv = dv + jnp.dot(pt.astype(BF16), jnp.where(hm, dob, jnp.zeros_like(dob)),
                                  preferred_element_type=F32)
                dk = dk + jnp.dot(dst.astype(BF16), jnp.where(hm, qb, jnp.zeros_like(qb)),
                                  preferred_element_type=F32)
                dn_new.append(dn[hh] + jnp.sum(dst, axis=1, keepdims=True))
            return dk, dv, tuple(dn_new)

        zero = jnp.zeros((t, LANES), F32)
        zcol = jnp.zeros((t, 1), F32)
        carry = step(j, (zero, zero, (zcol, zcol)), True)
        dk, dv, dn = lax.fori_loop(j + 1, nb, lambda i, c: step(i, c, False), carry)
        dk_ref[...] = dk
        dv_ref[...] = dv
        for hh in range(2):
            dn_ref[hh] = dn[hh]

    tile = pl.BlockSpec((t, LANES), lambda p, j: (j, p))
    full = pl.BlockSpec((rows, LANES), lambda p, j: (0, p))
    colspec = pl.BlockSpec((2, t, 1), lambda p, j: (p, j, 0))
    rowspec = pl.BlockSpec((B_HEADS, nb, t), lambda p, j: (0, 0, 0))
    big = jax.ShapeDtypeStruct((rows, B_WIDTH), F32)
    return pl.pallas_call(
        body, grid=(B_HEADS // 2, nb),
        in_specs=[tile, tile, full, full, colspec, rowspec, rowspec], out_specs=[tile, tile, colspec],
        out_shape=[big, big, jax.ShapeDtypeStruct((B_HEADS, rows, 1), F32)],
        name=name, compiler_params=_cparams(("parallel", "parallel")),
    )(k, v, q, do, nck_col, lse_rows, delta_rows)


_S5_ROWS = 256


def _s5_discretize(a_re, a_im, log_dt, b_re, b_im):
    dt = jnp.exp(log_dt)[:, None]
    mag = jnp.exp(a_re * dt)
    ab_re, ab_im = mag * jnp.cos(a_im * dt), mag * jnp.sin(a_im * dt)
    den = a_re * a_re + a_im * a_im
    nr, ni = ab_re - 1.0, ab_im
    cr = (nr * a_re + ni * a_im) / den
    ci = (ni * a_re - nr * a_im) / den
    bb_re = cr[..., None] * b_re - ci[..., None] * b_im
    bb_im = cr[..., None] * b_im + ci[..., None] * b_re
    return ab_re, ab_im, bb_re, bb_im


def _s5_block_diag(m):
    g, r, c = m.shape
    mb = m.reshape(S5_BLOCKS, 8, r, c)
    eye = jnp.eye(8, dtype=m.dtype)
    return jnp.einsum("bgrc,gh->bgrhc", mb, eye).reshape(S5_BLOCKS, 8 * r, 8 * c)


def _s5_block_diag_extract(m, r, c):
    mb = m.reshape(S5_BLOCKS, 8, r, 8, c)
    return jnp.einsum("bgrhc,gh->bgrc", mb, jnp.eye(8, dtype=m.dtype)).reshape(S5_GROUPS, r, c)


def _s5_tables(ab_re, ab_im, r):
    ar = jnp.broadcast_to(ab_re.reshape(1, -1), (r, S5_GROUPS * S5_STATE))
    ai = jnp.broadcast_to(ab_im.reshape(1, -1), (r, S5_GROUPS * S5_STATE))

    def mul(x, y):
        return x[0] * y[0] - x[1] * y[1], x[0] * y[1] + x[1] * y[0]

    return lax.associative_scan(mul, (ar, ai), axis=0)


def _scan_step(xr, xi, ar, ai, s, row, up):
    r = xr.shape[0]
    if up:
        ai = -ai
    if s < 8:
        if up:
            sr = jnp.where(row < r - s, pltpu.roll(xr, r - s, 0), 0.0)
            si = jnp.where(row < r - s, pltpu.roll(xi, r - s, 0), 0.0)
        else:
            sr = jnp.where(row >= s, pltpu.roll(xr, s, 0), 0.0)
            si = jnp.where(row >= s, pltpu.roll(xi, s, 0), 0.0)
        return xr + (ar * sr - ai * si), xi + (ar * si + ai * sr)
    if up:
        (dr, di), (sr, si) = (xr[:r - s], xi[:r - s]), (xr[s:], xi[s:])
        nr, ni = dr + (ar * sr - ai * si), di + (ar * si + ai * sr)
        return jnp.concatenate([nr, xr[r - s:]], axis=0), jnp.concatenate([ni, xi[r - s:]], axis=0)
    (dr, di), (sr, si) = (xr[s:], xi[s:]), (xr[:r - s], xi[:r - s])
    nr, ni = dr + (ar * sr - ai * si), di + (ar * si + ai * sr)
    return jnp.concatenate([xr[:s], nr], axis=0), jnp.concatenate([xi[:s], ni], axis=0)


def _s5_scan_tile(u_ref, bcat_ref, pr_ref, pi_ref, cin_r, cin_i, row):
    r = u_ref.shape[0]
    bu = jnp.dot(u_ref[...], bcat_ref[...], preferred_element_type=F32)
    xr, xi = bu[:, :S5_LANES], bu[:, S5_LANES:]
    s = 1
    while s < r:
        xr, xi = _scan_step(xr, xi, pr_ref[s - 1:s, :], pi_ref[s - 1:s, :], s, row, False)
        s *= 2
    pr, pi = pr_ref[...], pi_ref[...]
    xr, xi = xr + (pr * cin_r - pi * cin_i), xi + (pr * cin_i + pi * cin_r)
    return xr, xi


def _s5_fwd(name, u, bcat, ccat, pw_re, pw_im):
    rows = u.shape[0]
    r = pw_re.shape[0]
    nt = rows // r

    def body(u_ref, bcat_ref, ccat_ref, pr_ref, pi_ref, y_ref, xin_ref, carry):
        @pl.when(pl.program_id(1) == 0)
        def _():
            carry[...] = jnp.zeros(carry.shape, carry.dtype)

        row = lax.broadcasted_iota(jnp.int32, (r, 1), 0)
        xin_ref[...] = carry[...]
        xr, xi = _s5_scan_tile(u_ref, bcat_ref, pr_ref, pi_ref, carry[0:1, :S5_LANES], carry[0:1, S5_LANES:], row)
        xcat = jnp.concatenate([xr, xi], axis=1)
        carry[...] = jnp.broadcast_to(xcat[r - 1:r, :], carry.shape)
        y_ref[...] = jnp.dot(xcat.astype(BF16), ccat_ref[...], preferred_element_type=F32)

    return pl.pallas_call(
        body, grid=(S5_BLOCKS, nt),
        in_specs=[pl.BlockSpec((r, LANES), lambda b, i: (i, b)),
                  pl.BlockSpec((None, LANES, 2 * S5_LANES), lambda b, i: (b, 0, 0)),
                  pl.BlockSpec((None, 2 * S5_LANES, LANES), lambda b, i: (b, 0, 0)),
                  pl.BlockSpec((r, S5_LANES), lambda b, i: (0, b)), pl.BlockSpec((r, S5_LANES), lambda b, i: (0, b))],
        out_specs=[pl.BlockSpec((r, LANES), lambda b, i: (i, b)),
                   pl.BlockSpec((None, 8, 2 * S5_LANES), lambda b, i: (b, i, 0))],
        out_shape=[jax.ShapeDtypeStruct((rows, D_MODEL), F32),
                   jax.ShapeDtypeStruct((S5_BLOCKS, 8 * nt, 2 * S5_LANES), F32)],
        scratch_shapes=[pltpu.VMEM((8, 2 * S5_LANES), F32)], name=name,
        compiler_params=_cparams(("parallel", "arbitrary")),
    )(u, bcat, ccat, pw_re, pw_im)


def _s5_bwd(name, u, dy, xin, bcat, ccat, pw_re, pw_im, pwf_re, pwf_im):
    rows = u.shape[0]
    r = pw_re.shape[0]
    nt = rows // r

    def body(u_ref, dy_ref, xin_ref, bcat_ref, ccat_ref, pr_ref, pi_ref, fr_ref, fi_ref,
             du_ref, db_ref, dc_ref, dar_ref, dai_ref, carry):
        @pl.when(pl.program_id(1) == 0)
        def _():
            carry[...] = jnp.zeros(carry.shape, carry.dtype)
            for ref in (db_ref, dc_ref, dar_ref, dai_ref):
                ref[...] = jnp.zeros(ref.shape, ref.dtype)

        row = lax.broadcasted_iota(jnp.int32, (r, 1), 0)
        cin_r, cin_i = xin_ref[0:1, :S5_LANES], xin_ref[0:1, S5_LANES:]
        xr, xi = _s5_scan_tile(u_ref, bcat_ref, pr_ref, pi_ref, cin_r, cin_i, row)
        dy_b = dy_ref[...].astype(BF16)
        xcat = jnp.concatenate([xr, xi], axis=1).astype(BF16)
        dc_ref[...] += lax.dot_general(xcat, dy_b, _DOT_DIMS["tn"], preferred_element_type=F32)
        g = lax.dot_general(dy_b, ccat_ref[...], _DOT_DIMS["nt"], preferred_element_type=F32)
        lr, li = g[:, :S5_LANES], g[:, S5_LANES:]
        s = 1
        while s < r:
            lr, li = _scan_step(lr, li, pr_ref[s - 1:s, :], pi_ref[s - 1:s, :], s, row, True)
            s *= 2
        nr, ni = carry[0:1, :S5_LANES], carry[0:1, S5_LANES:]
        fr, fi = fr_ref[...], fi_ref[...]
        lr, li = lr + (fr * nr + fi * ni), li + (fr * ni - fi * nr)
        carry[...] = jnp.broadcast_to(jnp.concatenate([lr[0:1, :], li[0:1, :]], axis=1), carry.shape)
        lcat = jnp.concatenate([lr, li], axis=1).astype(BF16)
        du_ref[...] = lax.dot_general(lcat, bcat_ref[...], _DOT_DIMS["nt"], preferred_element_type=F32)
        db_ref[...] += lax.dot_general(u_ref[...], lcat, _DOT_DIMS["tn"], preferred_element_type=F32)
        pxr = jnp.where(row == 0, cin_r, pltpu.roll(xr, 1, 0))
        pxi = jnp.where(row == 0, cin_i, pltpu.roll(xi, 1, 0))
        dar_ref[...] += jnp.sum((lr * pxr + li * pxi).reshape(r // 8, 8, S5_LANES), axis=0)
        dai_ref[...] += jnp.sum((li * pxr - lr * pxi).reshape(r // 8, 8, S5_LANES), axis=0)

    rev = lambda b, i: (nt - 1 - i, b)
    tab = pl.BlockSpec((r, S5_LANES), lambda b, i: (0, b))
    return pl.pallas_call(
        body, grid=(S5_BLOCKS, nt),
        in_specs=[pl.BlockSpec((r, LANES), rev), pl.BlockSpec((r, LANES), rev),
                  pl.BlockSpec((None, 8, 2 * S5_LANES), lambda b, i: (b, nt - 1 - i, 0)),
                  pl.BlockSpec((None, LANES, 2 * S5_LANES), lambda b, i: (b, 0, 0)),
                  pl.BlockSpec((None, 2 * S5_LANES, LANES), lambda b, i: (b, 0, 0)), tab, tab, tab, tab],
        out_specs=[pl.BlockSpec((r, LANES), rev),
                   pl.BlockSpec((None, LANES, 2 * S5_LANES), lambda b, i: (b, 0, 0)),
                   pl.BlockSpec((None, 2 * S5_LANES, LANES), lambda b, i: (b, 0, 0)),
                   pl.BlockSpec((None, 8, S5_LANES), lambda b, i: (b, 0, 0)),
                   pl.BlockSpec((None, 8, S5_LANES), lambda b, i: (b, 0, 0))],
        out_shape=[jax.ShapeDtypeStruct((rows, D_MODEL), F32),
                   jax.ShapeDtypeStruct((S5_BLOCKS, LANES, 2 * S5_LANES), F32),
                   jax.ShapeDtypeStruct((S5_BLOCKS, 2 * S5_LANES, LANES), F32),
                   jax.ShapeDtypeStruct((S5_BLOCKS, 8, S5_LANES), F32),
                   jax.ShapeDtypeStruct((S5_BLOCKS, 8, S5_LANES), F32)],
        scratch_shapes=[pltpu.VMEM((8, 2 * S5_LANES), F32)], name=name,
        compiler_params=_cparams(("parallel", "arbitrary")),
    )(u, dy, xin, bcat, ccat, pw_re, pw_im, pwf_re, pwf_im)


def _ones_gain():
    return jnp.ones((1, D_MODEL), F32)


def _channel_fwd(i, x1, p_i, w, rp):
    hn, = _rmsnorm_fwd(f"ffn_norm_{i}", x1, rp["norm_ffn"][i][None], [BF16])
    hg = _mm(f"ffn_up_g_{i}", hn, w["up_g"], tn=1408)
    hu = _mm(f"ffn_up_u_{i}", hn, w["up_u"], tn=1408)
    a = _convffn_fwd(f"ffn_conv_{i}", hg, hu, w["cw_g"], w["cw_u"], w["cb_g"], w["cb_u"])
    x2 = _mm(f"ffn_down_{i}", a, w["down"], res=x1, tk=1408)
    r, = _rmsnorm_fwd(f"ple_norm_{i}", x2, _ones_gain(), [BF16])
    zg = _mm(f"ple_gate_{i}", r, w["ple_gate"])
    pp = _mm(f"ple_proj_{i}", p_i, w["ple_proj"])
    x3, = _rows(f"ple_out_{i}", lambda xv, zv, pv: (xv + _sigmoid(zv) * pv,), [x2, zg, pp], [], [(D_MODEL, F32)])
    return x3, dict(x1=x1, hn=hn, hg=hg, hu=hu, a=a, x2=x2, r=r, zg=zg, pp=pp, p_i=p_i)


def _channel_bwd(i, dx3, sv, w, rp):
    def ple_bwd(dv, zv, pv):
        gate = _sigmoid(zv)
        return dv * gate, (dv * pv) * (gate * (1.0 - gate))

    dpp, dzg = _rows(f"ple_out_bwd_{i}", ple_bwd, [dx3, sv["zg"], sv["pp"]], [], [(D_MODEL, BF16), (D_MODEL, BF16)])
    g = {}
    g["ple_proj"] = _mm(f"ple_proj_dw_{i}", sv["p_i"], dpp, "tn")
    g["ple_gate"] = _mm(f"ple_gate_dw_{i}", sv["r"], dzg, "tn")
    dr = _mm(f"ple_gate_dx_{i}", dzg, w["ple_gate"], "nt")
    dx2, _ = _rmsnorm_bwd(f"ple_norm_bwd_{i}", sv["x2"], dr, dx3, _ones_gain())
    da = _mm(f"ffn_down_dx_{i}", dx2, w["down"], "nt", tn=1408)
    g["down"] = _mm(f"ffn_down_dw_{i}", sv["a"], dx2, "tn", tm=1408)
    dcg, dcu, g["cw_g"], g["cw_u"], dbg, dbu = _convffn_bwd_gate(
        f"ffn_conv_bwd_{i}", da, sv["hg"], sv["hu"], w["cw_g"], w["cw_u"], w["cb_g"], w["cb_u"])
    g["conv_b"] = jnp.concatenate([dbg, dbu], axis=1)[0]
    dhg = _conv_transpose(f"ffn_conv_t_g_{i}", dcg, w["cw_g"])
    dhu = _conv_transpose(f"ffn_conv_t_u_{i}", dcu, w["cw_u"])
    g["up_g"] = _mm(f"ffn_up_g_dw_{i}", sv["hn"], dhg, "tn", tn=1408)
    g["up_u"] = _mm(f"ffn_up_u_dw_{i}", sv["hn"], dhu, "tn", tn=1408)
    dhn = _mm(f"ffn_up_g_dx_{i}", dhg, w["up_g"], "nt", tk=1408)
    dhn = _mm(f"ffn_up_u_dx_{i}", dhu, w["up_u"], "nt", res=dhn, tk=1408)
    dx1, dgf = _rmsnorm_bwd(f"ffn_norm_bwd_{i}", sv["x1"], dhn, dx2, rp["norm_ffn"][i][None])
    g["norm_ffn"] = dgf[0]
    return dx1, g


def _even_consts(e, rp):
    tri = jnp.tril(jnp.ones((A_CHUNK, A_CHUNK), dtype=bool))
    w_tril = jnp.where(tri[None], rp["ev_w_spatial"][e], 0.0).astype(BF16)
    b_exp = jnp.broadcast_to(rp["ev_b_spatial"][e][:, :, None], (A_GROUPS, A_CHUNK, LANES))
    seg = np.arange(B_WIDTH) // B_HEAD_DIM
    bd = jnp.asarray((seg[:, None] == seg[None, :]).astype(np.float32) / B_HEAD_DIM)
    return dict(
        tri=tri, w_tril=w_tril, w_tril_t=jnp.swapaxes(w_tril, 1, 2), b_exp=b_exp, bd=bd,
        v_gain=rp["ev_v_norm"][e][None], qg=jnp.tile(rp["ev_q_norm"][e], B_HEADS)[None],
        kg=jnp.tile(rp["ev_k_norm"][e], B_HEADS)[None],
        bf=jnp.pad(rp["ev_b_fgate"][e], (0, LANES - B_HEADS))[None])


def _even_fwd(i, x, w, rp):
    e = i // 2
    c = _even_consts(e, rp)
    rows = x.shape[0]
    t = min(_ATT_T, rows)
    h, = _rmsnorm_fwd(f"mix_norm_{i}", x, rp["norm_mix"][i][None], [BF16])
    zuv = _mm(f"in_uv_{i}", h, w["in_uv"])
    zqkv = _mm(f"in_qkv_{i}", h, w["in_qkv"], tn=768)
    zf = _mm(f"in_f_{i}", h, w["in_f"])
    ya = _gmlp_fwd(f"gmlp_{i}", zuv, c["v_gain"], c["w_tril"], c["b_exp"])
    q, k, v, ls = _qkv_prep_fwd(f"qkv_prep_{i}", zqkv, zf, c["qg"], c["kg"], c["bf"], c["bd"])
    csum = _cumsum_rows(f"forget_cumsum_{i}", ls)
    nck = -csum[:, :B_HEADS].T
    nck_rows = nck.reshape(B_HEADS, rows // t, t)
    nck_col = nck.reshape(B_HEADS, rows, 1)
    o, lse = _flash_fwd(f"attn_{i}", q, k, v, nck_rows)
    x1 = _mm(f"out_a_{i}", ya, w["out_a"], res=x)
    x1 = _mm(f"out_b_{i}", o, w["out_b"], res=x1)
    return x1, dict(x=x, h=h, zuv=zuv, zqkv=zqkv, zf=zf, ya=ya, q=q, k=k, v=v, nck_rows=nck_rows, nck_col=nck_col,
                    o=o, lse=lse)


def _even_bwd(i, dx1, sv, w, rp):
    e = i // 2
    c = _even_consts(e, rp)
    rows = dx1.shape[0]
    t = min(_ATT_T, rows)
    nb = rows // t
    g = {}
    dya = _mm(f"out_a_dx_{i}", dx1, w["out_a"], "nt")
    do = _mm(f"out_b_dx_{i}", dx1, w["out_b"], "nt")
    g["out_a"] = _mm(f"out_a_dw_{i}", sv["ya"], dx1, "tn")
    g["out_b"] = _mm(f"out_b_dw_{i}", sv["o"], dx1, "tn")
    dq, delta = _flash_bwd_dq(f"attn_dq_{i}", sv["q"], sv["k"], sv["v"], sv["nck_rows"], sv["o"], do, sv["lse"])
    dk, dv, dn = _flash_bwd_dkv(f"attn_dkv_{i}", sv["q"], sv["k"], sv["v"], sv["nck_col"], do,
                                sv["lse"].reshape(B_HEADS, nb, t), delta.reshape(B_HEADS, nb, t))
    dcs = jnp.pad(-dn.reshape(B_HEADS, rows).T, ((0, 0), (0, LANES - B_HEADS)))
    dls = _cumsum_rows(f"forget_cumsum_bwd_{i}", dcs, reverse=True)
    dzqkv, dzf, dqg, dkg, dbf = _qkv_prep_bwd(f"qkv_prep_bwd_{i}", sv["zqkv"], sv["zf"], dq, dk, dv, dls,
                                              c["qg"], c["kg"], c["bf"], c["bd"])
    dzuv, dws, dbs, dvg = _gmlp_bwd(f"gmlp_bwd_{i}", sv["zuv"], dya, c["v_gain"], c["w_tril"], c["w_tril_t"],
                                    c["b_exp"])
    g["in_uv"] = _mm(f"in_uv_dw_{i}", sv["h"], dzuv, "tn")
    g["in_qkv"] = _mm(f"in_qkv_dw_{i}", sv["h"], dzqkv, "tn", tn=768)
    g["in_f"] = _mm(f"in_f_dw_{i}", sv["h"], dzf, "tn")
    dh = _mm(f"in_uv_dx_{i}", dzuv, w["in_uv"], "nt")
    dh = _mm(f"in_qkv_dx_{i}", dzqkv, w["in_qkv"], "nt", res=dh, tk=768)
    dh = _mm(f"in_f_dx_{i}", dzf, w["in_f"], "nt", res=dh)
    dx, dgm = _rmsnorm_bwd(f"mix_norm_bwd_{i}", sv["x"], dh, dx1, rp["norm_mix"][i][None])
    g["norm_mix"] = dgm[0]
    g["ev_b_fgate"] = dbf[0, :B_HEADS]
    g["ev_q_norm"] = dqg.reshape(B_HEADS, B_HEAD_DIM).sum(axis=0)
    g["ev_k_norm"] = dkg.reshape(B_HEADS, B_HEAD_DIM).sum(axis=0)
    g["ev_v_norm"] = dvg[0]
    g["ev_w_spatial"] = jnp.where(c["tri"][None], dws, 0.0)
    g["ev_b_spatial"] = dbs.sum(axis=-1)
    return dx, g


def _s5_consts(o, rp, r):
    prm = (rp["od_a_re"][o], rp["od_a_im"][o], rp["od_log_dt"][o], rp["od_b_re"][o], rp["od_b_im"][o])
    (ab_re, ab_im, bb_re, bb_im), vjp = jax.vjp(_s5_discretize, *prm)
    bcat = jnp.concatenate([_s5_block_diag(bb_re.transpose(0, 2, 1)), _s5_block_diag(bb_im.transpose(0, 2, 1))], axis=2)
    c_re, c_im = rp["od_c_re"][o], rp["od_c_im"][o]
    ccat = jnp.concatenate([_s5_block_diag(c_re.transpose(0, 2, 1)), _s5_block_diag(-c_im.transpose(0, 2, 1))], axis=1)
    pw_re, pw_im = _s5_tables(ab_re, ab_im, r)
    return dict(vjp=vjp, bcat=bcat.astype(BF16), ccat=ccat.astype(BF16), pw_re=pw_re, pw_im=pw_im,
                pwf_re=jnp.flip(pw_re, axis=0), pwf_im=jnp.flip(pw_im, axis=0))


def _odd_fwd(i, x, w, rp):
    o = i // 2
    rows = x.shape[0]
    c = _s5_consts(o, rp, min(_S5_ROWS, rows))
    hb, hf = _rmsnorm_fwd(f"mix_norm_{i}", x, rp["norm_mix"][i][None], [BF16, F32])
    ys, xin = _s5_fwd(f"s5_{i}", hb, c["bcat"], c["ccat"], c["pw_re"], c["pw_im"])

    def skip_gelu(yv, hv, dv):
        y = yv + dv * hv
        return y, _gelu(y)

    y, ge = _rows(f"s5_skip_gelu_{i}", skip_gelu, [ys, hf], [w["od_d"]], [(D_MODEL, F32), (D_MODEL, BF16)])
    gl = _mm(f"glu_{i}", ge, w["glu"])

    def glu_out(xv, gv):
        return (xv + gv[:, :D_MODEL] * _sigmoid(gv[:, D_MODEL:]),)

    x1, = _rows(f"glu_out_{i}", glu_out, [x, gl], [], [(D_MODEL, F32)])
    return x1, dict(x=x, hb=hb, hf=hf, xin=xin, y=y, ge=ge, gl=gl, c=c)


def _odd_bwd(i, dx1, sv, w, rp):
    o = i // 2
    c = sv["c"]
    g = {}

    def glu_bwd(dv, gv):
        ga, gb = gv[:, :D_MODEL], gv[:, D_MODEL:]
        sg = _sigmoid(gb)
        return (jnp.concatenate([dv * sg, (dv * ga) * (sg * (1.0 - sg))], axis=1),)

    dgl, = _rows(f"glu_out_bwd_{i}", glu_bwd, [dx1, sv["gl"]], [], [(2 * D_MODEL, BF16)])
    g["glu"] = _mm(f"glu_dw_{i}", sv["ge"], dgl, "tn")
    dge = _mm(f"glu_dx_{i}", dgl, w["glu"], "nt")

    def gelu_bwd(dv, yv, hv):
        dy = dv * _gelu_grad(yv)
        return dy, jnp.sum(dy * hv, axis=0, keepdims=True)

    dy, dd = _rows(f"s5_skip_gelu_bwd_{i}", gelu_bwd, [dge, sv["y"], sv["hf"]], [], [(D_MODEL, F32)],
                   accs=[(1, D_MODEL)])
    g["od_d"] = dd[0]
    du, db, dc, dar, dai = _s5_bwd(f"s5_bwd_{i}", sv["hb"], dy, sv["xin"], c["bcat"], c["ccat"], c["pw_re"],
                                   c["pw_im"], c["pwf_re"], c["pwf_im"])
    dab_re = dar.sum(axis=1).reshape(S5_GROUPS, S5_STATE)
    dab_im = dai.sum(axis=1).reshape(S5_GROUPS, S5_STATE)
    dbb_re = _s5_block_diag_extract(db[:, :, :S5_LANES], S5_GROUP_CH, S5_STATE).transpose(0, 2, 1)
    dbb_im = _s5_block_diag_extract(db[:, :, S5_LANES:], S5_GROUP_CH, S5_STATE).transpose(0, 2, 1)
    g["od_a_re"], g["od_a_im"], g["od_log_dt"], g["od_b_re"], g["od_b_im"] = c["vjp"]((dab_re, dab_im, dbb_re, dbb_im))
    g["od_c_re"] = _s5_block_diag_extract(dc[:, :S5_LANES, :], S5_STATE, S5_GROUP_CH).transpose(0, 2, 1)
    g["od_c_im"] = -_s5_block_diag_extract(dc[:, S5_LANES:, :], S5_STATE, S5_GROUP_CH).transpose(0, 2, 1)

    def norm_bwd(xv, duv, dyv, drv, gv, dv):
        dh = duv + dv * dyv
        r = _rstd(xv)
        xh = xv * r
        dhg = dh * gv
        dx = drv + r * (dhg - xh * jnp.mean(dhg * xh, axis=-1, keepdims=True))
        return dx, jnp.sum(dh * xh, axis=0, keepdims=True)

    dx, dgm = _rows(f"mix_norm_bwd_{i}", norm_bwd, [sv["x"], du, dy, dx1], [rp["norm_mix"][i][None], w["od_d"]],
                    [(D_MODEL, F32)], accs=[(1, D_MODEL)])
    g["norm_mix"] = dgm[0]
    return dx, g


def _local_step(x, p, target, lw, rp):
    saved = []
    for i in range(DEPTH):
        x, s_mix = (_even_fwd if i % 2 == 0 else _odd_fwd)(i, x, lw[i], rp)
        x, s_ch = _channel_fwd(i, x, p[i], lw[i], rp)
        saved.append((s_mix, s_ch))

    def loss_fn(yv, tv):
        diff = yv - tv
        return diff * (1.0 / D_MODEL), jnp.sum(diff * diff, axis=0, keepdims=True)

    dx, sq = _rows("loss", loss_fn, [x, target], [], [(D_MODEL, F32)], accs=[(1, D_MODEL)])
    loss = 0.5 * jnp.sum(sq) / D_MODEL
    grads = [None] * DEPTH
    for i in reversed(range(DEPTH)):
        s_mix, s_ch = saved[i]
        dx, g_ch = _channel_bwd(i, dx, s_ch, lw[i], rp)
        dx, g_mix = (_even_bwd if i % 2 == 0 else _odd_bwd)(i, dx, s_mix, lw[i], rp)
        grads[i] = {**g_ch, **g_mix}
    return loss, dx, grads


WEIGHT_ORDER = ["norm_mix", "norm_ffn", "ev_w_in", "ev_b_fgate", "ev_q_norm", "ev_k_norm", "ev_v_norm", "ev_w_spatial",
                "ev_b_spatial", "ev_w_out", "od_a_re", "od_a_im", "od_log_dt", "od_b_re", "od_b_im", "od_c_re",
                "od_c_im", "od_d", "od_w_glu", "ffn_w_up", "ffn_conv_w", "ffn_conv_b", "ffn_w_down", "ple_w_proj",
                "ple_w_gate"]
SHARD_AXIS = {"ev_w_in": 2, "ev_w_out": 1, "od_d": 1, "od_w_glu": 2, "ffn_w_up": 2, "ffn_conv_w": 2, "ffn_w_down": 1,
              "ple_w_proj": 2, "ple_w_gate": 1}
BIG_WEIGHTS = [n for n in WEIGHT_ORDER if n in SHARD_AXIS]
SMALL_WEIGHTS = [n for n in WEIGHT_ORDER if n not in SHARD_AXIS]
KEPT_F32 = ("od_d", "ffn_conv_w")
IN_UV, IN_QKV_END, IN_COLS = 2 * A_WIDTH, 2 * A_WIDTH + 3 * B_WIDTH, 2 * A_WIDTH + 3 * B_WIDTH + B_HEADS


def _layer_weights(i, full, rp):
    w = {}
    up, cw, cb = full["ffn_w_up"][i], full["ffn_conv_w"][i], rp["ffn_conv_b"][i][None]
    w["up_g"], w["up_u"] = up[:, :D_FF], up[:, D_FF:]
    w["cw_g"], w["cw_u"] = cw[:, :D_FF], cw[:, D_FF:]
    w["cb_g"], w["cb_u"] = cb[:, :D_FF], cb[:, D_FF:]
    w["down"], w["ple_proj"], w["ple_gate"] = full["ffn_w_down"][i], full["ple_w_proj"][i], full["ple_w_gate"][i]
    if i % 2 == 0:
        win, wout = full["ev_w_in"][i // 2], full["ev_w_out"][i // 2]
        w["in_uv"], w["in_qkv"] = win[:, :IN_UV], win[:, IN_UV:IN_QKV_END]
        w["in_f"] = jnp.pad(win[:, IN_QKV_END:], ((0, 0), (0, LANES - B_HEADS)))
        w["out_a"], w["out_b"] = wout[:A_WIDTH], wout[A_WIDTH:]
    else:
        w["od_d"], w["glu"] = full["od_d"][i // 2][None], full["od_w_glu"][i // 2]
    return w


def _full_grads(grads):
    ev, od = [grads[i] for i in range(0, DEPTH, 2)], [grads[i] for i in range(1, DEPTH, 2)]
    out = {
        "norm_mix": jnp.stack([g["norm_mix"] for g in grads]), "norm_ffn": jnp.stack([g["norm_ffn"] for g in grads]),
        "ev_w_in": jnp.stack([jnp.concatenate([g["in_uv"], g["in_qkv"], g["in_f"][:, :B_HEADS]], axis=1) for g in ev]),
        "ev_w_out": jnp.stack([jnp.concatenate([g["out_a"], g["out_b"]], axis=0) for g in ev]),
        "od_w_glu": jnp.stack([g["glu"] for g in od]),
        "ffn_w_up": jnp.stack([jnp.concatenate([g["up_g"], g["up_u"]], axis=1) for g in grads]),
        "ffn_conv_w": jnp.stack([jnp.concatenate([g["cw_g"], g["cw_u"]], axis=1) for g in grads]),
        "ffn_conv_b": jnp.stack([g["conv_b"] for g in grads]),
        "ffn_w_down": jnp.stack([g["down"] for g in grads]),
        "ple_w_proj": jnp.stack([g["ple_proj"] for g in grads]),
        "ple_w_gate": jnp.stack([g["ple_gate"] for g in grads]),
    }
    for n in ("ev_b_fgate", "ev_q_norm", "ev_k_norm", "ev_v_norm", "ev_w_spatial", "ev_b_spatial"):
        out[n] = jnp.stack([g[n] for g in ev])
    for n in ("od_a_re", "od_a_im", "od_log_dt", "od_b_re", "od_b_im", "od_c_re", "od_c_im", "od_d"):
        out[n] = jnp.stack([g[n] for g in od])
    return out


def _pack(arrs, row_multiple):
    flat = jnp.concatenate([a.reshape(-1) for a in arrs])
    rows = -(-flat.shape[0] // (PACK_W * row_multiple)) * row_multiple
    return jnp.pad(flat, (0, rows * PACK_W - flat.shape[0])).reshape(rows, PACK_W)


def _unpack(buf, shapes):
    flat = buf.reshape(-1)
    out, at = [], 0
    for s in shapes:
        n = int(np.prod(s))
        out.append(flat[at:at + n].reshape(s))
        at += n
    return out


def _shard(name, a, k):
    ax = SHARD_AXIS[name]
    n = a.shape[ax] // N_CHIPS
    return lax.slice_in_dim(a, k * n, (k + 1) * n, axis=ax)


_ANY = pl.BlockSpec(memory_space=pl.ANY)


def _mesh_pos():
    return lax.axis_index("x"), lax.axis_index("y"), lax.axis_index("c")


def _other_chips(x, y):
    return [(1 - x, y), (x, 1 - y), (1 - x, 1 - y)]


def _gather_shards(name, shards):
    n = len(shards)

    def body(*refs):
        ins, outs = refs[:n], refs[n:2 * n]
        send_sems, recv_sems, local_sems = refs[2 * n:]
        x, y, c = _mesh_pos()
        sibling = (x, y, 1 - c)
        chips = _other_chips(x, y)

        def part(a, k, hc):
            half = shards[a].shape[0] // 2
            return outs[a].at[k, pl.ds(hc * half, half), :]

        def copy(sem, src, dst, to):
            return pltpu.make_async_remote_copy(src_ref=src, dst_ref=dst, send_sem=send_sems.at[sem],
                                                recv_sem=recv_sems.at[sem], device_id=to, device_id_type=MESH)

        local, sent, passed = [], [], []
        for a in range(n):
            half = shards[a].shape[0] // 2
            local.append(pltpu.make_async_copy(ins[a], outs[a].at[2 * x + y], local_sems.at[a]))
            local[-1].start()
            for j, (cx, cy) in enumerate(chips):
                sent.append(copy(6 * a + j, ins[a].at[pl.ds(c * half, half), :], part(a, 2 * x + y, c), (cx, cy, c)))
                sent[-1].start()
        for a in range(n):
            for j, (cx, cy) in enumerate(chips):
                blk = part(a, 2 * cx + cy, c)
                copy(6 * a + j, blk, blk, (cx, cy, c)).wait_recv()
                passed.append(copy(6 * a + 3 + j, blk, blk, sibling))
                passed[-1].start()
        for a in range(n):
            for j, (cx, cy) in enumerate(chips):
                blk = part(a, 2 * cx + cy, 1 - c)
                copy(6 * a + 3 + j, blk, blk, sibling).wait_recv()
        for cp in sent + passed:
            cp.wait_send()
        for cp in local:
            cp.wait()

    return pl.pallas_call(
        body, out_shape=[jax.ShapeDtypeStruct((N_CHIPS,) + s.shape, s.dtype) for s in shards],
        in_specs=[_ANY] * n, out_specs=[_ANY] * n,
        scratch_shapes=[pltpu.SemaphoreType.DMA((6 * n,)), pltpu.SemaphoreType.DMA((6 * n,)),
                        pltpu.SemaphoreType.DMA((n,))],
        name=name,
    )(*shards)


def _swap_halves(name, arrs):
    n = len(arrs)

    def body(*refs):
        ins, outs = refs[:n], refs[n:2 * n]
        send_sems, recv_sems = refs[2 * n:]
        x, y, c = _mesh_pos()
        cps = []
        for a in range(n):
            half = arrs[a].shape[1] // 2
            cps.append(pltpu.make_async_remote_copy(
                src_ref=ins[a].at[:, pl.ds((1 - c) * half, half), :], dst_ref=outs[a], send_sem=send_sems.at[a],
                recv_sem=recv_sems.at[a], device_id=(x, y, 1 - c), device_id_type=MESH))
            cps[-1].start()
        for cp in cps:
            cp.wait()

    return pl.pallas_call(
        body, out_shape=[jax.ShapeDtypeStruct((a.shape[0], a.shape[1] // 2, a.shape[2]), a.dtype) for a in arrs],
        in_specs=[_ANY] * n, out_specs=[_ANY] * n,
        scratch_shapes=[pltpu.SemaphoreType.DMA((n,)), pltpu.SemaphoreType.DMA((n,))], name=name,
    )(*arrs)


def _send_to_owner_chips(name, arrs):
    n = len(arrs)

    def body(*refs):
        ins, outs = refs[:n], refs[n:2 * n]
        send_sems, recv_sems = refs[2 * n:]
        x, y, c = _mesh_pos()
        cps = []
        for a in range(n):
            for j, (cx, cy) in enumerate(_other_chips(x, y)):
                cps.append(pltpu.make_async_remote_copy(
                    src_ref=ins[a].at[2 * cx + cy], dst_ref=outs[a].at[j], send_sem=send_sems.at[3 * a + j],
                    recv_sem=recv_sems.at[3 * a + j], device_id=(cx, cy, c), device_id_type=MESH))
                cps[-1].start()
        for cp in cps:
            cp.wait()

    return pl.pallas_call(
        body, out_shape=[jax.ShapeDtypeStruct((3,) + a.shape[1:], a.dtype) for a in arrs],
        in_specs=[_ANY] * n, out_specs=[_ANY] * n,
        scratch_shapes=[pltpu.SemaphoreType.DMA((3 * n,)), pltpu.SemaphoreType.DMA((3 * n,))], name=name,
    )(*arrs)


def _share_halves(name, arrs):
    n = len(arrs)

    def body(*refs):
        ins, outs = refs[:n], refs[n:2 * n]
        send_sems, recv_sems, local_sems = refs[2 * n:]
        x, y, c = _mesh_pos()
        cps, local = [], []
        for a in range(n):
            local.append(pltpu.make_async_copy(ins[a], outs[a].at[c], local_sems.at[a]))
            local[-1].start()
            cps.append(pltpu.make_async_remote_copy(
                src_ref=ins[a], dst_ref=outs[a].at[c], send_sem=send_sems.at[a], recv_sem=recv_sems.at[a],
                device_id=(x, y, 1 - c), device_id_type=MESH))
            cps[-1].start()
        for cp in cps:
            cp.wait()
        for cp in local:
            cp.wait()

    return pl.pallas_call(
        body, out_shape=[jax.ShapeDtypeStruct((2,) + a.shape, a.dtype) for a in arrs],
        in_specs=[_ANY] * n, out_specs=[_ANY] * n,
        scratch_shapes=[pltpu.SemaphoreType.DMA((n,)), pltpu.SemaphoreType.DMA((n,)), pltpu.SemaphoreType.DMA((n,))],
        name=name,
    )(*arrs)


def _all_gather_devices(name, a):
    rows, w = a.shape

    def body(a_ref, out_ref, send_sems, recv_sems, local_sem):
        x, y, c = _mesh_pos()
        me, sibling = (x, y, c), (x, y, 1 - c)
        chips = _other_chips(x, y)

        def slot(px, py, pc):
            return out_ref.at[4 * px + 2 * py + pc]

        def copy(sem, block, to, src=None):
            return pltpu.make_async_remote_copy(src_ref=slot(*block) if src is None else src, dst_ref=slot(*block),
                                                send_sem=send_sems.at[sem], recv_sem=recv_sems.at[sem], device_id=to,
                                                device_id_type=MESH)

        mine = pltpu.make_async_copy(a_ref, slot(*me), local_sem)
        mine.start()
        first = [copy(0, me, sibling, src=a_ref)]
        first += [copy(1 + j, me, (*chip, c), src=a_ref) for j, chip in enumerate(chips)]
        for cp in first:
            cp.start()
        passed = [copy(4 + j, (*chip, c), sibling) for j, chip in enumerate(chips)]
        for j, chip in enumerate(chips):
            copy(1 + j, (*chip, c), me).wait_recv()
            passed[j].start()
        copy(0, sibling, me).wait_recv()
        for j, chip in enumerate(chips):
            copy(4 + j, (*chip, 1 - c), me).wait_recv()
        for cp in first + passed:
            cp.wait_send()
        mine.wait()

    return pl.pallas_call(
        body, out_shape=jax.ShapeDtypeStruct((8, rows, w), a.dtype), in_specs=[_ANY], out_specs=_ANY,
        scratch_shapes=[pltpu.SemaphoreType.DMA((7,)), pltpu.SemaphoreType.DMA((7,)), pltpu.SemaphoreType.DMA],
        name=name,
    )(a)


_PACK_TILE = 256


def _sum_rows(name, arrs):
    def fn(*vals):
        tot = vals[0]
        for v in vals[1:]:
            tot = tot + v
        return (tot,)

    return _rows(name, fn, list(arrs), [], [(arrs[0].shape[1], F32)], tile=_PACK_TILE)[0]


def _adamw(name, w, g, m, v):
    def fn(wv, gv, mv, vv):
        m2 = ADAM_B1 * mv + (1.0 - ADAM_B1) * gv
        v2 = ADAM_B2 * vv + (1.0 - ADAM_B2) * (gv * gv)
        m_hat = m2 / (1.0 - ADAM_B1 ** ADAM_STEP)
        v_hat = v2 / (1.0 - ADAM_B2 ** ADAM_STEP)
        delta = -ADAM_LR * (m_hat / (jnp.sqrt(v_hat) + ADAM_EPS) + ADAM_WD * wv)
        return delta, m2, v2

    return _rows(name, fn, [w, g, m, v], [], [(w.shape[1], F32)] * 3, tile=_PACK_TILE)


_INPUT_ORDER = (["x", "p"] + WEIGHT_ORDER + ["loss_target"] + ["m_" + n for n in WEIGHT_ORDER]
                + ["v_" + n for n in WEIGHT_ORDER])


_SUM_ROWS = 128


def _pair_sum(name, g, got, core):
    nk, rows, w = g.shape
    half = rows // 2
    nt = half // _SUM_ROWS

    def body(c_ref, g_ref, got_ref, o_ref):
        o_ref[...] = g_ref[...] + got_ref[...]

    grid_spec = pltpu.PrefetchScalarGridSpec(
        num_scalar_prefetch=1, grid=(nk, nt),
        in_specs=[pl.BlockSpec((None, _SUM_ROWS, w), lambda k, i, c: (k, c[0] * nt + i, 0)),
                  pl.BlockSpec((None, _SUM_ROWS, w), lambda k, i, c: (k, i, 0))],
        out_specs=pl.BlockSpec((None, _SUM_ROWS, w), lambda k, i, c: (k, i, 0)))
    return pl.pallas_call(body, grid_spec=grid_spec, out_shape=jax.ShapeDtypeStruct((nk, half, w), F32), name=name,
                          compiler_params=_cparams(("parallel", "parallel")))(core, g, got)


def _owner_sum(name, pair, owed, chip):
    _, half, w = pair.shape

    def body(k_ref, p_ref, a_ref, b_ref, c_ref, o_ref):
        o_ref[...] = ((p_ref[...] + a_ref[...]) + b_ref[...]) + c_ref[...]

    def owed_spec(j):
        return pl.BlockSpec((None, _SUM_ROWS, w), lambda i, k: (j, i, 0))

    grid_spec = pltpu.PrefetchScalarGridSpec(
        num_scalar_prefetch=1, grid=(half // _SUM_ROWS,),
        in_specs=[pl.BlockSpec((None, _SUM_ROWS, w), lambda i, k: (k[0], i, 0)), owed_spec(0), owed_spec(1),
                  owed_spec(2)],
        out_specs=pl.BlockSpec((_SUM_ROWS, w), lambda i, k: (i, 0)))
    return pl.pallas_call(body, grid_spec=grid_spec, out_shape=jax.ShapeDtypeStruct((half, w), F32), name=name,
                          compiler_params=_cparams(("parallel",)))(chip, pair, owed, owed, owed)


MATRIX_WEIGHTS = [n for n in BIG_WEIGHTS if n not in KEPT_F32]
TINY_SHARDED = [n for n in BIG_WEIGHTS if n in KEPT_F32]


def _as_rows(a):
    return a.reshape(-1, a.shape[-1])


def _owner_major(grads):
    ev, od = [grads[i] for i in range(0, DEPTH, 2)], [grads[i] for i in range(1, DEPTH, 2)]

    def cols(m, k, n):
        w = m.shape[1] // n
        return m[:, k * w:(k + 1) * w]

    def rows(m, k, n):
        r = m.shape[0] // n
        return m[k * r:(k + 1) * r]

    w_in = [jnp.concatenate([g["in_uv"], g["in_qkv"], g["in_f"][:, :B_HEADS]], axis=1) for g in ev]
    per_chip = {
        "ev_w_in": lambda k: [cols(m, k, N_CHIPS) for m in w_in],
        "ev_w_out": lambda k: [rows(g["out_a"] if k < 2 else g["out_b"], k % 2, 2) for g in ev],
        "od_w_glu": lambda k: [cols(g["glu"], k, N_CHIPS) for g in od],
        "ffn_w_up": lambda k: [cols(g["up_g"] if k < 2 else g["up_u"], k % 2, 2) for g in grads],
        "ffn_w_down": lambda k: [rows(g["down"], k, N_CHIPS) for g in grads],
        "ple_w_proj": lambda k: [cols(g["ple_proj"], k, N_CHIPS) for g in grads],
        "ple_w_gate": lambda k: [rows(g["ple_gate"], k, N_CHIPS) for g in grads],
    }
    return {n: jnp.stack([jnp.concatenate(per_chip[n](k), axis=0) for k in range(N_CHIPS)]) for n in MATRIX_WEIGHTS}


def _small_grads(grads):
    ev, od = [grads[i] for i in range(0, DEPTH, 2)], [grads[i] for i in range(1, DEPTH, 2)]
    out = {"norm_mix": jnp.stack([g["norm_mix"] for g in grads]), "norm_ffn": jnp.stack([g["norm_ffn"] for g in grads]),
           "ffn_conv_w": jnp.stack([jnp.concatenate([g["cw_g"], g["cw_u"]], axis=1) for g in grads]),
           "ffn_conv_b": jnp.stack([g["conv_b"] for g in grads])}
    for n in ("ev_b_fgate", "ev_q_norm", "ev_k_norm", "ev_v_norm", "ev_w_spatial", "ev_b_spatial"):
        out[n] = jnp.stack([g[n] for g in ev])
    for n in ("od_a_re", "od_a_im", "od_log_dt", "od_b_re", "od_b_im", "od_c_re", "od_c_im", "od_d"):
        out[n] = jnp.stack([g[n] for g in od])
    return out


def _step(a):
    xi, yi, ci = _mesh_pos()
    chip = 2 * xi + yi
    core_arr, chip_arr = ci.astype(jnp.int32).reshape(1), chip.astype(jnp.int32).reshape(1)
    rp = {n: a[n] for n in SMALL_WEIGHTS}

    tiny = _pack([a[n] for n in TINY_SHARDED], 32)
    gathered = _gather_shards("gather_weights", [_as_rows(a[n]).astype(BF16) for n in MATRIX_WEIGHTS] + [tiny])
    full = {}
    for n, g in zip(MATRIX_WEIGHTS, gathered):
        full[n] = jnp.concatenate([g[k].reshape(a[n].shape) for k in range(N_CHIPS)], axis=SHARD_AXIS[n])
    tiny_parts = [_unpack(gathered[-1][k], [a[n].shape for n in TINY_SHARDED]) for k in range(N_CHIPS)]
    for idx, n in enumerate(TINY_SHARDED):
        full[n] = jnp.concatenate([tiny_parts[k][idx] for k in range(N_CHIPS)], axis=SHARD_AXIS[n])
    lw = [_layer_weights(i, full, rp) for i in range(DEPTH)]

    loss_local, grad_x, grads = _local_step(a["x"][0], a["p"][:, 0], a["loss_target"][0], lw, rp)
    loss = lax.psum(loss_local, ("x", "y", "c"))

    contrib = _owner_major(grads)
    mats = [contrib[n] for n in MATRIX_WEIGHTS]
    got = _swap_halves("grad_pair_swap", mats)
    pair = [_pair_sum(f"grad_pair_sum_{n}", g, h, core_arr) for n, g, h in zip(MATRIX_WEIGHTS, mats, got)]
    owed = _send_to_owner_chips("grad_to_owner", pair)
    mine = [_owner_sum(f"grad_owner_sum_{n}", p, o, chip_arr) for n, p, o in zip(MATRIX_WEIGHTS, pair, owed)]
    reduced = _share_halves("grad_half_share", mine)

    small_names = SMALL_WEIGHTS + TINY_SHARDED
    sg = _small_grads(grads)
    everyone = _all_gather_devices("small_grad_gather", _pack([sg[n] for n in small_names], _PACK_TILE))
    g_small = _sum_rows("small_grad_sum", [everyone[d] for d in range(8)])
    small_full = dict(zip(small_names, _unpack(g_small, [sg[n].shape for n in small_names])))

    out = {}
    for n, red in zip(MATRIX_WEIGHTS, reduced):
        shape = a[n].shape
        g2d = red.reshape(-1, shape[-1])
        delta, m2, v2 = _adamw(f"adamw_{n}", _as_rows(a[n]), g2d, _as_rows(a["m_" + n]), _as_rows(a["v_" + n]))
        for kind, val in (("grad", g2d), ("delta", delta), ("new_m", m2), ("new_v", v2)):
            out[kind + "_" + n] = val.reshape(shape)
    g_sm = {n: small_full[n] for n in SMALL_WEIGHTS}
    for n in TINY_SHARDED:
        width = a[n].shape[SHARD_AXIS[n]]
        g_sm[n] = lax.dynamic_slice_in_dim(small_full[n], chip * width, width, axis=SHARD_AXIS[n])
    shapes = [a[n].shape for n in small_names]
    w, m, v = (_pack([a[pre + n] for n in small_names], _PACK_TILE) for pre in ("", "m_", "v_"))
    g = _pack([g_sm[n] for n in small_names], _PACK_TILE)
    delta, m2, v2 = _adamw("adamw_small", w, g, m, v)
    for kind, buf in (("delta", delta), ("new_m", m2), ("new_v", v2)):
        for n, val in zip(small_names, _unpack(buf, shapes)):
            out[kind + "_" + n] = val
    for n in small_names:
        out["grad_" + n] = g_sm[n]
    res = [loss, grad_x[None]]
    for kind in ("grad", "delta", "new_m", "new_v"):
        res += [out[kind + "_" + n] for n in WEIGHT_ORDER]
    return tuple(res)


def kernel(x, p, norm_mix, norm_ffn, ev_w_in, ev_b_fgate, ev_q_norm, ev_k_norm, ev_v_norm, ev_w_spatial, ev_b_spatial, ev_w_out, od_a_re, od_a_im, od_log_dt, od_b_re, od_b_im, od_c_re, od_c_im, od_d, od_w_glu, ffn_w_up, ffn_conv_w, ffn_conv_b, ffn_w_down, ple_w_proj, ple_w_gate, loss_target, m_norm_mix, m_norm_ffn, m_ev_w_in, m_ev_b_fgate, m_ev_q_norm, m_ev_k_norm, m_ev_v_norm, m_ev_w_spatial, m_ev_b_spatial, m_ev_w_out, m_od_a_re, m_od_a_im, m_od_log_dt, m_od_b_re, m_od_b_im, m_od_c_re, m_od_c_im, m_od_d, m_od_w_glu, m_ffn_w_up, m_ffn_conv_w, m_ffn_conv_b, m_ffn_w_down, m_ple_w_proj, m_ple_w_gate, v_norm_mix, v_norm_ffn, v_ev_w_in, v_ev_b_fgate, v_ev_q_norm, v_ev_k_norm, v_ev_v_norm, v_ev_w_spatial, v_ev_b_spatial, v_ev_w_out, v_od_a_re, v_od_a_im, v_od_log_dt, v_od_b_re, v_od_b_im, v_od_c_re, v_od_c_im, v_od_d, v_od_w_glu, v_ffn_w_up, v_ffn_conv_w, v_ffn_conv_b, v_ffn_w_down, v_ple_w_proj, v_ple_w_gate):
    args = (x, p, norm_mix, norm_ffn, ev_w_in, ev_b_fgate, ev_q_norm, ev_k_norm, ev_v_norm, ev_w_spatial, ev_b_spatial, ev_w_out, od_a_re, od_a_im, od_log_dt, od_b_re, od_b_im, od_c_re, od_c_im, od_d, od_w_glu, ffn_w_up, ffn_conv_w, ffn_conv_b, ffn_w_down, ple_w_proj, ple_w_gate, loss_target, m_norm_mix, m_norm_ffn, m_ev_w_in, m_ev_b_fgate, m_ev_q_norm, m_ev_k_norm, m_ev_v_norm, m_ev_w_spatial, m_ev_b_spatial, m_ev_w_out, m_od_a_re, m_od_a_im, m_od_log_dt, m_od_b_re, m_od_b_im, m_od_c_re, m_od_c_im, m_od_d, m_od_w_glu, m_ffn_w_up, m_ffn_conv_w, m_ffn_conv_b, m_ffn_w_down, m_ple_w_proj, m_ple_w_gate, v_norm_mix, v_norm_ffn, v_ev_w_in, v_ev_b_fgate, v_ev_q_norm, v_ev_k_norm, v_ev_v_norm, v_ev_w_spatial, v_ev_b_spatial, v_ev_w_out, v_od_a_re, v_od_a_im, v_od_log_dt, v_od_b_re, v_od_b_im, v_od_c_re, v_od_c_im, v_od_d, v_od_w_glu, v_ffn_w_up, v_ffn_conv_w, v_ffn_conv_b, v_ffn_w_down, v_ple_w_proj, v_ple_w_gate)
    return _step(dict(zip(_INPUT_ORDER, args)))
```

```python
import functools
import math

import jax
import jax.numpy as jnp
import numpy as np
from jax import lax
from jax.experimental import pallas as pl
from jax.experimental.pallas import tpu as pltpu

F32 = jnp.float32
BF16 = jnp.bfloat16
MESH = pl.DeviceIdType.MESH

V7X_VMEM_LIMIT_BYTES = 56 * 1024 * 1024
LANES = 128

D_MODEL = 1024
DEPTH = 4
A_GROUPS = 4
A_CHUNK = 128
A_WIDTH = 512
B_HEADS = 8
B_HEAD_DIM = 64
B_WIDTH = 512
S5_GROUP_CH = 16
S5_GROUPS = 64
S5_STATE = 64
S5_BLOCKS = 8
S5_LANES = 512
D_FF = 2816
PLE_DIM = 256
EPS = 1e-6
NEG_INF = -1e30

ADAM_LR = 0.001
ADAM_B1 = 0.9
ADAM_B2 = 0.999
ADAM_EPS = 1e-08
ADAM_WD = 0.01
ADAM_STEP = 10

N_CHIPS = 4
PACK_W = 1024


def _cparams(sem):
    return pltpu.CompilerParams(dimension_semantics=sem, vmem_limit_bytes=V7X_VMEM_LIMIT_BYTES)


def _pick(n, target):
    if n <= target:
        return n
    t = (target // LANES) * LANES
    while t >= LANES:
        if n % t == 0:
            return t
        t -= LANES
    return n


_GELU_K = 0.7978845608028654
_GELU_C = 0.044715


def _gelu(x):
    return x * (0.5 * (1.0 + jnp.tanh(_GELU_K * (x + _GELU_C * (x * x * x)))))


def _gelu_grad(x):
    x2 = x * x
    t = jnp.tanh(_GELU_K * (x + _GELU_C * (x * x2)))
    return 0.5 * (1.0 + t) + (0.5 * x) * (1.0 - t * t) * (_GELU_K * (1.0 + (3.0 * _GELU_C) * x2))


def _sigmoid(x):
    return 1.0 / (1.0 + jnp.exp(-x))


def _log_sigmoid(x):
    return -(jnp.maximum(-x, 0.0) + jnp.log(1.0 + jnp.exp(-jnp.abs(x))))


def _rstd(x):
    return lax.rsqrt(jnp.mean(x * x, axis=-1, keepdims=True) + EPS)


def _rows(name, fn, row_ins, full_ins, outs, accs=(), tile=256):
    rows = row_ins[0].shape[0]
    r = min(tile, rows)
    n = rows // r
    n_in = len(row_ins) + len(full_ins)
    n_out = len(outs)

    def body(*refs):
        res = fn(*[ref[...] for ref in refs[:n_in]])
        for ref, v in zip(refs[n_in:n_in + n_out], res[:n_out]):
            ref[...] = v.astype(ref.dtype)
        acc_refs = refs[n_in + n_out:]
        if acc_refs:
            @pl.when(pl.program_id(0) == 0)
            def _():
                for ref in acc_refs:
                    ref[...] = jnp.zeros(ref.shape, ref.dtype)

            for ref, v in zip(acc_refs, res[n_out:]):
                ref[...] += v

    in_specs = [pl.BlockSpec((r, a.shape[1]), lambda i: (i, 0)) for a in row_ins]
    in_specs += [pl.BlockSpec(a.shape, lambda i, nd=a.ndim: (0,) * nd) for a in full_ins]
    out_shape = [jax.ShapeDtypeStruct((rows, w), dt) for (w, dt) in outs]
    out_shape += [jax.ShapeDtypeStruct(s, F32) for s in accs]
    out_specs = [pl.BlockSpec((r, w), lambda i: (i, 0)) for (w, dt) in outs]
    out_specs += [pl.BlockSpec(s, lambda i, nd=len(s): (0,) * nd) for s in accs]
    return pl.pallas_call(
        body, grid=(n,), in_specs=in_specs, out_specs=out_specs, out_shape=out_shape, name=name,
        compiler_params=_cparams(("arbitrary",) if accs else ("parallel",)),
    )(*row_ins, *full_ins)


_DOT_DIMS = {"nn": (((1,), (0,)), ((), ())), "nt": (((1,), (1,)), ((), ())), "tn": (((0,), (0,)), ((), ()))}


def _mm(name, a, b, mode="nn", out_dtype=F32, res=None, tm=1024, tn=1024, tk=1024, norm_gain=None):
    if mode == "nn":
        (m, k), (k2, n) = a.shape, b.shape
    elif mode == "nt":
        (m, k), (n, k2) = a.shape, b.shape
    else:
        (k, m), (k2, n) = a.shape, b.shape
    assert k == k2, (name, a.shape, b.shape, mode)
    tm, tn, tk = _pick(m, tm), _pick(n, tn), _pick(k, tk)
    nk = k // tk
    dims = _DOT_DIMS[mode]
    has_res = res is not None
    has_norm = norm_gain is not None
    assert not has_norm or tn == n, (name, tn, n)
    n_in = 2 + has_res + has_norm

    def body(*refs):
        a_ref, b_ref = refs[0], refs[1]
        res_ref = refs[2] if has_res else None
        gain_ref = refs[n_in - 1] if has_norm else None
        o_ref = refs[n_in]
        h_ref = refs[n_in + 1] if has_norm else None

        def finish(tot):
            if has_res:
                tot = res_ref[...] + tot
            o_ref[...] = tot.astype(o_ref.dtype)
            if has_norm:
                h_ref[...] = ((tot * _rstd(tot)) * gain_ref[...]).astype(h_ref.dtype)

        prod = lax.dot_general(a_ref[...].astype(BF16), b_ref[...].astype(BF16), dims, preferred_element_type=F32)
        if nk == 1:
            finish(prod)
            return
        acc = refs[-1]
        kk = pl.program_id(2)

        @pl.when(kk == 0)
        def _():
            acc[...] = prod

        @pl.when(kk > 0)
        def _():
            acc[...] += prod

        @pl.when(kk == nk - 1)
        def _():
            finish(acc[...])

    if mode == "nn":
        a_spec = pl.BlockSpec((tm, tk), lambda i, j, kk: (i, kk))
        b_spec = pl.BlockSpec((tk, tn), lambda i, j, kk: (kk, j))
    elif mode == "nt":
        a_spec = pl.BlockSpec((tm, tk), lambda i, j, kk: (i, kk))
        b_spec = pl.BlockSpec((tn, tk), lambda i, j, kk: (j, kk))
    else:
        a_spec = pl.BlockSpec((tk, tm), lambda i, j, kk: (kk, i))
        b_spec = pl.BlockSpec((tk, tn), lambda i, j, kk: (kk, j))
    o_spec = pl.BlockSpec((tm, tn), lambda i, j, kk: (i, j))
    in_specs = [a_spec, b_spec] + ([o_spec] if has_res else [])
    in_specs += [pl.BlockSpec((1, tn), lambda i, j, kk: (0, j))] if has_norm else []
    args = (a, b) + ((res,) if has_res else ()) + ((norm_gain,) if has_norm else ())
    out_shape = jax.ShapeDtypeStruct((m, n), out_dtype)
    return pl.pallas_call(
        body, grid=(m // tm, n // tn, nk), in_specs=in_specs, out_specs=[o_spec, o_spec] if has_norm else o_spec,
        out_shape=[out_shape, jax.ShapeDtypeStruct((m, n), BF16)] if has_norm else out_shape, name=name,
        scratch_shapes=[pltpu.VMEM((tm, tn), F32)] if nk > 1 else [],
        compiler_params=_cparams(("parallel", "parallel", "arbitrary")),
    )(*args)


def _rmsnorm_fwd(name, x, g, outs):
    def fn(xv, gv):
        y = (xv * _rstd(xv)) * gv
        return tuple(y for _ in outs)

    return _rows(name, fn, [x], [g], [(x.shape[1], dt) for dt in outs])


def _rmsnorm_bwd(name, x, dy, dres, g):
    def fn(xv, dyv, drv, gv):
        r = _rstd(xv)
        xh = xv * r
        dyg = dyv * gv
        dx = drv + r * (dyg - xh * jnp.mean(dyg * xh, axis=-1, keepdims=True))
        return dx, jnp.sum(dyv * xh, axis=0, keepdims=True)

    w = x.shape[1]
    return _rows(name, fn, [x, dy, dres], [g], [(w, F32)], accs=[(1, w)])


_CONV_ROWS = 256
_CONV_COLS = 1408


def _conv_taps(h_ref, halo_ref, first):
    h = h_ref[...]
    rows = h.shape[0]
    row = lax.broadcasted_iota(jnp.int32, (rows, 1), 0)
    keep = jnp.where(first, 0.0, 1.0)
    m1 = halo_ref[7:8, :] * keep
    m2 = halo_ref[6:7, :] * keep
    p1 = jnp.where(row == 0, m1, pltpu.roll(h, 1, 0))
    p2 = jnp.where(row == 0, m2, jnp.where(row == 1, m1, pltpu.roll(h, 2, 0)))
    return h, p1, p2


def _conv_specs(rows, r, cw):
    tile = pl.BlockSpec((r, cw), lambda j, i: (i, j))
    halo = pl.BlockSpec((8, cw), lambda j, i: (jnp.maximum(i * (r // 8) - 1, 0), j))
    vec3 = pl.BlockSpec((3, cw), lambda j, i: (0, j))
    vec1 = pl.BlockSpec((1, cw), lambda j, i: (0, j))
    return tile, halo, vec3, vec1


def _convffn_fwd(name, hg, hu, wg, wu, bg, bu):
    rows, f = hg.shape
    r, cw = min(_CONV_ROWS, rows), _pick(f, _CONV_COLS)

    def body(hg_ref, hgh_ref, hu_ref, huh_ref, wg_ref, wu_ref, bg_ref, bu_ref, o_ref):
        first = pl.program_id(1) == 0
        h, p1, p2 = _conv_taps(hg_ref, hgh_ref, first)
        g = bg_ref[...] + wg_ref[0:1, :] * p2 + wg_ref[1:2, :] * p1 + wg_ref[2:3, :] * h
        h, p1, p2 = _conv_taps(hu_ref, huh_ref, first)
        u = bu_ref[...] + wu_ref[0:1, :] * p2 + wu_ref[1:2, :] * p1 + wu_ref[2:3, :] * h
        o_ref[...] = ((g * _sigmoid(g)) * u).astype(o_ref.dtype)

    tile, halo, vec3, vec1 = _conv_specs(rows, r, cw)
    return pl.pallas_call(
        body, grid=(f // cw, rows // r), in_specs=[tile, halo, tile, halo, vec3, vec3, vec1, vec1], out_specs=tile,
        out_shape=jax.ShapeDtypeStruct((rows, f), BF16), name=name, compiler_params=_cparams(("parallel", "parallel")),
    )(hg, hg, hu, hu, wg, wu, bg, bu)


def _gate_grads(da, g, u):
    sg = _sigmoid(g)
    return da * u * (sg * (1.0 + g * (1.0 - sg))), da * (g * sg)


def _conv_back(dc, dc_next, w_ref, last):
    r = dc.shape[0]
    row = lax.broadcasted_iota(jnp.int32, (r, 1), 0)
    keep = jnp.where(last, 0.0, 1.0)
    n0 = dc_next[0:1, :] * keep
    n1 = dc_next[1:2, :] * keep
    f1 = jnp.where(row == r - 1, n0, pltpu.roll(dc, r - 1, 0))
    f2 = jnp.where(row == r - 1, n1, jnp.where(row == r - 2, n0, pltpu.roll(dc, r - 2, 0)))
    return w_ref[2:3, :] * dc + w_ref[1:2, :] * f1 + w_ref[0:1, :] * f2


def _conv_next_rows(h, nxt_ref, w_ref, b_ref):
    r = h.shape[0]
    hn = nxt_ref[...]
    row = lax.broadcasted_iota(jnp.int32, (8, 1), 0)
    m1, m2 = h[r - 1:r, :], h[r - 2:r - 1, :]
    p1 = jnp.where(row == 0, m1, pltpu.roll(hn, 1, 0))
    p2 = jnp.where(row == 0, m2, jnp.where(row == 1, m1, pltpu.roll(hn, 2, 0)))
    return b_ref[...] + w_ref[0:1, :] * p2 + w_ref[1:2, :] * p1 + w_ref[2:3, :] * hn


def _convffn_bwd(name, da, hg, hu, wg, wu, bg, bu):
    rows, f = hg.shape
    r, cw = min(_CONV_ROWS, rows), _pick(f, _CONV_COLS)
    nrt = rows // r

    def body(da_ref, dan_ref, hg_ref, hgh_ref, hgn_ref, hu_ref, huh_ref, hun_ref, wg_ref, wu_ref, bg_ref, bu_ref,
             dhg_ref, dhu_ref, dwg_ref, dwu_ref, dbg_ref, dbu_ref):
        first = pl.program_id(1) == 0
        last = pl.program_id(1) == nrt - 1
        hgv, g1, g2 = _conv_taps(hg_ref, hgh_ref, first)
        g = bg_ref[...] + wg_ref[0:1, :] * g2 + wg_ref[1:2, :] * g1 + wg_ref[2:3, :] * hgv
        huv, u1, u2 = _conv_taps(hu_ref, huh_ref, first)
        u = bu_ref[...] + wu_ref[0:1, :] * u2 + wu_ref[1:2, :] * u1 + wu_ref[2:3, :] * huv
        dcg, dcu = _gate_grads(da_ref[...], g, u)
        dcg_n, dcu_n = _gate_grads(dan_ref[...], _conv_next_rows(hgv, hgn_ref, wg_ref, bg_ref),
                                   _conv_next_rows(huv, hun_ref, wu_ref, bu_ref))
        dhg_ref[...] = _conv_back(dcg, dcg_n, wg_ref, last).astype(dhg_ref.dtype)
        dhu_ref[...] = _conv_back(dcu, dcu_n, wu_ref, last).astype(dhu_ref.dtype)

        @pl.when(first)
        def _():
            for ref in (dwg_ref, dwu_ref, dbg_ref, dbu_ref):
                ref[...] = jnp.zeros(ref.shape, ref.dtype)

        def colsum(v):
            return jnp.sum(v, axis=0, keepdims=True)

        dwg_ref[0:1, :] += colsum(dcg * g2)
        dwg_ref[1:2, :] += colsum(dcg * g1)
        dwg_ref[2:3, :] += colsum(dcg * hgv)
        dwu_ref[0:1, :] += colsum(dcu * u2)
        dwu_ref[1:2, :] += colsum(dcu * u1)
        dwu_ref[2:3, :] += colsum(dcu * huv)
        dbg_ref[...] += colsum(dcg)
        dbu_ref[...] += colsum(dcu)

    tile, halo, vec3, vec1 = _conv_specs(rows, r, cw)
    nxt = pl.BlockSpec((8, cw), lambda j, i: (jnp.minimum((i + 1) * (r // 8), rows // 8 - 1), j))
    big = jax.ShapeDtypeStruct((rows, f), BF16)
    return pl.pallas_call(
        body, grid=(f // cw, nrt),
        in_specs=[tile, nxt, tile, halo, nxt, tile, halo, nxt, vec3, vec3, vec1, vec1],
        out_specs=[tile, tile, vec3, vec3, vec1, vec1],
        out_shape=[big, big, jax.ShapeDtypeStruct((3, f), F32), jax.ShapeDtypeStruct((3, f), F32),
                   jax.ShapeDtypeStruct((1, f), F32), jax.ShapeDtypeStruct((1, f), F32)],
        name=name, compiler_params=_cparams(("parallel", "arbitrary")),
    )(da, da, hg, hg, hg, hu, hu, hu, wg, wu, bg, bu)


_GMLP_ROWS = 256


def _gmlp_group_norm(vg, gain):
    r = lax.rsqrt(jnp.mean(vg * vg, axis=-1, keepdims=True) + EPS)
    vh = vg * r
    return vh, r, vh * gain


def _gmlp_fwd(name, zuv, v_gain, w_tril, b_exp):
    rows = zuv.shape[0]
    r = min(_GMLP_ROWS, rows)

    def body(z_ref, gain_ref, w_ref, b_ref, o_ref):
        for ch in range(r // A_CHUNK):
            lo = ch * A_CHUNK
            for g in range(A_GROUPS):
                c0 = g * LANES
                u = _gelu(z_ref[lo:lo + A_CHUNK, c0:c0 + LANES])
                v = _gelu(z_ref[lo:lo + A_CHUNK, A_WIDTH + c0:A_WIDTH + c0 + LANES])
                _, _, vn = _gmlp_group_norm(v, gain_ref[:, c0:c0 + LANES])
                sv = jnp.dot(w_ref[g], vn.astype(BF16), preferred_element_type=F32) + b_ref[g]
                o_ref[lo:lo + A_CHUNK, c0:c0 + LANES] = (u * sv).astype(o_ref.dtype)

    return pl.pallas_call(
        body, grid=(rows // r,),
        in_specs=[pl.BlockSpec((r, 2 * A_WIDTH), lambda i: (i, 0)), pl.BlockSpec((1, A_WIDTH), lambda i: (0, 0)),
                  pl.BlockSpec((A_GROUPS, A_CHUNK, A_CHUNK), lambda i: (0, 0, 0)),
                  pl.BlockSpec((A_GROUPS, A_CHUNK, LANES), lambda i: (0, 0, 0))],
        out_specs=pl.BlockSpec((r, A_WIDTH), lambda i: (i, 0)),
        out_shape=jax.ShapeDtypeStruct((rows, A_WIDTH), BF16), name=name, compiler_params=_cparams(("parallel",)),
    )(zuv, v_gain, w_tril, b_exp)


def _gmlp_bwd(name, zuv, dya, v_gain, w_tril, w_tril_t, b_exp):
    rows = zuv.shape[0]
    r = min(_GMLP_ROWS, rows)

    def body(z_ref, dy_ref, gain_ref, w_ref, wt_ref, b_ref, dz_ref, dw_ref, db_ref, dgain_ref):
        @pl.when(pl.program_id(0) == 0)
        def _():
            for ref in (dw_ref, db_ref, dgain_ref):
                ref[...] = jnp.zeros(ref.shape, ref.dtype)

        for ch in range(r // A_CHUNK):
            lo = ch * A_CHUNK
            for g in range(A_GROUPS):
                c0 = g * LANES
                zu = z_ref[lo:lo + A_CHUNK, c0:c0 + LANES]
                zv = z_ref[lo:lo + A_CHUNK, A_WIDTH + c0:A_WIDTH + c0 + LANES]
                gain = gain_ref[:, c0:c0 + LANES]
                u = _gelu(zu)
                v = _gelu(zv)
                vh, rr, vn = _gmlp_group_norm(v, gain)
                vn_b = vn.astype(BF16)
                sv = jnp.dot(w_ref[g], vn_b, preferred_element_type=F32) + b_ref[g]
                dy = dy_ref[lo:lo + A_CHUNK, c0:c0 + LANES]
                dsv = dy * u
                dsv_b = dsv.astype(BF16)
                dz_ref[lo:lo + A_CHUNK, c0:c0 + LANES] = ((dy * sv) * _gelu_grad(zu)).astype(dz_ref.dtype)
                dw_ref[g] += lax.dot_general(dsv_b, vn_b, _DOT_DIMS["nt"], preferred_element_type=F32)
                db_ref[g] += dsv
                dvn = jnp.dot(wt_ref[g], dsv_b, preferred_element_type=F32)
                dgain_ref[:, c0:c0 + LANES] += jnp.sum(dvn * vh, axis=0, keepdims=True)
                dvh = dvn * gain
                dv = rr * (dvh - vh * jnp.mean(dvh * vh, axis=-1, keepdims=True))
                dz_ref[lo:lo + A_CHUNK, A_WIDTH + c0:A_WIDTH + c0 + LANES] = (dv * _gelu_grad(zv)).astype(dz_ref.dtype)

    wspec = pl.BlockSpec((A_GROUPS, A_CHUNK, A_CHUNK), lambda i: (0, 0, 0))
    bspec = pl.BlockSpec((A_GROUPS, A_CHUNK, LANES), lambda i: (0, 0, 0))
    gspec = pl.BlockSpec((1, A_WIDTH), lambda i: (0, 0))
    return pl.pallas_call(
        body, grid=(rows // r,),
        in_specs=[pl.BlockSpec((r, 2 * A_WIDTH), lambda i: (i, 0)), pl.BlockSpec((r, A_WIDTH), lambda i: (i, 0)),
                  gspec, wspec, wspec, bspec],
        out_specs=[pl.BlockSpec((r, 2 * A_WIDTH), lambda i: (i, 0)), wspec, bspec, gspec],
        out_shape=[jax.ShapeDtypeStruct((rows, 2 * A_WIDTH), BF16),
                   jax.ShapeDtypeStruct((A_GROUPS, A_CHUNK, A_CHUNK), F32),
                   jax.ShapeDtypeStruct((A_GROUPS, A_CHUNK, LANES), F32), jax.ShapeDtypeStruct((1, A_WIDTH), F32)],
        name=name, compiler_params=_cparams(("arbitrary",)),
    )(zuv, dya, v_gain, w_tril, w_tril_t, b_exp)


_ATT_T = 512
_Q_SCALE = B_HEAD_DIM ** -0.5


def _head_mean(v, bd):
    return jnp.dot(v, bd, preferred_element_type=F32, precision=lax.Precision.HIGHEST)


def _qkv_prep_fwd(name, zqkv, zf, qg, kg, bf, bd):
    def fn(z, f, qg_v, kg_v, bf_v, bd_v):
        zq, zk, zv = z[:, :B_WIDTH], z[:, B_WIDTH:2 * B_WIDTH], z[:, 2 * B_WIDTH:]
        q = (zq * lax.rsqrt(_head_mean(zq * zq, bd_v) + EPS)) * qg_v * _Q_SCALE
        k = (zk * lax.rsqrt(_head_mean(zk * zk, bd_v) + EPS)) * kg_v
        return q, k, zv, _log_sigmoid(f + bf_v)

    return _rows(name, fn, [zqkv, zf], [qg, kg, bf, bd],
                 [(B_WIDTH, BF16), (B_WIDTH, BF16), (B_WIDTH, BF16), (LANES, F32)])


def _qkv_prep_bwd(name, zqkv, zf, dq, dk, dv, dls, qg, kg, bf, bd):
    def fn(z, f, dq_v, dk_v, dv_v, dls_v, qg_v, kg_v, bf_v, bd_v):
        zq, zk = z[:, :B_WIDTH], z[:, B_WIDTH:2 * B_WIDTH]

        def norm_bwd(x, dy, gain):
            r = lax.rsqrt(_head_mean(x * x, bd_v) + EPS)
            xh = x * r
            dxh = dy * gain
            dx = r * (dxh - xh * _head_mean(dxh * xh, bd_v))
            return dx, jnp.sum(dy * xh, axis=0, keepdims=True)

        dzq, dqg = norm_bwd(zq, dq_v * _Q_SCALE, qg_v)
        dzk, dkg = norm_bwd(zk, dk_v, kg_v)
        dzf = dls_v * (1.0 - _sigmoid(f + bf_v))
        return jnp.concatenate([dzq, dzk, dv_v], axis=1), dzf, dqg, dkg, jnp.sum(dzf, axis=0, keepdims=True)

    return _rows(name, fn, [zqkv, zf, dq, dk, dv, dls], [qg, kg, bf, bd],
                 [(3 * B_WIDTH, BF16), (LANES, BF16)], accs=[(1, B_WIDTH), (1, B_WIDTH), (1, LANES)])


def _cumsum_rows(name, a, reverse=False, tile=512):
    rows, w = a.shape
    r = min(tile, rows)
    n = rows // r

    def body(a_ref, o_ref, carry):
        @pl.when(pl.program_id(0) == 0)
        def _():
            carry[...] = jnp.zeros(carry.shape, carry.dtype)

        x = a_ref[...]
        row = lax.broadcasted_iota(jnp.int32, (r, 1), 0)
        s = 1
        while s < r:
            if reverse:
                x = x + jnp.where(row < r - s, pltpu.roll(x, r - s, 0), 0.0)
            else:
                x = x + jnp.where(row >= s, pltpu.roll(x, s, 0), 0.0)
            s *= 2
        x = x + carry[0:1, :]
        o_ref[...] = x
        edge = x[0:1, :] if reverse else x[r - 1:r, :]
        carry[...] = jnp.broadcast_to(edge, carry.shape)

    idx = (lambda i: (n - 1 - i, 0)) if reverse else (lambda i: (i, 0))
    return pl.pallas_call(
        body, grid=(n,), in_specs=[pl.BlockSpec((r, w), idx)], out_specs=pl.BlockSpec((r, w), idx),
        out_shape=jax.ShapeDtypeStruct((rows, w), F32), scratch_shapes=[pltpu.VMEM((8, w), F32)], name=name,
        compiler_params=_cparams(("arbitrary",)),
    )(a)


def _head_masks():
    lane = lax.broadcasted_iota(jnp.int32, (1, LANES), 1)
    return [lane < B_HEAD_DIM, lane >= B_HEAD_DIM]


def _causal(t):
    row = lax.broadcasted_iota(jnp.int32, (t, t), 0)
    col = lax.broadcasted_iota(jnp.int32, (t, t), 1)
    return row, col


def _flash_fwd(name, q, k, v, nck_rows):
    rows = q.shape[0]
    t = min(_ATT_T, rows)
    nb = rows // t

    def body(q_ref, k_ref, v_ref, nck_ref, o_ref, lse_ref):
        pair, i = pl.program_id(0), pl.program_id(1)
        q2 = q_ref[...]
        row, col = _causal(t)
        masks = _head_masks()
        qh = [jnp.where(hm, q2, jnp.zeros_like(q2)) for hm in masks]

        def step(j, carry, diag):
            ml, acc = carry
            start = pl.multiple_of(j * t, t)
            kb = k_ref[pl.ds(start, t), :]
            vb = v_ref[pl.ds(start, t), :]
            new_ml = []
            for hh, hm in enumerate(masks):
                m, l = ml[hh]
                s = lax.dot_general(qh[hh], kb, _DOT_DIMS["nt"], preferred_element_type=F32)
                s = s + nck_ref[2 * pair + hh, pl.ds(j, 1), :]
                if diag:
                    s = jnp.where(col <= row, s, NEG_INF)
                m_new = jnp.maximum(m, jnp.max(s, axis=1, keepdims=True))
                p = jnp.exp(s - m_new)
                alpha = jnp.exp(m - m_new)
                new_ml.append((m_new, alpha * l + jnp.sum(p, axis=1, keepdims=True)))
                pv = jnp.dot(p.astype(BF16), jnp.where(hm, vb, jnp.zeros_like(vb)), preferred_element_type=F32)
                acc = acc * jnp.where(hm, alpha, 1.0) + pv
            return tuple(new_ml), acc

        def init_ml():
            return (jnp.full((t, 1), NEG_INF, F32), jnp.zeros((t, 1), F32))

        init = ((init_ml(), init_ml()), jnp.zeros((t, LANES), F32))
        carry = lax.fori_loop(0, i, lambda j, c: step(j, c, False), init)
        ml, acc = step(i, carry, True)
        o_ref[...] = acc / jnp.where(masks[0], ml[0][1], ml[1][1])
        for hh in range(2):
            lse_ref[hh] = ml[hh][0] + jnp.log(ml[hh][1])

    return pl.pallas_call(
        body, grid=(B_HEADS // 2, nb),
        in_specs=[pl.BlockSpec((t, LANES), lambda p, i: (i, p)), pl.BlockSpec((rows, LANES), lambda p, i: (0, p)),
                  pl.BlockSpec((rows, LANES), lambda p, i: (0, p)),
                  pl.BlockSpec((B_HEADS, nb, t), lambda p, i: (0, 0, 0))],
        out_specs=[pl.BlockSpec((t, LANES), lambda p, i: (i, p)), pl.BlockSpec((2, t, 1), lambda p, i: (p, i, 0))],
        out_shape=[jax.ShapeDtypeStruct((rows, B_WIDTH), F32), jax.ShapeDtypeStruct((B_HEADS, rows, 1), F32)],
        name=name, compiler_params=_cparams(("parallel", "parallel")),
    )(q, k, v, nck_rows)


def _flash_bwd_dq(name, q, k, v, nck_rows, o, do, lse_col):
    rows = q.shape[0]
    t = min(_ATT_T, rows)
    nb = rows // t

    def body(q_ref, k_ref, v_ref, nck_ref, o_ref, do_ref, lse_ref, dq_ref, delta_ref):
        pair, i = pl.program_id(0), pl.program_id(1)
        q2 = q_ref[...]
        do2 = do_ref[...]
        od = o_ref[...] * do2
        do_b = do2.astype(BF16)
        row, col = _causal(t)
        masks = _head_masks()
        qh = [jnp.where(hm, q2, jnp.zeros_like(q2)) for hm in masks]
        doh = [jnp.where(hm, do_b, jnp.zeros_like(do_b)) for hm in masks]
        delta = [jnp.sum(jnp.where(hm, od, 0.0), axis=1, keepdims=True) for hm in masks]
        lse = [lse_ref[hh] for hh in range(2)]

        def step(j, carry, diag):
            acc, rowsum = carry
            start = pl.multiple_of(j * t, t)
            kb = k_ref[pl.ds(start, t), :]
            vb = v_ref[pl.ds(start, t), :]
            new_rowsum = []
            for hh, hm in enumerate(masks):
                s = lax.dot_general(qh[hh], kb, _DOT_DIMS["nt"], preferred_element_type=F32)
                s = s + nck_ref[2 * pair + hh, pl.ds(j, 1), :]
                p = jnp.exp(s - lse[hh])
                if diag:
                    p = jnp.where(col <= row, p, 0.0)
                dp = lax.dot_general(doh[hh], vb, _DOT_DIMS["nt"], preferred_element_type=F32)
                ds = p * (dp - delta[hh])
                new_rowsum.append(rowsum[hh] + jnp.sum(ds, axis=1, keepdims=True))
                acc = acc + jnp.dot(ds.astype(BF16), jnp.where(hm, kb, jnp.zeros_like(kb)),
                                    preferred_element_type=F32)
            return acc, tuple(new_rowsum)

        zcol = jnp.zeros((t, 1), F32)
        carry = lax.fori_loop(0, i, lambda j, c: step(j, c, False), (jnp.zeros((t, LANES), F32), (zcol, zcol)))
        acc, rowsum = step(i, carry, True)
        dq_ref[...] = acc
        for hh in range(2):
            delta_ref[hh] = delta[hh] + rowsum[hh]

    tile = pl.BlockSpec((t, LANES), lambda p, i: (i, p))
    full = pl.BlockSpec((rows, LANES), lambda p, i: (0, p))
    colspec = pl.BlockSpec((2, t, 1), lambda p, i: (p, i, 0))
    return pl.pallas_call(
        body, grid=(B_HEADS // 2, nb),
        in_specs=[tile, full, full, pl.BlockSpec((B_HEADS, nb, t), lambda p, i: (0, 0, 0)), tile, tile, colspec],
        out_specs=[tile, colspec],
        out_shape=[jax.ShapeDtypeStruct((rows, B_WIDTH), F32), jax.ShapeDtypeStruct((B_HEADS, rows, 1), F32)],
        name=name, compiler_params=_cparams(("parallel", "parallel")),
    )(q, k, v, nck_rows, o, do, lse_col)


def _flash_bwd_dkv(name, q, k, v, nck_col, do, lse_rows, delta_rows):
    rows = q.shape[0]
    t = min(_ATT_T, rows)
    nb = rows // t

    def body(k_ref, v_ref, q_ref, do_ref, nck_ref, lse_ref, delta_ref, dk_ref, dv_ref, dn_ref):
        pair, j = pl.program_id(0), pl.program_id(1)
        k2 = k_ref[...]
        v2 = v_ref[...]
        row, col = _causal(t)
        masks = _head_masks()
        kh = [jnp.where(hm, k2, jnp.zeros_like(k2)) for hm in masks]
        vh = [jnp.where(hm, v2, jnp.zeros_like(v2)) for hm in masks]
        nck = [nck_ref[hh] for hh in range(2)]

        def step(i, carry, diag):
            dk, dv, dn = carry
            start = pl.multiple_of(i * t, t)
            qb = q_ref[pl.ds(start, t), :]
            dob = do_ref[pl.ds(start, t), :].astype(BF16)
            dn_new = []
            for hh, hm in enumerate(masks):
                head = 2 * pair + hh
                st = lax.dot_general(kh[hh], qb, _DOT_DIMS["nt"], preferred_element_type=F32) + nck[hh]
                pt = jnp.exp(st - lse_ref[head, pl.ds(i, 1), :])
                if diag:
                    pt = jnp.where(row <= col, pt, 0.0)
                dpt = lax.dot_general(vh[hh], dob, _DOT_DIMS["nt"], preferred_element_type=F32)
                dst = pt * (dpt - delta_ref[head, pl.ds(i, 1), :])
                dv = dv + jnp.dot(pt.astype(BF16), jnp.where(hm, dob, jnp.zeros_like(dob)),
                                  preferred_element_type=F32)
                dk = dk + jnp.dot(dst.astype(BF16), jnp.where(hm, qb, jnp.zeros_like(qb)),
                                  preferred_element_type=F32)
                dn_new.append(dn[hh] + jnp.sum(dst, axis=1, keepdims=True))
            return dk, dv, tuple(dn_new)

        zero = jnp.zeros((t, LANES), F32)
        zcol = jnp.zeros((t, 1), F32)
        carry = step(j, (zero, zero, (zcol, zcol)), True)
        dk, dv, dn = lax.fori_loop(j + 1, nb, lambda i, c: step(i, c, False), carry)
        dk_ref[...] = dk
        dv_ref[...] = dv
        for hh in range(2):
            dn_ref[hh] = dn[hh]

    tile = pl.BlockSpec((t, LANES), lambda p, j: (j, p))
    full = pl.BlockSpec((rows, LANES), lambda p, j: (0, p))
    colspec = pl.BlockSpec((2, t, 1), lambda p, j: (p, j, 0))
    rowspec = pl.BlockSpec((B_HEADS, nb, t), lambda p, j: (0, 0, 0))
    big = jax.ShapeDtypeStruct((rows, B_WIDTH), F32)
    return pl.pallas_call(
        body, grid=(B_HEADS // 2, nb),
        in_specs=[tile, tile, full, full, colspec, rowspec, rowspec], out_specs=[tile, tile, colspec],
        out_shape=[big, big, jax.ShapeDtypeStruct((B_HEADS, rows, 1), F32)],
        name=name, compiler_params=_cparams(("parallel", "parallel")),
    )(k, v, q, do, nck_col, lse_rows, delta_rows)


_S5_ROWS = 256


def _s5_discretize(a_re, a_im, log_dt, b_re, b_im):
    dt = jnp.exp(log_dt)[:, None]
    mag = jnp.exp(a_re * dt)
    ab_re, ab_im = mag * jnp.cos(a_im * dt), mag * jnp.sin(a_im * dt)
    den = a_re * a_re + a_im * a_im
    nr, ni = ab_re - 1.0, ab_im
    cr = (nr * a_re + ni * a_im) / den
    ci = (ni * a_re - nr * a_im) / den
    bb_re = cr[..., None] * b_re - ci[..., None] * b_im
    bb_im = cr[..., None] * b_im + ci[..., None] * b_re
    return ab_re, ab_im, bb_re, bb_im


def _s5_block_diag(m):
    g, r, c = m.shape
    mb = m.reshape(S5_BLOCKS, 8, r, c)
    eye = jnp.eye(8, dtype=m.dtype)
    return jnp.einsum("bgrc,gh->bgrhc", mb, eye).reshape(S5_BLOCKS, 8 * r, 8 * c)


def _s5_block_diag_extract(m, r, c):
    mb = m.reshape(S5_BLOCKS, 8, r, 8, c)
    return jnp.einsum("bgrhc,gh->bgrc", mb, jnp.eye(8, dtype=m.dtype)).reshape(S5_GROUPS, r, c)


def _s5_tables(ab_re, ab_im, r):
    ar = jnp.broadcast_to(ab_re.reshape(1, -1), (r, S5_GROUPS * S5_STATE))
    ai = jnp.broadcast_to(ab_im.reshape(1, -1), (r, S5_GROUPS * S5_STATE))

    def mul(x, y):
        return x[0] * y[0] - x[1] * y[1], x[0] * y[1] + x[1] * y[0]

    return lax.associative_scan(mul, (ar, ai), axis=0)


def _scan_step(xr, xi, ar, ai, s, row, up):
    r = xr.shape[0]
    if up:
        ai = -ai
    if s < 8:
        if up:
            sr = jnp.where(row < r - s, pltpu.roll(xr, r - s, 0), 0.0)
            si = jnp.where(row < r - s, pltpu.roll(xi, r - s, 0), 0.0)
        else:
            sr = jnp.where(row >= s, pltpu.roll(xr, s, 0), 0.0)
            si = jnp.where(row >= s, pltpu.roll(xi, s, 0), 0.0)
        return xr + (ar * sr - ai * si), xi + (ar * si + ai * sr)
    if up:
        (dr, di), (sr, si) = (xr[:r - s], xi[:r - s]), (xr[s:], xi[s:])
        nr, ni = dr + (ar * sr - ai * si), di + (ar * si + ai * sr)
        return jnp.concatenate([nr, xr[r - s:]], axis=0), jnp.concatenate([ni, xi[r - s:]], axis=0)
    (dr, di), (sr, si) = (xr[s:], xi[s:]), (xr[:r - s], xi[:r - s])
    nr, ni = dr + (ar * sr - ai * si), di + (ar * si + ai * sr)
    return jnp.concatenate([xr[:s], nr], axis=0), jnp.concatenate([xi[:s], ni], axis=0)


def _s5_scan_tile(u_ref, bcat_ref, pr_ref, pi_ref, cin_r, cin_i, row):
    r = u_ref.shape[0]
    bu = jnp.dot(u_ref[...], bcat_ref[...], preferred_element_type=F32)
    xr, xi = bu[:, :S5_LANES], bu[:, S5_LANES:]
    ar, ai = pr_ref[0:1, :], pi_ref[0:1, :]
    first = lax.broadcasted_iota(jnp.int32, (8, 1), 0) == 0
    xr = jnp.concatenate([xr[:8] + jnp.where(first, ar * cin_r - ai * cin_i, 0.0), xr[8:]], axis=0)
    xi = jnp.concatenate([xi[:8] + jnp.where(first, ar * cin_i + ai * cin_r, 0.0), xi[8:]], axis=0)
    s = 1
    while s < r:
        xr, xi = _scan_step(xr, xi, pr_ref[s - 1:s, :], pi_ref[s - 1:s, :], s, row, False)
        s *= 2
    return xr, xi


def _s5_fwd(name, u, bcat, ccat, pw_re, pw_im):
    rows = u.shape[0]
    r = pw_re.shape[0]
    nt = rows // r

    def body(u_ref, bcat_ref, ccat_ref, pr_ref, pi_ref, y_ref, xin_ref, carry):
        @pl.when(pl.program_id(1) == 0)
        def _():
            carry[...] = jnp.zeros(carry.shape, carry.dtype)

        row = lax.broadcasted_iota(jnp.int32, (r, 1), 0)
        xin_ref[...] = carry[...]
        xr, xi = _s5_scan_tile(u_ref, bcat_ref, pr_ref, pi_ref, carry[0:1, :S5_LANES], carry[0:1, S5_LANES:], row)
        xcat = jnp.concatenate([xr, xi], axis=1)
        carry[...] = jnp.broadcast_to(xcat[r - 1:r, :], carry.shape)
        y_ref[...] = jnp.dot(xcat.astype(BF16), ccat_ref[...], preferred_element_type=F32)

    return pl.pallas_call(
        body, grid=(S5_BLOCKS, nt),
        in_specs=[pl.BlockSpec((r, LANES), lambda b, i: (i, b)),
                  pl.BlockSpec((None, LANES, 2 * S5_LANES), lambda b, i: (b, 0, 0)),
                  pl.BlockSpec((None, 2 * S5_LANES, LANES), lambda b, i: (b, 0, 0)),
                  pl.BlockSpec((r, S5_LANES), lambda b, i: (0, b)), pl.BlockSpec((r, S5_LANES), lambda b, i: (0, b))],
        out_specs=[pl.BlockSpec((r, LANES), lambda b, i: (i, b)),
                   pl.BlockSpec((None, 8, 2 * S5_LANES), lambda b, i: (b, i, 0))],
        out_shape=[jax.ShapeDtypeStruct((rows, D_MODEL), F32),
                   jax.ShapeDtypeStruct((S5_BLOCKS, 8 * nt, 2 * S5_LANES), F32)],
        scratch_shapes=[pltpu.VMEM((8, 2 * S5_LANES), F32)], name=name,
        compiler_params=_cparams(("parallel", "arbitrary")),
    )(u, bcat, ccat, pw_re, pw_im)


def _s5_bwd(name, u, dy, xin, bcat, ccat, pw_re, pw_im):
    rows = u.shape[0]
    r = pw_re.shape[0]
    nt = rows // r

    def body(u_ref, dy_ref, xin_ref, bcat_ref, ccat_ref, pr_ref, pi_ref,
             du_ref, db_ref, dc_ref, dar_ref, dai_ref, carry):
        @pl.when(pl.program_id(1) == 0)
        def _():
            carry[...] = jnp.zeros(carry.shape, carry.dtype)
            for ref in (db_ref, dc_ref, dar_ref, dai_ref):
                ref[...] = jnp.zeros(ref.shape, ref.dtype)

        row = lax.broadcasted_iota(jnp.int32, (r, 1), 0)
        cin_r, cin_i = xin_ref[0:1, :S5_LANES], xin_ref[0:1, S5_LANES:]
        xr, xi = _s5_scan_tile(u_ref, bcat_ref, pr_ref, pi_ref, cin_r, cin_i, row)
        dy_b = dy_ref[...].astype(BF16)
        xcat = jnp.concatenate([xr, xi], axis=1).astype(BF16)
        dc_ref[...] += lax.dot_general(xcat, dy_b, _DOT_DIMS["tn"], preferred_element_type=F32)
        g = lax.dot_general(dy_b, ccat_ref[...], _DOT_DIMS["nt"], preferred_element_type=F32)
        lr, li = g[:, :S5_LANES], g[:, S5_LANES:]
        nr, ni = carry[0:1, :S5_LANES], carry[0:1, S5_LANES:]
        ar, ai = pr_ref[0:1, :], pi_ref[0:1, :]
        final = lax.broadcasted_iota(jnp.int32, (8, 1), 0) == 7
        lr = jnp.concatenate([lr[:r - 8], lr[r - 8:] + jnp.where(final, ar * nr + ai * ni, 0.0)], axis=0)
        li = jnp.concatenate([li[:r - 8], li[r - 8:] + jnp.where(final, ar * ni - ai * nr, 0.0)], axis=0)
        s = 1
        while s < r:
            lr, li = _scan_step(lr, li, pr_ref[s - 1:s, :], pi_ref[s - 1:s, :], s, row, True)
            s *= 2
        carry[...] = jnp.broadcast_to(jnp.concatenate([lr[0:1, :], li[0:1, :]], axis=1), carry.shape)
        lcat = jnp.concatenate([lr, li], axis=1).astype(BF16)
        du_ref[...] = lax.dot_general(lcat, bcat_ref[...], _DOT_DIMS["nt"], preferred_element_type=F32)
        db_ref[...] += lax.dot_general(u_ref[...], lcat, _DOT_DIMS["tn"], preferred_element_type=F32)
        pxr = jnp.where(row == 0, cin_r, pltpu.roll(xr, 1, 0))
        pxi = jnp.where(row == 0, cin_i, pltpu.roll(xi, 1, 0))
        dar_ref[...] += jnp.sum((lr * pxr + li * pxi).reshape(r // 8, 8, S5_LANES), axis=0)
        dai_ref[...] += jnp.sum((li * pxr - lr * pxi).reshape(r // 8, 8, S5_LANES), axis=0)

    rev = lambda b, i: (nt - 1 - i, b)
    tab = pl.BlockSpec((r, S5_LANES), lambda b, i: (0, b))
    return pl.pallas_call(
        body, grid=(S5_BLOCKS, nt),
        in_specs=[pl.BlockSpec((r, LANES), rev), pl.BlockSpec((r, LANES), rev),
                  pl.BlockSpec((None, 8, 2 * S5_LANES), lambda b, i: (b, nt - 1 - i, 0)),
                  pl.BlockSpec((None, LANES, 2 * S5_LANES), lambda b, i: (b, 0, 0)),
                  pl.BlockSpec((None, 2 * S5_LANES, LANES), lambda b, i: (b, 0, 0)), tab, tab],
        out_specs=[pl.BlockSpec((r, LANES), rev),
                   pl.BlockSpec((None, LANES, 2 * S5_LANES), lambda b, i: (b, 0, 0)),
                   pl.BlockSpec((None, 2 * S5_LANES, LANES), lambda b, i: (b, 0, 0)),
                   pl.BlockSpec((None, 8, S5_LANES), lambda b, i: (b, 0, 0)),
                   pl.BlockSpec((None, 8, S5_LANES), lambda b, i: (b, 0, 0))],
        out_shape=[jax.ShapeDtypeStruct((rows, D_MODEL), F32),
                   jax.ShapeDtypeStruct((S5_BLOCKS, LANES, 2 * S5_LANES), F32),
                   jax.ShapeDtypeStruct((S5_BLOCKS, 2 * S5_LANES, LANES), F32),
                   jax.ShapeDtypeStruct((S5_BLOCKS, 8, S5_LANES), F32),
                   jax.ShapeDtypeStruct((S5_BLOCKS, 8, S5_LANES), F32)],
        scratch_shapes=[pltpu.VMEM((8, 2 * S5_LANES), F32)], name=name,
        compiler_params=_cparams(("parallel", "arbitrary")),
    )(u, dy, xin, bcat, ccat, pw_re, pw_im)


def _ones_gain():
    return jnp.ones((1, D_MODEL), F32)


def _channel_fwd(i, x1, p_i, w, rp, hn=None, next_norm=None):
    if hn is None:
        hn, = _rmsnorm_fwd(f"ffn_norm_{i}", x1, rp["norm_ffn"][i][None], [BF16])
    hg = _mm(f"ffn_up_g_{i}", hn, w["up_g"], tn=1408)
    hu = _mm(f"ffn_up_u_{i}", hn, w["up_u"], tn=1408)
    a = _convffn_fwd(f"ffn_conv_{i}", hg, hu, w["cw_g"], w["cw_u"], w["cb_g"], w["cb_u"])
    x2, r = _mm(f"ffn_down_{i}", a, w["down"], res=x1, tk=1408, norm_gain=_ones_gain())
    zg = _mm(f"ple_gate_{i}", r, w["ple_gate"])
    pp = _mm(f"ple_proj_{i}", p_i, w["ple_proj"])
    saved = dict(x1=x1, hn=hn, hg=hg, hu=hu, a=a, x2=x2, r=r, zg=zg, pp=pp, p_i=p_i)
    if next_norm is None:
        x3, = _rows(f"ple_out_{i}", lambda xv, zv, pv: (xv + _sigmoid(zv) * pv,), [x2, zg, pp], [], [(D_MODEL, F32)])
        return x3, None, saved
    gain, dtypes = next_norm

    def ple_out_norm(xv, zv, pv, gv):
        x3v = xv + _sigmoid(zv) * pv
        h = (x3v * _rstd(x3v)) * gv
        return (x3v,) + tuple(h for _ in dtypes)

    x3, *h_next = _rows(f"ple_out_{i}", ple_out_norm, [x2, zg, pp], [gain],
                        [(D_MODEL, F32)] + [(D_MODEL, dt) for dt in dtypes])
    return x3, h_next, saved


def _channel_bwd(i, dx3, sv, w, rp):
    def ple_bwd(dv, zv, pv):
        gate = _sigmoid(zv)
        return dv * gate, (dv * pv) * (gate * (1.0 - gate))

    dpp, dzg = _rows(f"ple_out_bwd_{i}", ple_bwd, [dx3, sv["zg"], sv["pp"]], [], [(D_MODEL, BF16), (D_MODEL, BF16)])
    g = {}
    g["ple_proj"] = _mm(f"ple_proj_dw_{i}", sv["p_i"], dpp, "tn")
    g["ple_gate"] = _mm(f"ple_gate_dw_{i}", sv["r"], dzg, "tn")
    dr = _mm(f"ple_gate_dx_{i}", dzg, w["ple_gate"], "nt")
    dx2, _ = _rmsnorm_bwd(f"ple_norm_bwd_{i}", sv["x2"], dr, dx3, _ones_gain())
    da = _mm(f"ffn_down_dx_{i}", dx2, w["down"], "nt", tn=1408)
    g["down"] = _mm(f"ffn_down_dw_{i}", sv["a"], dx2, "tn", tm=1408)
    dhg, dhu, g["cw_g"], g["cw_u"], dbg, dbu = _convffn_bwd(
        f"ffn_conv_bwd_{i}", da, sv["hg"], sv["hu"], w["cw_g"], w["cw_u"], w["cb_g"], w["cb_u"])
    g["conv_b"] = jnp.concatenate([dbg, dbu], axis=1)[0]
    g["up_g"] = _mm(f"ffn_up_g_dw_{i}", sv["hn"], dhg, "tn", tn=1408)
    g["up_u"] = _mm(f"ffn_up_u_dw_{i}", sv["hn"], dhu, "tn", tn=1408)
    dhn = _mm(f"ffn_up_g_dx_{i}", dhg, w["up_g"], "nt", tk=1408)
    dhn = _mm(f"ffn_up_u_dx_{i}", dhu, w["up_u"], "nt", res=dhn, tk=1408)
    dx1, dgf = _rmsnorm_bwd(f"ffn_norm_bwd_{i}", sv["x1"], dhn, dx2, rp["norm_ffn"][i][None])
    g["norm_ffn"] = dgf[0]
    return dx1, g


def _even_consts(e, rp):
    tri = jnp.tril(jnp.ones((A_CHUNK, A_CHUNK), dtype=bool))
    w_tril = jnp.where(tri[None], rp["ev_w_spatial"][e], 0.0).astype(BF16)
    b_exp = jnp.broadcast_to(rp["ev_b_spatial"][e][:, :, None], (A_GROUPS, A_CHUNK, LANES))
    seg = np.arange(B_WIDTH) // B_HEAD_DIM
    bd = jnp.asarray((seg[:, None] == seg[None, :]).astype(np.float32) / B_HEAD_DIM)
    return dict(
        tri=tri, w_tril=w_tril, w_tril_t=jnp.swapaxes(w_tril, 1, 2), b_exp=b_exp, bd=bd,
        v_gain=rp["ev_v_norm"][e][None], qg=jnp.tile(rp["ev_q_norm"][e], B_HEADS)[None],
        kg=jnp.tile(rp["ev_k_norm"][e], B_HEADS)[None],
        bf=jnp.pad(rp["ev_b_fgate"][e], (0, LANES - B_HEADS))[None])


def _even_fwd(i, x, w, rp, h_in=None):
    e = i // 2
    c = _even_consts(e, rp)
    rows = x.shape[0]
    t = min(_ATT_T, rows)
    h, = h_in if h_in is not None else _rmsnorm_fwd(f"mix_norm_{i}", x, rp["norm_mix"][i][None], [BF16])
    zuv = _mm(f"in_uv_{i}", h, w["in_uv"])
    zqkv = _mm(f"in_qkv_{i}", h, w["in_qkv"], tn=768)
    zf = _mm(f"in_f_{i}", h, w["in_f"])
    ya = _gmlp_fwd(f"gmlp_{i}", zuv, c["v_gain"], c["w_tril"], c["b_exp"])
    q, k, v, ls = _qkv_prep_fwd(f"qkv_prep_{i}", zqkv, zf, c["qg"], c["kg"], c["bf"], c["bd"])
    csum = _cumsum_rows(f"forget_cumsum_{i}", ls)
    nck = -csum[:, :B_HEADS].T
    nck_rows = nck.reshape(B_HEADS, rows // t, t)
    nck_col = nck.reshape(B_HEADS, rows, 1)
    o, lse = _flash_fwd(f"attn_{i}", q, k, v, nck_rows)
    x1 = _mm(f"out_a_{i}", ya, w["out_a"], res=x)
    x1, hn = _mm(f"out_b_{i}", o, w["out_b"], res=x1, norm_gain=rp["norm_ffn"][i][None])
    return x1, hn, dict(x=x, h=h, zuv=zuv, zqkv=zqkv, zf=zf, ya=ya, q=q, k=k, v=v, nck_rows=nck_rows,
                          nck_col=nck_col, o=o, lse=lse)


def _even_bwd(i, dx1, sv, w, rp):
    e = i // 2
    c = _even_consts(e, rp)
    rows = dx1.shape[0]
    t = min(_ATT_T, rows)
    nb = rows // t
    g = {}
    dya = _mm(f"out_a_dx_{i}", dx1, w["out_a"], "nt")
    do = _mm(f"out_b_dx_{i}", dx1, w["out_b"], "nt")
    g["out_a"] = _mm(f"out_a_dw_{i}", sv["ya"], dx1, "tn")
    g["out_b"] = _mm(f"out_b_dw_{i}", sv["o"], dx1, "tn")
    dq, delta = _flash_bwd_dq(f"attn_dq_{i}", sv["q"], sv["k"], sv["v"], sv["nck_rows"], sv["o"], do, sv["lse"])
    dk, dv, dn = _flash_bwd_dkv(f"attn_dkv_{i}", sv["q"], sv["k"], sv["v"], sv["nck_col"], do,
                                sv["lse"].reshape(B_HEADS, nb, t), delta.reshape(B_HEADS, nb, t))
    dcs = jnp.pad(-dn.reshape(B_HEADS, rows).T, ((0, 0), (0, LANES - B_HEADS)))
    dls = _cumsum_rows(f"forget_cumsum_bwd_{i}", dcs, reverse=True)
    dzqkv, dzf, dqg, dkg, dbf = _qkv_prep_bwd(f"qkv_prep_bwd_{i}", sv["zqkv"], sv["zf"], dq, dk, dv, dls,
                                              c["qg"], c["kg"], c["bf"], c["bd"])
    dzuv, dws, dbs, dvg = _gmlp_bwd(f"gmlp_bwd_{i}", sv["zuv"], dya, c["v_gain"], c["w_tril"], c["w_tril_t"],
                                    c["b_exp"])
    g["in_uv"] = _mm(f"in_uv_dw_{i}", sv["h"], dzuv, "tn")
    g["in_qkv"] = _mm(f"in_qkv_dw_{i}", sv["h"], dzqkv, "tn", tn=768)
    g["in_f"] = _mm(f"in_f_dw_{i}", sv["h"], dzf, "tn")
    dh = _mm(f"in_uv_dx_{i}", dzuv, w["in_uv"], "nt")
    dh = _mm(f"in_qkv_dx_{i}", dzqkv, w["in_qkv"], "nt", res=dh, tk=768)
    dh = _mm(f"in_f_dx_{i}", dzf, w["in_f"], "nt", res=dh)
    dx, dgm = _rmsnorm_bwd(f"mix_norm_bwd_{i}", sv["x"], dh, dx1, rp["norm_mix"][i][None])
    g["norm_mix"] = dgm[0]
    g["ev_b_fgate"] = dbf[0, :B_HEADS]
    g["ev_q_norm"] = dqg.reshape(B_HEADS, B_HEAD_DIM).sum(axis=0)
    g["ev_k_norm"] = dkg.reshape(B_HEADS, B_HEAD_DIM).sum(axis=0)
    g["ev_v_norm"] = dvg[0]
    g["ev_w_spatial"] = jnp.where(c["tri"][None], dws, 0.0)
    g["ev_b_spatial"] = dbs.sum(axis=-1)
    return dx, g


def _s5_consts(o, rp, r):
    prm = (rp["od_a_re"][o], rp["od_a_im"][o], rp["od_log_dt"][o], rp["od_b_re"][o], rp["od_b_im"][o])
    (ab_re, ab_im, bb_re, bb_im), vjp = jax.vjp(_s5_discretize, *prm)
    bcat = jnp.concatenate([_s5_block_diag(bb_re.transpose(0, 2, 1)), _s5_block_diag(bb_im.transpose(0, 2, 1))], axis=2)
    c_re, c_im = rp["od_c_re"][o], rp["od_c_im"][o]
    ccat = jnp.concatenate([_s5_block_diag(c_re.transpose(0, 2, 1)), _s5_block_diag(-c_im.transpose(0, 2, 1))], axis=1)
    pw_re, pw_im = _s5_tables(ab_re, ab_im, r)
    return dict(vjp=vjp, bcat=bcat.astype(BF16), ccat=ccat.astype(BF16), pw_re=pw_re, pw_im=pw_im)


def _odd_fwd(i, x, w, rp, h_in=None):
    o = i // 2
    rows = x.shape[0]
    c = _s5_consts(o, rp, min(_S5_ROWS, rows))
    hb, hf = h_in if h_in is not None else _rmsnorm_fwd(f"mix_norm_{i}", x, rp["norm_mix"][i][None], [BF16, F32])
    ys, xin = _s5_fwd(f"s5_{i}", hb, c["bcat"], c["ccat"], c["pw_re"], c["pw_im"])

    def skip_gelu(yv, hv, dv):
        y = yv + dv * hv
        return y, _gelu(y)

    y, ge = _rows(f"s5_skip_gelu_{i}", skip_gelu, [ys, hf], [w["od_d"]], [(D_MODEL, F32), (D_MODEL, BF16)])
    gl = _mm(f"glu_{i}", ge, w["glu"])

    def glu_out(xv, gv, nv):
        x1v = xv + gv[:, :D_MODEL] * _sigmoid(gv[:, D_MODEL:])
        return x1v, (x1v * _rstd(x1v)) * nv

    x1, hn = _rows(f"glu_out_{i}", glu_out, [x, gl], [rp["norm_ffn"][i][None]], [(D_MODEL, F32), (D_MODEL, BF16)])
    return x1, hn, dict(x=x, hb=hb, hf=hf, xin=xin, y=y, ge=ge, gl=gl, c=c)


def _odd_bwd(i, dx1, sv, w, rp):
    o = i // 2
    c = sv["c"]
    g = {}

    def glu_bwd(dv, gv):
        ga, gb = gv[:, :D_MODEL], gv[:, D_MODEL:]
        sg = _sigmoid(gb)
        return (jnp.concatenate([dv * sg, (dv * ga) * (sg * (1.0 - sg))], axis=1),)

    dgl, = _rows(f"glu_out_bwd_{i}", glu_bwd, [dx1, sv["gl"]], [], [(2 * D_MODEL, BF16)])
    g["glu"] = _mm(f"glu_dw_{i}", sv["ge"], dgl, "tn")
    dge = _mm(f"glu_dx_{i}", dgl, w["glu"], "nt")

    def gelu_bwd(dv, yv, hv):
        dy = dv * _gelu_grad(yv)
        return dy, jnp.sum(dy * hv, axis=0, keepdims=True)

    dy, dd = _rows(f"s5_skip_gelu_bwd_{i}", gelu_bwd, [dge, sv["y"], sv["hf"]], [], [(D_MODEL, F32)],
                   accs=[(1, D_MODEL)])
    g["od_d"] = dd[0]
    du, db, dc, dar, dai = _s5_bwd(f"s5_bwd_{i}", sv["hb"], dy, sv["xin"], c["bcat"], c["ccat"], c["pw_re"],
                                   c["pw_im"])
    dab_re = dar.sum(axis=1).reshape(S5_GROUPS, S5_STATE)
    dab_im = dai.sum(axis=1).reshape(S5_GROUPS, S5_STATE)
    dbb_re = _s5_block_diag_extract(db[:, :, :S5_LANES], S5_GROUP_CH, S5_STATE).transpose(0, 2, 1)
    dbb_im = _s5_block_diag_extract(db[:, :, S5_LANES:], S5_GROUP_CH, S5_STATE).transpose(0, 2, 1)
    g["od_a_re"], g["od_a_im"], g["od_log_dt"], g["od_b_re"], g["od_b_im"] = c["vjp"]((dab_re, dab_im, dbb_re, dbb_im))
    g["od_c_re"] = _s5_block_diag_extract(dc[:, :S5_LANES, :], S5_STATE, S5_GROUP_CH).transpose(0, 2, 1)
    g["od_c_im"] = -_s5_block_diag_extract(dc[:, S5_LANES:, :], S5_STATE, S5_GROUP_CH).transpose(0, 2, 1)

    def norm_bwd(xv, duv, dyv, drv, gv, dv):
        dh = duv + dv * dyv
        r = _rstd(xv)
        xh = xv * r
        dhg = dh * gv
        dx = drv + r * (dhg - xh * jnp.mean(dhg * xh, axis=-1, keepdims=True))
        return dx, jnp.sum(dh * xh, axis=0, keepdims=True)

    dx, dgm = _rows(f"mix_norm_bwd_{i}", norm_bwd, [sv["x"], du, dy, dx1], [rp["norm_mix"][i][None], w["od_d"]],
                    [(D_MODEL, F32)], accs=[(1, D_MODEL)])
    g["norm_mix"] = dgm[0]
    return dx, g


def _local_step(x, p, target, lw, rp):
    saved = []
    h_next = None
    for i in range(DEPTH):
        x, hn, s_mix = (_even_fwd if i % 2 == 0 else _odd_fwd)(i, x, lw[i], rp, h_next)
        nxt = None
        if i + 1 < DEPTH:
            nxt = (rp["norm_mix"][i + 1][None], [BF16, F32] if (i + 1) % 2 else [BF16])
        x, h_next, s_ch = _channel_fwd(i, x, p[i], lw[i], rp, hn, nxt)
        saved.append((s_mix, s_ch))

    def loss_fn(yv, tv):
        diff = yv - tv
        return diff * (1.0 / D_MODEL), jnp.sum(diff * diff, axis=0, keepdims=True)

    dx, sq = _rows("loss", loss_fn, [x, target], [], [(D_MODEL, F32)], accs=[(1, D_MODEL)])
    loss = 0.5 * jnp.sum(sq) / D_MODEL
    grads = [None] * DEPTH
    for i in reversed(range(DEPTH)):
        s_mix, s_ch = saved[i]
        dx, g_ch = _channel_bwd(i, dx, s_ch, lw[i], rp)
        dx, g_mix = (_even_bwd if i % 2 == 0 else _odd_bwd)(i, dx, s_mix, lw[i], rp)
        grads[i] = {**g_ch, **g_mix}
    return loss, dx, grads


WEIGHT_ORDER = ["norm_mix", "norm_ffn", "ev_w_in", "ev_b_fgate", "ev_q_norm", "ev_k_norm", "ev_v_norm", "ev_w_spatial",
                "ev_b_spatial", "ev_w_out", "od_a_re", "od_a_im", "od_log_dt", "od_b_re", "od_b_im", "od_c_re",
                "od_c_im", "od_d", "od_w_glu", "ffn_w_up", "ffn_conv_w", "ffn_conv_b", "ffn_w_down", "ple_w_proj",
                "ple_w_gate"]
SHARD_AXIS = {"ev_w_in": 2, "ev_w_out": 1, "od_d": 1, "od_w_glu": 2, "ffn_w_up": 2, "ffn_conv_w": 2, "ffn_w_down": 1,
              "ple_w_proj": 2, "ple_w_gate": 1}
BIG_WEIGHTS = [n for n in WEIGHT_ORDER if n in SHARD_AXIS]
SMALL_WEIGHTS = [n for n in WEIGHT_ORDER if n not in SHARD_AXIS]
KEPT_F32 = ("od_d", "ffn_conv_w")
IN_UV, IN_QKV_END, IN_COLS = 2 * A_WIDTH, 2 * A_WIDTH + 3 * B_WIDTH, 2 * A_WIDTH + 3 * B_WIDTH + B_HEADS


def _layer_weights(i, full, rp):
    w = {}
    up, cw, cb = full["ffn_w_up"][i], full["ffn_conv_w"][i], rp["ffn_conv_b"][i][None]
    w["up_g"], w["up_u"] = up[:, :D_FF], up[:, D_FF:]
    w["cw_g"], w["cw_u"] = cw[:, :D_FF], cw[:, D_FF:]
    w["cb_g"], w["cb_u"] = cb[:, :D_FF], cb[:, D_FF:]
    w["down"], w["ple_proj"], w["ple_gate"] = full["ffn_w_down"][i], full["ple_w_proj"][i], full["ple_w_gate"][i]
    if i % 2 == 0:
        win, wout = full["ev_w_in"][i // 2], full["ev_w_out"][i // 2]
        w["in_uv"], w["in_qkv"] = win[:, :IN_UV], win[:, IN_UV:IN_QKV_END]
        w["in_f"] = jnp.pad(win[:, IN_QKV_END:], ((0, 0), (0, LANES - B_HEADS)))
        w["out_a"], w["out_b"] = wout[:A_WIDTH], wout[A_WIDTH:]
    else:
        w["od_d"], w["glu"] = full["od_d"][i // 2][None], full["od_w_glu"][i // 2]
    return w


def _full_grads(grads):
    ev, od = [grads[i] for i in range(0, DEPTH, 2)], [grads[i] for i in range(1, DEPTH, 2)]
    out = {
        "norm_mix": jnp.stack([g["norm_mix"] for g in grads]), "norm_ffn": jnp.stack([g["norm_ffn"] for g in grads]),
        "ev_w_in": jnp.stack([jnp.concatenate([g["in_uv"], g["in_qkv"], g["in_f"][:, :B_HEADS]], axis=1) for g in ev]),
        "ev_w_out": jnp.stack([jnp.concatenate([g["out_a"], g["out_b"]], axis=0) for g in ev]),
        "od_w_glu": jnp.stack([g["glu"] for g in od]),
        "ffn_w_up": jnp.stack([jnp.concatenate([g["up_g"], g["up_u"]], axis=1) for g in grads]),
        "ffn_conv_w": jnp.stack([jnp.concatenate([g["cw_g"], g["cw_u"]], axis=1) for g in grads]),
        "ffn_conv_b": jnp.stack([g["conv_b"] for g in grads]),
        "ffn_w_down": jnp.stack([g["down"] for g in grads]),
        "ple_w_proj": jnp.stack([g["ple_proj"] for g in grads]),
        "ple_w_gate": jnp.stack([g["ple_gate"] for g in grads]),
    }
    for n in ("ev_b_fgate", "ev_q_norm", "ev_k_norm", "ev_v_norm", "ev_w_spatial", "ev_b_spatial"):
        out[n] = jnp.stack([g[n] for g in ev])
    for n in ("od_a_re", "od_a_im", "od_log_dt", "od_b_re", "od_b_im", "od_c_re", "od_c_im", "od_d"):
        out[n] = jnp.stack([g[n] for g in od])
    return out


def _pack(arrs, row_multiple):
    flat = jnp.concatenate([a.reshape(-1) for a in arrs])
    rows = -(-flat.shape[0] // (PACK_W * row_multiple)) * row_multiple
    return jnp.pad(flat, (0, rows * PACK_W - flat.shape[0])).reshape(rows, PACK_W)


def _unpack(buf, shapes):
    flat = buf.reshape(-1)
    out, at = [], 0
    for s in shapes:
        n = int(np.prod(s))
        out.append(flat[at:at + n].reshape(s))
        at += n
    return out


def _shard(name, a, k):
    ax = SHARD_AXIS[name]
    n = a.shape[ax] // N_CHIPS
    return lax.slice_in_dim(a, k * n, (k + 1) * n, axis=ax)


_ANY = pl.BlockSpec(memory_space=pl.ANY)


def _mesh_pos():
    return lax.axis_index("x"), lax.axis_index("y"), lax.axis_index("c")


def _other_chips(x, y):
    return [(1 - x, y), (x, 1 - y), (1 - x, 1 - y)]


def _gather_shards(name, shards):
    n = len(shards)

    def body(*refs):
        ins, outs = refs[:n], refs[n:2 * n]
        send_sems, recv_sems, local_sems = refs[2 * n:]
        x, y, c = _mesh_pos()
        sibling = (x, y, 1 - c)
        chips = _other_chips(x, y)

        def part(a, k, hc):
            half = shards[a].shape[0] // 2
            return outs[a].at[k, pl.ds(hc * half, half), :]

        def copy(sem, src, dst, to):
            return pltpu.make_async_remote_copy(src_ref=src, dst_ref=dst, send_sem=send_sems.at[sem],
                                                recv_sem=recv_sems.at[sem], device_id=to, device_id_type=MESH)

        local, sent, passed = [], [], []
        for a in range(n):
            half = shards[a].shape[0] // 2
            local.append(pltpu.make_async_copy(ins[a], outs[a].at[2 * x + y], local_sems.at[a]))
            local[-1].start()
            for j, (cx, cy) in enumerate(chips):
                sent.append(copy(6 * a + j, ins[a].at[pl.ds(c * half, half), :], part(a, 2 * x + y, c), (cx, cy, c)))
                sent[-1].start()
        for a in range(n):
            for j, (cx, cy) in enumerate(chips):
                blk = part(a, 2 * cx + cy, c)
                copy(6 * a + j, blk, blk, (cx, cy, c)).wait_recv()
                passed.append(copy(6 * a + 3 + j, blk, blk, sibling))
                passed[-1].start()
        for a in range(n):
            for j, (cx, cy) in enumerate(chips):
                blk = part(a, 2 * cx + cy, 1 - c)
                copy(6 * a + 3 + j, blk, blk, sibling).wait_recv()
        for cp in sent + passed:
            cp.wait_send()
        for cp in local:
            cp.wait()

    return pl.pallas_call(
        body, out_shape=[jax.ShapeDtypeStruct((N_CHIPS,) + s.shape, s.dtype) for s in shards],
        in_specs=[_ANY] * n, out_specs=[_ANY] * n,
        scratch_shapes=[pltpu.SemaphoreType.DMA((6 * n,)), pltpu.SemaphoreType.DMA((6 * n,)),
                        pltpu.SemaphoreType.DMA((n,))],
        name=name,
    )(*shards)


def _swap_halves(name, arrs):
    n = len(arrs)

    def body(*refs):
        ins, outs = refs[:n], refs[n:2 * n]
        send_sems, recv_sems = refs[2 * n:]
        x, y, c = _mesh_pos()
        cps = []
        for a in range(n):
            half = arrs[a].shape[1] // 2
            cps.append(pltpu.make_async_remote_copy(
                src_ref=ins[a].at[:, pl.ds((1 - c) * half, half), :], dst_ref=outs[a], send_sem=send_sems.at[a],
                recv_sem=recv_sems.at[a], device_id=(x, y, 1 - c), device_id_type=MESH))
            cps[-1].start()
        for cp in cps:
            cp.wait()

    return pl.pallas_call(
        body, out_shape=[jax.ShapeDtypeStruct((a.shape[0], a.shape[1] // 2, a.shape[2]), a.dtype) for a in arrs],
        in_specs=[_ANY] * n, out_specs=[_ANY] * n,
        scratch_shapes=[pltpu.SemaphoreType.DMA((n,)), pltpu.SemaphoreType.DMA((n,))], name=name,
    )(*arrs)


def _send_to_owner_chips(name, arrs):
    n = len(arrs)

    def body(*refs):
        ins, outs = refs[:n], refs[n:2 * n]
        send_sems, recv_sems = refs[2 * n:]
        x, y, c = _mesh_pos()
        cps = []
        for a in range(n):
            for j, (cx, cy) in enumerate(_other_chips(x, y)):
                cps.append(pltpu.make_async_remote_copy(
                    src_ref=ins[a].at[2 * cx + cy], dst_ref=outs[a].at[j], send_sem=send_sems.at[3 * a + j],
                    recv_sem=recv_sems.at[3 * a + j], device_id=(cx, cy, c), device_id_type=MESH))
                cps[-1].start()
        for cp in cps:
            cp.wait()

    return pl.pallas_call(
        body, out_shape=[jax.ShapeDtypeStruct((3,) + a.shape[1:], a.dtype) for a in arrs],
        in_specs=[_ANY] * n, out_specs=[_ANY] * n,
        scratch_shapes=[pltpu.SemaphoreType.DMA((3 * n,)), pltpu.SemaphoreType.DMA((3 * n,))], name=name,
    )(*arrs)


def _share_halves(name, arrs):
    n = len(arrs)

    def body(*refs):
        ins, outs = refs[:n], refs[n:2 * n]
        send_sems, recv_sems, local_sems = refs[2 * n:]
        x, y, c = _mesh_pos()
        cps, local = [], []
        for a in range(n):
            local.append(pltpu.make_async_copy(ins[a], outs[a].at[c], local_sems.at[a]))
            local[-1].start()
            cps.append(pltpu.make_async_remote_copy(
                src_ref=ins[a], dst_ref=outs[a].at[c], send_sem=send_sems.at[a], recv_sem=recv_sems.at[a],
                device_id=(x, y, 1 - c), device_id_type=MESH))
            cps[-1].start()
        for cp in cps:
            cp.wait()
        for cp in local:
            cp.wait()

    return pl.pallas_call(
        body, out_shape=[jax.ShapeDtypeStruct((2,) + a.shape, a.dtype) for a in arrs],
        in_specs=[_ANY] * n, out_specs=[_ANY] * n,
        scratch_shapes=[pltpu.SemaphoreType.DMA((n,)), pltpu.SemaphoreType.DMA((n,)), pltpu.SemaphoreType.DMA((n,))],
        name=name,
    )(*arrs)


def _all_gather_devices(name, a):
    rows, w = a.shape

    def body(a_ref, out_ref, send_sems, recv_sems, local_sem):
        x, y, c = _mesh_pos()
        me, sibling = (x, y, c), (x, y, 1 - c)
        chips = _other_chips(x, y)

        def slot(px, py, pc):
            return out_ref.at[4 * px + 2 * py + pc]

        def copy(sem, block, to, src=None):
            return pltpu.make_async_remote_copy(src_ref=slot(*block) if src is None else src, dst_ref=slot(*block),
                                                send_sem=send_sems.at[sem], recv_sem=recv_sems.at[sem], device_id=to,
                                                device_id_type=MESH)

        mine = pltpu.make_async_copy(a_ref, slot(*me), local_sem)
        mine.start()
        first = [copy(0, me, sibling, src=a_ref)]
        first += [copy(1 + j, me, (*chip, c), src=a_ref) for j, chip in enumerate(chips)]
        for cp in first:
            cp.start()
        passed = [copy(4 + j, (*chip, c), sibling) for j, chip in enumerate(chips)]
        for j, chip in enumerate(chips):
            copy(1 + j, (*chip, c), me).wait_recv()
            passed[j].start()
        copy(0, sibling, me).wait_recv()
        for j, chip in enumerate(chips):
            copy(4 + j, (*chip, 1 - c), me).wait_recv()
        for cp in first + passed:
            cp.wait_send()
        mine.wait()

    return pl.pallas_call(
        body, out_shape=jax.ShapeDtypeStruct((8, rows, w), a.dtype), in_specs=[_ANY], out_specs=_ANY,
        scratch_shapes=[pltpu.SemaphoreType.DMA((7,)), pltpu.SemaphoreType.DMA((7,)), pltpu.SemaphoreType.DMA],
        name=name,
    )(a)


_PACK_TILE = 256


def _sum_rows(name, arrs):
    def fn(*vals):
        tot = vals[0]
        for v in vals[1:]:
            tot = tot + v
        return (tot,)

    return _rows(name, fn, list(arrs), [], [(arrs[0].shape[1], F32)], tile=_PACK_TILE)[0]


def _adamw(name, w, g, m, v):
    def fn(wv, gv, mv, vv):
        m2 = ADAM_B1 * mv + (1.0 - ADAM_B1) * gv
        v2 = ADAM_B2 * vv + (1.0 - ADAM_B2) * (gv * gv)
        m_hat = m2 / (1.0 - ADAM_B1 ** ADAM_STEP)
        v_hat = v2 / (1.0 - ADAM_B2 ** ADAM_STEP)
        delta = -ADAM_LR * (m_hat / (jnp.sqrt(v_hat) + ADAM_EPS) + ADAM_WD * wv)
        return delta, m2, v2

    return _rows(name, fn, [w, g, m, v], [], [(w.shape[1], F32)] * 3, tile=_PACK_TILE)


_INPUT_ORDER = (["x", "p"] + WEIGHT_ORDER + ["loss_target"] + ["m_" + n for n in WEIGHT_ORDER]
                + ["v_" + n for n in WEIGHT_ORDER])


_SUM_ROWS = 128


def _pair_sum(name, g, got, core):
    nk, rows, w = g.shape
    half = rows // 2
    nt = half // _SUM_ROWS

    def body(c_ref, g_ref, got_ref, o_ref, ob_ref):
        tot = g_ref[...] + got_ref[...]
        o_ref[...] = tot
        ob_ref[...] = tot.astype(BF16)

    spec = pl.BlockSpec((None, _SUM_ROWS, w), lambda k, i, c: (k, i, 0))
    grid_spec = pltpu.PrefetchScalarGridSpec(
        num_scalar_prefetch=1, grid=(nk, nt),
        in_specs=[pl.BlockSpec((None, _SUM_ROWS, w), lambda k, i, c: (k, c[0] * nt + i, 0)), spec],
        out_specs=[spec, spec])
    return pl.pallas_call(
        body, grid_spec=grid_spec, name=name, compiler_params=_cparams(("parallel", "parallel")),
        out_shape=[jax.ShapeDtypeStruct((nk, half, w), F32), jax.ShapeDtypeStruct((nk, half, w), BF16)])(core, g, got)


def _owner_sum(name, pair, owed, chip):
    _, half, w = pair.shape

    def body(k_ref, p_ref, a_ref, b_ref, c_ref, o_ref):
        o_ref[...] = ((p_ref[...] + a_ref[...].astype(F32)) + b_ref[...].astype(F32)) + c_ref[...].astype(F32)

    def owed_spec(j):
        return pl.BlockSpec((None, _SUM_ROWS, w), lambda i, k: (j, i, 0))

    grid_spec = pltpu.PrefetchScalarGridSpec(
        num_scalar_prefetch=1, grid=(half // _SUM_ROWS,),
        in_specs=[pl.BlockSpec((None, _SUM_ROWS, w), lambda i, k: (k[0], i, 0)), owed_spec(0), owed_spec(1),
                  owed_spec(2)],
        out_specs=pl.BlockSpec((_SUM_ROWS, w), lambda i, k: (i, 0)))
    return pl.pallas_call(body, grid_spec=grid_spec, out_shape=jax.ShapeDtypeStruct((half, w), F32), name=name,
                          compiler_params=_cparams(("parallel",)))(chip, pair, owed, owed, owed)


MATRIX_WEIGHTS = [n for n in BIG_WEIGHTS if n not in KEPT_F32]
TINY_SHARDED = [n for n in BIG_WEIGHTS if n in KEPT_F32]


def _as_rows(a):
    return a.reshape(-1, a.shape[-1])


def _owner_major(grads):
    ev, od = [grads[i] for i in range(0, DEPTH, 2)], [grads[i] for i in range(1, DEPTH, 2)]

    def cols(m, k, n):
        w = m.shape[1] // n
        return m[:, k * w:(k + 1) * w]

    def rows(m, k, n):
        r = m.shape[0] // n
        return m[k * r:(k + 1) * r]

    w_in = [jnp.concatenate([g["in_uv"], g["in_qkv"], g["in_f"][:, :B_HEADS]], axis=1) for g in ev]
    per_chip = {
        "ev_w_in": lambda k: [cols(m, k, N_CHIPS) for m in w_in],
        "ev_w_out": lambda k: [rows(g["out_a"] if k < 2 else g["out_b"], k % 2, 2) for g in ev],
        "od_w_glu": lambda k: [cols(g["glu"], k, N_CHIPS) for g in od],
        "ffn_w_up": lambda k: [cols(g["up_g"] if k < 2 else g["up_u"], k % 2, 2) for g in grads],
        "ffn_w_down": lambda k: [rows(g["down"], k, N_CHIPS) for g in grads],
        "ple_w_proj": lambda k: [cols(g["ple_proj"], k, N_CHIPS) for g in grads],
        "ple_w_gate": lambda k: [rows(g["ple_gate"], k, N_CHIPS) for g in grads],
    }
    return {n: jnp.stack([jnp.concatenate(per_chip[n](k), axis=0) for k in range(N_CHIPS)]) for n in MATRIX_WEIGHTS}


def _small_grads(grads):
    ev, od = [grads[i] for i in range(0, DEPTH, 2)], [grads[i] for i in range(1, DEPTH, 2)]
    out = {"norm_mix": jnp.stack([g["norm_mix"] for g in grads]), "norm_ffn": jnp.stack([g["norm_ffn"] for g in grads]),
           "ffn_conv_w": jnp.stack([jnp.concatenate([g["cw_g"], g["cw_u"]], axis=1) for g in grads]),
           "ffn_conv_b": jnp.stack([g["conv_b"] for g in grads])}
    for n in ("ev_b_fgate", "ev_q_norm", "ev_k_norm", "ev_v_norm", "ev_w_spatial", "ev_b_spatial"):
        out[n] = jnp.stack([g[n] for g in ev])
    for n in ("od_a_re", "od_a_im", "od_log_dt", "od_b_re", "od_b_im", "od_c_re", "od_c_im", "od_d"):
        out[n] = jnp.stack([g[n] for g in od])
    return out


def _step(a):
    xi, yi, ci = _mesh_pos()
    chip = 2 * xi + yi
    core_arr, chip_arr = ci.astype(jnp.int32).reshape(1), chip.astype(jnp.int32).reshape(1)
    rp = {n: a[n] for n in SMALL_WEIGHTS}

    tiny = _pack([a[n] for n in TINY_SHARDED], 32)
    gathered = _gather_shards("gather_weights", [_as_rows(a[n]).astype(BF16) for n in MATRIX_WEIGHTS] + [tiny])
    full = {}
    for n, g in zip(MATRIX_WEIGHTS, gathered):
        full[n] = jnp.concatenate([g[k].reshape(a[n].shape) for k in range(N_CHIPS)], axis=SHARD_AXIS[n])
    tiny_parts = [_unpack(gathered[-1][k], [a[n].shape for n in TINY_SHARDED]) for k in range(N_CHIPS)]
    for idx, n in enumerate(TINY_SHARDED):
        full[n] = jnp.concatenate([tiny_parts[k][idx] for k in range(N_CHIPS)], axis=SHARD_AXIS[n])
    lw = [_layer_weights(i, full, rp) for i in range(DEPTH)]

    loss_local, grad_x, grads = _local_step(a["x"][0], a["p"][:, 0], a["loss_target"][0], lw, rp)
    loss = lax.psum(loss_local, ("x", "y", "c"))

    contrib = _owner_major(grads)
    mats = [contrib[n] for n in MATRIX_WEIGHTS]
    got = _swap_halves("grad_pair_swap", mats)
    pair = [_pair_sum(f"grad_pair_sum_{n}", g, h, core_arr) for n, g, h in zip(MATRIX_WEIGHTS, mats, got)]
    owed = _send_to_owner_chips("grad_to_owner", [pb for _, pb in pair])
    mine = [_owner_sum(f"grad_owner_sum_{n}", p, o, chip_arr) for n, (p, _), o in zip(MATRIX_WEIGHTS, pair, owed)]
    reduced = _share_halves("grad_half_share", mine)

    small_names = SMALL_WEIGHTS + TINY_SHARDED
    sg = _small_grads(grads)
    everyone = _all_gather_devices("small_grad_gather", _pack([sg[n] for n in small_names], _PACK_TILE))
    g_small = _sum_rows("small_grad_sum", [everyone[d] for d in range(8)])
    small_full = dict(zip(small_names, _unpack(g_small, [sg[n].shape for n in small_names])))

    out = {}
    for n, red in zip(MATRIX_WEIGHTS, reduced):
        shape = a[n].shape
        g2d = red.reshape(-1, shape[-1])
        delta, m2, v2 = _adamw(f"adamw_{n}", _as_rows(a[n]), g2d, _as_rows(a["m_" + n]), _as_rows(a["v_" + n]))
        for kind, val in (("grad", g2d), ("delta", delta), ("new_m", m2), ("new_v", v2)):
            out[kind + "_" + n] = val.reshape(shape)
    g_sm = {n: small_full[n] for n in SMALL_WEIGHTS}
    for n in TINY_SHARDED:
        width = a[n].shape[SHARD_AXIS[n]]
        g_sm[n] = lax.dynamic_slice_in_dim(small_full[n], chip * width, width, axis=SHARD_AXIS[n])
    shapes = [a[n].shape for n in small_names]
    w, m, v = (_pack([a[pre + n] for n in small_names], _PACK_TILE) for pre in ("", "m_", "v_"))
    g = _pack([g_sm[n] for n in small_names], _PACK_TILE)
    delta, m2, v2 = _adamw("adamw_small", w, g, m, v)
    for kind, buf in (("delta", delta), ("new_m", m2), ("new_v", v2)):
        for n, val in zip(small_names, _unpack(buf, shapes)):
            out[kind + "_" + n] = val
    for n in small_names:
        out["grad_" + n] = g_sm[n]
    res = [loss, grad_x[None]]
    for kind in ("grad", "delta", "new_m", "new_v"):
        res += [out[kind + "_" + n] for n in WEIGHT_ORDER]
    return tuple(res)


def kernel(x, p, norm_mix, norm_ffn, ev_w_in, ev_b_fgate, ev_q_norm, ev_k_norm, ev_v_norm, ev_w_spatial, ev_b_spatial, ev_w_out, od_a_re, od_a_im, od_log_dt, od_b_re, od_b_im, od_c_re, od_c_im, od_d, od_w_glu, ffn_w_up, ffn_conv_w, ffn_conv_b, ffn_w_down, ple_w_proj, ple_w_gate, loss_target, m_norm_mix, m_norm_ffn, m_ev_w_in, m_ev_b_fgate, m_ev_q_norm, m_ev_k_norm, m_ev_v_norm, m_ev_w_spatial, m_ev_b_spatial, m_ev_w_out, m_od_a_re, m_od_a_im, m_od_log_dt, m_od_b_re, m_od_b_im, m_od_c_re, m_od_c_im, m_od_d, m_od_w_glu, m_ffn_w_up, m_ffn_conv_w, m_ffn_conv_b, m_ffn_w_down, m_ple_w_proj, m_ple_w_gate, v_norm_mix, v_norm_ffn, v_ev_w_in, v_ev_b_fgate, v_ev_q_norm, v_ev_k_norm, v_ev_v_norm, v_ev_w_spatial, v_ev_b_spatial, v_ev_w_out, v_od_a_re, v_od_a_im, v_od_log_dt, v_od_b_re, v_od_b_im, v_od_c_re, v_od_c_im, v_od_d, v_od_w_glu, v_ffn_w_up, v_ffn_conv_w, v_ffn_conv_b, v_ffn_w_down, v_ple_w_proj, v_ple_w_gate):
    args = (x, p, norm_mix, norm_ffn, ev_w_in, ev_b_fgate, ev_q_norm, ev_k_norm, ev_v_norm, ev_w_spatial, ev_b_spatial, ev_w_out, od_a_re, od_a_im, od_log_dt, od_b_re, od_b_im, od_c_re, od_c_im, od_d, od_w_glu, ffn_w_up, ffn_conv_w, ffn_conv_b, ffn_w_down, ple_w_proj, ple_w_gate, loss_target, m_norm_mix, m_norm_ffn, m_ev_w_in, m_ev_b_fgate, m_ev_q_norm, m_ev_k_norm, m_ev_v_norm, m_ev_w_spatial, m_ev_b_spatial, m_ev_w_out, m_od_a_re, m_od_a_im, m_od_log_dt, m_od_b_re, m_od_b_im, m_od_c_re, m_od_c_im, m_od_d, m_od_w_glu, m_ffn_w_up, m_ffn_conv_w, m_ffn_conv_b, m_ffn_w_down, m_ple_w_proj, m_ple_w_gate, v_norm_mix, v_norm_ffn, v_ev_w_in, v_ev_b_fgate, v_ev_q_norm, v_ev_k_norm, v_ev_v_norm, v_ev_w_spatial, v_ev_b_spatial, v_ev_w_out, v_od_a_re, v_od_a_im, v_od_log_dt, v_od_b_re, v_od_b_im, v_od_c_re, v_od_c_im, v_od_d, v_od_w_glu, v_ffn_w_up, v_ffn_conv_w, v_ffn_conv_b, v_ffn_w_down, v_ple_w_proj, v_ple_w_gate)
    return _step(dict(zip(_INPUT_ORDER, args)))
```

```python
import functools
import math

import jax
import jax.numpy as jnp
import numpy as np
from jax import lax
from jax.experimental import pallas as pl
from jax.experimental.pallas import tpu as pltpu

F32 = jnp.float32
BF16 = jnp.bfloat16
MESH = pl.DeviceIdType.MESH

V7X_VMEM_LIMIT_BYTES = 56 * 1024 * 1024
LANES = 128

D_MODEL = 1024
DEPTH = 4
A_GROUPS = 4
A_CHUNK = 128
A_WIDTH = 512
B_HEADS = 8
B_HEAD_DIM = 64
B_WIDTH = 512
S5_GROUP_CH = 16
S5_GROUPS = 64
S5_STATE = 64
S5_BLOCKS = 8
S5_LANES = 512
D_FF = 2816
PLE_DIM = 256
EPS = 1e-6
NEG_INF = -1e30

ADAM_LR = 0.001
ADAM_B1 = 0.9
ADAM_B2 = 0.999
ADAM_EPS = 1e-08
ADAM_WD = 0.01
ADAM_STEP = 10

N_CHIPS = 4
PACK_W = 1024


def _cparams(sem):
    return pltpu.CompilerParams(dimension_semantics=sem, vmem_limit_bytes=V7X_VMEM_LIMIT_BYTES)


def _pick(n, target):
    if n <= target:
        return n
    t = (target // LANES) * LANES
    while t >= LANES:
        if n % t == 0:
            return t
        t -= LANES
    return n


_GELU_K = 0.7978845608028654
_GELU_C = 0.044715


def _gelu(x):
    return x * (0.5 * (1.0 + jnp.tanh(_GELU_K * (x + _GELU_C * (x * x * x)))))


def _gelu_grad(x):
    x2 = x * x
    t = jnp.tanh(_GELU_K * (x + _GELU_C * (x * x2)))
    return 0.5 * (1.0 + t) + (0.5 * x) * (1.0 - t * t) * (_GELU_K * (1.0 + (3.0 * _GELU_C) * x2))


def _sigmoid(x):
    return 1.0 / (1.0 + jnp.exp(-x))


def _log_sigmoid(x):
    return -(jnp.maximum(-x, 0.0) + jnp.log(1.0 + jnp.exp(-jnp.abs(x))))


def _rstd(x):
    return lax.rsqrt(jnp.mean(x * x, axis=-1, keepdims=True) + EPS)


def _rows(name, fn, row_ins, full_ins, outs, accs=(), tile=256):
    rows = row_ins[0].shape[0]
    r = min(tile, rows)
    n = rows // r
    n_in = len(row_ins) + len(full_ins)
    n_out = len(outs)

    def body(*refs):
        res = fn(*[ref[...] for ref in refs[:n_in]])
        for ref, v in zip(refs[n_in:n_in + n_out], res[:n_out]):
            ref[...] = v.astype(ref.dtype)
        acc_refs = refs[n_in + n_out:]
        if acc_refs:
            @pl.when(pl.program_id(0) == 0)
            def _():
                for ref in acc_refs:
                    ref[...] = jnp.zeros(ref.shape, ref.dtype)

            for ref, v in zip(acc_refs, res[n_out:]):
                ref[...] += v

    in_specs = [pl.BlockSpec((r, a.shape[1]), lambda i: (i, 0)) for a in row_ins]
    in_specs += [pl.BlockSpec(a.shape, lambda i, nd=a.ndim: (0,) * nd) for a in full_ins]
    out_shape = [jax.ShapeDtypeStruct((rows, w), dt) for (w, dt) in outs]
    out_shape += [jax.ShapeDtypeStruct(s, F32) for s in accs]
    out_specs = [pl.BlockSpec((r, w), lambda i: (i, 0)) for (w, dt) in outs]
    out_specs += [pl.BlockSpec(s, lambda i, nd=len(s): (0,) * nd) for s in accs]
    return pl.pallas_call(
        body, grid=(n,), in_specs=in_specs, out_specs=out_specs, out_shape=out_shape, name=name,
        compiler_params=_cparams(("arbitrary",) if accs else ("parallel",)),
    )(*row_ins, *full_ins)


_DOT_DIMS = {"nn": (((1,), (0,)), ((), ())), "nt": (((1,), (1,)), ((), ())), "tn": (((0,), (0,)), ((), ()))}


def _mm(name, a, b, mode="nn", out_dtype=F32, res=None, tm=1024, tn=1024, tk=1024, norm_gain=None):
    if mode == "nn":
        (m, k), (k2, n) = a.shape, b.shape
    elif mode == "nt":
        (m, k), (n, k2) = a.shape, b.shape
    else:
        (k, m), (k2, n) = a.shape, b.shape
    assert k == k2, (name, a.shape, b.shape, mode)
    tm, tn, tk = _pick(m, tm), _pick(n, tn), _pick(k, tk)
    nk = k // tk
    dims = _DOT_DIMS[mode]
    has_res = res is not None
    has_norm = norm_gain is not None
    assert not has_norm or tn == n, (name, tn, n)
    n_in = 2 + has_res + has_norm

    def body(*refs):
        a_ref, b_ref = refs[0], refs[1]
        res_ref = refs[2] if has_res else None
        gain_ref = refs[n_in - 1] if has_norm else None
        o_ref = refs[n_in]
        h_ref = refs[n_in + 1] if has_norm else None

        def finish(tot):
            if has_res:
                tot = res_ref[...] + tot
            o_ref[...] = tot.astype(o_ref.dtype)
            if has_norm:
                h_ref[...] = ((tot * _rstd(tot)) * gain_ref[...]).astype(h_ref.dtype)

        prod = lax.dot_general(a_ref[...].astype(BF16), b_ref[...].astype(BF16), dims, preferred_element_type=F32)
        if nk == 1:
            finish(prod)
            return
        acc = refs[-1]
        kk = pl.program_id(2)

        @pl.when(kk == 0)
        def _():
            acc[...] = prod

        @pl.when(kk > 0)
        def _():
            acc[...] += prod

        @pl.when(kk == nk - 1)
        def _():
            finish(acc[...])

    if mode == "nn":
        a_spec = pl.BlockSpec((tm, tk), lambda i, j, kk: (i, kk))
        b_spec = pl.BlockSpec((tk, tn), lambda i, j, kk: (kk, j))
    elif mode == "nt":
        a_spec = pl.BlockSpec((tm, tk), lambda i, j, kk: (i, kk))
        b_spec = pl.BlockSpec((tn, tk), lambda i, j, kk: (j, kk))
    else:
        a_spec = pl.BlockSpec((tk, tm), lambda i, j, kk: (kk, i))
        b_spec = pl.BlockSpec((tk, tn), lambda i, j, kk: (kk, j))
    o_spec = pl.BlockSpec((tm, tn), lambda i, j, kk: (i, j))
    in_specs = [a_spec, b_spec] + ([o_spec] if has_res else [])
    in_specs += [pl.BlockSpec((1, tn), lambda i, j, kk: (0, j))] if has_norm else []
    args = (a, b) + ((res,) if has_res else ()) + ((norm_gain,) if has_norm else ())
    out_shape = jax.ShapeDtypeStruct((m, n), out_dtype)
    return pl.pallas_call(
        body, grid=(m // tm, n // tn, nk), in_specs=in_specs, out_specs=[o_spec, o_spec] if has_norm else o_spec,
        out_shape=[out_shape, jax.ShapeDtypeStruct((m, n), BF16)] if has_norm else out_shape, name=name,
        scratch_shapes=[pltpu.VMEM((tm, tn), F32)] if nk > 1 else [],
        compiler_params=_cparams(("parallel", "parallel", "arbitrary")),
    )(*args)


def _mm_norm_bwd(name, a, b, mode, x, dres, gain, res=None, tm=512, tk=1024):
    if mode == "nn":
        (m, k), (k2, n) = a.shape, b.shape
    else:
        (m, k), (n, k2) = a.shape, b.shape
    assert k == k2, (name, a.shape, b.shape, mode)
    tm, tk = _pick(m, tm), _pick(k, tk)
    nk = k // tk
    dims = _DOT_DIMS[mode]
    has_res = res is not None
    n_in = 5 + has_res

    def body(*refs):
        a_ref, b_ref = refs[0], refs[1]
        res_ref = refs[2] if has_res else None
        x_ref, dres_ref, gain_ref = refs[n_in - 3:n_in]
        o_ref, dg_ref = refs[n_in], refs[n_in + 1]

        def finish(d):
            if has_res:
                d = res_ref[...] + d
            xv = x_ref[...]
            r = _rstd(xv)
            xh = xv * r
            dyg = d * gain_ref[...]
            o_ref[...] = dres_ref[...] + r * (dyg - xh * jnp.mean(dyg * xh, axis=-1, keepdims=True))
            part = jnp.sum(d * xh, axis=0, keepdims=True)

            @pl.when(pl.program_id(0) == 0)
            def _():
                dg_ref[...] = part

            @pl.when(pl.program_id(0) > 0)
            def _():
                dg_ref[...] += part

        prod = lax.dot_general(a_ref[...].astype(BF16), b_ref[...].astype(BF16), dims, preferred_element_type=F32)
        if nk == 1:
            finish(prod)
            return
        acc = refs[-1]
        kk = pl.program_id(1)

        @pl.when(kk == 0)
        def _():
            acc[...] = prod

        @pl.when(kk > 0)
        def _():
            acc[...] += prod

        @pl.when(kk == nk - 1)
        def _():
            finish(acc[...])

    a_spec = pl.BlockSpec((tm, tk), lambda i, kk: (i, kk))
    b_spec = pl.BlockSpec((tk, n), lambda i, kk: (kk, 0)) if mode == "nn" else pl.BlockSpec((n, tk), lambda i, kk: (0, kk))
    row_spec = pl.BlockSpec((tm, n), lambda i, kk: (i, 0))
    vec_spec = pl.BlockSpec((1, n), lambda i, kk: (0, 0))
    in_specs = [a_spec, b_spec] + ([row_spec] if has_res else []) + [row_spec, row_spec, vec_spec]
    args = (a, b) + ((res,) if has_res else ()) + (x, dres, gain)
    return pl.pallas_call(
        body, grid=(m // tm, nk), in_specs=in_specs, out_specs=[row_spec, vec_spec],
        out_shape=[jax.ShapeDtypeStruct((m, n), F32), jax.ShapeDtypeStruct((1, n), F32)], name=name,
        scratch_shapes=[pltpu.VMEM((tm, n), F32)] if nk > 1 else [],
        compiler_params=_cparams(("arbitrary", "arbitrary")),
    )(*args)


def _rmsnorm_fwd(name, x, g, outs):
    def fn(xv, gv):
        y = (xv * _rstd(xv)) * gv
        return tuple(y for _ in outs)

    return _rows(name, fn, [x], [g], [(x.shape[1], dt) for dt in outs])


_CONV_ROWS = 256
_CONV_COLS = 1408


def _conv_taps(h_ref, halo_ref, first):
    h = h_ref[...]
    rows = h.shape[0]
    row = lax.broadcasted_iota(jnp.int32, (rows, 1), 0)
    keep = jnp.where(first, 0.0, 1.0)
    m1 = halo_ref[7:8, :] * keep
    m2 = halo_ref[6:7, :] * keep
    p1 = jnp.where(row == 0, m1, pltpu.roll(h, 1, 0))
    p2 = jnp.where(row == 0, m2, jnp.where(row == 1, m1, pltpu.roll(h, 2, 0)))
    return h, p1, p2


def _conv_specs(rows, r, cw):
    tile = pl.BlockSpec((r, cw), lambda j, i: (i, j))
    halo = pl.BlockSpec((8, cw), lambda j, i: (jnp.maximum(i * (r // 8) - 1, 0), j))
    vec3 = pl.BlockSpec((3, cw), lambda j, i: (0, j))
    vec1 = pl.BlockSpec((1, cw), lambda j, i: (0, j))
    return tile, halo, vec3, vec1


def _convffn_fwd(name, hg, hu, wg, wu, bg, bu):
    rows, f = hg.shape
    r, cw = min(_CONV_ROWS, rows), _pick(f, _CONV_COLS)

    def body(hg_ref, hgh_ref, hu_ref, huh_ref, wg_ref, wu_ref, bg_ref, bu_ref, o_ref):
        first = pl.program_id(1) == 0
        h, p1, p2 = _conv_taps(hg_ref, hgh_ref, first)
        g = bg_ref[...] + wg_ref[0:1, :] * p2 + wg_ref[1:2, :] * p1 + wg_ref[2:3, :] * h
        h, p1, p2 = _conv_taps(hu_ref, huh_ref, first)
        u = bu_ref[...] + wu_ref[0:1, :] * p2 + wu_ref[1:2, :] * p1 + wu_ref[2:3, :] * h
        o_ref[...] = ((g * _sigmoid(g)) * u).astype(o_ref.dtype)

    tile, halo, vec3, vec1 = _conv_specs(rows, r, cw)
    return pl.pallas_call(
        body, grid=(f // cw, rows // r), in_specs=[tile, halo, tile, halo, vec3, vec3, vec1, vec1], out_specs=tile,
        out_shape=jax.ShapeDtypeStruct((rows, f), BF16), name=name, compiler_params=_cparams(("parallel", "parallel")),
    )(hg, hg, hu, hu, wg, wu, bg, bu)


def _gate_grads(da, g, u):
    sg = _sigmoid(g)
    return da * u * (sg * (1.0 + g * (1.0 - sg))), da * (g * sg)


def _conv_back(dc, dc_next, w_ref, last):
    r = dc.shape[0]
    row = lax.broadcasted_iota(jnp.int32, (r, 1), 0)
    keep = jnp.where(last, 0.0, 1.0)
    n0 = dc_next[0:1, :] * keep
    n1 = dc_next[1:2, :] * keep
    f1 = jnp.where(row == r - 1, n0, pltpu.roll(dc, r - 1, 0))
    f2 = jnp.where(row == r - 1, n1, jnp.where(row == r - 2, n0, pltpu.roll(dc, r - 2, 0)))
    return w_ref[2:3, :] * dc + w_ref[1:2, :] * f1 + w_ref[0:1, :] * f2


def _conv_next_rows(h, nxt_ref, w_ref, b_ref):
    r = h.shape[0]
    hn = nxt_ref[...]
    row = lax.broadcasted_iota(jnp.int32, (8, 1), 0)
    m1, m2 = h[r - 1:r, :], h[r - 2:r - 1, :]
    p1 = jnp.where(row == 0, m1, pltpu.roll(hn, 1, 0))
    p2 = jnp.where(row == 0, m2, jnp.where(row == 1, m1, pltpu.roll(hn, 2, 0)))
    return b_ref[...] + w_ref[0:1, :] * p2 + w_ref[1:2, :] * p1 + w_ref[2:3, :] * hn


def _convffn_bwd(name, da, hg, hu, wg, wu, bg, bu):
    rows, f = hg.shape
    r, cw = min(_CONV_ROWS, rows), _pick(f, _CONV_COLS)
    nrt = rows // r

    def body(da_ref, dan_ref, hg_ref, hgh_ref, hgn_ref, hu_ref, huh_ref, hun_ref, wg_ref, wu_ref, bg_ref, bu_ref,
             dhg_ref, dhu_ref, dwg_ref, dwu_ref, dbg_ref, dbu_ref):
        first = pl.program_id(1) == 0
        last = pl.program_id(1) == nrt - 1
        hgv, g1, g2 = _conv_taps(hg_ref, hgh_ref, first)
        g = bg_ref[...] + wg_ref[0:1, :] * g2 + wg_ref[1:2, :] * g1 + wg_ref[2:3, :] * hgv
        huv, u1, u2 = _conv_taps(hu_ref, huh_ref, first)
        u = bu_ref[...] + wu_ref[0:1, :] * u2 + wu_ref[1:2, :] * u1 + wu_ref[2:3, :] * huv
        dcg, dcu = _gate_grads(da_ref[...], g, u)
        dcg_n, dcu_n = _gate_grads(dan_ref[...], _conv_next_rows(hgv, hgn_ref, wg_ref, bg_ref),
                                   _conv_next_rows(huv, hun_ref, wu_ref, bu_ref))
        dhg_ref[...] = _conv_back(dcg, dcg_n, wg_ref, last).astype(dhg_ref.dtype)
        dhu_ref[...] = _conv_back(dcu, dcu_n, wu_ref, last).astype(dhu_ref.dtype)

        @pl.when(first)
        def _():
            for ref in (dwg_ref, dwu_ref, dbg_ref, dbu_ref):
                ref[...] = jnp.zeros(ref.shape, ref.dtype)

        def colsum(v):
            return jnp.sum(v, axis=0, keepdims=True)

        dwg_ref[0:1, :] += colsum(dcg * g2)
        dwg_ref[1:2, :] += colsum(dcg * g1)
        dwg_ref[2:3, :] += colsum(dcg * hgv)
        dwu_ref[0:1, :] += colsum(dcu * u2)
        dwu_ref[1:2, :] += colsum(dcu * u1)
        dwu_ref[2:3, :] += colsum(dcu * huv)
        dbg_ref[...] += colsum(dcg)
        dbu_ref[...] += colsum(dcu)

    tile, halo, vec3, vec1 = _conv_specs(rows, r, cw)
    nxt = pl.BlockSpec((8, cw), lambda j, i: (jnp.minimum((i + 1) * (r // 8), rows // 8 - 1), j))
    big = jax.ShapeDtypeStruct((rows, f), BF16)
    return pl.pallas_call(
        body, grid=(f // cw, nrt),
        in_specs=[tile, nxt, tile, halo, nxt, tile, halo, nxt, vec3, vec3, vec1, vec1],
        out_specs=[tile, tile, vec3, vec3, vec1, vec1],
        out_shape=[big, big, jax.ShapeDtypeStruct((3, f), F32), jax.ShapeDtypeStruct((3, f), F32),
                   jax.ShapeDtypeStruct((1, f), F32), jax.ShapeDtypeStruct((1, f), F32)],
        name=name, compiler_params=_cparams(("parallel", "arbitrary")),
    )(da, da, hg, hg, hg, hu, hu, hu, wg, wu, bg, bu)


_GMLP_ROWS = 256


def _gmlp_group_norm(vg, gain):
    r = lax.rsqrt(jnp.mean(vg * vg, axis=-1, keepdims=True) + EPS)
    vh = vg * r
    return vh, r, vh * gain


def _gmlp_fwd(name, zuv, v_gain, w_tril, b_exp):
    rows = zuv.shape[0]
    r = min(_GMLP_ROWS, rows)

    def body(z_ref, gain_ref, w_ref, b_ref, o_ref):
        for ch in range(r // A_CHUNK):
            lo = ch * A_CHUNK
            for g in range(A_GROUPS):
                c0 = g * LANES
                u = _gelu(z_ref[lo:lo + A_CHUNK, c0:c0 + LANES])
                v = _gelu(z_ref[lo:lo + A_CHUNK, A_WIDTH + c0:A_WIDTH + c0 + LANES])
                _, _, vn = _gmlp_group_norm(v, gain_ref[:, c0:c0 + LANES])
                sv = jnp.dot(w_ref[g], vn.astype(BF16), preferred_element_type=F32) + b_ref[g]
                o_ref[lo:lo + A_CHUNK, c0:c0 + LANES] = (u * sv).astype(o_ref.dtype)

    return pl.pallas_call(
        body, grid=(rows // r,),
        in_specs=[pl.BlockSpec((r, 2 * A_WIDTH), lambda i: (i, 0)), pl.BlockSpec((1, A_WIDTH), lambda i: (0, 0)),
                  pl.BlockSpec((A_GROUPS, A_CHUNK, A_CHUNK), lambda i: (0, 0, 0)),
                  pl.BlockSpec((A_GROUPS, A_CHUNK, LANES), lambda i: (0, 0, 0))],
        out_specs=pl.BlockSpec((r, A_WIDTH), lambda i: (i, 0)),
        out_shape=jax.ShapeDtypeStruct((rows, A_WIDTH), BF16), name=name, compiler_params=_cparams(("parallel",)),
    )(zuv, v_gain, w_tril, b_exp)


def _gmlp_bwd(name, zuv, dya, v_gain, w_tril, w_tril_t, b_exp):
    rows = zuv.shape[0]
    r = min(_GMLP_ROWS, rows)

    def body(z_ref, dy_ref, gain_ref, w_ref, wt_ref, b_ref, dz_ref, dw_ref, db_ref, dgain_ref):
        @pl.when(pl.program_id(0) == 0)
        def _():
            for ref in (dw_ref, db_ref, dgain_ref):
                ref[...] = jnp.zeros(ref.shape, ref.dtype)

        for ch in range(r // A_CHUNK):
            lo = ch * A_CHUNK
            for g in range(A_GROUPS):
                c0 = g * LANES
                zu = z_ref[lo:lo + A_CHUNK, c0:c0 + LANES]
                zv = z_ref[lo:lo + A_CHUNK, A_WIDTH + c0:A_WIDTH + c0 + LANES]
                gain = gain_ref[:, c0:c0 + LANES]
                u = _gelu(zu)
                v = _gelu(zv)
                vh, rr, vn = _gmlp_group_norm(v, gain)
                vn_b = vn.astype(BF16)
                sv = jnp.dot(w_ref[g], vn_b, preferred_element_type=F32) + b_ref[g]
                dy = dy_ref[lo:lo + A_CHUNK, c0:c0 + LANES]
                dsv = dy * u
                dsv_b = dsv.astype(BF16)
                dz_ref[lo:lo + A_CHUNK, c0:c0 + LANES] = ((dy * sv) * _gelu_grad(zu)).astype(dz_ref.dtype)
                dw_ref[g] += lax.dot_general(dsv_b, vn_b, _DOT_DIMS["nt"], preferred_element_type=F32)
                db_ref[g] += dsv
                dvn = jnp.dot(wt_ref[g], dsv_b, preferred_element_type=F32)
                dgain_ref[:, c0:c0 + LANES] += jnp.sum(dvn * vh, axis=0, keepdims=True)
                dvh = dvn * gain
                dv = rr * (dvh - vh * jnp.mean(dvh * vh, axis=-1, keepdims=True))
                dz_ref[lo:lo + A_CHUNK, A_WIDTH + c0:A_WIDTH + c0 + LANES] = (dv * _gelu_grad(zv)).astype(dz_ref.dtype)

    wspec = pl.BlockSpec((A_GROUPS, A_CHUNK, A_CHUNK), lambda i: (0, 0, 0))
    bspec = pl.BlockSpec((A_GROUPS, A_CHUNK, LANES), lambda i: (0, 0, 0))
    gspec = pl.BlockSpec((1, A_WIDTH), lambda i: (0, 0))
    return pl.pallas_call(
        body, grid=(rows // r,),
        in_specs=[pl.BlockSpec((r, 2 * A_WIDTH), lambda i: (i, 0)), pl.BlockSpec((r, A_WIDTH), lambda i: (i, 0)),
                  gspec, wspec, wspec, bspec],
        out_specs=[pl.BlockSpec((r, 2 * A_WIDTH), lambda i: (i, 0)), wspec, bspec, gspec],
        out_shape=[jax.ShapeDtypeStruct((rows, 2 * A_WIDTH), BF16),
                   jax.ShapeDtypeStruct((A_GROUPS, A_CHUNK, A_CHUNK), F32),
                   jax.ShapeDtypeStruct((A_GROUPS, A_CHUNK, LANES), F32), jax.ShapeDtypeStruct((1, A_WIDTH), F32)],
        name=name, compiler_params=_cparams(("arbitrary",)),
    )(zuv, dya, v_gain, w_tril, w_tril_t, b_exp)


_ATT_T = 512
_Q_SCALE = B_HEAD_DIM ** -0.5


def _head_mean(v, bd):
    return jnp.dot(v, bd, preferred_element_type=F32, precision=lax.Precision.HIGHEST)


def _qkv_prep_fwd(name, zqkv, zf, qg, kg, bf, bd):
    def fn(z, f, qg_v, kg_v, bf_v, bd_v):
        zq, zk, zv = z[:, :B_WIDTH], z[:, B_WIDTH:2 * B_WIDTH], z[:, 2 * B_WIDTH:]
        q = (zq * lax.rsqrt(_head_mean(zq * zq, bd_v) + EPS)) * qg_v * _Q_SCALE
        k = (zk * lax.rsqrt(_head_mean(zk * zk, bd_v) + EPS)) * kg_v
        return q, k, zv, _log_sigmoid(f + bf_v)

    return _rows(name, fn, [zqkv, zf], [qg, kg, bf, bd],
                 [(B_WIDTH, BF16), (B_WIDTH, BF16), (B_WIDTH, BF16), (LANES, F32)])


def _qkv_prep_bwd(name, zqkv, zf, dq, dk, dv, dls, qg, kg, bf, bd):
    def fn(z, f, dq_v, dk_v, dv_v, dls_v, qg_v, kg_v, bf_v, bd_v):
        zq, zk = z[:, :B_WIDTH], z[:, B_WIDTH:2 * B_WIDTH]

        def norm_bwd(x, dy, gain):
            r = lax.rsqrt(_head_mean(x * x, bd_v) + EPS)
            xh = x * r
            dxh = dy * gain
            dx = r * (dxh - xh * _head_mean(dxh * xh, bd_v))
            return dx, jnp.sum(dy * xh, axis=0, keepdims=True)

        dzq, dqg = norm_bwd(zq, dq_v * _Q_SCALE, qg_v)
        dzk, dkg = norm_bwd(zk, dk_v, kg_v)
        dzf = dls_v * (1.0 - _sigmoid(f + bf_v))
        return jnp.concatenate([dzq, dzk, dv_v], axis=1), dzf, dqg, dkg, jnp.sum(dzf, axis=0, keepdims=True)

    return _rows(name, fn, [zqkv, zf, dq, dk, dv, dls], [qg, kg, bf, bd],
                 [(3 * B_WIDTH, BF16), (LANES, BF16)], accs=[(1, B_WIDTH), (1, B_WIDTH), (1, LANES)])


def _cumsum_rows(name, a, reverse=False, tile=512):
    rows, w = a.shape
    r = min(tile, rows)
    n = rows // r

    def body(a_ref, o_ref, carry):
        @pl.when(pl.program_id(0) == 0)
        def _():
            carry[...] = jnp.zeros(carry.shape, carry.dtype)

        x = a_ref[...]
        row = lax.broadcasted_iota(jnp.int32, (r, 1), 0)
        s = 1
        while s < r:
            if reverse:
                x = x + jnp.where(row < r - s, pltpu.roll(x, r - s, 0), 0.0)
            else:
                x = x + jnp.where(row >= s, pltpu.roll(x, s, 0), 0.0)
            s *= 2
        x = x + carry[0:1, :]
        o_ref[...] = x
        edge = x[0:1, :] if reverse else x[r - 1:r, :]
        carry[...] = jnp.broadcast_to(edge, carry.shape)

    idx = (lambda i: (n - 1 - i, 0)) if reverse else (lambda i: (i, 0))
    return pl.pallas_call(
        body, grid=(n,), in_specs=[pl.BlockSpec((r, w), idx)], out_specs=pl.BlockSpec((r, w), idx),
        out_shape=jax.ShapeDtypeStruct((rows, w), F32), scratch_shapes=[pltpu.VMEM((8, w), F32)], name=name,
        compiler_params=_cparams(("arbitrary",)),
    )(a)


def _head_masks():
    lane = lax.broadcasted_iota(jnp.int32, (1, LANES), 1)
    return [lane < B_HEAD_DIM, lane >= B_HEAD_DIM]


def _causal(t):
    row = lax.broadcasted_iota(jnp.int32, (t, t), 0)
    col = lax.broadcasted_iota(jnp.int32, (t, t), 1)
    return row, col


def _flash_fwd(name, q, k, v, nck_rows):
    rows = q.shape[0]
    t = min(_ATT_T, rows)
    nb = rows // t

    def body(q_ref, k_ref, v_ref, nck_ref, o_ref, lse_ref):
        pair, i = pl.program_id(0), pl.program_id(1)
        q2 = q_ref[...]
        row, col = _causal(t)
        masks = _head_masks()
        qh = [jnp.where(hm, q2, jnp.zeros_like(q2)) for hm in masks]

        def step(j, carry, diag):
            ml, acc = carry
            start = pl.multiple_of(j * t, t)
            kb = k_ref[pl.ds(start, t), :]
            vb = v_ref[pl.ds(start, t), :]
            new_ml = []
            for hh, hm in enumerate(masks):
                m, l = ml[hh]
                s = lax.dot_general(qh[hh], kb, _DOT_DIMS["nt"], preferred_element_type=F32)
                s = s + nck_ref[2 * pair + hh, pl.ds(j, 1), :]
                if diag:
                    s = jnp.where(col <= row, s, NEG_INF)
                m_new = jnp.maximum(m, jnp.max(s, axis=1, keepdims=True))
                p = jnp.exp(s - m_new)
                alpha = jnp.exp(m - m_new)
                new_ml.append((m_new, alpha * l + jnp.sum(p, axis=1, keepdims=True)))
                pv = jnp.dot(p.astype(BF16), jnp.where(hm, vb, jnp.zeros_like(vb)), preferred_element_type=F32)
                acc = acc * jnp.where(hm, alpha, 1.0) + pv
            return tuple(new_ml), acc

        def init_ml():
            return (jnp.full((t, 1), NEG_INF, F32), jnp.zeros((t, 1), F32))

        init = ((init_ml(), init_ml()), jnp.zeros((t, LANES), F32))
        carry = lax.fori_loop(0, i, lambda j, c: step(j, c, False), init)
        ml, acc = step(i, carry, True)
        o_ref[...] = acc / jnp.where(masks[0], ml[0][1], ml[1][1])
        for hh in range(2):
            lse_ref[hh] = ml[hh][0] + jnp.log(ml[hh][1])

    return pl.pallas_call(
        body, grid=(B_HEADS // 2, nb),
        in_specs=[pl.BlockSpec((t, LANES), lambda p, i: (i, p)), pl.BlockSpec((rows, LANES), lambda p, i: (0, p)),
                  pl.BlockSpec((rows, LANES), lambda p, i: (0, p)),
                  pl.BlockSpec((B_HEADS, nb, t), lambda p, i: (0, 0, 0))],
        out_specs=[pl.BlockSpec((t, LANES), lambda p, i: (i, p)), pl.BlockSpec((2, t, 1), lambda p, i: (p, i, 0))],
        out_shape=[jax.ShapeDtypeStruct((rows, B_WIDTH), F32), jax.ShapeDtypeStruct((B_HEADS, rows, 1), F32)],
        name=name, compiler_params=_cparams(("parallel", "parallel")),
    )(q, k, v, nck_rows)


def _flash_bwd_dq(name, q, k, v, nck_rows, o, do, lse_col):
    rows = q.shape[0]
    t = min(_ATT_T, rows)
    nb = rows // t

    def body(q_ref, k_ref, v_ref, nck_ref, o_ref, do_ref, lse_ref, dq_ref, delta_ref):
        pair, i = pl.program_id(0), pl.program_id(1)
        q2 = q_ref[...]
        do2 = do_ref[...]
        od = o_ref[...] * do2
        do_b = do2.astype(BF16)
        row, col = _causal(t)
        masks = _head_masks()
        qh = [jnp.where(hm, q2, jnp.zeros_like(q2)) for hm in masks]
        doh = [jnp.where(hm, do_b, jnp.zeros_like(do_b)) for hm in masks]
        delta = [jnp.sum(jnp.where(hm, od, 0.0), axis=1, keepdims=True) for hm in masks]
        lse = [lse_ref[hh] for hh in range(2)]

        def step(j, carry, diag):
            acc, rowsum = carry
            start = pl.multiple_of(j * t, t)
            kb = k_ref[pl.ds(start, t), :]
            vb = v_ref[pl.ds(start, t), :]
            new_rowsum = []
            for hh, hm in enumerate(masks):
                s = lax.dot_general(qh[hh], kb, _DOT_DIMS["nt"], preferred_element_type=F32)
                s = s + nck_ref[2 * pair + hh, pl.ds(j, 1), :]
                p = jnp.exp(s - lse[hh])
                if diag:
                    p = jnp.where(col <= row, p, 0.0)
                dp = lax.dot_general(doh[hh], vb, _DOT_DIMS["nt"], preferred_element_type=F32)
                ds = p * (dp - delta[hh])
                new_rowsum.append(rowsum[hh] + jnp.sum(ds, axis=1, keepdims=True))
                acc = acc + jnp.dot(ds.astype(BF16), jnp.where(hm, kb, jnp.zeros_like(kb)),
                                    preferred_element_type=F32)
            return acc, tuple(new_rowsum)

        zcol = jnp.zeros((t, 1), F32)
        carry = lax.fori_loop(0, i, lambda j, c: step(j, c, False), (jnp.zeros((t, LANES), F32), (zcol, zcol)))
        acc, rowsum = step(i, carry, True)
        dq_ref[...] = acc
        for hh in range(2):
            delta_ref[hh] = delta[hh] + rowsum[hh]

    tile = pl.BlockSpec((t, LANES), lambda p, i: (i, p))
    full = pl.BlockSpec((rows, LANES), lambda p, i: (0, p))
    colspec = pl.BlockSpec((2, t, 1), lambda p, i: (p, i, 0))
    return pl.pallas_call(
        body, grid=(B_HEADS // 2, nb),
        in_specs=[tile, full, full, pl.BlockSpec((B_HEADS, nb, t), lambda p, i: (0, 0, 0)), tile, tile, colspec],
        out_specs=[tile, colspec],
        out_shape=[jax.ShapeDtypeStruct((rows, B_WIDTH), F32), jax.ShapeDtypeStruct((B_HEADS, rows, 1), F32)],
        name=name, compiler_params=_cparams(("parallel", "parallel")),
    )(q, k, v, nck_rows, o, do, lse_col)


def _flash_bwd_dkv(name, q, k, v, nck_col, do, lse_rows, delta_rows):
    rows = q.shape[0]
    t = min(_ATT_T, rows)
    nb = rows // t

    def body(k_ref, v_ref, q_ref, do_ref, nck_ref, lse_ref, delta_ref, dk_ref, dv_ref, dn_ref):
        pair, j = pl.program_id(0), pl.program_id(1)
        k2 = k_ref[...]
        v2 = v_ref[...]
        row, col = _causal(t)
        masks = _head_masks()
        kh = [jnp.where(hm, k2, jnp.zeros_like(k2)) for hm in masks]
        vh = [jnp.where(hm, v2, jnp.zeros_like(v2)) for hm in masks]
        nck = [nck_ref[hh] for hh in range(2)]

        def step(i, carry, diag):
            dk, dv, dn = carry
            start = pl.multiple_of(i * t, t)
            qb = q_ref[pl.ds(start, t), :]
            dob = do_ref[pl.ds(start, t), :].astype(BF16)
            dn_new = []
            for hh, hm in enumerate(masks):
                head = 2 * pair + hh
                st = lax.dot_general(kh[hh], qb, _DOT_DIMS["nt"], preferred_element_type=F32) + nck[hh]
                pt = jnp.exp(st - lse_ref[head, pl.ds(i, 1), :])
                if diag:
                    pt = jnp.where(row <= col, pt, 0.0)
                dpt = lax.dot_general(vh[hh], dob, _DOT_DIMS["nt"], preferred_element_type=F32)
                dst = pt * (dpt - delta_ref[head, pl.ds(i, 1), :])
                dv = dv + jnp.dot(pt.astype(BF16), jnp.where(hm, dob, jnp.zeros_like(dob)),
                                  preferred_element_type=F32)
                dk = dk + jnp.dot(dst.astype(BF16), jnp.where(hm, qb, jnp.zeros_like(qb)),
                                  preferred_element_type=F32)
                dn_new.append(dn[hh] + jnp.sum(dst, axis=1, keepdims=True))
            return dk, dv, tuple(dn_new)

        zero = jnp.zeros((t, LANES), F32)
        zcol = jnp.zeros((t, 1), F32)
        carry = step(j, (zero, zero, (zcol, zcol)), True)
        dk, dv, dn = lax.fori_loop(j + 1, nb, lambda i, c: step(i, c, False), carry)
        dk_ref[...] = dk
        dv_ref[...] = dv
        for hh in range(2):
            dn_ref[hh] = dn[hh]

    tile = pl.BlockSpec((t, LANES), lambda p, j: (j, p))
    full = pl.BlockSpec((rows, LANES), lambda p, j: (0, p))
    colspec = pl.BlockSpec((2, t, 1), lambda p, j: (p, j, 0))
    rowspec = pl.BlockSpec((B_HEADS, nb, t), lambda p, j: (0, 0, 0))
    big = jax.ShapeDtypeStruct((rows, B_WIDTH), F32)
    return pl.pallas_call(
        body, grid=(B_HEADS // 2, nb),
        in_specs=[tile, tile, full, full, colspec, rowspec, rowspec], out_specs=[tile, tile, colspec],
        out_shape=[big, big, jax.ShapeDtypeStruct((B_HEADS, rows, 1), F32)],
        name=name, compiler_params=_cparams(("parallel", "parallel")),
    )(k, v, q, do, nck_col, lse_rows, delta_rows)


_S5_ROWS = 256


def _s5_discretize(a_re, a_im, log_dt, b_re, b_im):
    dt = jnp.exp(log_dt)[:, None]
    mag = jnp.exp(a_re * dt)
    ab_re, ab_im = mag * jnp.cos(a_im * dt), mag * jnp.sin(a_im * dt)
    den = a_re * a_re + a_im * a_im
    nr, ni = ab_re - 1.0, ab_im
    cr = (nr * a_re + ni * a_im) / den
    ci = (ni * a_re - nr * a_im) / den
    bb_re = cr[..., None] * b_re - ci[..., None] * b_im
    bb_im = cr[..., None] * b_im + ci[..., None] * b_re
    return ab_re, ab_im, bb_re, bb_im


def _s5_block_diag(m):
    g, r, c = m.shape
    mb = m.reshape(S5_BLOCKS, 8, r, c)
    eye = jnp.eye(8, dtype=m.dtype)
    return jnp.einsum("bgrc,gh->bgrhc", mb, eye).reshape(S5_BLOCKS, 8 * r, 8 * c)


def _s5_block_diag_extract(m, r, c):
    mb = m.reshape(S5_BLOCKS, 8, r, 8, c)
    return jnp.einsum("bgrhc,gh->bgrc", mb, jnp.eye(8, dtype=m.dtype)).reshape(S5_GROUPS, r, c)


def _s5_tables(ab_re, ab_im, r):
    ar = jnp.broadcast_to(ab_re.reshape(1, -1), (r, S5_GROUPS * S5_STATE))
    ai = jnp.broadcast_to(ab_im.reshape(1, -1), (r, S5_GROUPS * S5_STATE))

    def mul(x, y):
        return x[0] * y[0] - x[1] * y[1], x[0] * y[1] + x[1] * y[0]

    return lax.associative_scan(mul, (ar, ai), axis=0)


def _scan_step(xr, xi, ar, ai, s, row, up):
    r = xr.shape[0]
    if up:
        ai = -ai
    if s < 8:
        if up:
            sr = jnp.where(row < r - s, pltpu.roll(xr, r - s, 0), 0.0)
            si = jnp.where(row < r - s, pltpu.roll(xi, r - s, 0), 0.0)
        else:
            sr = jnp.where(row >= s, pltpu.roll(xr, s, 0), 0.0)
            si = jnp.where(row >= s, pltpu.roll(xi, s, 0), 0.0)
        return xr + (ar * sr - ai * si), xi + (ar * si + ai * sr)
    if up:
        (dr, di), (sr, si) = (xr[:r - s], xi[:r - s]), (xr[s:], xi[s:])
        nr, ni = dr + (ar * sr - ai * si), di + (ar * si + ai * sr)
        return jnp.concatenate([nr, xr[r - s:]], axis=0), jnp.concatenate([ni, xi[r - s:]], axis=0)
    (dr, di), (sr, si) = (xr[s:], xi[s:]), (xr[:r - s], xi[:r - s])
    nr, ni = dr + (ar * sr - ai * si), di + (ar * si + ai * sr)
    return jnp.concatenate([xr[:s], nr], axis=0), jnp.concatenate([xi[:s], ni], axis=0)


def _s5_scan_tile(u_ref, bcat_ref, pr_ref, pi_ref, cin_r, cin_i, row):
    r = u_ref.shape[0]
    bu = jnp.dot(u_ref[...], bcat_ref[...], preferred_element_type=F32)
    xr, xi = bu[:, :S5_LANES], bu[:, S5_LANES:]
    ar, ai = pr_ref[0:1, :], pi_ref[0:1, :]
    first = lax.broadcasted_iota(jnp.int32, (8, 1), 0) == 0
    xr = jnp.concatenate([xr[:8] + jnp.where(first, ar * cin_r - ai * cin_i, 0.0), xr[8:]], axis=0)
    xi = jnp.concatenate([xi[:8] + jnp.where(first, ar * cin_i + ai * cin_r, 0.0), xi[8:]], axis=0)
    s = 1
    while s < r:
        xr, xi = _scan_step(xr, xi, pr_ref[s - 1:s, :], pi_ref[s - 1:s, :], s, row, False)
        s *= 2
    return xr, xi


def _s5_fwd(name, u, bcat, ccat, pw_re, pw_im):
    rows = u.shape[0]
    r = pw_re.shape[0]
    nt = rows // r

    def body(u_ref, bcat_ref, ccat_ref, pr_ref, pi_ref, y_ref, xin_ref, carry):
        @pl.when(pl.program_id(1) == 0)
        def _():
            carry[...] = jnp.zeros(carry.shape, carry.dtype)

        row = lax.broadcasted_iota(jnp.int32, (r, 1), 0)
        xin_ref[...] = carry[...]
        xr, xi = _s5_scan_tile(u_ref, bcat_ref, pr_ref, pi_ref, carry[0:1, :S5_LANES], carry[0:1, S5_LANES:], row)
        xcat = jnp.concatenate([xr, xi], axis=1)
        carry[...] = jnp.broadcast_to(xcat[r - 1:r, :], carry.shape)
        y_ref[...] = jnp.dot(xcat.astype(BF16), ccat_ref[...], preferred_element_type=F32)

    return pl.pallas_call(
        body, grid=(S5_BLOCKS, nt),
        in_specs=[pl.BlockSpec((r, LANES), lambda b, i: (i, b)),
                  pl.BlockSpec((None, LANES, 2 * S5_LANES), lambda b, i: (b, 0, 0)),
                  pl.BlockSpec((None, 2 * S5_LANES, LANES), lambda b, i: (b, 0, 0)),
                  pl.BlockSpec((r, S5_LANES), lambda b, i: (0, b)), pl.BlockSpec((r, S5_LANES), lambda b, i: (0, b))],
        out_specs=[pl.BlockSpec((r, LANES), lambda b, i: (i, b)),
                   pl.BlockSpec((None, 8, 2 * S5_LANES), lambda b, i: (b, i, 0))],
        out_shape=[jax.ShapeDtypeStruct((rows, D_MODEL), F32),
                   jax.ShapeDtypeStruct((S5_BLOCKS, 8 * nt, 2 * S5_LANES), F32)],
        scratch_shapes=[pltpu.VMEM((8, 2 * S5_LANES), F32)], name=name,
        compiler_params=_cparams(("parallel", "arbitrary")),
    )(u, bcat, ccat, pw_re, pw_im)


def _s5_bwd(name, u, dy, xin, bcat, ccat, pw_re, pw_im):
    rows = u.shape[0]
    r = pw_re.shape[0]
    nt = rows // r

    def body(u_ref, dy_ref, xin_ref, bcat_ref, ccat_ref, pr_ref, pi_ref,
             du_ref, db_ref, dc_ref, dar_ref, dai_ref, carry):
        @pl.when(pl.program_id(1) == 0)
        def _():
            carry[...] = jnp.zeros(carry.shape, carry.dtype)
            for ref in (db_ref, dc_ref, dar_ref, dai_ref):
                ref[...] = jnp.zeros(ref.shape, ref.dtype)

        row = lax.broadcasted_iota(jnp.int32, (r, 1), 0)
        cin_r, cin_i = xin_ref[0:1, :S5_LANES], xin_ref[0:1, S5_LANES:]
        xr, xi = _s5_scan_tile(u_ref, bcat_ref, pr_ref, pi_ref, cin_r, cin_i, row)
        dy_b = dy_ref[...].astype(BF16)
        xcat = jnp.concatenate([xr, xi], axis=1).astype(BF16)
        dc_ref[...] += lax.dot_general(xcat, dy_b, _DOT_DIMS["tn"], preferred_element_type=F32)
        g = lax.dot_general(dy_b, ccat_ref[...], _DOT_DIMS["nt"], preferred_element_type=F32)
        lr, li = g[:, :S5_LANES], g[:, S5_LANES:]
        nr, ni = carry[0:1, :S5_LANES], carry[0:1, S5_LANES:]
        ar, ai = pr_ref[0:1, :], pi_ref[0:1, :]
        final = lax.broadcasted_iota(jnp.int32, (8, 1), 0) == 7
        lr = jnp.concatenate([lr[:r - 8], lr[r - 8:] + jnp.where(final, ar * nr + ai * ni, 0.0)], axis=0)
        li = jnp.concatenate([li[:r - 8], li[r - 8:] + jnp.where(final, ar * ni - ai * nr, 0.0)], axis=0)
        s = 1
        while s < r:
            lr, li = _scan_step(lr, li, pr_ref[s - 1:s, :], pi_ref[s - 1:s, :], s, row, True)
            s *= 2
        carry[...] = jnp.broadcast_to(jnp.concatenate([lr[0:1, :], li[0:1, :]], axis=1), carry.shape)
        lcat = jnp.concatenate([lr, li], axis=1).astype(BF16)
        du_ref[...] = lax.dot_general(lcat, bcat_ref[...], _DOT_DIMS["nt"], preferred_element_type=F32)
        db_ref[...] += lax.dot_general(u_ref[...], lcat, _DOT_DIMS["tn"], preferred_element_type=F32)
        pxr = jnp.where(row == 0, cin_r, pltpu.roll(xr, 1, 0))
        pxi = jnp.where(row == 0, cin_i, pltpu.roll(xi, 1, 0))
        dar_ref[...] += jnp.sum((lr * pxr + li * pxi).reshape(r // 8, 8, S5_LANES), axis=0)
        dai_ref[...] += jnp.sum((li * pxr - lr * pxi).reshape(r // 8, 8, S5_LANES), axis=0)

    rev = lambda b, i: (nt - 1 - i, b)
    tab = pl.BlockSpec((r, S5_LANES), lambda b, i: (0, b))
    return pl.pallas_call(
        body, grid=(S5_BLOCKS, nt),
        in_specs=[pl.BlockSpec((r, LANES), rev), pl.BlockSpec((r, LANES), rev),
                  pl.BlockSpec((None, 8, 2 * S5_LANES), lambda b, i: (b, nt - 1 - i, 0)),
                  pl.BlockSpec((None, LANES, 2 * S5_LANES), lambda b, i: (b, 0, 0)),
                  pl.BlockSpec((None, 2 * S5_LANES, LANES), lambda b, i: (b, 0, 0)), tab, tab],
        out_specs=[pl.BlockSpec((r, LANES), rev),
                   pl.BlockSpec((None, LANES, 2 * S5_LANES), lambda b, i: (b, 0, 0)),
                   pl.BlockSpec((None, 2 * S5_LANES, LANES), lambda b, i: (b, 0, 0)),
                   pl.BlockSpec((None, 8, S5_LANES), lambda b, i: (b, 0, 0)),
                   pl.BlockSpec((None, 8, S5_LANES), lambda b, i: (b, 0, 0))],
        out_shape=[jax.ShapeDtypeStruct((rows, D_MODEL), F32),
                   jax.ShapeDtypeStruct((S5_BLOCKS, LANES, 2 * S5_LANES), F32),
                   jax.ShapeDtypeStruct((S5_BLOCKS, 2 * S5_LANES, LANES), F32),
                   jax.ShapeDtypeStruct((S5_BLOCKS, 8, S5_LANES), F32),
                   jax.ShapeDtypeStruct((S5_BLOCKS, 8, S5_LANES), F32)],
        scratch_shapes=[pltpu.VMEM((8, 2 * S5_LANES), F32)], name=name,
        compiler_params=_cparams(("parallel", "arbitrary")),
    )(u, dy, xin, bcat, ccat, pw_re, pw_im)


def _ones_gain():
    return jnp.ones((1, D_MODEL), F32)


def _channel_fwd(i, x1, p_i, w, rp, hn=None, next_norm=None):
    if hn is None:
        hn, = _rmsnorm_fwd(f"ffn_norm_{i}", x1, rp["norm_ffn"][i][None], [BF16])
    hg = _mm(f"ffn_up_g_{i}", hn, w["up_g"], tn=1408)
    hu = _mm(f"ffn_up_u_{i}", hn, w["up_u"], tn=1408)
    a = _convffn_fwd(f"ffn_conv_{i}", hg, hu, w["cw_g"], w["cw_u"], w["cb_g"], w["cb_u"])
    x2, r = _mm(f"ffn_down_{i}", a, w["down"], res=x1, tk=1408, norm_gain=_ones_gain())
    zg = _mm(f"ple_gate_{i}", r, w["ple_gate"])
    pp = _mm(f"ple_proj_{i}", p_i, w["ple_proj"])
    saved = dict(x1=x1, hn=hn, hg=hg, hu=hu, a=a, x2=x2, r=r, zg=zg, pp=pp, p_i=p_i)
    if next_norm is None:
        x3, = _rows(f"ple_out_{i}", lambda xv, zv, pv: (xv + _sigmoid(zv) * pv,), [x2, zg, pp], [], [(D_MODEL, F32)])
        return x3, None, saved
    gain, dtypes = next_norm

    def ple_out_norm(xv, zv, pv, gv):
        x3v = xv + _sigmoid(zv) * pv
        h = (x3v * _rstd(x3v)) * gv
        return (x3v,) + tuple(h for _ in dtypes)

    x3, *h_next = _rows(f"ple_out_{i}", ple_out_norm, [x2, zg, pp], [gain],
                        [(D_MODEL, F32)] + [(D_MODEL, dt) for dt in dtypes])
    return x3, h_next, saved


def _channel_bwd(i, dx3, sv, w, rp):
    def ple_bwd(dv, zv, pv):
        gate = _sigmoid(zv)
        return dv * gate, (dv * pv) * (gate * (1.0 - gate))

    dpp, dzg = _rows(f"ple_out_bwd_{i}", ple_bwd, [dx3, sv["zg"], sv["pp"]], [], [(D_MODEL, BF16), (D_MODEL, BF16)])
    g = {}
    g["ple_proj"] = _mm(f"ple_proj_dw_{i}", sv["p_i"], dpp, "tn")
    g["ple_gate"] = _mm(f"ple_gate_dw_{i}", sv["r"], dzg, "tn")
    dx2, _ = _mm_norm_bwd(f"ple_gate_dx_{i}", dzg, w["ple_gate"], "nt", sv["x2"], dx3, _ones_gain())
    da = _mm(f"ffn_down_dx_{i}", dx2, w["down"], "nt", tn=1408)
    g["down"] = _mm(f"ffn_down_dw_{i}", sv["a"], dx2, "tn", tm=1408)
    dhg, dhu, g["cw_g"], g["cw_u"], dbg, dbu = _convffn_bwd(
        f"ffn_conv_bwd_{i}", da, sv["hg"], sv["hu"], w["cw_g"], w["cw_u"], w["cb_g"], w["cb_u"])
    g["conv_b"] = jnp.concatenate([dbg, dbu], axis=1)[0]
    g["up_g"] = _mm(f"ffn_up_g_dw_{i}", sv["hn"], dhg, "tn", tn=1408)
    g["up_u"] = _mm(f"ffn_up_u_dw_{i}", sv["hn"], dhu, "tn", tn=1408)
    dhn = _mm(f"ffn_up_g_dx_{i}", dhg, w["up_g"], "nt", tk=1408)
    dx1, dgf = _mm_norm_bwd(f"ffn_up_u_dx_{i}", dhu, w["up_u"], "nt", sv["x1"], dx2, rp["norm_ffn"][i][None],
                            res=dhn, tk=1408)
    g["norm_ffn"] = dgf[0]
    return dx1, g


def _even_consts(e, rp):
    tri = jnp.tril(jnp.ones((A_CHUNK, A_CHUNK), dtype=bool))
    w_tril = jnp.where(tri[None], rp["ev_w_spatial"][e], 0.0).astype(BF16)
    b_exp = jnp.broadcast_to(rp["ev_b_spatial"][e][:, :, None], (A_GROUPS, A_CHUNK, LANES))
    seg = np.arange(B_WIDTH) // B_HEAD_DIM
    bd = jnp.asarray((seg[:, None] == seg[None, :]).astype(np.float32) / B_HEAD_DIM)
    return dict(
        tri=tri, w_tril=w_tril, w_tril_t=jnp.swapaxes(w_tril, 1, 2), b_exp=b_exp, bd=bd,
        v_gain=rp["ev_v_norm"][e][None], qg=jnp.tile(rp["ev_q_norm"][e], B_HEADS)[None],
        kg=jnp.tile(rp["ev_k_norm"][e], B_HEADS)[None],
        bf=jnp.pad(rp["ev_b_fgate"][e], (0, LANES - B_HEADS))[None])


def _even_fwd(i, x, w, rp, h_in=None):
    e = i // 2
    c = _even_consts(e, rp)
    rows = x.shape[0]
    t = min(_ATT_T, rows)
    h, = h_in if h_in is not None else _rmsnorm_fwd(f"mix_norm_{i}", x, rp["norm_mix"][i][None], [BF16])
    zuv = _mm(f"in_uv_{i}", h, w["in_uv"])
    zqkv = _mm(f"in_qkv_{i}", h, w["in_qkv"], tn=768)
    zf = _mm(f"in_f_{i}", h, w["in_f"])
    ya = _gmlp_fwd(f"gmlp_{i}", zuv, c["v_gain"], c["w_tril"], c["b_exp"])
    q, k, v, ls = _qkv_prep_fwd(f"qkv_prep_{i}", zqkv, zf, c["qg"], c["kg"], c["bf"], c["bd"])
    csum = _cumsum_rows(f"forget_cumsum_{i}", ls)
    nck = -csum[:, :B_HEADS].T
    nck_rows = nck.reshape(B_HEADS, rows // t, t)
    nck_col = nck.reshape(B_HEADS, rows, 1)
    o, lse = _flash_fwd(f"attn_{i}", q, k, v, nck_rows)
    x1 = _mm(f"out_a_{i}", ya, w["out_a"], res=x)
    x1, hn = _mm(f"out_b_{i}", o, w["out_b"], res=x1, norm_gain=rp["norm_ffn"][i][None])
    return x1, hn, dict(x=x, h=h, zuv=zuv, zqkv=zqkv, zf=zf, ya=ya, q=q, k=k, v=v, nck_rows=nck_rows,
                          nck_col=nck_col, o=o, lse=lse)


def _even_bwd(i, dx1, sv, w, rp):
    e = i // 2
    c = _even_consts(e, rp)
    rows = dx1.shape[0]
    t = min(_ATT_T, rows)
    nb = rows // t
    g = {}
    dya = _mm(f"out_a_dx_{i}", dx1, w["out_a"], "nt")
    do = _mm(f"out_b_dx_{i}", dx1, w["out_b"], "nt")
    g["out_a"] = _mm(f"out_a_dw_{i}", sv["ya"], dx1, "tn")
    g["out_b"] = _mm(f"out_b_dw_{i}", sv["o"], dx1, "tn")
    dq, delta = _flash_bwd_dq(f"attn_dq_{i}", sv["q"], sv["k"], sv["v"], sv["nck_rows"], sv["o"], do, sv["lse"])
    dk, dv, dn = _flash_bwd_dkv(f"attn_dkv_{i}", sv["q"], sv["k"], sv["v"], sv["nck_col"], do,
                                sv["lse"].reshape(B_HEADS, nb, t), delta.reshape(B_HEADS, nb, t))
    dcs = jnp.pad(-dn.reshape(B_HEADS, rows).T, ((0, 0), (0, LANES - B_HEADS)))
    dls = _cumsum_rows(f"forget_cumsum_bwd_{i}", dcs, reverse=True)
    dzqkv, dzf, dqg, dkg, dbf = _qkv_prep_bwd(f"qkv_prep_bwd_{i}", sv["zqkv"], sv["zf"], dq, dk, dv, dls,
                                              c["qg"], c["kg"], c["bf"], c["bd"])
    dzuv, dws, dbs, dvg = _gmlp_bwd(f"gmlp_bwd_{i}", sv["zuv"], dya, c["v_gain"], c["w_tril"], c["w_tril_t"],
                                    c["b_exp"])
    g["in_uv"] = _mm(f"in_uv_dw_{i}", sv["h"], dzuv, "tn")
    g["in_qkv"] = _mm(f"in_qkv_dw_{i}", sv["h"], dzqkv, "tn", tn=768)
    g["in_f"] = _mm(f"in_f_dw_{i}", sv["h"], dzf, "tn")
    dh = _mm(f"in_uv_dx_{i}", dzuv, w["in_uv"], "nt")
    dh = _mm(f"in_qkv_dx_{i}", dzqkv, w["in_qkv"], "nt", res=dh, tk=768)
    dx, dgm = _mm_norm_bwd(f"in_f_dx_{i}", dzf, w["in_f"], "nt", sv["x"], dx1, rp["norm_mix"][i][None], res=dh)
    g["norm_mix"] = dgm[0]
    g["ev_b_fgate"] = dbf[0, :B_HEADS]
    g["ev_q_norm"] = dqg.reshape(B_HEADS, B_HEAD_DIM).sum(axis=0)
    g["ev_k_norm"] = dkg.reshape(B_HEADS, B_HEAD_DIM).sum(axis=0)
    g["ev_v_norm"] = dvg[0]
    g["ev_w_spatial"] = jnp.where(c["tri"][None], dws, 0.0)
    g["ev_b_spatial"] = dbs.sum(axis=-1)
    return dx, g


def _s5_consts(o, rp, r):
    prm = (rp["od_a_re"][o], rp["od_a_im"][o], rp["od_log_dt"][o], rp["od_b_re"][o], rp["od_b_im"][o])
    (ab_re, ab_im, bb_re, bb_im), vjp = jax.vjp(_s5_discretize, *prm)
    bcat = jnp.concatenate([_s5_block_diag(bb_re.transpose(0, 2, 1)), _s5_block_diag(bb_im.transpose(0, 2, 1))], axis=2)
    c_re, c_im = rp["od_c_re"][o], rp["od_c_im"][o]
    ccat = jnp.concatenate([_s5_block_diag(c_re.transpose(0, 2, 1)), _s5_block_diag(-c_im.transpose(0, 2, 1))], axis=1)
    pw_re, pw_im = _s5_tables(ab_re, ab_im, r)
    return dict(vjp=vjp, bcat=bcat.astype(BF16), ccat=ccat.astype(BF16), pw_re=pw_re, pw_im=pw_im)


def _odd_fwd(i, x, w, rp, h_in=None):
    o = i // 2
    rows = x.shape[0]
    c = _s5_consts(o, rp, min(_S5_ROWS, rows))
    hb, hf = h_in if h_in is not None else _rmsnorm_fwd(f"mix_norm_{i}", x, rp["norm_mix"][i][None], [BF16, F32])
    ys, xin = _s5_fwd(f"s5_{i}", hb, c["bcat"], c["ccat"], c["pw_re"], c["pw_im"])

    def skip_gelu(yv, hv, dv):
        y = yv + dv * hv
        return y, _gelu(y)

    y, ge = _rows(f"s5_skip_gelu_{i}", skip_gelu, [ys, hf], [w["od_d"]], [(D_MODEL, F32), (D_MODEL, BF16)])
    gl = _mm(f"glu_{i}", ge, w["glu"])

    def glu_out(xv, gv, nv):
        x1v = xv + gv[:, :D_MODEL] * _sigmoid(gv[:, D_MODEL:])
        return x1v, (x1v * _rstd(x1v)) * nv

    x1, hn = _rows(f"glu_out_{i}", glu_out, [x, gl], [rp["norm_ffn"][i][None]], [(D_MODEL, F32), (D_MODEL, BF16)])
    return x1, hn, dict(x=x, hb=hb, hf=hf, xin=xin, y=y, ge=ge, gl=gl, c=c)


def _odd_bwd(i, dx1, sv, w, rp):
    o = i // 2
    c = sv["c"]
    g = {}

    def glu_bwd(dv, gv):
        ga, gb = gv[:, :D_MODEL], gv[:, D_MODEL:]
        sg = _sigmoid(gb)
        return (jnp.concatenate([dv * sg, (dv * ga) * (sg * (1.0 - sg))], axis=1),)

    dgl, = _rows(f"glu_out_bwd_{i}", glu_bwd, [dx1, sv["gl"]], [], [(2 * D_MODEL, BF16)])
    g["glu"] = _mm(f"glu_dw_{i}", sv["ge"], dgl, "tn")
    dge = _mm(f"glu_dx_{i}", dgl, w["glu"], "nt")

    def gelu_bwd(dv, yv, hv):
        dy = dv * _gelu_grad(yv)
        return dy, jnp.sum(dy * hv, axis=0, keepdims=True)

    dy, dd = _rows(f"s5_skip_gelu_bwd_{i}", gelu_bwd, [dge, sv["y"], sv["hf"]], [], [(D_MODEL, F32)],
                   accs=[(1, D_MODEL)])
    g["od_d"] = dd[0]
    du, db, dc, dar, dai = _s5_bwd(f"s5_bwd_{i}", sv["hb"], dy, sv["xin"], c["bcat"], c["ccat"], c["pw_re"],
                                   c["pw_im"])
    dab_re = dar.sum(axis=1).reshape(S5_GROUPS, S5_STATE)
    dab_im = dai.sum(axis=1).reshape(S5_GROUPS, S5_STATE)
    dbb_re = _s5_block_diag_extract(db[:, :, :S5_LANES], S5_GROUP_CH, S5_STATE).transpose(0, 2, 1)
    dbb_im = _s5_block_diag_extract(db[:, :, S5_LANES:], S5_GROUP_CH, S5_STATE).transpose(0, 2, 1)
    g["od_a_re"], g["od_a_im"], g["od_log_dt"], g["od_b_re"], g["od_b_im"] = c["vjp"]((dab_re, dab_im, dbb_re, dbb_im))
    g["od_c_re"] = _s5_block_diag_extract(dc[:, :S5_LANES, :], S5_STATE, S5_GROUP_CH).transpose(0, 2, 1)
    g["od_c_im"] = -_s5_block_diag_extract(dc[:, S5_LANES:, :], S5_STATE, S5_GROUP_CH).transpose(0, 2, 1)

    def norm_bwd(xv, duv, dyv, drv, gv, dv):
        dh = duv + dv * dyv
        r = _rstd(xv)
        xh = xv * r
        dhg = dh * gv
        dx = drv + r * (dhg - xh * jnp.mean(dhg * xh, axis=-1, keepdims=True))
        return dx, jnp.sum(dh * xh, axis=0, keepdims=True)

    dx, dgm = _rows(f"mix_norm_bwd_{i}", norm_bwd, [sv["x"], du, dy, dx1], [rp["norm_mix"][i][None], w["od_d"]],
                    [(D_MODEL, F32)], accs=[(1, D_MODEL)])
    g["norm_mix"] = dgm[0]
    return dx, g


def _local_step(x, p, target, lw, rp):
    saved = []
    h_next = None
    for i in range(DEPTH):
        x, hn, s_mix = (_even_fwd if i % 2 == 0 else _odd_fwd)(i, x, lw[i], rp, h_next)
        nxt = None
        if i + 1 < DEPTH:
            nxt = (rp["norm_mix"][i + 1][None], [BF16, F32] if (i + 1) % 2 else [BF16])
        x, h_next, s_ch = _channel_fwd(i, x, p[i], lw[i], rp, hn, nxt)
        saved.append((s_mix, s_ch))

    def loss_fn(yv, tv):
        diff = yv - tv
        return diff * (1.0 / D_MODEL), jnp.sum(diff * diff, axis=0, keepdims=True)

    dx, sq = _rows("loss", loss_fn, [x, target], [], [(D_MODEL, F32)], accs=[(1, D_MODEL)])
    loss = 0.5 * jnp.sum(sq) / D_MODEL
    grads = [None] * DEPTH
    for i in reversed(range(DEPTH)):
        s_mix, s_ch = saved[i]
        dx, g_ch = _channel_bwd(i, dx, s_ch, lw[i], rp)
        dx, g_mix = (_even_bwd if i % 2 == 0 else _odd_bwd)(i, dx, s_mix, lw[i], rp)
        grads[i] = {**g_ch, **g_mix}
    return loss, dx, grads


WEIGHT_ORDER = ["norm_mix", "norm_ffn", "ev_w_in", "ev_b_fgate", "ev_q_norm", "ev_k_norm", "ev_v_norm", "ev_w_spatial",
                "ev_b_spatial", "ev_w_out", "od_a_re", "od_a_im", "od_log_dt", "od_b_re", "od_b_im", "od_c_re",
                "od_c_im", "od_d", "od_w_glu", "ffn_w_up", "ffn_conv_w", "ffn_conv_b", "ffn_w_down", "ple_w_proj",
                "ple_w_gate"]
SHARD_AXIS = {"ev_w_in": 2, "ev_w_out": 1, "od_d": 1, "od_w_glu": 2, "ffn_w_up": 2, "ffn_conv_w": 2, "ffn_w_down": 1,
              "ple_w_proj": 2, "ple_w_gate": 1}
BIG_WEIGHTS = [n for n in WEIGHT_ORDER if n in SHARD_AXIS]
SMALL_WEIGHTS = [n for n in WEIGHT_ORDER if n not in SHARD_AXIS]
KEPT_F32 = ("od_d", "ffn_conv_w")
IN_UV, IN_QKV_END, IN_COLS = 2 * A_WIDTH, 2 * A_WIDTH + 3 * B_WIDTH, 2 * A_WIDTH + 3 * B_WIDTH + B_HEADS


def _layer_weights(i, full, rp):
    w = {}
    up, cw, cb = full["ffn_w_up"][i], full["ffn_conv_w"][i], rp["ffn_conv_b"][i][None]
    w["up_g"], w["up_u"] = up[:, :D_FF], up[:, D_FF:]
    w["cw_g"], w["cw_u"] = cw[:, :D_FF], cw[:, D_FF:]
    w["cb_g"], w["cb_u"] = cb[:, :D_FF], cb[:, D_FF:]
    w["down"], w["ple_proj"], w["ple_gate"] = full["ffn_w_down"][i], full["ple_w_proj"][i], full["ple_w_gate"][i]
    if i % 2 == 0:
        win, wout = full["ev_w_in"][i // 2], full["ev_w_out"][i // 2]
        w["in_uv"], w["in_qkv"] = win[:, :IN_UV], win[:, IN_UV:IN_QKV_END]
        w["in_f"] = jnp.pad(win[:, IN_QKV_END:], ((0, 0), (0, LANES - B_HEADS)))
        w["out_a"], w["out_b"] = wout[:A_WIDTH], wout[A_WIDTH:]
    else:
        w["od_d"], w["glu"] = full["od_d"][i // 2][None], full["od_w_glu"][i // 2]
    return w


def _full_grads(grads):
    ev, od = [grads[i] for i in range(0, DEPTH, 2)], [grads[i] for i in range(1, DEPTH, 2)]
    out = {
        "norm_mix": jnp.stack([g["norm_mix"] for g in grads]), "norm_ffn": jnp.stack([g["norm_ffn"] for g in grads]),
        "ev_w_in": jnp.stack([jnp.concatenate([g["in_uv"], g["in_qkv"], g["in_f"][:, :B_HEADS]], axis=1) for g in ev]),
        "ev_w_out": jnp.stack([jnp.concatenate([g["out_a"], g["out_b"]], axis=0) for g in ev]),
        "od_w_glu": jnp.stack([g["glu"] for g in od]),
        "ffn_w_up": jnp.stack([jnp.concatenate([g["up_g"], g["up_u"]], axis=1) for g in grads]),
        "ffn_conv_w": jnp.stack([jnp.concatenate([g["cw_g"], g["cw_u"]], axis=1) for g in grads]),
        "ffn_conv_b": jnp.stack([g["conv_b"] for g in grads]),
        "ffn_w_down": jnp.stack([g["down"] for g in grads]),
        "ple_w_proj": jnp.stack([g["ple_proj"] for g in grads]),
        "ple_w_gate": jnp.stack([g["ple_gate"] for g in grads]),
    }
    for n in ("ev_b_fgate", "ev_q_norm", "ev_k_norm", "ev_v_norm", "ev_w_spatial", "ev_b_spatial"):
        out[n] = jnp.stack([g[n] for g in ev])
    for n in ("od_a_re", "od_a_im", "od_log_dt", "od_b_re", "od_b_im", "od_c_re", "od_c_im", "od_d"):
        out[n] = jnp.stack([g[n] for g in od])
    return out


def _pack(arrs, row_multiple):
    flat = jnp.concatenate([a.reshape(-1) for a in arrs])
    rows = -(-flat.shape[0] // (PACK_W * row_multiple)) * row_multiple
    return jnp.pad(flat, (0, rows * PACK_W - flat.shape[0])).reshape(rows, PACK_W)


def _unpack(buf, shapes):
    flat = buf.reshape(-1)
    out, at = [], 0
    for s in shapes:
        n = int(np.prod(s))
        out.append(flat[at:at + n].reshape(s))
        at += n
    return out


def _shard(name, a, k):
    ax = SHARD_AXIS[name]
    n = a.shape[ax] // N_CHIPS
    return lax.slice_in_dim(a, k * n, (k + 1) * n, axis=ax)


_ANY = pl.BlockSpec(memory_space=pl.ANY)


def _mesh_pos():
    return lax.axis_index("x"), lax.axis_index("y"), lax.axis_index("c")


def _other_chips(x, y):
    return [(1 - x, y), (x, 1 - y), (1 - x, 1 - y)]


def _gather_shards(name, shards):
    n = len(shards)

    def body(*refs):
        ins, outs = refs[:n], refs[n:2 * n]
        send_sems, recv_sems, local_sems = refs[2 * n:]
        x, y, c = _mesh_pos()
        sibling = (x, y, 1 - c)
        chips = _other_chips(x, y)

        def part(a, k, hc):
            half = shards[a].shape[0] // 2
            return outs[a].at[k, pl.ds(hc * half, half), :]

        def copy(sem, src, dst, to):
            return pltpu.make_async_remote_copy(src_ref=src, dst_ref=dst, send_sem=send_sems.at[sem],
                                                recv_sem=recv_sems.at[sem], device_id=to, device_id_type=MESH)

        local, sent, passed = [], [], []
        for a in range(n):
            half = shards[a].shape[0] // 2
            local.append(pltpu.make_async_copy(ins[a], outs[a].at[2 * x + y], local_sems.at[a]))
            local[-1].start()
            for j, (cx, cy) in enumerate(chips):
                sent.append(copy(6 * a + j, ins[a].at[pl.ds(c * half, half), :], part(a, 2 * x + y, c), (cx, cy, c)))
                sent[-1].start()
        for a in range(n):
            for j, (cx, cy) in enumerate(chips):
                blk = part(a, 2 * cx + cy, c)
                copy(6 * a + j, blk, blk, (cx, cy, c)).wait_recv()
                passed.append(copy(6 * a + 3 + j, blk, blk, sibling))
                passed[-1].start()
        for a in range(n):
            for j, (cx, cy) in enumerate(chips):
                blk = part(a, 2 * cx + cy, 1 - c)
                copy(6 * a + 3 + j, blk, blk, sibling).wait_recv()
        for cp in sent + passed:
            cp.wait_send()
        for cp in local:
            cp.wait()

    return pl.pallas_call(
        body, out_shape=[jax.ShapeDtypeStruct((N_CHIPS,) + s.shape, s.dtype) for s in shards],
        in_specs=[_ANY] * n, out_specs=[_ANY] * n,
        scratch_shapes=[pltpu.SemaphoreType.DMA((6 * n,)), pltpu.SemaphoreType.DMA((6 * n,)),
                        pltpu.SemaphoreType.DMA((n,))],
        name=name,
    )(*shards)


def _swap_halves(name, arrs):
    n = len(arrs)

    def body(*refs):
        ins, outs = refs[:n], refs[n:2 * n]
        send_sems, recv_sems = refs[2 * n:]
        x, y, c = _mesh_pos()
        cps = []
        for a in range(n):
            half = arrs[a].shape[1] // 2
            cps.append(pltpu.make_async_remote_copy(
                src_ref=ins[a].at[:, pl.ds((1 - c) * half, half), :], dst_ref=outs[a], send_sem=send_sems.at[a],
                recv_sem=recv_sems.at[a], device_id=(x, y, 1 - c), device_id_type=MESH))
            cps[-1].start()
        for cp in cps:
            cp.wait()

    return pl.pallas_call(
        body, out_shape=[jax.ShapeDtypeStruct((a.shape[0], a.shape[1] // 2, a.shape[2]), a.dtype) for a in arrs],
        in_specs=[_ANY] * n, out_specs=[_ANY] * n,
        scratch_shapes=[pltpu.SemaphoreType.DMA((n,)), pltpu.SemaphoreType.DMA((n,))], name=name,
    )(*arrs)


def _send_to_owner_chips(name, arrs):
    n = len(arrs)

    def body(*refs):
        ins, outs = refs[:n], refs[n:2 * n]
        send_sems, recv_sems = refs[2 * n:]
        x, y, c = _mesh_pos()
        cps = []
        for a in range(n):
            for j, (cx, cy) in enumerate(_other_chips(x, y)):
                cps.append(pltpu.make_async_remote_copy(
                    src_ref=ins[a].at[2 * cx + cy], dst_ref=outs[a].at[j], send_sem=send_sems.at[3 * a + j],
                    recv_sem=recv_sems.at[3 * a + j], device_id=(cx, cy, c), device_id_type=MESH))
                cps[-1].start()
        for cp in cps:
            cp.wait()

    return pl.pallas_call(
        body, out_shape=[jax.ShapeDtypeStruct((3,) + a.shape[1:], a.dtype) for a in arrs],
        in_specs=[_ANY] * n, out_specs=[_ANY] * n,
        scratch_shapes=[pltpu.SemaphoreType.DMA((3 * n,)), pltpu.SemaphoreType.DMA((3 * n,))], name=name,
    )(*arrs)


def _swap_with_sibling(name, arrs):
    n = len(arrs)

    def body(*refs):
        ins, outs = refs[:n], refs[n:2 * n]
        send_sems, recv_sems = refs[2 * n:]
        x, y, c = _mesh_pos()
        cps = []
        for a in range(n):
            cps.append(pltpu.make_async_remote_copy(
                src_ref=ins[a], dst_ref=outs[a], send_sem=send_sems.at[a], recv_sem=recv_sems.at[a],
                device_id=(x, y, 1 - c), device_id_type=MESH))
            cps[-1].start()
        for cp in cps:
            cp.wait()

    return pl.pallas_call(
        body, out_shape=[jax.ShapeDtypeStruct(a.shape, a.dtype) for a in arrs],
        in_specs=[_ANY] * n, out_specs=[_ANY] * n,
        scratch_shapes=[pltpu.SemaphoreType.DMA((n,)), pltpu.SemaphoreType.DMA((n,))], name=name,
    )(*arrs)


def _all_gather_devices(name, a):
    rows, w = a.shape

    def body(a_ref, out_ref, send_sems, recv_sems, local_sem):
        x, y, c = _mesh_pos()
        me, sibling = (x, y, c), (x, y, 1 - c)
        chips = _other_chips(x, y)

        def slot(px, py, pc):
            return out_ref.at[4 * px + 2 * py + pc]

        def copy(sem, block, to, src=None):
            return pltpu.make_async_remote_copy(src_ref=slot(*block) if src is None else src, dst_ref=slot(*block),
                                                send_sem=send_sems.at[sem], recv_sem=recv_sems.at[sem], device_id=to,
                                                device_id_type=MESH)

        mine = pltpu.make_async_copy(a_ref, slot(*me), local_sem)
        mine.start()
        first = [copy(0, me, sibling, src=a_ref)]
        first += [copy(1 + j, me, (*chip, c), src=a_ref) for j, chip in enumerate(chips)]
        for cp in first:
            cp.start()
        passed = [copy(4 + j, (*chip, c), sibling) for j, chip in enumerate(chips)]
        for j, chip in enumerate(chips):
            copy(1 + j, (*chip, c), me).wait_recv()
            passed[j].start()
        copy(0, sibling, me).wait_recv()
        for j, chip in enumerate(chips):
            copy(4 + j, (*chip, 1 - c), me).wait_recv()
        for cp in first + passed:
            cp.wait_send()
        mine.wait()

    return pl.pallas_call(
        body, out_shape=jax.ShapeDtypeStruct((8, rows, w), a.dtype), in_specs=[_ANY], out_specs=_ANY,
        scratch_shapes=[pltpu.SemaphoreType.DMA((7,)), pltpu.SemaphoreType.DMA((7,)), pltpu.SemaphoreType.DMA],
        name=name,
    )(a)


_PACK_TILE = 256


def _sum_rows(name, arrs):
    def fn(*vals):
        tot = vals[0]
        for v in vals[1:]:
            tot = tot + v
        return (tot,)

    return _rows(name, fn, list(arrs), [], [(arrs[0].shape[1], F32)], tile=_PACK_TILE)[0]


def _adam_math(wv, gv, mv, vv):
    m2 = ADAM_B1 * mv + (1.0 - ADAM_B1) * gv
    v2 = ADAM_B2 * vv + (1.0 - ADAM_B2) * (gv * gv)
    m_hat = m2 / (1.0 - ADAM_B1 ** ADAM_STEP)
    v_hat = v2 / (1.0 - ADAM_B2 ** ADAM_STEP)
    delta = -ADAM_LR * (m_hat / (jnp.sqrt(v_hat) + ADAM_EPS) + ADAM_WD * wv)
    return delta, m2, v2


def _adamw(name, w, g, m, v):
    return _rows(name, _adam_math, [w, g, m, v], [], [(w.shape[1], F32)] * 3, tile=_PACK_TILE)


_INPUT_ORDER = (["x", "p"] + WEIGHT_ORDER + ["loss_target"] + ["m_" + n for n in WEIGHT_ORDER]
                + ["v_" + n for n in WEIGHT_ORDER])


_SUM_ROWS = 128


def _pair_sum(name, g, got, core):
    nk, rows, w = g.shape
    half = rows // 2
    nt = half // _SUM_ROWS

    def body(c_ref, g_ref, got_ref, o_ref, ob_ref):
        tot = g_ref[...] + got_ref[...]
        o_ref[...] = tot
        ob_ref[...] = tot.astype(BF16)

    spec = pl.BlockSpec((None, _SUM_ROWS, w), lambda k, i, c: (k, i, 0))
    grid_spec = pltpu.PrefetchScalarGridSpec(
        num_scalar_prefetch=1, grid=(nk, nt),
        in_specs=[pl.BlockSpec((None, _SUM_ROWS, w), lambda k, i, c: (k, c[0] * nt + i, 0)), spec],
        out_specs=[spec, spec])
    return pl.pallas_call(
        body, grid_spec=grid_spec, name=name, compiler_params=_cparams(("parallel", "parallel")),
        out_shape=[jax.ShapeDtypeStruct((nk, half, w), F32), jax.ShapeDtypeStruct((nk, half, w), BF16)])(core, g, got)


def _owner_sum(name, pair, owed, chip):
    _, half, w = pair.shape

    def body(k_ref, p_ref, a_ref, b_ref, c_ref, o_ref):
        o_ref[...] = ((p_ref[...] + a_ref[...].astype(F32)) + b_ref[...].astype(F32)) + c_ref[...].astype(F32)

    def owed_spec(j):
        return pl.BlockSpec((None, _SUM_ROWS, w), lambda i, k: (j, i, 0))

    grid_spec = pltpu.PrefetchScalarGridSpec(
        num_scalar_prefetch=1, grid=(half // _SUM_ROWS,),
        in_specs=[pl.BlockSpec((None, _SUM_ROWS, w), lambda i, k: (k[0], i, 0)), owed_spec(0), owed_spec(1),
                  owed_spec(2)],
        out_specs=pl.BlockSpec((_SUM_ROWS, w), lambda i, k: (i, 0)))
    return pl.pallas_call(body, grid_spec=grid_spec, out_shape=jax.ShapeDtypeStruct((half, w), F32), name=name,
                          compiler_params=_cparams(("parallel",)))(chip, pair, owed, owed, owed)


def _adamw_halves(name, w, mine, other, m, v, core):
    rows, wd = w.shape
    nh = (rows // 2) // _SUM_ROWS

    def body(c_ref, w_ref, a_ref, b_ref, m_ref, v_ref, g_ref, d_ref, m2_ref, v2_ref):
        own = (pl.program_id(0) // nh) == c_ref[0]
        g = jnp.where(own, a_ref[...], b_ref[...])
        g_ref[...] = g
        d_ref[...], m2_ref[...], v2_ref[...] = _adam_math(w_ref[...], g, m_ref[...], v_ref[...])

    full = pl.BlockSpec((_SUM_ROWS, wd), lambda i, c: (i, 0))
    part = pl.BlockSpec((_SUM_ROWS, wd), lambda i, c: (lax.rem(i, nh), 0))
    grid_spec = pltpu.PrefetchScalarGridSpec(num_scalar_prefetch=1, grid=(2 * nh,),
                                             in_specs=[full, part, part, full, full], out_specs=[full] * 4)
    return pl.pallas_call(body, grid_spec=grid_spec, out_shape=[jax.ShapeDtypeStruct((rows, wd), F32)] * 4, name=name,
                          compiler_params=_cparams(("parallel",)))(core, w, mine, other, m, v)


MATRIX_WEIGHTS = [n for n in BIG_WEIGHTS if n not in KEPT_F32]
TINY_SHARDED = [n for n in BIG_WEIGHTS if n in KEPT_F32]


def _as_rows(a):
    return a.reshape(-1, a.shape[-1])


def _owner_major(grads):
    ev, od = [grads[i] for i in range(0, DEPTH, 2)], [grads[i] for i in range(1, DEPTH, 2)]

    def cols(m, k, n):
        w = m.shape[1] // n
        return m[:, k * w:(k + 1) * w]

    def rows(m, k, n):
        r = m.shape[0] // n
        return m[k * r:(k + 1) * r]

    w_in = [jnp.concatenate([g["in_uv"], g["in_qkv"], g["in_f"][:, :B_HEADS]], axis=1) for g in ev]
    per_chip = {
        "ev_w_in": lambda k: [cols(m, k, N_CHIPS) for m in w_in],
        "ev_w_out": lambda k: [rows(g["out_a"] if k < 2 else g["out_b"], k % 2, 2) for g in ev],
        "od_w_glu": lambda k: [cols(g["glu"], k, N_CHIPS) for g in od],
        "ffn_w_up": lambda k: [cols(g["up_g"] if k < 2 else g["up_u"], k % 2, 2) for g in grads],
        "ffn_w_down": lambda k: [rows(g["down"], k, N_CHIPS) for g in grads],
        "ple_w_proj": lambda k: [cols(g["ple_proj"], k, N_CHIPS) for g in grads],
        "ple_w_gate": lambda k: [rows(g["ple_gate"], k, N_CHIPS) for g in grads],
    }
    return {n: jnp.stack([jnp.concatenate(per_chip[n](k), axis=0) for k in range(N_CHIPS)]) for n in MATRIX_WEIGHTS}


def _small_grads(grads):
    ev, od = [grads[i] for i in range(0, DEPTH, 2)], [grads[i] for i in range(1, DEPTH, 2)]
    out = {"norm_mix": jnp.stack([g["norm_mix"] for g in grads]), "norm_ffn": jnp.stack([g["norm_ffn"] for g in grads]),
           "ffn_conv_w": jnp.stack([jnp.concatenate([g["cw_g"], g["cw_u"]], axis=1) for g in grads]),
           "ffn_conv_b": jnp.stack([g["conv_b"] for g in grads])}
    for n in ("ev_b_fgate", "ev_q_norm", "ev_k_norm", "ev_v_norm", "ev_w_spatial", "ev_b_spatial"):
        out[n] = jnp.stack([g[n] for g in ev])
    for n in ("od_a_re", "od_a_im", "od_log_dt", "od_b_re", "od_b_im", "od_c_re", "od_c_im", "od_d"):
        out[n] = jnp.stack([g[n] for g in od])
    return out


def _step(a):
    xi, yi, ci = _mesh_pos()
    chip = 2 * xi + yi
    core_arr, chip_arr = ci.astype(jnp.int32).reshape(1), chip.astype(jnp.int32).reshape(1)
    rp = {n: a[n] for n in SMALL_WEIGHTS}

    tiny = _pack([a[n] for n in TINY_SHARDED], 32)
    gathered = _gather_shards("gather_weights", [_as_rows(a[n]).astype(BF16) for n in MATRIX_WEIGHTS] + [tiny])
    full = {}
    for n, g in zip(MATRIX_WEIGHTS, gathered):
        full[n] = jnp.concatenate([g[k].reshape(a[n].shape) for k in range(N_CHIPS)], axis=SHARD_AXIS[n])
    tiny_parts = [_unpack(gathered[-1][k], [a[n].shape for n in TINY_SHARDED]) for k in range(N_CHIPS)]
    for idx, n in enumerate(TINY_SHARDED):
        full[n] = jnp.concatenate([tiny_parts[k][idx] for k in range(N_CHIPS)], axis=SHARD_AXIS[n])
    lw = [_layer_weights(i, full, rp) for i in range(DEPTH)]

    loss_local, grad_x, grads = _local_step(a["x"][0], a["p"][:, 0], a["loss_target"][0], lw, rp)
    loss = lax.psum(loss_local, ("x", "y", "c"))

    contrib = _owner_major(grads)
    mats = [contrib[n] for n in MATRIX_WEIGHTS]
    got = _swap_halves("grad_pair_swap", mats)
    pair = [_pair_sum(f"grad_pair_sum_{n}", g, h, core_arr) for n, g, h in zip(MATRIX_WEIGHTS, mats, got)]
    owed = _send_to_owner_chips("grad_to_owner", [pb for _, pb in pair])
    mine = [_owner_sum(f"grad_owner_sum_{n}", p, o, chip_arr) for n, (p, _), o in zip(MATRIX_WEIGHTS, pair, owed)]
    theirs = _swap_with_sibling("grad_half_swap", mine)

    small_names = SMALL_WEIGHTS + TINY_SHARDED
    sg = _small_grads(grads)
    everyone = _all_gather_devices("small_grad_gather", _pack([sg[n] for n in small_names], _PACK_TILE))
    g_small = _sum_rows("small_grad_sum", [everyone[d] for d in range(8)])
    small_full = dict(zip(small_names, _unpack(g_small, [sg[n].shape for n in small_names])))

    out = {}
    for n, own_half, other_half in zip(MATRIX_WEIGHTS, mine, theirs):
        shape = a[n].shape
        g2d, delta, m2, v2 = _adamw_halves(f"adamw_{n}", _as_rows(a[n]), own_half, other_half,
                                           _as_rows(a["m_" + n]), _as_rows(a["v_" + n]), core_arr)
        for kind, val in (("grad", g2d), ("delta", delta), ("new_m", m2), ("new_v", v2)):
            out[kind + "_" + n] = val.reshape(shape)
    g_sm = {n: small_full[n] for n in SMALL_WEIGHTS}
    for n in TINY_SHARDED:
        width = a[n].shape[SHARD_AXIS[n]]
        g_sm[n] = lax.dynamic_slice_in_dim(small_full[n], chip * width, width, axis=SHARD_AXIS[n])
    shapes = [a[n].shape for n in small_names]
    w, m, v = (_pack([a[pre + n] for n in small_names], _PACK_TILE) for pre in ("", "m_", "v_"))
    g = _pack([g_sm[n] for n in small_names], _PACK_TILE)
    delta, m2, v2 = _adamw("adamw_small", w, g, m, v)
    for kind, buf in (("delta", delta), ("new_m", m2), ("new_v", v2)):
        for n, val in zip(small_names, _unpack(buf, shapes)):
            out[kind + "_" + n] = val
    for n in small_names:
        out["grad_" + n] = g_sm[n]
    res = [loss, grad_x[None]]
    for kind in ("grad", "delta", "new_m", "new_v"):
        res += [out[kind + "_" + n] for n in WEIGHT_ORDER]
    return tuple(res)


def kernel(x, p, norm_mix, norm_ffn, ev_w_in, ev_b_fgate, ev_q_norm, ev_k_norm, ev_v_norm, ev_w_spatial, ev_b_spatial, ev_w_out, od_a_re, od_a_im, od_log_dt, od_b_re, od_b_im, od_c_re, od_c_im, od_d, od_w_glu, ffn_w_up, ffn_conv_w, ffn_conv_b, ffn_w_down, ple_w_proj, ple_w_gate, loss_target, m_norm_mix, m_norm_ffn, m_ev_w_in, m_ev_b_fgate, m_ev_q_norm, m_ev_k_norm, m_ev_v_norm, m_ev_w_spatial, m_ev_b_spatial, m_ev_w_out, m_od_a_re, m_od_a_im, m_od_log_dt, m_od_b_re, m_od_b_im, m_od_c_re, m_od_c_im, m_od_d, m_od_w_glu, m_ffn_w_up, m_ffn_conv_w, m_ffn_conv_b, m_ffn_w_down, m_ple_w_proj, m_ple_w_gate, v_norm_mix, v_norm_ffn, v_ev_w_in, v_ev_b_fgate, v_ev_q_norm, v_ev_k_norm, v_ev_v_norm, v_ev_w_spatial, v_ev_b_spatial, v_ev_w_out, v_od_a_re, v_od_a_im, v_od_log_dt, v_od_b_re, v_od_b_im, v_od_c_re, v_od_c_im, v_od_d, v_od_w_glu, v_ffn_w_up, v_ffn_conv_w, v_ffn_conv_b, v_ffn_w_down, v_ple_w_proj, v_ple_w_gate):
    args = (x, p, norm_mix, norm_ffn, ev_w_in, ev_b_fgate, ev_q_norm, ev_k_norm, ev_v_norm, ev_w_spatial, ev_b_spatial, ev_w_out, od_a_re, od_a_im, od_log_dt, od_b_re, od_b_im, od_c_re, od_c_im, od_d, od_w_glu, ffn_w_up, ffn_conv_w, ffn_conv_b, ffn_w_down, ple_w_proj, ple_w_gate, loss_target, m_norm_mix, m_norm_ffn, m_ev_w_in, m_ev_b_fgate, m_ev_q_norm, m_ev_k_norm, m_ev_v_norm, m_ev_w_spatial, m_ev_b_spatial, m_ev_w_out, m_od_a_re, m_od_a_im, m_od_log_dt, m_od_b_re, m_od_b_im, m_od_c_re, m_od_c_im, m_od_d, m_od_w_glu, m_ffn_w_up, m_ffn_conv_w, m_ffn_conv_b, m_ffn_w_down, m_ple_w_proj, m_ple_w_gate, v_norm_mix, v_norm_ffn, v_ev_w_in, v_ev_b_fgate, v_ev_q_norm, v_ev_k_norm, v_ev_v_norm, v_ev_w_spatial, v_ev_b_spatial, v_ev_w_out, v_od_a_re, v_od_a_im, v_od_log_dt, v_od_b_re, v_od_b_im, v_od_c_re, v_od_c_im, v_od_d, v_od_w_glu, v_ffn_w_up, v_ffn_conv_w, v_ffn_conv_b, v_ffn_w_down, v_ple_w_proj, v_ple_w_gate)
    return _step(dict(zip(_INPUT_ORDER, args)))
```

```python
import functools
import math

import jax
import jax.numpy as jnp
import numpy as np
from jax import lax
from jax.experimental import pallas as pl
from jax.experimental.pallas import tpu as pltpu

F32 = jnp.float32
BF16 = jnp.bfloat16
MESH = pl.DeviceIdType.MESH

V7X_VMEM_LIMIT_BYTES = 56 * 1024 * 1024
LANES = 128

D_MODEL = 1024
DEPTH = 4
A_GROUPS = 4
A_CHUNK = 128
A_WIDTH = 512
B_HEADS = 8
B_HEAD_DIM = 64
B_WIDTH = 512
S5_GROUP_CH = 16
S5_GROUPS = 64
S5_STATE = 64
S5_BLOCKS = 8
S5_LANES = 512
D_FF = 2816
PLE_DIM = 256
EPS = 1e-6
NEG_INF = -1e30

ADAM_LR = 0.001
ADAM_B1 = 0.9
ADAM_B2 = 0.999
ADAM_EPS = 1e-08
ADAM_WD = 0.01
ADAM_STEP = 10

N_CHIPS = 4
PACK_W = 1024


def _cparams(sem):
    return pltpu.CompilerParams(dimension_semantics=sem, vmem_limit_bytes=V7X_VMEM_LIMIT_BYTES)


def _pick(n, target):
    if n <= target:
        return n
    t = (target // LANES) * LANES
    while t >= LANES:
        if n % t == 0:
            return t
        t -= LANES
    return n


_GELU_K = 0.7978845608028654
_GELU_C = 0.044715


def _gelu(x):
    return x * (0.5 * (1.0 + jnp.tanh(_GELU_K * (x + _GELU_C * (x * x * x)))))


def _gelu_grad(x):
    x2 = x * x
    t = jnp.tanh(_GELU_K * (x + _GELU_C * (x * x2)))
    return 0.5 * (1.0 + t) + (0.5 * x) * (1.0 - t * t) * (_GELU_K * (1.0 + (3.0 * _GELU_C) * x2))


def _sigmoid(x):
    return 1.0 / (1.0 + jnp.exp(-x))


def _log_sigmoid(x):
    return -(jnp.maximum(-x, 0.0) + jnp.log(1.0 + jnp.exp(-jnp.abs(x))))


def _rstd(x):
    return lax.rsqrt(jnp.mean(x * x, axis=-1, keepdims=True) + EPS)


def _rows(name, fn, row_ins, full_ins, outs, accs=(), tile=256):
    rows = row_ins[0].shape[0]
    r = min(tile, rows)
    n = rows // r
    n_in = len(row_ins) + len(full_ins)
    n_out = len(outs)

    def body(*refs):
        res = fn(*[ref[...] for ref in refs[:n_in]])
        for ref, v in zip(refs[n_in:n_in + n_out], res[:n_out]):
            ref[...] = v.astype(ref.dtype)
        acc_refs = refs[n_in + n_out:]
        if acc_refs:
            @pl.when(pl.program_id(0) == 0)
            def _():
                for ref in acc_refs:
                    ref[...] = jnp.zeros(ref.shape, ref.dtype)

            for ref, v in zip(acc_refs, res[n_out:]):
                ref[...] += v

    in_specs = [pl.BlockSpec((r, a.shape[1]), lambda i: (i, 0)) for a in row_ins]
    in_specs += [pl.BlockSpec(a.shape, lambda i, nd=a.ndim: (0,) * nd) for a in full_ins]
    out_shape = [jax.ShapeDtypeStruct((rows, w), dt) for (w, dt) in outs]
    out_shape += [jax.ShapeDtypeStruct(s, F32) for s in accs]
    out_specs = [pl.BlockSpec((r, w), lambda i: (i, 0)) for (w, dt) in outs]
    out_specs += [pl.BlockSpec(s, lambda i, nd=len(s): (0,) * nd) for s in accs]
    return pl.pallas_call(
        body, grid=(n,), in_specs=in_specs, out_specs=out_specs, out_shape=out_shape, name=name,
        compiler_params=_cparams(("arbitrary",) if accs else ("parallel",)),
    )(*row_ins, *full_ins)


_DOT_DIMS = {"nn": (((1,), (0,)), ((), ())), "nt": (((1,), (1,)), ((), ())), "tn": (((0,), (0,)), ((), ()))}


def _mm(name, a, b, mode="nn", out_dtype=F32, res=None, tm=1024, tn=1024, tk=1024, norm_gain=None):
    if mode == "nn":
        (m, k), (k2, n) = a.shape, b.shape
    elif mode == "nt":
        (m, k), (n, k2) = a.shape, b.shape
    else:
        (k, m), (k2, n) = a.shape, b.shape
    assert k == k2, (name, a.shape, b.shape, mode)
    tm, tn, tk = _pick(m, tm), _pick(n, tn), _pick(k, tk)
    nk = k // tk
    dims = _DOT_DIMS[mode]
    has_res = res is not None
    has_norm = norm_gain is not None
    assert not has_norm or tn == n, (name, tn, n)
    n_in = 2 + has_res + has_norm

    def body(*refs):
        a_ref, b_ref = refs[0], refs[1]
        res_ref = refs[2] if has_res else None
        gain_ref = refs[n_in - 1] if has_norm else None
        o_ref = refs[n_in]
        h_ref = refs[n_in + 1] if has_norm else None

        def finish(tot):
            if has_res:
                tot = res_ref[...] + tot
            o_ref[...] = tot.astype(o_ref.dtype)
            if has_norm:
                h_ref[...] = ((tot * _rstd(tot)) * gain_ref[...]).astype(h_ref.dtype)

        prod = lax.dot_general(a_ref[...].astype(BF16), b_ref[...].astype(BF16), dims, preferred_element_type=F32)
        if nk == 1:
            finish(prod)
            return
        acc = refs[-1]
        kk = pl.program_id(2)

        @pl.when(kk == 0)
        def _():
            acc[...] = prod

        @pl.when(kk > 0)
        def _():
            acc[...] += prod

        @pl.when(kk == nk - 1)
        def _():
            finish(acc[...])

    if mode == "nn":
        a_spec = pl.BlockSpec((tm, tk), lambda i, j, kk: (i, kk))
        b_spec = pl.BlockSpec((tk, tn), lambda i, j, kk: (kk, j))
    elif mode == "nt":
        a_spec = pl.BlockSpec((tm, tk), lambda i, j, kk: (i, kk))
        b_spec = pl.BlockSpec((tn, tk), lambda i, j, kk: (j, kk))
    else:
        a_spec = pl.BlockSpec((tk, tm), lambda i, j, kk: (kk, i))
        b_spec = pl.BlockSpec((tk, tn), lambda i, j, kk: (kk, j))
    o_spec = pl.BlockSpec((tm, tn), lambda i, j, kk: (i, j))
    in_specs = [a_spec, b_spec] + ([o_spec] if has_res else [])
    in_specs += [pl.BlockSpec((1, tn), lambda i, j, kk: (0, j))] if has_norm else []
    args = (a, b) + ((res,) if has_res else ()) + ((norm_gain,) if has_norm else ())
    out_shape = jax.ShapeDtypeStruct((m, n), out_dtype)
    return pl.pallas_call(
        body, grid=(m // tm, n // tn, nk), in_specs=in_specs, out_specs=[o_spec, o_spec] if has_norm else o_spec,
        out_shape=[out_shape, jax.ShapeDtypeStruct((m, n), BF16)] if has_norm else out_shape, name=name,
        scratch_shapes=[pltpu.VMEM((tm, tn), F32)] if nk > 1 else [],
        compiler_params=_cparams(("parallel", "parallel", "arbitrary")),
    )(*args)


def _mm_norm_bwd(name, a, b, mode, x, dres, gain, res=None, tm=512, tk=1024):
    if mode == "nn":
        (m, k), (k2, n) = a.shape, b.shape
    else:
        (m, k), (n, k2) = a.shape, b.shape
    assert k == k2, (name, a.shape, b.shape, mode)
    tm, tk = _pick(m, tm), _pick(k, tk)
    nk = k // tk
    dims = _DOT_DIMS[mode]
    has_res = res is not None
    n_in = 5 + has_res

    def body(*refs):
        a_ref, b_ref = refs[0], refs[1]
        res_ref = refs[2] if has_res else None
        x_ref, dres_ref, gain_ref = refs[n_in - 3:n_in]
        o_ref, dg_ref = refs[n_in], refs[n_in + 1]

        def finish(d):
            if has_res:
                d = res_ref[...] + d
            xv = x_ref[...]
            r = _rstd(xv)
            xh = xv * r
            dyg = d * gain_ref[...]
            o_ref[...] = dres_ref[...] + r * (dyg - xh * jnp.mean(dyg * xh, axis=-1, keepdims=True))
            part = jnp.sum(d * xh, axis=0, keepdims=True)

            @pl.when(pl.program_id(0) == 0)
            def _():
                dg_ref[...] = part

            @pl.when(pl.program_id(0) > 0)
            def _():
                dg_ref[...] += part

        prod = lax.dot_general(a_ref[...].astype(BF16), b_ref[...].astype(BF16), dims, preferred_element_type=F32)
        if nk == 1:
            finish(prod)
            return
        acc = refs[-1]
        kk = pl.program_id(1)

        @pl.when(kk == 0)
        def _():
            acc[...] = prod

        @pl.when(kk > 0)
        def _():
            acc[...] += prod

        @pl.when(kk == nk - 1)
        def _():
            finish(acc[...])

    a_spec = pl.BlockSpec((tm, tk), lambda i, kk: (i, kk))
    b_spec = pl.BlockSpec((tk, n), lambda i, kk: (kk, 0)) if mode == "nn" else pl.BlockSpec((n, tk), lambda i, kk: (0, kk))
    row_spec = pl.BlockSpec((tm, n), lambda i, kk: (i, 0))
    vec_spec = pl.BlockSpec((1, n), lambda i, kk: (0, 0))
    in_specs = [a_spec, b_spec] + ([row_spec] if has_res else []) + [row_spec, row_spec, vec_spec]
    args = (a, b) + ((res,) if has_res else ()) + (x, dres, gain)
    return pl.pallas_call(
        body, grid=(m // tm, nk), in_specs=in_specs, out_specs=[row_spec, vec_spec],
        out_shape=[jax.ShapeDtypeStruct((m, n), F32), jax.ShapeDtypeStruct((1, n), F32)], name=name,
        scratch_shapes=[pltpu.VMEM((tm, n), F32)] if nk > 1 else [],
        compiler_params=_cparams(("arbitrary", "arbitrary")),
    )(*args)


def _rmsnorm_fwd(name, x, g, outs):
    def fn(xv, gv):
        y = (xv * _rstd(xv)) * gv
        return tuple(y for _ in outs)

    return _rows(name, fn, [x], [g], [(x.shape[1], dt) for dt in outs])


_CONV_ROWS = 256
_CONV_COLS = 1408


def _conv_taps(h_ref, halo_ref, first):
    h = h_ref[...]
    rows = h.shape[0]
    row = lax.broadcasted_iota(jnp.int32, (rows, 1), 0)
    keep = jnp.where(first, 0.0, 1.0)
    m1 = halo_ref[7:8, :] * keep
    m2 = halo_ref[6:7, :] * keep
    p1 = jnp.where(row == 0, m1, pltpu.roll(h, 1, 0))
    p2 = jnp.where(row == 0, m2, jnp.where(row == 1, m1, pltpu.roll(h, 2, 0)))
    return h, p1, p2


def _conv_specs(rows, r, cw):
    tile = pl.BlockSpec((r, cw), lambda j, i: (i, j))
    halo = pl.BlockSpec((8, cw), lambda j, i: (jnp.maximum(i * (r // 8) - 1, 0), j))
    vec3 = pl.BlockSpec((3, cw), lambda j, i: (0, j))
    vec1 = pl.BlockSpec((1, cw), lambda j, i: (0, j))
    return tile, halo, vec3, vec1


def _convffn_fwd(name, hg, hu, wg, wu, bg, bu):
    rows, f = hg.shape
    r, cw = min(_CONV_ROWS, rows), _pick(f, _CONV_COLS)

    def body(hg_ref, hgh_ref, hu_ref, huh_ref, wg_ref, wu_ref, bg_ref, bu_ref, o_ref):
        first = pl.program_id(1) == 0
        h, p1, p2 = _conv_taps(hg_ref, hgh_ref, first)
        g = bg_ref[...] + wg_ref[0:1, :] * p2 + wg_ref[1:2, :] * p1 + wg_ref[2:3, :] * h
        h, p1, p2 = _conv_taps(hu_ref, huh_ref, first)
        u = bu_ref[...] + wu_ref[0:1, :] * p2 + wu_ref[1:2, :] * p1 + wu_ref[2:3, :] * h
        o_ref[...] = ((g * _sigmoid(g)) * u).astype(o_ref.dtype)

    tile, halo, vec3, vec1 = _conv_specs(rows, r, cw)
    return pl.pallas_call(
        body, grid=(f // cw, rows // r), in_specs=[tile, halo, tile, halo, vec3, vec3, vec1, vec1], out_specs=tile,
        out_shape=jax.ShapeDtypeStruct((rows, f), BF16), name=name, compiler_params=_cparams(("parallel", "parallel")),
    )(hg, hg, hu, hu, wg, wu, bg, bu)


def _gate_grads(da, g, u):
    sg = _sigmoid(g)
    return da * u * (sg * (1.0 + g * (1.0 - sg))), da * (g * sg)


def _conv_back(dc, dc_next, w_ref, last):
    r = dc.shape[0]
    row = lax.broadcasted_iota(jnp.int32, (r, 1), 0)
    keep = jnp.where(last, 0.0, 1.0)
    n0 = dc_next[0:1, :] * keep
    n1 = dc_next[1:2, :] * keep
    f1 = jnp.where(row == r - 1, n0, pltpu.roll(dc, r - 1, 0))
    f2 = jnp.where(row == r - 1, n1, jnp.where(row == r - 2, n0, pltpu.roll(dc, r - 2, 0)))
    return w_ref[2:3, :] * dc + w_ref[1:2, :] * f1 + w_ref[0:1, :] * f2


def _conv_next_rows(h, nxt_ref, w_ref, b_ref):
    r = h.shape[0]
    hn = nxt_ref[...]
    row = lax.broadcasted_iota(jnp.int32, (8, 1), 0)
    m1, m2 = h[r - 1:r, :], h[r - 2:r - 1, :]
    p1 = jnp.where(row == 0, m1, pltpu.roll(hn, 1, 0))
    p2 = jnp.where(row == 0, m2, jnp.where(row == 1, m1, pltpu.roll(hn, 2, 0)))
    return b_ref[...] + w_ref[0:1, :] * p2 + w_ref[1:2, :] * p1 + w_ref[2:3, :] * hn


def _convffn_bwd(name, da, hg, hu, wg, wu, bg, bu):
    rows, f = hg.shape
    r, cw = min(_CONV_ROWS, rows), _pick(f, _CONV_COLS)
    nrt = rows // r

    def body(da_ref, dan_ref, hg_ref, hgh_ref, hgn_ref, hu_ref, huh_ref, hun_ref, wg_ref, wu_ref, bg_ref, bu_ref,
             dhg_ref, dhu_ref, dwg_ref, dwu_ref, dbg_ref, dbu_ref):
        first = pl.program_id(1) == 0
        last = pl.program_id(1) == nrt - 1
        hgv, g1, g2 = _conv_taps(hg_ref, hgh_ref, first)
        g = bg_ref[...] + wg_ref[0:1, :] * g2 + wg_ref[1:2, :] * g1 + wg_ref[2:3, :] * hgv
        huv, u1, u2 = _conv_taps(hu_ref, huh_ref, first)
        u = bu_ref[...] + wu_ref[0:1, :] * u2 + wu_ref[1:2, :] * u1 + wu_ref[2:3, :] * huv
        dcg, dcu = _gate_grads(da_ref[...], g, u)
        dcg_n, dcu_n = _gate_grads(dan_ref[...], _conv_next_rows(hgv, hgn_ref, wg_ref, bg_ref),
                                   _conv_next_rows(huv, hun_ref, wu_ref, bu_ref))
        dhg_ref[...] = _conv_back(dcg, dcg_n, wg_ref, last).astype(dhg_ref.dtype)
        dhu_ref[...] = _conv_back(dcu, dcu_n, wu_ref, last).astype(dhu_ref.dtype)

        @pl.when(first)
        def _():
            for ref in (dwg_ref, dwu_ref, dbg_ref, dbu_ref):
                ref[...] = jnp.zeros(ref.shape, ref.dtype)

        def colsum(v):
            return jnp.sum(v, axis=0, keepdims=True)

        dwg_ref[0:1, :] += colsum(dcg * g2)
        dwg_ref[1:2, :] += colsum(dcg * g1)
        dwg_ref[2:3, :] += colsum(dcg * hgv)
        dwu_ref[0:1, :] += colsum(dcu * u2)
        dwu_ref[1:2, :] += colsum(dcu * u1)
        dwu_ref[2:3, :] += colsum(dcu * huv)
        dbg_ref[...] += colsum(dcg)
        dbu_ref[...] += colsum(dcu)

    tile, halo, vec3, vec1 = _conv_specs(rows, r, cw)
    nxt = pl.BlockSpec((8, cw), lambda j, i: (jnp.minimum((i + 1) * (r // 8), rows // 8 - 1), j))
    big = jax.ShapeDtypeStruct((rows, f), BF16)
    return pl.pallas_call(
        body, grid=(f // cw, nrt),
        in_specs=[tile, nxt, tile, halo, nxt, tile, halo, nxt, vec3, vec3, vec1, vec1],
        out_specs=[tile, tile, vec3, vec3, vec1, vec1],
        out_shape=[big, big, jax.ShapeDtypeStruct((3, f), F32), jax.ShapeDtypeStruct((3, f), F32),
                   jax.ShapeDtypeStruct((1, f), F32), jax.ShapeDtypeStruct((1, f), F32)],
        name=name, compiler_params=_cparams(("parallel", "arbitrary")),
    )(da, da, hg, hg, hg, hu, hu, hu, wg, wu, bg, bu)


_GMLP_ROWS = 256


def _gmlp_group_norm(vg, gain):
    r = lax.rsqrt(jnp.mean(vg * vg, axis=-1, keepdims=True) + EPS)
    vh = vg * r
    return vh, r, vh * gain


def _gmlp_fwd(name, zuv, v_gain, w_tril, b_exp):
    rows = zuv.shape[0]
    r = min(_GMLP_ROWS, rows)

    def body(z_ref, gain_ref, w_ref, b_ref, o_ref):
        for ch in range(r // A_CHUNK):
            lo = ch * A_CHUNK
            for g in range(A_GROUPS):
                c0 = g * LANES
                u = _gelu(z_ref[lo:lo + A_CHUNK, c0:c0 + LANES])
                v = _gelu(z_ref[lo:lo + A_CHUNK, A_WIDTH + c0:A_WIDTH + c0 + LANES])
                _, _, vn = _gmlp_group_norm(v, gain_ref[:, c0:c0 + LANES])
                sv = jnp.dot(w_ref[g], vn.astype(BF16), preferred_element_type=F32) + b_ref[g]
                o_ref[lo:lo + A_CHUNK, c0:c0 + LANES] = (u * sv).astype(o_ref.dtype)

    return pl.pallas_call(
        body, grid=(rows // r,),
        in_specs=[pl.BlockSpec((r, 2 * A_WIDTH), lambda i: (i, 0)), pl.BlockSpec((1, A_WIDTH), lambda i: (0, 0)),
                  pl.BlockSpec((A_GROUPS, A_CHUNK, A_CHUNK), lambda i: (0, 0, 0)),
                  pl.BlockSpec((A_GROUPS, A_CHUNK, LANES), lambda i: (0, 0, 0))],
        out_specs=pl.BlockSpec((r, A_WIDTH), lambda i: (i, 0)),
        out_shape=jax.ShapeDtypeStruct((rows, A_WIDTH), BF16), name=name, compiler_params=_cparams(("parallel",)),
    )(zuv, v_gain, w_tril, b_exp)


def _gmlp_bwd(name, zuv, dya, v_gain, w_tril, w_tril_t, b_exp):
    rows = zuv.shape[0]
    r = min(_GMLP_ROWS, rows)

    def body(z_ref, dy_ref, gain_ref, w_ref, wt_ref, b_ref, dz_ref, dw_ref, db_ref, dgain_ref):
        @pl.when(pl.program_id(0) == 0)
        def _():
            for ref in (dw_ref, db_ref, dgain_ref):
                ref[...] = jnp.zeros(ref.shape, ref.dtype)

        for ch in range(r // A_CHUNK):
            lo = ch * A_CHUNK
            for g in range(A_GROUPS):
                c0 = g * LANES
                zu = z_ref[lo:lo + A_CHUNK, c0:c0 + LANES]
                zv = z_ref[lo:lo + A_CHUNK, A_WIDTH + c0:A_WIDTH + c0 + LANES]
                gain = gain_ref[:, c0:c0 + LANES]
                u = _gelu(zu)
                v = _gelu(zv)
                vh, rr, vn = _gmlp_group_norm(v, gain)
                vn_b = vn.astype(BF16)
                sv = jnp.dot(w_ref[g], vn_b, preferred_element_type=F32) + b_ref[g]
                dy = dy_ref[lo:lo + A_CHUNK, c0:c0 + LANES]
                dsv = dy * u
                dsv_b = dsv.astype(BF16)
                dz_ref[lo:lo + A_CHUNK, c0:c0 + LANES] = ((dy * sv) * _gelu_grad(zu)).astype(dz_ref.dtype)
                dw_ref[g] += lax.dot_general(dsv_b, vn_b, _DOT_DIMS["nt"], preferred_element_type=F32)
                db_ref[g] += dsv
                dvn = jnp.dot(wt_ref[g], dsv_b, preferred_element_type=F32)
                dgain_ref[:, c0:c0 + LANES] += jnp.sum(dvn * vh, axis=0, keepdims=True)
                dvh = dvn * gain
                dv = rr * (dvh - vh * jnp.mean(dvh * vh, axis=-1, keepdims=True))
                dz_ref[lo:lo + A_CHUNK, A_WIDTH + c0:A_WIDTH + c0 + LANES] = (dv * _gelu_grad(zv)).astype(dz_ref.dtype)

    wspec = pl.BlockSpec((A_GROUPS, A_CHUNK, A_CHUNK), lambda i: (0, 0, 0))
    bspec = pl.BlockSpec((A_GROUPS, A_CHUNK, LANES), lambda i: (0, 0, 0))
    gspec = pl.BlockSpec((1, A_WIDTH), lambda i: (0, 0))
    return pl.pallas_call(
        body, grid=(rows // r,),
        in_specs=[pl.BlockSpec((r, 2 * A_WIDTH), lambda i: (i, 0)), pl.BlockSpec((r, A_WIDTH), lambda i: (i, 0)),
                  gspec, wspec, wspec, bspec],
        out_specs=[pl.BlockSpec((r, 2 * A_WIDTH), lambda i: (i, 0)), wspec, bspec, gspec],
        out_shape=[jax.ShapeDtypeStruct((rows, 2 * A_WIDTH), BF16),
                   jax.ShapeDtypeStruct((A_GROUPS, A_CHUNK, A_CHUNK), F32),
                   jax.ShapeDtypeStruct((A_GROUPS, A_CHUNK, LANES), F32), jax.ShapeDtypeStruct((1, A_WIDTH), F32)],
        name=name, compiler_params=_cparams(("arbitrary",)),
    )(zuv, dya, v_gain, w_tril, w_tril_t, b_exp)


_ATT_T = 512
_Q_SCALE = B_HEAD_DIM ** -0.5


def _head_mean(v, bd):
    return jnp.dot(v, bd, preferred_element_type=F32, precision=lax.Precision.HIGHEST)


def _qkv_prep_fwd(name, zqkv, zf, qg, kg, bf, bd):
    def fn(z, f, qg_v, kg_v, bf_v, bd_v):
        zq, zk, zv = z[:, :B_WIDTH], z[:, B_WIDTH:2 * B_WIDTH], z[:, 2 * B_WIDTH:]
        q = (zq * lax.rsqrt(_head_mean(zq * zq, bd_v) + EPS)) * qg_v * _Q_SCALE
        k = (zk * lax.rsqrt(_head_mean(zk * zk, bd_v) + EPS)) * kg_v
        return q, k, zv, _log_sigmoid(f + bf_v)

    return _rows(name, fn, [zqkv, zf], [qg, kg, bf, bd],
                 [(B_WIDTH, BF16), (B_WIDTH, BF16), (B_WIDTH, BF16), (LANES, F32)])


def _qkv_prep_bwd(name, zqkv, zf, dq, dk, dv, dls, qg, kg, bf, bd):
    def fn(z, f, dq_v, dk_v, dv_v, dls_v, qg_v, kg_v, bf_v, bd_v):
        zq, zk = z[:, :B_WIDTH], z[:, B_WIDTH:2 * B_WIDTH]

        def norm_bwd(x, dy, gain):
            r = lax.rsqrt(_head_mean(x * x, bd_v) + EPS)
            xh = x * r
            dxh = dy * gain
            dx = r * (dxh - xh * _head_mean(dxh * xh, bd_v))
            return dx, jnp.sum(dy * xh, axis=0, keepdims=True)

        dzq, dqg = norm_bwd(zq, dq_v * _Q_SCALE, qg_v)
        dzk, dkg = norm_bwd(zk, dk_v, kg_v)
        dzf = dls_v * (1.0 - _sigmoid(f + bf_v))
        return jnp.concatenate([dzq, dzk, dv_v], axis=1), dzf, dqg, dkg, jnp.sum(dzf, axis=0, keepdims=True)

    return _rows(name, fn, [zqkv, zf, dq, dk, dv, dls], [qg, kg, bf, bd],
                 [(3 * B_WIDTH, BF16), (LANES, BF16)], accs=[(1, B_WIDTH), (1, B_WIDTH), (1, LANES)])


def _cumsum_rows(name, a, reverse=False, tile=512):
    rows, w = a.shape
    r = min(tile, rows)
    n = rows // r

    def body(a_ref, o_ref, carry):
        @pl.when(pl.program_id(0) == 0)
        def _():
            carry[...] = jnp.zeros(carry.shape, carry.dtype)

        x = a_ref[...]
        row = lax.broadcasted_iota(jnp.int32, (r, 1), 0)
        s = 1
        while s < r:
            if reverse:
                x = x + jnp.where(row < r - s, pltpu.roll(x, r - s, 0), 0.0)
            else:
                x = x + jnp.where(row >= s, pltpu.roll(x, s, 0), 0.0)
            s *= 2
        x = x + carry[0:1, :]
        o_ref[...] = x
        edge = x[0:1, :] if reverse else x[r - 1:r, :]
        carry[...] = jnp.broadcast_to(edge, carry.shape)

    idx = (lambda i: (n - 1 - i, 0)) if reverse else (lambda i: (i, 0))
    return pl.pallas_call(
        body, grid=(n,), in_specs=[pl.BlockSpec((r, w), idx)], out_specs=pl.BlockSpec((r, w), idx),
        out_shape=jax.ShapeDtypeStruct((rows, w), F32), scratch_shapes=[pltpu.VMEM((8, w), F32)], name=name,
        compiler_params=_cparams(("arbitrary",)),
    )(a)


def _head_masks():
    lane = lax.broadcasted_iota(jnp.int32, (1, LANES), 1)
    return [lane < B_HEAD_DIM, lane >= B_HEAD_DIM]


def _causal(t):
    row = lax.broadcasted_iota(jnp.int32, (t, t), 0)
    col = lax.broadcasted_iota(jnp.int32, (t, t), 1)
    return row, col


def _flash_fwd(name, q, k, v, nck_rows):
    rows = q.shape[0]
    t = min(_ATT_T, rows)
    nb = rows // t

    def body(q_ref, k_ref, v_ref, nck_ref, o_ref, lse_ref):
        pair, i = pl.program_id(0), pl.program_id(1)
        q2 = q_ref[...]
        row, col = _causal(t)
        masks = _head_masks()
        qh = [jnp.where(hm, q2, jnp.zeros_like(q2)) for hm in masks]

        def step(j, carry, diag):
            ml, acc = carry
            start = pl.multiple_of(j * t, t)
            kb = k_ref[pl.ds(start, t), :]
            vb = v_ref[pl.ds(start, t), :]
            new_ml = []
            for hh, hm in enumerate(masks):
                m, l = ml[hh]
                s = lax.dot_general(qh[hh], kb, _DOT_DIMS["nt"], preferred_element_type=F32)
                s = s + nck_ref[2 * pair + hh, pl.ds(j, 1), :]
                if diag:
                    s = jnp.where(col <= row, s, NEG_INF)
                m_new = jnp.maximum(m, jnp.max(s, axis=1, keepdims=True))
                p = jnp.exp(s - m_new)
                alpha = jnp.exp(m - m_new)
                new_ml.append((m_new, alpha * l + jnp.sum(p, axis=1, keepdims=True)))
                pv = jnp.dot(p.astype(BF16), jnp.where(hm, vb, jnp.zeros_like(vb)), preferred_element_type=F32)
                acc = acc * jnp.where(hm, alpha, 1.0) + pv
            return tuple(new_ml), acc

        def init_ml():
            return (jnp.full((t, 1), NEG_INF, F32), jnp.zeros((t, 1), F32))

        init = ((init_ml(), init_ml()), jnp.zeros((t, LANES), F32))
        carry = lax.fori_loop(0, i, lambda j, c: step(j, c, False), init)
        ml, acc = step(i, carry, True)
        o_ref[...] = acc / jnp.where(masks[0], ml[0][1], ml[1][1])
        for hh in range(2):
            lse_ref[hh] = ml[hh][0] + jnp.log(ml[hh][1])

    return pl.pallas_call(
        body, grid=(B_HEADS // 2, nb),
        in_specs=[pl.BlockSpec((t, LANES), lambda p, i: (i, p)), pl.BlockSpec((rows, LANES), lambda p, i: (0, p)),
                  pl.BlockSpec((rows, LANES), lambda p, i: (0, p)),
                  pl.BlockSpec((B_HEADS, nb, t), lambda p, i: (0, 0, 0))],
        out_specs=[pl.BlockSpec((t, LANES), lambda p, i: (i, p)), pl.BlockSpec((2, t, 1), lambda p, i: (p, i, 0))],
        out_shape=[jax.ShapeDtypeStruct((rows, B_WIDTH), F32), jax.ShapeDtypeStruct((B_HEADS, rows, 1), F32)],
        name=name, compiler_params=_cparams(("parallel", "parallel")),
    )(q, k, v, nck_rows)


def _flash_bwd_dq(name, q, k, v, nck_rows, o, do, lse_col):
    rows = q.shape[0]
    t = min(_ATT_T, rows)
    nb = rows // t

    def body(q_ref, k_ref, v_ref, nck_ref, o_ref, do_ref, lse_ref, dq_ref, delta_ref):
        pair, i = pl.program_id(0), pl.program_id(1)
        q2 = q_ref[...]
        do2 = do_ref[...]
        od = o_ref[...] * do2
        do_b = do2.astype(BF16)
        row, col = _causal(t)
        masks = _head_masks()
        qh = [jnp.where(hm, q2, jnp.zeros_like(q2)) for hm in masks]
        doh = [jnp.where(hm, do_b, jnp.zeros_like(do_b)) for hm in masks]
        delta = [jnp.sum(jnp.where(hm, od, 0.0), axis=1, keepdims=True) for hm in masks]
        lse = [lse_ref[hh] for hh in range(2)]

        def step(j, carry, diag):
            acc, rowsum = carry
            start = pl.multiple_of(j * t, t)
            kb = k_ref[pl.ds(start, t), :]
            vb = v_ref[pl.ds(start, t), :]
            new_rowsum = []
            for hh, hm in enumerate(masks):
                s = lax.dot_general(qh[hh], kb, _DOT_DIMS["nt"], preferred_element_type=F32)
                s = s + nck_ref[2 * pair + hh, pl.ds(j, 1), :]
                p = jnp.exp(s - lse[hh])
                if diag:
                    p = jnp.where(col <= row, p, 0.0)
                dp = lax.dot_general(doh[hh], vb, _DOT_DIMS["nt"], preferred_element_type=F32)
                ds = p * (dp - delta[hh])
                new_rowsum.append(rowsum[hh] + jnp.sum(ds, axis=1, keepdims=True))
                acc = acc + jnp.dot(ds.astype(BF16), jnp.where(hm, kb, jnp.zeros_like(kb)),
                                    preferred_element_type=F32)
            return acc, tuple(new_rowsum)

        zcol = jnp.zeros((t, 1), F32)
        carry = lax.fori_loop(0, i, lambda j, c: step(j, c, False), (jnp.zeros((t, LANES), F32), (zcol, zcol)))
        acc, rowsum = step(i, carry, True)
        dq_ref[...] = acc
        for hh in range(2):
            delta_ref[hh] = delta[hh] + rowsum[hh]

    tile = pl.BlockSpec((t, LANES), lambda p, i: (i, p))
    full = pl.BlockSpec((rows, LANES), lambda p, i: (0, p))
    colspec = pl.BlockSpec((2, t, 1), lambda p, i: (p, i, 0))
    return pl.pallas_call(
        body, grid=(B_HEADS // 2, nb),
        in_specs=[tile, full, full, pl.BlockSpec((B_HEADS, nb, t), lambda p, i: (0, 0, 0)), tile, tile, colspec],
        out_specs=[tile, colspec],
        out_shape=[jax.ShapeDtypeStruct((rows, B_WIDTH), F32), jax.ShapeDtypeStruct((B_HEADS, rows, 1), F32)],
        name=name, compiler_params=_cparams(("parallel", "parallel")),
    )(q, k, v, nck_rows, o, do, lse_col)


def _flash_bwd_dkv(name, q, k, v, nck_col, do, lse_rows, delta_rows):
    rows = q.shape[0]
    t = min(_ATT_T, rows)
    nb = rows // t

    def body(k_ref, v_ref, q_ref, do_ref, nck_ref, lse_ref, delta_ref, dk_ref, dv_ref, dn_ref):
        pair, j = pl.program_id(0), pl.program_id(1)
        k2 = k_ref[...]
        v2 = v_ref[...]
        row, col = _causal(t)
        masks = _head_masks()
        kh = [jnp.where(hm, k2, jnp.zeros_like(k2)) for hm in masks]
        vh = [jnp.where(hm, v2, jnp.zeros_like(v2)) for hm in masks]
        nck = [nck_ref[hh] for hh in range(2)]

        def step(i, carry, diag):
            dk, dv, dn = carry
            start = pl.multiple_of(i * t, t)
            qb = q_ref[pl.ds(start, t), :]
            dob = do_ref[pl.ds(start, t), :].astype(BF16)
            dn_new = []
            for hh, hm in enumerate(masks):
                head = 2 * pair + hh
                st = lax.dot_general(kh[hh], qb, _DOT_DIMS["nt"], preferred_element_type=F32) + nck[hh]
                pt = jnp.exp(st - lse_ref[head, pl.ds(i, 1), :])
                if diag:
                    pt = jnp.where(row <= col, pt, 0.0)
                dpt = lax.dot_general(vh[hh], dob, _DOT_DIMS["nt"], preferred_element_type=F32)
                dst = pt * (dpt - delta_ref[head, pl.ds(i, 1), :])
                dv = dv + jnp.dot(pt.astype(BF16), jnp.where(hm, dob, jnp.zeros_like(dob)),
                                  preferred_element_type=F32)
                dk = dk + jnp.dot(dst.astype(BF16), jnp.where(hm, qb, jnp.zeros_like(qb)),
                                  preferred_element_type=F32)
                dn_new.append(dn[hh] + jnp.sum(dst, axis=1, keepdims=True))
            return dk, dv, tuple(dn_new)

        zero = jnp.zeros((t, LANES), F32)
        zcol = jnp.zeros((t, 1), F32)
        carry = step(j, (zero, zero, (zcol, zcol)), True)
        dk, dv, dn = lax.fori_loop(j + 1, nb, lambda i, c: step(i, c, False), carry)
        dk_ref[...] = dk
        dv_ref[...] = dv
        for hh in range(2):
            dn_ref[hh] = dn[hh]

    tile = pl.BlockSpec((t, LANES), lambda p, j: (j, p))
    full = pl.BlockSpec((rows, LANES), lambda p, j: (0, p))
    colspec = pl.BlockSpec((2, t, 1), lambda p, j: (p, j, 0))
    rowspec = pl.BlockSpec((B_HEADS, nb, t), lambda p, j: (0, 0, 0))
    big = jax.ShapeDtypeStruct((rows, B_WIDTH), F32)
    return pl.pallas_call(
        body, grid=(B_HEADS // 2, nb),
        in_specs=[tile, tile, full, full, colspec, rowspec, rowspec], out_specs=[tile, tile, colspec],
        out_shape=[big, big, jax.ShapeDtypeStruct((B_HEADS, rows, 1), F32)],
        name=name, compiler_params=_cparams(("parallel", "parallel")),
    )(k, v, q, do, nck_col, lse_rows, delta_rows)


_S5_ROWS = 256


def _s5_discretize(a_re, a_im, log_dt, b_re, b_im):
    dt = jnp.exp(log_dt)[:, None]
    mag = jnp.exp(a_re * dt)
    ab_re, ab_im = mag * jnp.cos(a_im * dt), mag * jnp.sin(a_im * dt)
    den = a_re * a_re + a_im * a_im
    nr, ni = ab_re - 1.0, ab_im
    cr = (nr * a_re + ni * a_im) / den
    ci = (ni * a_re - nr * a_im) / den
    bb_re = cr[..., None] * b_re - ci[..., None] * b_im
    bb_im = cr[..., None] * b_im + ci[..., None] * b_re
    return ab_re, ab_im, bb_re, bb_im


def _s5_block_diag(m):
    g, r, c = m.shape
    mb = m.reshape(S5_BLOCKS, 8, r, c)
    eye = jnp.eye(8, dtype=m.dtype)
    return jnp.einsum("bgrc,gh->bgrhc", mb, eye).reshape(S5_BLOCKS, 8 * r, 8 * c)


def _s5_block_diag_extract(m, r, c):
    mb = m.reshape(S5_BLOCKS, 8, r, 8, c)
    return jnp.einsum("bgrhc,gh->bgrc", mb, jnp.eye(8, dtype=m.dtype)).reshape(S5_GROUPS, r, c)


def _s5_tables(ab_re, ab_im, r):
    ar = jnp.broadcast_to(ab_re.reshape(1, -1), (r, S5_GROUPS * S5_STATE))
    ai = jnp.broadcast_to(ab_im.reshape(1, -1), (r, S5_GROUPS * S5_STATE))

    def mul(x, y):
        return x[0] * y[0] - x[1] * y[1], x[0] * y[1] + x[1] * y[0]

    return lax.associative_scan(mul, (ar, ai), axis=0)


def _scan_step(xr, xi, ar, ai, s, row, up):
    r = xr.shape[0]
    if up:
        ai = -ai
    if s < 8:
        if up:
            sr = jnp.where(row < r - s, pltpu.roll(xr, r - s, 0), 0.0)
            si = jnp.where(row < r - s, pltpu.roll(xi, r - s, 0), 0.0)
        else:
            sr = jnp.where(row >= s, pltpu.roll(xr, s, 0), 0.0)
            si = jnp.where(row >= s, pltpu.roll(xi, s, 0), 0.0)
        return xr + (ar * sr - ai * si), xi + (ar * si + ai * sr)
    if up:
        (dr, di), (sr, si) = (xr[:r - s], xi[:r - s]), (xr[s:], xi[s:])
        nr, ni = dr + (ar * sr - ai * si), di + (ar * si + ai * sr)
        return jnp.concatenate([nr, xr[r - s:]], axis=0), jnp.concatenate([ni, xi[r - s:]], axis=0)
    (dr, di), (sr, si) = (xr[s:], xi[s:]), (xr[:r - s], xi[:r - s])
    nr, ni = dr + (ar * sr - ai * si), di + (ar * si + ai * sr)
    return jnp.concatenate([xr[:s], nr], axis=0), jnp.concatenate([xi[:s], ni], axis=0)


_S5_CHUNK = 16


def _scan_tile(xr, xi, pr_ref, pi_ref, tr_ref, ti_ref, edge_ref, up):
    r, nl = xr.shape
    ch = _S5_CHUNK
    nch = r // ch
    sub = lax.broadcasted_iota(jnp.int32, (r, 1), 0) & (ch - 1)
    s = 1
    while s < ch:
        ar, ai = pr_ref[s - 1:s, :], pi_ref[s - 1:s, :]
        if up:
            ai, keep, shift = -ai, sub < ch - s, r - s
        else:
            keep, shift = sub >= s, s
        sr = jnp.where(keep, pltpu.roll(xr, shift, 0), 0.0)
        si = jnp.where(keep, pltpu.roll(xi, shift, 0), 0.0)
        xr, xi = xr + (ar * sr - ai * si), xi + (ar * si + ai * sr)
        s *= 2
    nb = nl // LANES
    for k in range(nb):
        edge_ref[k] = xr[:, k * LANES:(k + 1) * LANES]
        edge_ref[nb + k] = xi[:, k * LANES:(k + 1) * LANES]
    e0 = 0 if up else ch - 1
    er = jnp.concatenate([edge_ref[k, pl.ds(e0, nch, stride=ch), :] for k in range(nb)], axis=1)
    ei = jnp.concatenate([edge_ref[nb + k, pl.ds(e0, nch, stride=ch), :] for k in range(nb)], axis=1)
    rowc = lax.broadcasted_iota(jnp.int32, (nch, 1), 0)
    s = 1
    while s < nch:
        er, ei = _scan_step(er, ei, pr_ref[ch * s - 1:ch * s, :], pi_ref[ch * s - 1:ch * s, :], s, rowc, up)
        s *= 2
    if up:
        nr = jnp.where(rowc < nch - 1, pltpu.roll(er, nch - 1, 0), 0.0)
        ni = jnp.where(rowc < nch - 1, pltpu.roll(ei, nch - 1, 0), 0.0)
    else:
        nr = jnp.where(rowc >= 1, pltpu.roll(er, 1, 0), 0.0)
        ni = jnp.where(rowc >= 1, pltpu.roll(ei, 1, 0), 0.0)
    br = jnp.concatenate([jnp.broadcast_to(nr[n:n + 1, :], (ch, nl)) for n in range(nch)], axis=0)
    bi = jnp.concatenate([jnp.broadcast_to(ni[n:n + 1, :], (ch, nl)) for n in range(nch)], axis=0)
    tr, ti = tr_ref[...], ti_ref[...]
    if up:
        ti = -ti
    return xr + (tr * br - ti * bi), xi + (tr * bi + ti * br)


def _s5_scan_tile(u_ref, bcat_ref, pr_ref, pi_ref, tr_ref, ti_ref, edge_ref, cin_r, cin_i):
    bu = jnp.dot(u_ref[...], bcat_ref[...], preferred_element_type=F32)
    xr, xi = bu[:, :S5_LANES], bu[:, S5_LANES:]
    ar, ai = pr_ref[0:1, :], pi_ref[0:1, :]
    first = lax.broadcasted_iota(jnp.int32, (8, 1), 0) == 0
    xr = jnp.concatenate([xr[:8] + jnp.where(first, ar * cin_r - ai * cin_i, 0.0), xr[8:]], axis=0)
    xi = jnp.concatenate([xi[:8] + jnp.where(first, ar * cin_i + ai * cin_r, 0.0), xi[8:]], axis=0)
    return _scan_tile(xr, xi, pr_ref, pi_ref, tr_ref, ti_ref, edge_ref, False)


def _s5_fwd(name, u, bcat, ccat, pw_re, pw_im, pt_re, pt_im):
    rows = u.shape[0]
    r = pw_re.shape[0]
    nt = rows // r

    def body(u_ref, bcat_ref, ccat_ref, pr_ref, pi_ref, tr_ref, ti_ref, y_ref, xin_ref, carry, edge):
        @pl.when(pl.program_id(1) == 0)
        def _():
            carry[...] = jnp.zeros(carry.shape, carry.dtype)

        xin_ref[...] = carry[...]
        xr, xi = _s5_scan_tile(u_ref, bcat_ref, pr_ref, pi_ref, tr_ref, ti_ref, edge,
                               carry[0:1, :S5_LANES], carry[0:1, S5_LANES:])
        xcat = jnp.concatenate([xr, xi], axis=1)
        carry[...] = jnp.broadcast_to(xcat[r - 1:r, :], carry.shape)
        y_ref[...] = jnp.dot(xcat.astype(BF16), ccat_ref[...], preferred_element_type=F32)

    tab = pl.BlockSpec((r, S5_LANES), lambda b, i: (0, b))
    return pl.pallas_call(
        body, grid=(S5_BLOCKS, nt),
        in_specs=[pl.BlockSpec((r, LANES), lambda b, i: (i, b)),
                  pl.BlockSpec((None, LANES, 2 * S5_LANES), lambda b, i: (b, 0, 0)),
                  pl.BlockSpec((None, 2 * S5_LANES, LANES), lambda b, i: (b, 0, 0)), tab, tab, tab, tab],
        out_specs=[pl.BlockSpec((r, LANES), lambda b, i: (i, b)),
                   pl.BlockSpec((None, 8, 2 * S5_LANES), lambda b, i: (b, i, 0))],
        out_shape=[jax.ShapeDtypeStruct((rows, D_MODEL), F32),
                   jax.ShapeDtypeStruct((S5_BLOCKS, 8 * nt, 2 * S5_LANES), F32)],
        scratch_shapes=[pltpu.VMEM((8, 2 * S5_LANES), F32), pltpu.VMEM((2 * S5_LANES // LANES, r, LANES), F32)], name=name,
        compiler_params=_cparams(("parallel", "arbitrary")),
    )(u, bcat, ccat, pw_re, pw_im, pt_re, pt_im)


def _s5_bwd(name, u, dy, xin, bcat, ccat, pw_re, pw_im, pt_re, pt_im, ptu_re, ptu_im):
    rows = u.shape[0]
    r = pw_re.shape[0]
    nt = rows // r

    def body(u_ref, dy_ref, xin_ref, bcat_ref, ccat_ref, pr_ref, pi_ref, tr_ref, ti_ref, ur_ref, ui_ref,
             du_ref, db_ref, dc_ref, dar_ref, dai_ref, carry, edge):
        @pl.when(pl.program_id(1) == 0)
        def _():
            carry[...] = jnp.zeros(carry.shape, carry.dtype)
            for ref in (db_ref, dc_ref, dar_ref, dai_ref):
                ref[...] = jnp.zeros(ref.shape, ref.dtype)

        row = lax.broadcasted_iota(jnp.int32, (r, 1), 0)
        cin_r, cin_i = xin_ref[0:1, :S5_LANES], xin_ref[0:1, S5_LANES:]
        xr, xi = _s5_scan_tile(u_ref, bcat_ref, pr_ref, pi_ref, tr_ref, ti_ref, edge, cin_r, cin_i)
        dy_b = dy_ref[...].astype(BF16)
        xcat = jnp.concatenate([xr, xi], axis=1).astype(BF16)
        dc_ref[...] += lax.dot_general(xcat, dy_b, _DOT_DIMS["tn"], preferred_element_type=F32)
        g = lax.dot_general(dy_b, ccat_ref[...], _DOT_DIMS["nt"], preferred_element_type=F32)
        lr, li = g[:, :S5_LANES], g[:, S5_LANES:]
        nr, ni = carry[0:1, :S5_LANES], carry[0:1, S5_LANES:]
        ar, ai = pr_ref[0:1, :], pi_ref[0:1, :]
        final = lax.broadcasted_iota(jnp.int32, (8, 1), 0) == 7
        lr = jnp.concatenate([lr[:r - 8], lr[r - 8:] + jnp.where(final, ar * nr + ai * ni, 0.0)], axis=0)
        li = jnp.concatenate([li[:r - 8], li[r - 8:] + jnp.where(final, ar * ni - ai * nr, 0.0)], axis=0)
        lr, li = _scan_tile(lr, li, pr_ref, pi_ref, ur_ref, ui_ref, edge, True)
        carry[...] = jnp.broadcast_to(jnp.concatenate([lr[0:1, :], li[0:1, :]], axis=1), carry.shape)
        lcat = jnp.concatenate([lr, li], axis=1).astype(BF16)
        du_ref[...] = lax.dot_general(lcat, bcat_ref[...], _DOT_DIMS["nt"], preferred_element_type=F32)
        db_ref[...] += lax.dot_general(u_ref[...], lcat, _DOT_DIMS["tn"], preferred_element_type=F32)
        pxr = jnp.where(row == 0, cin_r, pltpu.roll(xr, 1, 0))
        pxi = jnp.where(row == 0, cin_i, pltpu.roll(xi, 1, 0))
        dar_ref[...] += jnp.sum((lr * pxr + li * pxi).reshape(r // 8, 8, S5_LANES), axis=0)
        dai_ref[...] += jnp.sum((li * pxr - lr * pxi).reshape(r // 8, 8, S5_LANES), axis=0)

    rev = lambda b, i: (nt - 1 - i, b)
    tab = pl.BlockSpec((r, S5_LANES), lambda b, i: (0, b))
    return pl.pallas_call(
        body, grid=(S5_BLOCKS, nt),
        in_specs=[pl.BlockSpec((r, LANES), rev), pl.BlockSpec((r, LANES), rev),
                  pl.BlockSpec((None, 8, 2 * S5_LANES), lambda b, i: (b, nt - 1 - i, 0)),
                  pl.BlockSpec((None, LANES, 2 * S5_LANES), lambda b, i: (b, 0, 0)),
                  pl.BlockSpec((None, 2 * S5_LANES, LANES), lambda b, i: (b, 0, 0)), tab, tab, tab, tab, tab, tab],
        out_specs=[pl.BlockSpec((r, LANES), rev),
                   pl.BlockSpec((None, LANES, 2 * S5_LANES), lambda b, i: (b, 0, 0)),
                   pl.BlockSpec((None, 2 * S5_LANES, LANES), lambda b, i: (b, 0, 0)),
                   pl.BlockSpec((None, 8, S5_LANES), lambda b, i: (b, 0, 0)),
                   pl.BlockSpec((None, 8, S5_LANES), lambda b, i: (b, 0, 0))],
        out_shape=[jax.ShapeDtypeStruct((rows, D_MODEL), F32),
                   jax.ShapeDtypeStruct((S5_BLOCKS, LANES, 2 * S5_LANES), F32),
                   jax.ShapeDtypeStruct((S5_BLOCKS, 2 * S5_LANES, LANES), F32),
                   jax.ShapeDtypeStruct((S5_BLOCKS, 8, S5_LANES), F32),
                   jax.ShapeDtypeStruct((S5_BLOCKS, 8, S5_LANES), F32)],
        scratch_shapes=[pltpu.VMEM((8, 2 * S5_LANES), F32), pltpu.VMEM((2 * S5_LANES // LANES, r, LANES), F32)], name=name,
        compiler_params=_cparams(("parallel", "arbitrary")),
    )(u, dy, xin, bcat, ccat, pw_re, pw_im, pt_re, pt_im, ptu_re, ptu_im)


def _ones_gain():
    return jnp.ones((1, D_MODEL), F32)


def _channel_fwd(i, x1, p_i, w, rp, hn=None, next_norm=None):
    if hn is None:
        hn, = _rmsnorm_fwd(f"ffn_norm_{i}", x1, rp["norm_ffn"][i][None], [BF16])
    hg = _mm(f"ffn_up_g_{i}", hn, w["up_g"], tn=1408)
    hu = _mm(f"ffn_up_u_{i}", hn, w["up_u"], tn=1408)
    a = _convffn_fwd(f"ffn_conv_{i}", hg, hu, w["cw_g"], w["cw_u"], w["cb_g"], w["cb_u"])
    x2, r = _mm(f"ffn_down_{i}", a, w["down"], res=x1, tk=1408, norm_gain=_ones_gain())
    zg = _mm(f"ple_gate_{i}", r, w["ple_gate"])
    pp = _mm(f"ple_proj_{i}", p_i, w["ple_proj"])
    saved = dict(x1=x1, hn=hn, hg=hg, hu=hu, a=a, x2=x2, r=r, zg=zg, pp=pp, p_i=p_i)
    if next_norm is None:
        x3, = _rows(f"ple_out_{i}", lambda xv, zv, pv: (xv + _sigmoid(zv) * pv,), [x2, zg, pp], [], [(D_MODEL, F32)])
        return x3, None, saved
    gain, dtypes = next_norm

    def ple_out_norm(xv, zv, pv, gv):
        x3v = xv + _sigmoid(zv) * pv
        h = (x3v * _rstd(x3v)) * gv
        return (x3v,) + tuple(h for _ in dtypes)

    x3, *h_next = _rows(f"ple_out_{i}", ple_out_norm, [x2, zg, pp], [gain],
                        [(D_MODEL, F32)] + [(D_MODEL, dt) for dt in dtypes])
    return x3, h_next, saved


def _channel_bwd(i, dx3, sv, w, rp):
    def ple_bwd(dv, zv, pv):
        gate = _sigmoid(zv)
        return dv * gate, (dv * pv) * (gate * (1.0 - gate))

    dpp, dzg = _rows(f"ple_out_bwd_{i}", ple_bwd, [dx3, sv["zg"], sv["pp"]], [], [(D_MODEL, BF16), (D_MODEL, BF16)])
    g = {}
    g["ple_proj"] = _mm(f"ple_proj_dw_{i}", sv["p_i"], dpp, "tn")
    g["ple_gate"] = _mm(f"ple_gate_dw_{i}", sv["r"], dzg, "tn")
    dx2, _ = _mm_norm_bwd(f"ple_gate_dx_{i}", dzg, w["ple_gate"], "nt", sv["x2"], dx3, _ones_gain())
    da = _mm(f"ffn_down_dx_{i}", dx2, w["down"], "nt", tn=1408)
    g["down"] = _mm(f"ffn_down_dw_{i}", sv["a"], dx2, "tn", tm=1408)
    dhg, dhu, g["cw_g"], g["cw_u"], dbg, dbu = _convffn_bwd(
        f"ffn_conv_bwd_{i}", da, sv["hg"], sv["hu"], w["cw_g"], w["cw_u"], w["cb_g"], w["cb_u"])
    g["conv_b"] = jnp.concatenate([dbg, dbu], axis=1)[0]
    g["up_g"] = _mm(f"ffn_up_g_dw_{i}", sv["hn"], dhg, "tn", tn=1408)
    g["up_u"] = _mm(f"ffn_up_u_dw_{i}", sv["hn"], dhu, "tn", tn=1408)
    dhn = _mm(f"ffn_up_g_dx_{i}", dhg, w["up_g"], "nt", tk=1408)
    dx1, dgf = _mm_norm_bwd(f"ffn_up_u_dx_{i}", dhu, w["up_u"], "nt", sv["x1"], dx2, rp["norm_ffn"][i][None],
                            res=dhn, tk=1408)
    g["norm_ffn"] = dgf[0]
    return dx1, g


def _even_consts(e, rp):
    tri = jnp.tril(jnp.ones((A_CHUNK, A_CHUNK), dtype=bool))
    w_tril = jnp.where(tri[None], rp["ev_w_spatial"][e], 0.0).astype(BF16)
    b_exp = jnp.broadcast_to(rp["ev_b_spatial"][e][:, :, None], (A_GROUPS, A_CHUNK, LANES))
    seg = np.arange(B_WIDTH) // B_HEAD_DIM
    bd = jnp.asarray((seg[:, None] == seg[None, :]).astype(np.float32) / B_HEAD_DIM)
    return dict(
        tri=tri, w_tril=w_tril, w_tril_t=jnp.swapaxes(w_tril, 1, 2), b_exp=b_exp, bd=bd,
        v_gain=rp["ev_v_norm"][e][None], qg=jnp.tile(rp["ev_q_norm"][e], B_HEADS)[None],
        kg=jnp.tile(rp["ev_k_norm"][e], B_HEADS)[None],
        bf=jnp.pad(rp["ev_b_fgate"][e], (0, LANES - B_HEADS))[None])


def _even_fwd(i, x, w, rp, h_in=None):
    e = i // 2
    c = _even_consts(e, rp)
    rows = x.shape[0]
    t = min(_ATT_T, rows)
    h, = h_in if h_in is not None else _rmsnorm_fwd(f"mix_norm_{i}", x, rp["norm_mix"][i][None], [BF16])
    zuv = _mm(f"in_uv_{i}", h, w["in_uv"])
    zqkv = _mm(f"in_qkv_{i}", h, w["in_qkv"], tn=768)
    zf = _mm(f"in_f_{i}", h, w["in_f"])
    ya = _gmlp_fwd(f"gmlp_{i}", zuv, c["v_gain"], c["w_tril"], c["b_exp"])
    q, k, v, ls = _qkv_prep_fwd(f"qkv_prep_{i}", zqkv, zf, c["qg"], c["kg"], c["bf"], c["bd"])
    csum = _cumsum_rows(f"forget_cumsum_{i}", ls)
    nck = -csum[:, :B_HEADS].T
    nck_rows = nck.reshape(B_HEADS, rows // t, t)
    nck_col = nck.reshape(B_HEADS, rows, 1)
    o, lse = _flash_fwd(f"attn_{i}", q, k, v, nck_rows)
    x1 = _mm(f"out_a_{i}", ya, w["out_a"], res=x)
    x1, hn = _mm(f"out_b_{i}", o, w["out_b"], res=x1, norm_gain=rp["norm_ffn"][i][None])
    return x1, hn, dict(x=x, h=h, zuv=zuv, zqkv=zqkv, zf=zf, ya=ya, q=q, k=k, v=v, nck_rows=nck_rows,
                          nck_col=nck_col, o=o, lse=lse)


def _even_bwd(i, dx1, sv, w, rp):
    e = i // 2
    c = _even_consts(e, rp)
    rows = dx1.shape[0]
    t = min(_ATT_T, rows)
    nb = rows // t
    g = {}
    dya = _mm(f"out_a_dx_{i}", dx1, w["out_a"], "nt")
    do = _mm(f"out_b_dx_{i}", dx1, w["out_b"], "nt")
    g["out_a"] = _mm(f"out_a_dw_{i}", sv["ya"], dx1, "tn")
    g["out_b"] = _mm(f"out_b_dw_{i}", sv["o"], dx1, "tn")
    dq, delta = _flash_bwd_dq(f"attn_dq_{i}", sv["q"], sv["k"], sv["v"], sv["nck_rows"], sv["o"], do, sv["lse"])
    dk, dv, dn = _flash_bwd_dkv(f"attn_dkv_{i}", sv["q"], sv["k"], sv["v"], sv["nck_col"], do,
                                sv["lse"].reshape(B_HEADS, nb, t), delta.reshape(B_HEADS, nb, t))
    dcs = jnp.pad(-dn.reshape(B_HEADS, rows).T, ((0, 0), (0, LANES - B_HEADS)))
    dls = _cumsum_rows(f"forget_cumsum_bwd_{i}", dcs, reverse=True)
    dzqkv, dzf, dqg, dkg, dbf = _qkv_prep_bwd(f"qkv_prep_bwd_{i}", sv["zqkv"], sv["zf"], dq, dk, dv, dls,
                                              c["qg"], c["kg"], c["bf"], c["bd"])
    dzuv, dws, dbs, dvg = _gmlp_bwd(f"gmlp_bwd_{i}", sv["zuv"], dya, c["v_gain"], c["w_tril"], c["w_tril_t"],
                                    c["b_exp"])
    g["in_uv"] = _mm(f"in_uv_dw_{i}", sv["h"], dzuv, "tn")
    g["in_qkv"] = _mm(f"in_qkv_dw_{i}", sv["h"], dzqkv, "tn", tn=768)
    g["in_f"] = _mm(f"in_f_dw_{i}", sv["h"], dzf, "tn")
    dh = _mm(f"in_uv_dx_{i}", dzuv, w["in_uv"], "nt")
    dh = _mm(f"in_qkv_dx_{i}", dzqkv, w["in_qkv"], "nt", res=dh, tk=768)
    dx, dgm = _mm_norm_bwd(f"in_f_dx_{i}", dzf, w["in_f"], "nt", sv["x"], dx1, rp["norm_mix"][i][None], res=dh)
    g["norm_mix"] = dgm[0]
    g["ev_b_fgate"] = dbf[0, :B_HEADS]
    g["ev_q_norm"] = dqg.reshape(B_HEADS, B_HEAD_DIM).sum(axis=0)
    g["ev_k_norm"] = dkg.reshape(B_HEADS, B_HEAD_DIM).sum(axis=0)
    g["ev_v_norm"] = dvg[0]
    g["ev_w_spatial"] = jnp.where(c["tri"][None], dws, 0.0)
    g["ev_b_spatial"] = dbs.sum(axis=-1)
    return dx, g


def _s5_consts(o, rp, r):
    prm = (rp["od_a_re"][o], rp["od_a_im"][o], rp["od_log_dt"][o], rp["od_b_re"][o], rp["od_b_im"][o])
    (ab_re, ab_im, bb_re, bb_im), vjp = jax.vjp(_s5_discretize, *prm)
    bcat = jnp.concatenate([_s5_block_diag(bb_re.transpose(0, 2, 1)), _s5_block_diag(bb_im.transpose(0, 2, 1))], axis=2)
    c_re, c_im = rp["od_c_re"][o], rp["od_c_im"][o]
    ccat = jnp.concatenate([_s5_block_diag(c_re.transpose(0, 2, 1)), _s5_block_diag(-c_im.transpose(0, 2, 1))], axis=1)
    pw = tuple(_s5_tables(ab_re, ab_im, r))
    reps = (r // _S5_CHUNK, 1)
    down = tuple(jnp.tile(t[:_S5_CHUNK], reps) for t in pw)
    up = tuple(jnp.tile(jnp.flip(t[:_S5_CHUNK], axis=0), reps) for t in pw)
    return dict(vjp=vjp, bcat=bcat.astype(BF16), ccat=ccat.astype(BF16), fwd_tabs=pw + down, bwd_tabs=pw + down + up)


def _odd_fwd(i, x, w, rp, h_in=None):
    o = i // 2
    rows = x.shape[0]
    c = _s5_consts(o, rp, min(_S5_ROWS, rows))
    hb, hf = h_in if h_in is not None else _rmsnorm_fwd(f"mix_norm_{i}", x, rp["norm_mix"][i][None], [BF16, F32])
    ys, xin = _s5_fwd(f"s5_{i}", hb, c["bcat"], c["ccat"], *c["fwd_tabs"])

    def skip_gelu(yv, hv, dv):
        y = yv + dv * hv
        return y, _gelu(y)

    y, ge = _rows(f"s5_skip_gelu_{i}", skip_gelu, [ys, hf], [w["od_d"]], [(D_MODEL, F32), (D_MODEL, BF16)])
    gl = _mm(f"glu_{i}", ge, w["glu"])

    def glu_out(xv, gv, nv):
        x1v = xv + gv[:, :D_MODEL] * _sigmoid(gv[:, D_MODEL:])
        return x1v, (x1v * _rstd(x1v)) * nv

    x1, hn = _rows(f"glu_out_{i}", glu_out, [x, gl], [rp["norm_ffn"][i][None]], [(D_MODEL, F32), (D_MODEL, BF16)])
    return x1, hn, dict(x=x, hb=hb, hf=hf, xin=xin, y=y, ge=ge, gl=gl, c=c)


def _odd_bwd(i, dx1, sv, w, rp):
    o = i // 2
    c = sv["c"]
    g = {}

    def glu_bwd(dv, gv):
        ga, gb = gv[:, :D_MODEL], gv[:, D_MODEL:]
        sg = _sigmoid(gb)
        return (jnp.concatenate([dv * sg, (dv * ga) * (sg * (1.0 - sg))], axis=1),)

    dgl, = _rows(f"glu_out_bwd_{i}", glu_bwd, [dx1, sv["gl"]], [], [(2 * D_MODEL, BF16)])
    g["glu"] = _mm(f"glu_dw_{i}", sv["ge"], dgl, "tn")
    dge = _mm(f"glu_dx_{i}", dgl, w["glu"], "nt")

    def gelu_bwd(dv, yv, hv):
        dy = dv * _gelu_grad(yv)
        return dy, jnp.sum(dy * hv, axis=0, keepdims=True)

    dy, dd = _rows(f"s5_skip_gelu_bwd_{i}", gelu_bwd, [dge, sv["y"], sv["hf"]], [], [(D_MODEL, F32)],
                   accs=[(1, D_MODEL)])
    g["od_d"] = dd[0]
    du, db, dc, dar, dai = _s5_bwd(f"s5_bwd_{i}", sv["hb"], dy, sv["xin"], c["bcat"], c["ccat"], *c["bwd_tabs"])
    dab_re = dar.sum(axis=1).reshape(S5_GROUPS, S5_STATE)
    dab_im = dai.sum(axis=1).reshape(S5_GROUPS, S5_STATE)
    dbb_re = _s5_block_diag_extract(db[:, :, :S5_LANES], S5_GROUP_CH, S5_STATE).transpose(0, 2, 1)
    dbb_im = _s5_block_diag_extract(db[:, :, S5_LANES:], S5_GROUP_CH, S5_STATE).transpose(0, 2, 1)
    g["od_a_re"], g["od_a_im"], g["od_log_dt"], g["od_b_re"], g["od_b_im"] = c["vjp"]((dab_re, dab_im, dbb_re, dbb_im))
    g["od_c_re"] = _s5_block_diag_extract(dc[:, :S5_LANES, :], S5_STATE, S5_GROUP_CH).transpose(0, 2, 1)
    g["od_c_im"] = -_s5_block_diag_extract(dc[:, S5_LANES:, :], S5_STATE, S5_GROUP_CH).transpose(0, 2, 1)

    def norm_bwd(xv, duv, dyv, drv, gv, dv):
        dh = duv + dv * dyv
        r = _rstd(xv)
        xh = xv * r
        dhg = dh * gv
        dx = drv + r * (dhg - xh * jnp.mean(dhg * xh, axis=-1, keepdims=True))
        return dx, jnp.sum(dh * xh, axis=0, keepdims=True)

    dx, dgm = _rows(f"mix_norm_bwd_{i}", norm_bwd, [sv["x"], du, dy, dx1], [rp["norm_mix"][i][None], w["od_d"]],
                    [(D_MODEL, F32)], accs=[(1, D_MODEL)])
    g["norm_mix"] = dgm[0]
    return dx, g


def _local_step(x, p, target, lw, rp):
    saved = []
    h_next = None
    for i in range(DEPTH):
        x, hn, s_mix = (_even_fwd if i % 2 == 0 else _odd_fwd)(i, x, lw[i], rp, h_next)
        nxt = None
        if i + 1 < DEPTH:
            nxt = (rp["norm_mix"][i + 1][None], [BF16, F32] if (i + 1) % 2 else [BF16])
        x, h_next, s_ch = _channel_fwd(i, x, p[i], lw[i], rp, hn, nxt)
        saved.append((s_mix, s_ch))

    def loss_fn(yv, tv):
        diff = yv - tv
        return diff * (1.0 / D_MODEL), jnp.sum(diff * diff, axis=0, keepdims=True)

    dx, sq = _rows("loss", loss_fn, [x, target], [], [(D_MODEL, F32)], accs=[(1, D_MODEL)])
    loss = 0.5 * jnp.sum(sq) / D_MODEL
    grads = [None] * DEPTH
    for i in reversed(range(DEPTH)):
        s_mix, s_ch = saved[i]
        dx, g_ch = _channel_bwd(i, dx, s_ch, lw[i], rp)
        dx, g_mix = (_even_bwd if i % 2 == 0 else _odd_bwd)(i, dx, s_mix, lw[i], rp)
        grads[i] = {**g_ch, **g_mix}
    return loss, dx, grads


WEIGHT_ORDER = ["norm_mix", "norm_ffn", "ev_w_in", "ev_b_fgate", "ev_q_norm", "ev_k_norm", "ev_v_norm", "ev_w_spatial",
                "ev_b_spatial", "ev_w_out", "od_a_re", "od_a_im", "od_log_dt", "od_b_re", "od_b_im", "od_c_re",
                "od_c_im", "od_d", "od_w_glu", "ffn_w_up", "ffn_conv_w", "ffn_conv_b", "ffn_w_down", "ple_w_proj",
                "ple_w_gate"]
SHARD_AXIS = {"ev_w_in": 2, "ev_w_out": 1, "od_d": 1, "od_w_glu": 2, "ffn_w_up": 2, "ffn_conv_w": 2, "ffn_w_down": 1,
              "ple_w_proj": 2, "ple_w_gate": 1}
BIG_WEIGHTS = [n for n in WEIGHT_ORDER if n in SHARD_AXIS]
SMALL_WEIGHTS = [n for n in WEIGHT_ORDER if n not in SHARD_AXIS]
KEPT_F32 = ("od_d", "ffn_conv_w")
IN_UV, IN_QKV_END, IN_COLS = 2 * A_WIDTH, 2 * A_WIDTH + 3 * B_WIDTH, 2 * A_WIDTH + 3 * B_WIDTH + B_HEADS


def _layer_weights(i, full, rp):
    w = {}
    up, cw, cb = full["ffn_w_up"][i], full["ffn_conv_w"][i], rp["ffn_conv_b"][i][None]
    w["up_g"], w["up_u"] = up[:, :D_FF], up[:, D_FF:]
    w["cw_g"], w["cw_u"] = cw[:, :D_FF], cw[:, D_FF:]
    w["cb_g"], w["cb_u"] = cb[:, :D_FF], cb[:, D_FF:]
    w["down"], w["ple_proj"], w["ple_gate"] = full["ffn_w_down"][i], full["ple_w_proj"][i], full["ple_w_gate"][i]
    if i % 2 == 0:
        win, wout = full["ev_w_in"][i // 2], full["ev_w_out"][i // 2]
        w["in_uv"], w["in_qkv"] = win[:, :IN_UV], win[:, IN_UV:IN_QKV_END]
        w["in_f"] = jnp.pad(win[:, IN_QKV_END:], ((0, 0), (0, LANES - B_HEADS)))
        w["out_a"], w["out_b"] = wout[:A_WIDTH], wout[A_WIDTH:]
    else:
        w["od_d"], w["glu"] = full["od_d"][i // 2][None], full["od_w_glu"][i // 2]
    return w


def _full_grads(grads):
    ev, od = [grads[i] for i in range(0, DEPTH, 2)], [grads[i] for i in range(1, DEPTH, 2)]
    out = {
        "norm_mix": jnp.stack([g["norm_mix"] for g in grads]), "norm_ffn": jnp.stack([g["norm_ffn"] for g in grads]),
        "ev_w_in": jnp.stack([jnp.concatenate([g["in_uv"], g["in_qkv"], g["in_f"][:, :B_HEADS]], axis=1) for g in ev]),
        "ev_w_out": jnp.stack([jnp.concatenate([g["out_a"], g["out_b"]], axis=0) for g in ev]),
        "od_w_glu": jnp.stack([g["glu"] for g in od]),
        "ffn_w_up": jnp.stack([jnp.concatenate([g["up_g"], g["up_u"]], axis=1) for g in grads]),
        "ffn_conv_w": jnp.stack([jnp.concatenate([g["cw_g"], g["cw_u"]], axis=1) for g in grads]),
        "ffn_conv_b": jnp.stack([g["conv_b"] for g in grads]),
        "ffn_w_down": jnp.stack([g["down"] for g in grads]),
        "ple_w_proj": jnp.stack([g["ple_proj"] for g in grads]),
        "ple_w_gate": jnp.stack([g["ple_gate"] for g in grads]),
    }
    for n in ("ev_b_fgate", "ev_q_norm", "ev_k_norm", "ev_v_norm", "ev_w_spatial", "ev_b_spatial"):
        out[n] = jnp.stack([g[n] for g in ev])
    for n in ("od_a_re", "od_a_im", "od_log_dt", "od_b_re", "od_b_im", "od_c_re", "od_c_im", "od_d"):
        out[n] = jnp.stack([g[n] for g in od])
    return out


def _pack(arrs, row_multiple):
    flat = jnp.concatenate([a.reshape(-1) for a in arrs])
    rows = -(-flat.shape[0] // (PACK_W * row_multiple)) * row_multiple
    return jnp.pad(flat, (0, rows * PACK_W - flat.shape[0])).reshape(rows, PACK_W)


def _unpack(buf, shapes):
    flat = buf.reshape(-1)
    out, at = [], 0
    for s in shapes:
        n = int(np.prod(s))
        out.append(flat[at:at + n].reshape(s))
        at += n
    return out


def _shard(name, a, k):
    ax = SHARD_AXIS[name]
    n = a.shape[ax] // N_CHIPS
    return lax.slice_in_dim(a, k * n, (k + 1) * n, axis=ax)


_ANY = pl.BlockSpec(memory_space=pl.ANY)


def _mesh_pos():
    return lax.axis_index("x"), lax.axis_index("y"), lax.axis_index("c")


def _other_chips(x, y):
    return [(1 - x, y), (x, 1 - y), (1 - x, 1 - y)]


def _gather_shards(name, shards):
    n = len(shards)

    def body(*refs):
        ins, outs = refs[:n], refs[n:2 * n]
        send_sems, recv_sems, local_sems = refs[2 * n:]
        x, y, c = _mesh_pos()
        sibling = (x, y, 1 - c)
        chips = _other_chips(x, y)

        def part(a, k, hc):
            half = shards[a].shape[0] // 2
            return outs[a].at[k, pl.ds(hc * half, half), :]

        def copy(sem, src, dst, to):
            return pltpu.make_async_remote_copy(src_ref=src, dst_ref=dst, send_sem=send_sems.at[sem],
                                                recv_sem=recv_sems.at[sem], device_id=to, device_id_type=MESH)

        local, sent, passed = [], [], []
        for a in range(n):
            half = shards[a].shape[0] // 2
            local.append(pltpu.make_async_copy(ins[a], outs[a].at[2 * x + y], local_sems.at[a]))
            local[-1].start()
            for j, (cx, cy) in enumerate(chips):
                sent.append(copy(6 * a + j, ins[a].at[pl.ds(c * half, half), :], part(a, 2 * x + y, c), (cx, cy, c)))
                sent[-1].start()
        for a in range(n):
            for j, (cx, cy) in enumerate(chips):
                blk = part(a, 2 * cx + cy, c)
                copy(6 * a + j, blk, blk, (cx, cy, c)).wait_recv()
                passed.append(copy(6 * a + 3 + j, blk, blk, sibling))
                passed[-1].start()
        for a in range(n):
            for j, (cx, cy) in enumerate(chips):
                blk = part(a, 2 * cx + cy, 1 - c)
                copy(6 * a + 3 + j, blk, blk, sibling).wait_recv()
        for cp in sent + passed:
            cp.wait_send()
        for cp in local:
            cp.wait()

    return pl.pallas_call(
        body, out_shape=[jax.ShapeDtypeStruct((N_CHIPS,) + s.shape, s.dtype) for s in shards],
        in_specs=[_ANY] * n, out_specs=[_ANY] * n,
        scratch_shapes=[pltpu.SemaphoreType.DMA((6 * n,)), pltpu.SemaphoreType.DMA((6 * n,)),
                        pltpu.SemaphoreType.DMA((n,))],
        name=name,
    )(*shards)


def _swap_halves(name, arrs):
    n = len(arrs)

    def body(*refs):
        ins, outs = refs[:n], refs[n:2 * n]
        send_sems, recv_sems = refs[2 * n:]
        x, y, c = _mesh_pos()
        cps = []
        for a in range(n):
            half = arrs[a].shape[1] // 2
            cps.append(pltpu.make_async_remote_copy(
                src_ref=ins[a].at[:, pl.ds((1 - c) * half, half), :], dst_ref=outs[a], send_sem=send_sems.at[a],
                recv_sem=recv_sems.at[a], device_id=(x, y, 1 - c), device_id_type=MESH))
            cps[-1].start()
        for cp in cps:
            cp.wait()

    return pl.pallas_call(
        body, out_shape=[jax.ShapeDtypeStruct((a.shape[0], a.shape[1] // 2, a.shape[2]), a.dtype) for a in arrs],
        in_specs=[_ANY] * n, out_specs=[_ANY] * n,
        scratch_shapes=[pltpu.SemaphoreType.DMA((n,)), pltpu.SemaphoreType.DMA((n,))], name=name,
    )(*arrs)


def _send_to_owner_chips(name, arrs):
    n = len(arrs)

    def body(*refs):
        ins, outs = refs[:n], refs[n:2 * n]
        send_sems, recv_sems = refs[2 * n:]
        x, y, c = _mesh_pos()
        cps = []
        for a in range(n):
            for j, (cx, cy) in enumerate(_other_chips(x, y)):
                cps.append(pltpu.make_async_remote_copy(
                    src_ref=ins[a].at[2 * cx + cy], dst_ref=outs[a].at[j], send_sem=send_sems.at[3 * a + j],
                    recv_sem=recv_sems.at[3 * a + j], device_id=(cx, cy, c), device_id_type=MESH))
                cps[-1].start()
        for cp in cps:
            cp.wait()

    return pl.pallas_call(
        body, out_shape=[jax.ShapeDtypeStruct((3,) + a.shape[1:], a.dtype) for a in arrs],
        in_specs=[_ANY] * n, out_specs=[_ANY] * n,
        scratch_shapes=[pltpu.SemaphoreType.DMA((3 * n,)), pltpu.SemaphoreType.DMA((3 * n,))], name=name,
    )(*arrs)


def _swap_with_sibling(name, arrs):
    n = len(arrs)

    def body(*refs):
        ins, outs = refs[:n], refs[n:2 * n]
        send_sems, recv_sems = refs[2 * n:]
        x, y, c = _mesh_pos()
        cps = []
        for a in range(n):
            cps.append(pltpu.make_async_remote_copy(
                src_ref=ins[a], dst_ref=outs[a], send_sem=send_sems.at[a], recv_sem=recv_sems.at[a],
                device_id=(x, y, 1 - c), device_id_type=MESH))
            cps[-1].start()
        for cp in cps:
            cp.wait()

    return pl.pallas_call(
        body, out_shape=[jax.ShapeDtypeStruct(a.shape, a.dtype) for a in arrs],
        in_specs=[_ANY] * n, out_specs=[_ANY] * n,
        scratch_shapes=[pltpu.SemaphoreType.DMA((n,)), pltpu.SemaphoreType.DMA((n,))], name=name,
    )(*arrs)


def _all_gather_devices(name, a):
    rows, w = a.shape

    def body(a_ref, out_ref, send_sems, recv_sems, local_sem):
        x, y, c = _mesh_pos()
        me, sibling = (x, y, c), (x, y, 1 - c)
        chips = _other_chips(x, y)

        def slot(px, py, pc):
            return out_ref.at[4 * px + 2 * py + pc]

        def copy(sem, block, to, src=None):
            return pltpu.make_async_remote_copy(src_ref=slot(*block) if src is None else src, dst_ref=slot(*block),
                                                send_sem=send_sems.at[sem], recv_sem=recv_sems.at[sem], device_id=to,
                                                device_id_type=MESH)

        mine = pltpu.make_async_copy(a_ref, slot(*me), local_sem)
        mine.start()
        first = [copy(0, me, sibling, src=a_ref)]
        first += [copy(1 + j, me, (*chip, c), src=a_ref) for j, chip in enumerate(chips)]
        for cp in first:
            cp.start()
        passed = [copy(4 + j, (*chip, c), sibling) for j, chip in enumerate(chips)]
        for j, chip in enumerate(chips):
            copy(1 + j, (*chip, c), me).wait_recv()
            passed[j].start()
        copy(0, sibling, me).wait_recv()
        for j, chip in enumerate(chips):
            copy(4 + j, (*chip, 1 - c), me).wait_recv()
        for cp in first + passed:
            cp.wait_send()
        mine.wait()

    return pl.pallas_call(
        body, out_shape=jax.ShapeDtypeStruct((8, rows, w), a.dtype), in_specs=[_ANY], out_specs=_ANY,
        scratch_shapes=[pltpu.SemaphoreType.DMA((7,)), pltpu.SemaphoreType.DMA((7,)), pltpu.SemaphoreType.DMA],
        name=name,
    )(a)


_PACK_TILE = 256


def _sum_rows(name, arrs):
    def fn(*vals):
        tot = vals[0]
        for v in vals[1:]:
            tot = tot + v
        return (tot,)

    return _rows(name, fn, list(arrs), [], [(arrs[0].shape[1], F32)], tile=_PACK_TILE)[0]


def _adam_math(wv, gv, mv, vv):
    m2 = ADAM_B1 * mv + (1.0 - ADAM_B1) * gv
    v2 = ADAM_B2 * vv + (1.0 - ADAM_B2) * (gv * gv)
    m_hat = m2 / (1.0 - ADAM_B1 ** ADAM_STEP)
    v_hat = v2 / (1.0 - ADAM_B2 ** ADAM_STEP)
    delta = -ADAM_LR * (m_hat / (jnp.sqrt(v_hat) + ADAM_EPS) + ADAM_WD * wv)
    return delta, m2, v2


def _adamw(name, w, g, m, v):
    return _rows(name, _adam_math, [w, g, m, v], [], [(w.shape[1], F32)] * 3, tile=_PACK_TILE)


_INPUT_ORDER = (["x", "p"] + WEIGHT_ORDER + ["loss_target"] + ["m_" + n for n in WEIGHT_ORDER]
                + ["v_" + n for n in WEIGHT_ORDER])


_SUM_ROWS = 128


def _pair_sum(name, g, got, core):
    nk, rows, w = g.shape
    half = rows // 2
    nt = half // _SUM_ROWS

    def body(c_ref, g_ref, got_ref, o_ref, ob_ref):
        tot = g_ref[...] + got_ref[...]
        o_ref[...] = tot
        ob_ref[...] = tot.astype(BF16)

    spec = pl.BlockSpec((None, _SUM_ROWS, w), lambda k, i, c: (k, i, 0))
    grid_spec = pltpu.PrefetchScalarGridSpec(
        num_scalar_prefetch=1, grid=(nk, nt),
        in_specs=[pl.BlockSpec((None, _SUM_ROWS, w), lambda k, i, c: (k, c[0] * nt + i, 0)), spec],
        out_specs=[spec, spec])
    return pl.pallas_call(
        body, grid_spec=grid_spec, name=name, compiler_params=_cparams(("parallel", "parallel")),
        out_shape=[jax.ShapeDtypeStruct((nk, half, w), F32), jax.ShapeDtypeStruct((nk, half, w), BF16)])(core, g, got)


def _owner_sum(name, pair, owed, chip):
    _, half, w = pair.shape

    def body(k_ref, p_ref, a_ref, b_ref, c_ref, o_ref):
        o_ref[...] = ((p_ref[...] + a_ref[...].astype(F32)) + b_ref[...].astype(F32)) + c_ref[...].astype(F32)

    def owed_spec(j):
        return pl.BlockSpec((None, _SUM_ROWS, w), lambda i, k: (j, i, 0))

    grid_spec = pltpu.PrefetchScalarGridSpec(
        num_scalar_prefetch=1, grid=(half // _SUM_ROWS,),
        in_specs=[pl.BlockSpec((None, _SUM_ROWS, w), lambda i, k: (k[0], i, 0)), owed_spec(0), owed_spec(1),
                  owed_spec(2)],
        out_specs=pl.BlockSpec((_SUM_ROWS, w), lambda i, k: (i, 0)))
    return pl.pallas_call(body, grid_spec=grid_spec, out_shape=jax.ShapeDtypeStruct((half, w), F32), name=name,
                          compiler_params=_cparams(("parallel",)))(chip, pair, owed, owed, owed)


def _adamw_halves(name, w, mine, other, m, v, core):
    rows, wd = w.shape
    nh = (rows // 2) // _SUM_ROWS

    def body(c_ref, w_ref, a_ref, b_ref, m_ref, v_ref, g_ref, d_ref, m2_ref, v2_ref):
        own = (pl.program_id(0) // nh) == c_ref[0]
        g = jnp.where(own, a_ref[...], b_ref[...])
        g_ref[...] = g
        d_ref[...], m2_ref[...], v2_ref[...] = _adam_math(w_ref[...], g, m_ref[...], v_ref[...])

    full = pl.BlockSpec((_SUM_ROWS, wd), lambda i, c: (i, 0))
    part = pl.BlockSpec((_SUM_ROWS, wd), lambda i, c: (lax.rem(i, nh), 0))
    grid_spec = pltpu.PrefetchScalarGridSpec(num_scalar_prefetch=1, grid=(2 * nh,),
                                             in_specs=[full, part, part, full, full], out_specs=[full] * 4)
    return pl.pallas_call(body, grid_spec=grid_spec, out_shape=[jax.ShapeDtypeStruct((rows, wd), F32)] * 4, name=name,
                          compiler_params=_cparams(("parallel",)))(core, w, mine, other, m, v)


MATRIX_WEIGHTS = [n for n in BIG_WEIGHTS if n not in KEPT_F32]
TINY_SHARDED = [n for n in BIG_WEIGHTS if n in KEPT_F32]


def _as_rows(a):
    return a.reshape(-1, a.shape[-1])


def _owner_major(grads):
    ev, od = [grads[i] for i in range(0, DEPTH, 2)], [grads[i] for i in range(1, DEPTH, 2)]

    def cols(m, k, n):
        w = m.shape[1] // n
        return m[:, k * w:(k + 1) * w]

    def rows(m, k, n):
        r = m.shape[0] // n
        return m[k * r:(k + 1) * r]

    w_in = [jnp.concatenate([g["in_uv"], g["in_qkv"], g["in_f"][:, :B_HEADS]], axis=1) for g in ev]
    per_chip = {
        "ev_w_in": lambda k: [cols(m, k, N_CHIPS) for m in w_in],
        "ev_w_out": lambda k: [rows(g["out_a"] if k < 2 else g["out_b"], k % 2, 2) for g in ev],
        "od_w_glu": lambda k: [cols(g["glu"], k, N_CHIPS) for g in od],
        "ffn_w_up": lambda k: [cols(g["up_g"] if k < 2 else g["up_u"], k % 2, 2) for g in grads],
        "ffn_w_down": lambda k: [rows(g["down"], k, N_CHIPS) for g in grads],
        "ple_w_proj": lambda k: [cols(g["ple_proj"], k, N_CHIPS) for g in grads],
        "ple_w_gate": lambda k: [rows(g["ple_gate"], k, N_CHIPS) for g in grads],
    }
    return {n: jnp.stack([jnp.concatenate(per_chip[n](k), axis=0) for k in range(N_CHIPS)]) for n in MATRIX_WEIGHTS}


def _small_grads(grads):
    ev, od = [grads[i] for i in range(0, DEPTH, 2)], [grads[i] for i in range(1, DEPTH, 2)]
    out = {"norm_mix": jnp.stack([g["norm_mix"] for g in grads]), "norm_ffn": jnp.stack([g["norm_ffn"] for g in grads]),
           "ffn_conv_w": jnp.stack([jnp.concatenate([g["cw_g"], g["cw_u"]], axis=1) for g in grads]),
           "ffn_conv_b": jnp.stack([g["conv_b"] for g in grads])}
    for n in ("ev_b_fgate", "ev_q_norm", "ev_k_norm", "ev_v_norm", "ev_w_spatial", "ev_b_spatial"):
        out[n] = jnp.stack([g[n] for g in ev])
    for n in ("od_a_re", "od_a_im", "od_log_dt", "od_b_re", "od_b_im", "od_c_re", "od_c_im", "od_d"):
        out[n] = jnp.stack([g[n] for g in od])
    return out


def _step(a):
    xi, yi, ci = _mesh_pos()
    chip = 2 * xi + yi
    core_arr, chip_arr = ci.astype(jnp.int32).reshape(1), chip.astype(jnp.int32).reshape(1)
    rp = {n: a[n] for n in SMALL_WEIGHTS}

    tiny = _pack([a[n] for n in TINY_SHARDED], 32)
    gathered = _gather_shards("gather_weights", [_as_rows(a[n]).astype(BF16) for n in MATRIX_WEIGHTS] + [tiny])
    full = {}
    for n, g in zip(MATRIX_WEIGHTS, gathered):
        full[n] = jnp.concatenate([g[k].reshape(a[n].shape) for k in range(N_CHIPS)], axis=SHARD_AXIS[n])
    tiny_parts = [_unpack(gathered[-1][k], [a[n].shape for n in TINY_SHARDED]) for k in range(N_CHIPS)]
    for idx, n in enumerate(TINY_SHARDED):
        full[n] = jnp.concatenate([tiny_parts[k][idx] for k in range(N_CHIPS)], axis=SHARD_AXIS[n])
    lw = [_layer_weights(i, full, rp) for i in range(DEPTH)]

    loss_local, grad_x, grads = _local_step(a["x"][0], a["p"][:, 0], a["loss_target"][0], lw, rp)
    loss = lax.psum(loss_local, ("x", "y", "c"))

    contrib = _owner_major(grads)
    mats = [contrib[n] for n in MATRIX_WEIGHTS]
    got = _swap_halves("grad_pair_swap", mats)
    pair = [_pair_sum(f"grad_pair_sum_{n}", g, h, core_arr) for n, g, h in zip(MATRIX_WEIGHTS, mats, got)]
    owed = _send_to_owner_chips("grad_to_owner", [pb for _, pb in pair])
    mine = [_owner_sum(f"grad_owner_sum_{n}", p, o, chip_arr) for n, (p, _), o in zip(MATRIX_WEIGHTS, pair, owed)]
    theirs = _swap_with_sibling("grad_half_swap", mine)

    small_names = SMALL_WEIGHTS + TINY_SHARDED
    sg = _small_grads(grads)
    everyone = _all_gather_devices("small_grad_gather", _pack([sg[n] for n in small_names], _PACK_TILE))
    g_small = _sum_rows("small_grad_sum", [everyone[d] for d in range(8)])
    small_full = dict(zip(small_names, _unpack(g_small, [sg[n].shape for n in small_names])))

    out = {}
    for n, own_half, other_half in zip(MATRIX_WEIGHTS, mine, theirs):
        shape = a[n].shape
        g2d, delta, m2, v2 = _adamw_halves(f"adamw_{n}", _as_rows(a[n]), own_half, other_half,
                                           _as_rows(a["m_" + n]), _as_rows(a["v_" + n]), core_arr)
        for kind, val in (("grad", g2d), ("delta", delta), ("new_m", m2), ("new_v", v2)):
            out[kind + "_" + n] = val.reshape(shape)
    g_sm = {n: small_full[n] for n in SMALL_WEIGHTS}
    for n in TINY_SHARDED:
        width = a[n].shape[SHARD_AXIS[n]]
        g_sm[n] = lax.dynamic_slice_in_dim(small_full[n], chip * width, width, axis=SHARD_AXIS[n])
    shapes = [a[n].shape for n in small_names]
    w, m, v = (_pack([a[pre + n] for n in small_names], _PACK_TILE) for pre in ("", "m_", "v_"))
    g = _pack([g_sm[n] for n in small_names], _PACK_TILE)
    delta, m2, v2 = _adamw("adamw_small", w, g, m, v)
    for kind, buf in (("delta", delta), ("new_m", m2), ("new_v", v2)):
        for n, val in zip(small_names, _unpack(buf, shapes)):
            out[kind + "_" + n] = val
    for n in small_names:
        out["grad_" + n] = g_sm[n]
    res = [loss, grad_x[None]]
    for kind in ("grad", "delta", "new_m", "new_v"):
        res += [out[kind + "_" + n] for n in WEIGHT_ORDER]
    return tuple(res)


def kernel(x, p, norm_mix, norm_ffn, ev_w_in, ev_b_fgate, ev_q_norm, ev_k_norm, ev_v_norm, ev_w_spatial, ev_b_spatial, ev_w_out, od_a_re, od_a_im, od_log_dt, od_b_re, od_b_im, od_c_re, od_c_im, od_d, od_w_glu, ffn_w_up, ffn_conv_w, ffn_conv_b, ffn_w_down, ple_w_proj, ple_w_gate, loss_target, m_norm_mix, m_norm_ffn, m_ev_w_in, m_ev_b_fgate, m_ev_q_norm, m_ev_k_norm, m_ev_v_norm, m_ev_w_spatial, m_ev_b_spatial, m_ev_w_out, m_od_a_re, m_od_a_im, m_od_log_dt, m_od_b_re, m_od_b_im, m_od_c_re, m_od_c_im, m_od_d, m_od_w_glu, m_ffn_w_up, m_ffn_conv_w, m_ffn_conv_b, m_ffn_w_down, m_ple_w_proj, m_ple_w_gate, v_norm_mix, v_norm_ffn, v_ev_w_in, v_ev_b_fgate, v_ev_q_norm, v_ev_k_norm, v_ev_v_norm, v_ev_w_spatial, v_ev_b_spatial, v_ev_w_out, v_od_a_re, v_od_a_im, v_od_log_dt, v_od_b_re, v_od_b_im, v_od_c_re, v_od_c_im, v_od_d, v_od_w_glu, v_ffn_w_up, v_ffn_conv_w, v_ffn_conv_b, v_ffn_w_down, v_ple_w_proj, v_ple_w_gate):
    args = (x, p, norm_mix, norm_ffn, ev_w_in, ev_b_fgate, ev_q_norm, ev_k_norm, ev_v_norm, ev_w_spatial, ev_b_spatial, ev_w_out, od_a_re, od_a_im, od_log_dt, od_b_re, od_b_im, od_c_re, od_c_im, od_d, od_w_glu, ffn_w_up, ffn_conv_w, ffn_conv_b, ffn_w_down, ple_w_proj, ple_w_gate, loss_target, m_norm_mix, m_norm_ffn, m_ev_w_in, m_ev_b_fgate, m_ev_q_norm, m_ev_k_norm, m_ev_v_norm, m_ev_w_spatial, m_ev_b_spatial, m_ev_w_out, m_od_a_re, m_od_a_im, m_od_log_dt, m_od_b_re, m_od_b_im, m_od_c_re, m_od_c_im, m_od_d, m_od_w_glu, m_ffn_w_up, m_ffn_conv_w, m_ffn_conv_b, m_ffn_w_down, m_ple_w_proj, m_ple_w_gate, v_norm_mix, v_norm_ffn, v_ev_w_in, v_ev_b_fgate, v_ev_q_norm, v_ev_k_norm, v_ev_v_norm, v_ev_w_spatial, v_ev_b_spatial, v_ev_w_out, v_od_a_re, v_od_a_im, v_od_log_dt, v_od_b_re, v_od_b_im, v_od_c_re, v_od_c_im, v_od_d, v_od_w_glu, v_ffn_w_up, v_ffn_conv_w, v_ffn_conv_b, v_ffn_w_down, v_ple_w_proj, v_ple_w_gate)
    return _step(dict(zip(_INPUT_ORDER, args)))
```

```python
import functools
import math

import jax
import jax.numpy as jnp
import numpy as np
from jax import lax
from jax.experimental import pallas as pl
from jax.experimental.pallas import tpu as pltpu

F32 = jnp.float32
BF16 = jnp.bfloat16
MESH = pl.DeviceIdType.MESH

V7X_VMEM_LIMIT_BYTES = 56 * 1024 * 1024
LANES = 128

D_MODEL = 1024
DEPTH = 4
A_GROUPS = 4
A_CHUNK = 128
A_WIDTH = 512
B_HEADS = 8
B_HEAD_DIM = 64
B_WIDTH = 512
S5_GROUP_CH = 16
S5_GROUPS = 64
S5_STATE = 64
S5_BLOCKS = 8
S5_LANES = 512
D_FF = 2816
PLE_DIM = 256
EPS = 1e-6
NEG_INF = -1e30

ADAM_LR = 0.001
ADAM_B1 = 0.9
ADAM_B2 = 0.999
ADAM_EPS = 1e-08
ADAM_WD = 0.01
ADAM_STEP = 10

N_CHIPS = 4
PACK_W = 1024


def _cparams(sem):
    return pltpu.CompilerParams(dimension_semantics=sem, vmem_limit_bytes=V7X_VMEM_LIMIT_BYTES)


def _pick(n, target):
    if n <= target:
        return n
    t = (target // LANES) * LANES
    while t >= LANES:
        if n % t == 0:
            return t
        t -= LANES
    return n


_GELU_K = 0.7978845608028654
_GELU_C = 0.044715


def _gelu(x):
    return x * (0.5 * (1.0 + jnp.tanh(_GELU_K * (x + _GELU_C * (x * x * x)))))


def _gelu_grad(x):
    x2 = x * x
    t = jnp.tanh(_GELU_K * (x + _GELU_C * (x * x2)))
    return 0.5 * (1.0 + t) + (0.5 * x) * (1.0 - t * t) * (_GELU_K * (1.0 + (3.0 * _GELU_C) * x2))


def _sigmoid(x):
    return 1.0 / (1.0 + jnp.exp(-x))


def _log_sigmoid(x):
    return -(jnp.maximum(-x, 0.0) + jnp.log(1.0 + jnp.exp(-jnp.abs(x))))


def _rstd(x):
    return lax.rsqrt(jnp.mean(x * x, axis=-1, keepdims=True) + EPS)


def _rows(name, fn, row_ins, full_ins, outs, accs=(), tile=256):
    rows = row_ins[0].shape[0]
    r = min(tile, rows)
    n = rows // r
    n_in = len(row_ins) + len(full_ins)
    n_out = len(outs)

    def body(*refs):
        res = fn(*[ref[...] for ref in refs[:n_in]])
        for ref, v in zip(refs[n_in:n_in + n_out], res[:n_out]):
            ref[...] = v.astype(ref.dtype)
        acc_refs = refs[n_in + n_out:]
        if acc_refs:
            @pl.when(pl.program_id(0) == 0)
            def _():
                for ref in acc_refs:
                    ref[...] = jnp.zeros(ref.shape, ref.dtype)

            for ref, v in zip(acc_refs, res[n_out:]):
                ref[...] += v

    in_specs = [pl.BlockSpec((r, a.shape[1]), lambda i: (i, 0)) for a in row_ins]
    in_specs += [pl.BlockSpec(a.shape, lambda i, nd=a.ndim: (0,) * nd) for a in full_ins]
    out_shape = [jax.ShapeDtypeStruct((rows, w), dt) for (w, dt) in outs]
    out_shape += [jax.ShapeDtypeStruct(s, F32) for s in accs]
    out_specs = [pl.BlockSpec((r, w), lambda i: (i, 0)) for (w, dt) in outs]
    out_specs += [pl.BlockSpec(s, lambda i, nd=len(s): (0,) * nd) for s in accs]
    return pl.pallas_call(
        body, grid=(n,), in_specs=in_specs, out_specs=out_specs, out_shape=out_shape, name=name,
        compiler_params=_cparams(("arbitrary",) if accs else ("parallel",)),
    )(*row_ins, *full_ins)


_DOT_DIMS = {"nn": (((1,), (0,)), ((), ())), "nt": (((1,), (1,)), ((), ())), "tn": (((0,), (0,)), ((), ()))}


def _mm(name, a, b, mode="nn", out_dtype=F32, res=None, tm=1024, tn=1024, tk=1024, norm_gain=None):
    if mode == "nn":
        (m, k), (k2, n) = a.shape, b.shape
    elif mode == "nt":
        (m, k), (n, k2) = a.shape, b.shape
    else:
        (k, m), (k2, n) = a.shape, b.shape
    assert k == k2, (name, a.shape, b.shape, mode)
    tm, tn, tk = _pick(m, tm), _pick(n, tn), _pick(k, tk)
    nk = k // tk
    dims = _DOT_DIMS[mode]
    has_res = res is not None
    has_norm = norm_gain is not None
    assert not has_norm or tn == n, (name, tn, n)
    n_in = 2 + has_res + has_norm

    def body(*refs):
        a_ref, b_ref = refs[0], refs[1]
        res_ref = refs[2] if has_res else None
        gain_ref = refs[n_in - 1] if has_norm else None
        o_ref = refs[n_in]
        h_ref = refs[n_in + 1] if has_norm else None

        def finish(tot):
            if has_res:
                tot = res_ref[...] + tot
            o_ref[...] = tot.astype(o_ref.dtype)
            if has_norm:
                h_ref[...] = ((tot * _rstd(tot)) * gain_ref[...]).astype(h_ref.dtype)

        prod = lax.dot_general(a_ref[...].astype(BF16), b_ref[...].astype(BF16), dims, preferred_element_type=F32)
        if nk == 1:
            finish(prod)
            return
        acc = refs[-1]
        kk = pl.program_id(2)

        @pl.when(kk == 0)
        def _():
            acc[...] = prod

        @pl.when(kk > 0)
        def _():
            acc[...] += prod

        @pl.when(kk == nk - 1)
        def _():
            finish(acc[...])

    if mode == "nn":
        a_spec = pl.BlockSpec((tm, tk), lambda i, j, kk: (i, kk))
        b_spec = pl.BlockSpec((tk, tn), lambda i, j, kk: (kk, j))
    elif mode == "nt":
        a_spec = pl.BlockSpec((tm, tk), lambda i, j, kk: (i, kk))
        b_spec = pl.BlockSpec((tn, tk), lambda i, j, kk: (j, kk))
    else:
        a_spec = pl.BlockSpec((tk, tm), lambda i, j, kk: (kk, i))
        b_spec = pl.BlockSpec((tk, tn), lambda i, j, kk: (kk, j))
    o_spec = pl.BlockSpec((tm, tn), lambda i, j, kk: (i, j))
    in_specs = [a_spec, b_spec] + ([o_spec] if has_res else [])
    in_specs += [pl.BlockSpec((1, tn), lambda i, j, kk: (0, j))] if has_norm else []
    args = (a, b) + ((res,) if has_res else ()) + ((norm_gain,) if has_norm else ())
    out_shape = jax.ShapeDtypeStruct((m, n), out_dtype)
    return pl.pallas_call(
        body, grid=(m // tm, n // tn, nk), in_specs=in_specs, out_specs=[o_spec, o_spec] if has_norm else o_spec,
        out_shape=[out_shape, jax.ShapeDtypeStruct((m, n), BF16)] if has_norm else out_shape, name=name,
        scratch_shapes=[pltpu.VMEM((tm, tn), F32)] if nk > 1 else [],
        compiler_params=_cparams(("parallel", "parallel", "arbitrary")),
    )(*args)


def _mm_norm_bwd(name, a, b, mode, x, dres, gain, res=None, tm=512, tk=1024):
    if mode == "nn":
        (m, k), (k2, n) = a.shape, b.shape
    else:
        (m, k), (n, k2) = a.shape, b.shape
    assert k == k2, (name, a.shape, b.shape, mode)
    tm, tk = _pick(m, tm), _pick(k, tk)
    nk = k // tk
    dims = _DOT_DIMS[mode]
    has_res = res is not None
    n_in = 5 + has_res

    def body(*refs):
        a_ref, b_ref = refs[0], refs[1]
        res_ref = refs[2] if has_res else None
        x_ref, dres_ref, gain_ref = refs[n_in - 3:n_in]
        o_ref, dg_ref = refs[n_in], refs[n_in + 1]

        def finish(d):
            if has_res:
                d = res_ref[...] + d
            xv = x_ref[...]
            r = _rstd(xv)
            xh = xv * r
            dyg = d * gain_ref[...]
            o_ref[...] = dres_ref[...] + r * (dyg - xh * jnp.mean(dyg * xh, axis=-1, keepdims=True))
            part = jnp.sum(d * xh, axis=0, keepdims=True)

            @pl.when(pl.program_id(0) == 0)
            def _():
                dg_ref[...] = part

            @pl.when(pl.program_id(0) > 0)
            def _():
                dg_ref[...] += part

        prod = lax.dot_general(a_ref[...].astype(BF16), b_ref[...].astype(BF16), dims, preferred_element_type=F32)
        if nk == 1:
            finish(prod)
            return
        acc = refs[-1]
        kk = pl.program_id(1)

        @pl.when(kk == 0)
        def _():
            acc[...] = prod

        @pl.when(kk > 0)
        def _():
            acc[...] += prod

        @pl.when(kk == nk - 1)
        def _():
            finish(acc[...])

    a_spec = pl.BlockSpec((tm, tk), lambda i, kk: (i, kk))
    b_spec = pl.BlockSpec((tk, n), lambda i, kk: (kk, 0)) if mode == "nn" else pl.BlockSpec((n, tk), lambda i, kk: (0, kk))
    row_spec = pl.BlockSpec((tm, n), lambda i, kk: (i, 0))
    vec_spec = pl.BlockSpec((1, n), lambda i, kk: (0, 0))
    in_specs = [a_spec, b_spec] + ([row_spec] if has_res else []) + [row_spec, row_spec, vec_spec]
    args = (a, b) + ((res,) if has_res else ()) + (x, dres, gain)
    return pl.pallas_call(
        body, grid=(m // tm, nk), in_specs=in_specs, out_specs=[row_spec, vec_spec],
        out_shape=[jax.ShapeDtypeStruct((m, n), F32), jax.ShapeDtypeStruct((1, n), F32)], name=name,
        scratch_shapes=[pltpu.VMEM((tm, n), F32)] if nk > 1 else [],
        compiler_params=_cparams(("arbitrary", "arbitrary")),
    )(*args)


def _rmsnorm_fwd(name, x, g, outs):
    def fn(xv, gv):
        y = (xv * _rstd(xv)) * gv
        return tuple(y for _ in outs)

    return _rows(name, fn, [x], [g], [(x.shape[1], dt) for dt in outs])


_CONV_ROWS = 256
_CONV_COLS = 1408


def _conv_taps(h_ref, halo_ref, first):
    h = h_ref[...]
    rows = h.shape[0]
    row = lax.broadcasted_iota(jnp.int32, (rows, 1), 0)
    keep = jnp.where(first, 0.0, 1.0)
    m1 = halo_ref[7:8, :] * keep
    m2 = halo_ref[6:7, :] * keep
    p1 = jnp.where(row == 0, m1, pltpu.roll(h, 1, 0))
    p2 = jnp.where(row == 0, m2, jnp.where(row == 1, m1, pltpu.roll(h, 2, 0)))
    return h, p1, p2


def _conv_specs(rows, r, cw):
    tile = pl.BlockSpec((r, cw), lambda j, i: (i, j))
    halo = pl.BlockSpec((8, cw), lambda j, i: (jnp.maximum(i * (r // 8) - 1, 0), j))
    vec3 = pl.BlockSpec((3, cw), lambda j, i: (0, j))
    vec1 = pl.BlockSpec((1, cw), lambda j, i: (0, j))
    return tile, halo, vec3, vec1


def _convffn_fwd(name, hg, hu, wg, wu, bg, bu):
    rows, f = hg.shape
    r, cw = min(_CONV_ROWS, rows), _pick(f, _CONV_COLS)

    def body(hg_ref, hgh_ref, hu_ref, huh_ref, wg_ref, wu_ref, bg_ref, bu_ref, o_ref):
        first = pl.program_id(1) == 0
        h, p1, p2 = _conv_taps(hg_ref, hgh_ref, first)
        g = bg_ref[...] + wg_ref[0:1, :] * p2 + wg_ref[1:2, :] * p1 + wg_ref[2:3, :] * h
        h, p1, p2 = _conv_taps(hu_ref, huh_ref, first)
        u = bu_ref[...] + wu_ref[0:1, :] * p2 + wu_ref[1:2, :] * p1 + wu_ref[2:3, :] * h
        o_ref[...] = ((g * _sigmoid(g)) * u).astype(o_ref.dtype)

    tile, halo, vec3, vec1 = _conv_specs(rows, r, cw)
    return pl.pallas_call(
        body, grid=(f // cw, rows // r), in_specs=[tile, halo, tile, halo, vec3, vec3, vec1, vec1], out_specs=tile,
        out_shape=jax.ShapeDtypeStruct((rows, f), BF16), name=name, compiler_params=_cparams(("parallel", "parallel")),
    )(hg, hg, hu, hu, wg, wu, bg, bu)


def _gate_grads(da, g, u):
    sg = _sigmoid(g)
    return da * u * (sg * (1.0 + g * (1.0 - sg))), da * (g * sg)


def _conv_back(dc, dc_next, w_ref, last):
    r = dc.shape[0]
    row = lax.broadcasted_iota(jnp.int32, (r, 1), 0)
    keep = jnp.where(last, 0.0, 1.0)
    n0 = dc_next[0:1, :] * keep
    n1 = dc_next[1:2, :] * keep
    f1 = jnp.where(row == r - 1, n0, pltpu.roll(dc, r - 1, 0))
    f2 = jnp.where(row == r - 1, n1, jnp.where(row == r - 2, n0, pltpu.roll(dc, r - 2, 0)))
    return w_ref[2:3, :] * dc + w_ref[1:2, :] * f1 + w_ref[0:1, :] * f2


def _conv_next_rows(h, nxt_ref, w_ref, b_ref):
    r = h.shape[0]
    hn = nxt_ref[...]
    row = lax.broadcasted_iota(jnp.int32, (8, 1), 0)
    m1, m2 = h[r - 1:r, :], h[r - 2:r - 1, :]
    p1 = jnp.where(row == 0, m1, pltpu.roll(hn, 1, 0))
    p2 = jnp.where(row == 0, m2, jnp.where(row == 1, m1, pltpu.roll(hn, 2, 0)))
    return b_ref[...] + w_ref[0:1, :] * p2 + w_ref[1:2, :] * p1 + w_ref[2:3, :] * hn


def _convffn_bwd(name, da, hg, hu, wg, wu, bg, bu):
    rows, f = hg.shape
    r, cw = min(_CONV_ROWS, rows), _pick(f, _CONV_COLS)
    nrt = rows // r

    def body(da_ref, dan_ref, hg_ref, hgh_ref, hgn_ref, hu_ref, huh_ref, hun_ref, wg_ref, wu_ref, bg_ref, bu_ref,
             dhg_ref, dhu_ref, dwg_ref, dwu_ref, dbg_ref, dbu_ref):
        first = pl.program_id(1) == 0
        last = pl.program_id(1) == nrt - 1
        hgv, g1, g2 = _conv_taps(hg_ref, hgh_ref, first)
        g = bg_ref[...] + wg_ref[0:1, :] * g2 + wg_ref[1:2, :] * g1 + wg_ref[2:3, :] * hgv
        huv, u1, u2 = _conv_taps(hu_ref, huh_ref, first)
        u = bu_ref[...] + wu_ref[0:1, :] * u2 + wu_ref[1:2, :] * u1 + wu_ref[2:3, :] * huv
        dcg, dcu = _gate_grads(da_ref[...], g, u)
        dcg_n, dcu_n = _gate_grads(dan_ref[...], _conv_next_rows(hgv, hgn_ref, wg_ref, bg_ref),
                                   _conv_next_rows(huv, hun_ref, wu_ref, bu_ref))
        dhg_ref[...] = _conv_back(dcg, dcg_n, wg_ref, last).astype(dhg_ref.dtype)
        dhu_ref[...] = _conv_back(dcu, dcu_n, wu_ref, last).astype(dhu_ref.dtype)

        @pl.when(first)
        def _():
            for ref in (dwg_ref, dwu_ref, dbg_ref, dbu_ref):
                ref[...] = jnp.zeros(ref.shape, ref.dtype)

        def colsum(v):
            return jnp.sum(v, axis=0, keepdims=True)

        dwg_ref[0:1, :] += colsum(dcg * g2)
        dwg_ref[1:2, :] += colsum(dcg * g1)
        dwg_ref[2:3, :] += colsum(dcg * hgv)
        dwu_ref[0:1, :] += colsum(dcu * u2)
        dwu_ref[1:2, :] += colsum(dcu * u1)
        dwu_ref[2:3, :] += colsum(dcu * huv)
        dbg_ref[...] += colsum(dcg)
        dbu_ref[...] += colsum(dcu)

    tile, halo, vec3, vec1 = _conv_specs(rows, r, cw)
    nxt = pl.BlockSpec((8, cw), lambda j, i: (jnp.minimum((i + 1) * (r // 8), rows // 8 - 1), j))
    big = jax.ShapeDtypeStruct((rows, f), BF16)
    return pl.pallas_call(
        body, grid=(f // cw, nrt),
        in_specs=[tile, nxt, tile, halo, nxt, tile, halo, nxt, vec3, vec3, vec1, vec1],
        out_specs=[tile, tile, vec3, vec3, vec1, vec1],
        out_shape=[big, big, jax.ShapeDtypeStruct((3, f), F32), jax.ShapeDtypeStruct((3, f), F32),
                   jax.ShapeDtypeStruct((1, f), F32), jax.ShapeDtypeStruct((1, f), F32)],
        name=name, compiler_params=_cparams(("parallel", "arbitrary")),
    )(da, da, hg, hg, hg, hu, hu, hu, wg, wu, bg, bu)


_GMLP_ROWS = 256


def _gmlp_group_norm(vg, gain):
    r = lax.rsqrt(jnp.mean(vg * vg, axis=-1, keepdims=True) + EPS)
    vh = vg * r
    return vh, r, vh * gain


def _gmlp_fwd(name, zuv, v_gain, w_tril, b_exp):
    rows = zuv.shape[0]
    r = min(_GMLP_ROWS, rows)

    def body(z_ref, gain_ref, w_ref, b_ref, o_ref):
        for ch in range(r // A_CHUNK):
            lo = ch * A_CHUNK
            for g in range(A_GROUPS):
                c0 = g * LANES
                u = _gelu(z_ref[lo:lo + A_CHUNK, c0:c0 + LANES])
                v = _gelu(z_ref[lo:lo + A_CHUNK, A_WIDTH + c0:A_WIDTH + c0 + LANES])
                _, _, vn = _gmlp_group_norm(v, gain_ref[:, c0:c0 + LANES])
                sv = jnp.dot(w_ref[g], vn.astype(BF16), preferred_element_type=F32) + b_ref[g]
                o_ref[lo:lo + A_CHUNK, c0:c0 + LANES] = (u * sv).astype(o_ref.dtype)

    return pl.pallas_call(
        body, grid=(rows // r,),
        in_specs=[pl.BlockSpec((r, 2 * A_WIDTH), lambda i: (i, 0)), pl.BlockSpec((1, A_WIDTH), lambda i: (0, 0)),
                  pl.BlockSpec((A_GROUPS, A_CHUNK, A_CHUNK), lambda i: (0, 0, 0)),
                  pl.BlockSpec((A_GROUPS, A_CHUNK, LANES), lambda i: (0, 0, 0))],
        out_specs=pl.BlockSpec((r, A_WIDTH), lambda i: (i, 0)),
        out_shape=jax.ShapeDtypeStruct((rows, A_WIDTH), BF16), name=name, compiler_params=_cparams(("parallel",)),
    )(zuv, v_gain, w_tril, b_exp)


def _gmlp_bwd(name, zuv, dya, v_gain, w_tril, w_tril_t, b_exp):
    rows = zuv.shape[0]
    r = min(_GMLP_ROWS, rows)

    def body(z_ref, dy_ref, gain_ref, w_ref, wt_ref, b_ref, dz_ref, dw_ref, db_ref, dgain_ref):
        @pl.when(pl.program_id(0) == 0)
        def _():
            for ref in (dw_ref, db_ref, dgain_ref):
                ref[...] = jnp.zeros(ref.shape, ref.dtype)

        for ch in range(r // A_CHUNK):
            lo = ch * A_CHUNK
            for g in range(A_GROUPS):
                c0 = g * LANES
                zu = z_ref[lo:lo + A_CHUNK, c0:c0 + LANES]
                zv = z_ref[lo:lo + A_CHUNK, A_WIDTH + c0:A_WIDTH + c0 + LANES]
                gain = gain_ref[:, c0:c0 + LANES]
                u = _gelu(zu)
                v = _gelu(zv)
                vh, rr, vn = _gmlp_group_norm(v, gain)
                vn_b = vn.astype(BF16)
                sv = jnp.dot(w_ref[g], vn_b, preferred_element_type=F32) + b_ref[g]
                dy = dy_ref[lo:lo + A_CHUNK, c0:c0 + LANES]
                dsv = dy * u
                dsv_b = dsv.astype(BF16)
                dz_ref[lo:lo + A_CHUNK, c0:c0 + LANES] = ((dy * sv) * _gelu_grad(zu)).astype(dz_ref.dtype)
                dw_ref[g] += lax.dot_general(dsv_b, vn_b, _DOT_DIMS["nt"], preferred_element_type=F32)
                db_ref[g] += dsv
                dvn = jnp.dot(wt_ref[g], dsv_b, preferred_element_type=F32)
                dgain_ref[:, c0:c0 + LANES] += jnp.sum(dvn * vh, axis=0, keepdims=True)
                dvh = dvn * gain
                dv = rr * (dvh - vh * jnp.mean(dvh * vh, axis=-1, keepdims=True))
                dz_ref[lo:lo + A_CHUNK, A_WIDTH + c0:A_WIDTH + c0 + LANES] = (dv * _gelu_grad(zv)).astype(dz_ref.dtype)

    wspec = pl.BlockSpec((A_GROUPS, A_CHUNK, A_CHUNK), lambda i: (0, 0, 0))
    bspec = pl.BlockSpec((A_GROUPS, A_CHUNK, LANES), lambda i: (0, 0, 0))
    gspec = pl.BlockSpec((1, A_WIDTH), lambda i: (0, 0))
    return pl.pallas_call(
        body, grid=(rows // r,),
        in_specs=[pl.BlockSpec((r, 2 * A_WIDTH), lambda i: (i, 0)), pl.BlockSpec((r, A_WIDTH), lambda i: (i, 0)),
                  gspec, wspec, wspec, bspec],
        out_specs=[pl.BlockSpec((r, 2 * A_WIDTH), lambda i: (i, 0)), wspec, bspec, gspec],
        out_shape=[jax.ShapeDtypeStruct((rows, 2 * A_WIDTH), BF16),
                   jax.ShapeDtypeStruct((A_GROUPS, A_CHUNK, A_CHUNK), F32),
                   jax.ShapeDtypeStruct((A_GROUPS, A_CHUNK, LANES), F32), jax.ShapeDtypeStruct((1, A_WIDTH), F32)],
        name=name, compiler_params=_cparams(("arbitrary",)),
    )(zuv, dya, v_gain, w_tril, w_tril_t, b_exp)


_ATT_T = 512
_Q_SCALE = B_HEAD_DIM ** -0.5


def _head_mean(v, bd):
    hi = v.astype(BF16)
    lo = (v - hi.astype(F32)).astype(BF16)
    tot = jnp.dot(hi, bd, preferred_element_type=F32) + jnp.dot(lo, bd, preferred_element_type=F32)
    return tot * (1.0 / B_HEAD_DIM)


def _qkv_prep_fwd(name, zqkv, zf, qg, kg, bf, bd):
    def fn(z, f, qg_v, kg_v, bf_v, bd_v):
        zq, zk, zv = z[:, :B_WIDTH], z[:, B_WIDTH:2 * B_WIDTH], z[:, 2 * B_WIDTH:]
        q = (zq * lax.rsqrt(_head_mean(zq * zq, bd_v) + EPS)) * qg_v * _Q_SCALE
        k = (zk * lax.rsqrt(_head_mean(zk * zk, bd_v) + EPS)) * kg_v
        return q, k, zv, _log_sigmoid(f + bf_v)

    return _rows(name, fn, [zqkv, zf], [qg, kg, bf, bd],
                 [(B_WIDTH, BF16), (B_WIDTH, BF16), (B_WIDTH, BF16), (LANES, F32)])


def _qkv_prep_bwd(name, zqkv, zf, dq, dk, dv, dls, qg, kg, bf, bd):
    def fn(z, f, dq_v, dk_v, dv_v, dls_v, qg_v, kg_v, bf_v, bd_v):
        zq, zk = z[:, :B_WIDTH], z[:, B_WIDTH:2 * B_WIDTH]

        def norm_bwd(x, dy, gain):
            r = lax.rsqrt(_head_mean(x * x, bd_v) + EPS)
            xh = x * r
            dxh = dy * gain
            dx = r * (dxh - xh * _head_mean(dxh * xh, bd_v))
            return dx, jnp.sum(dy * xh, axis=0, keepdims=True)

        dzq, dqg = norm_bwd(zq, dq_v * _Q_SCALE, qg_v)
        dzk, dkg = norm_bwd(zk, dk_v, kg_v)
        dzf = dls_v * (1.0 - _sigmoid(f + bf_v))
        return jnp.concatenate([dzq, dzk, dv_v], axis=1), dzf, dqg, dkg, jnp.sum(dzf, axis=0, keepdims=True)

    return _rows(name, fn, [zqkv, zf, dq, dk, dv, dls], [qg, kg, bf, bd],
                 [(3 * B_WIDTH, BF16), (LANES, BF16)], accs=[(1, B_WIDTH), (1, B_WIDTH), (1, LANES)])


def _cumsum_rows(name, a, reverse=False, tile=512):
    rows, w = a.shape
    r = min(tile, rows)
    n = rows // r

    def body(a_ref, o_ref, carry):
        @pl.when(pl.program_id(0) == 0)
        def _():
            carry[...] = jnp.zeros(carry.shape, carry.dtype)

        x = a_ref[...]
        row = lax.broadcasted_iota(jnp.int32, (r, 1), 0)
        s = 1
        while s < r:
            if reverse:
                x = x + jnp.where(row < r - s, pltpu.roll(x, r - s, 0), 0.0)
            else:
                x = x + jnp.where(row >= s, pltpu.roll(x, s, 0), 0.0)
            s *= 2
        x = x + carry[0:1, :]
        o_ref[...] = x
        edge = x[0:1, :] if reverse else x[r - 1:r, :]
        carry[...] = jnp.broadcast_to(edge, carry.shape)

    idx = (lambda i: (n - 1 - i, 0)) if reverse else (lambda i: (i, 0))
    return pl.pallas_call(
        body, grid=(n,), in_specs=[pl.BlockSpec((r, w), idx)], out_specs=pl.BlockSpec((r, w), idx),
        out_shape=jax.ShapeDtypeStruct((rows, w), F32), scratch_shapes=[pltpu.VMEM((8, w), F32)], name=name,
        compiler_params=_cparams(("arbitrary",)),
    )(a)


def _head_masks():
    lane = lax.broadcasted_iota(jnp.int32, (1, LANES), 1)
    return [lane < B_HEAD_DIM, lane >= B_HEAD_DIM]


def _causal(t):
    row = lax.broadcasted_iota(jnp.int32, (t, t), 0)
    col = lax.broadcasted_iota(jnp.int32, (t, t), 1)
    return row, col


def _col_from_row(row_vec):
    return jnp.transpose(jnp.broadcast_to(row_vec, (LANES, row_vec.shape[1])))[:, 0:1]


def _row_from_col(col):
    return jnp.transpose(jnp.broadcast_to(col, (col.shape[0], LANES)))[0:1, :]


def _flash_fwd(name, q, k, v, nck_rows):
    rows = q.shape[0]
    t = min(_ATT_T, rows)
    nb = rows // t

    def body(q_ref, k_ref, v_ref, nck_ref, o_ref, lse_ref):
        pair, i = pl.program_id(0), pl.program_id(1)
        q2 = q_ref[...]
        row, col = _causal(t)
        masks = _head_masks()
        qh = [jnp.where(hm, q2, jnp.zeros_like(q2)) for hm in masks]

        def step(j, carry, diag):
            ml, acc = carry
            start = pl.multiple_of(j * t, t)
            kb = k_ref[pl.ds(start, t), :]
            vb = v_ref[pl.ds(start, t), :]
            new_ml = []
            for hh, hm in enumerate(masks):
                m, l = ml[hh]
                s = lax.dot_general(qh[hh], kb, _DOT_DIMS["nt"], preferred_element_type=F32)
                s = s + nck_ref[2 * pair + hh, pl.ds(j, 1), :]
                if diag:
                    s = jnp.where(col <= row, s, NEG_INF)
                m_new = jnp.maximum(m, jnp.max(s, axis=1, keepdims=True))
                p = jnp.exp(s - m_new)
                alpha = jnp.exp(m - m_new)
                new_ml.append((m_new, alpha * l + jnp.sum(p, axis=1, keepdims=True)))
                pv = jnp.dot(p.astype(BF16), jnp.where(hm, vb, jnp.zeros_like(vb)), preferred_element_type=F32)
                acc = acc * jnp.where(hm, alpha, 1.0) + pv
            return tuple(new_ml), acc

        def init_ml():
            return (jnp.full((t, 1), NEG_INF, F32), jnp.zeros((t, 1), F32))

        init = ((init_ml(), init_ml()), jnp.zeros((t, LANES), F32))
        carry = lax.fori_loop(0, i, lambda j, c: step(j, c, False), init)
        ml, acc = step(i, carry, True)
        o_ref[...] = acc / jnp.where(masks[0], ml[0][1], ml[1][1])
        for hh in range(2):
            lse_ref[hh, 0] = _row_from_col(ml[hh][0] + jnp.log(ml[hh][1]))

    return pl.pallas_call(
        body, grid=(B_HEADS // 2, nb),
        in_specs=[pl.BlockSpec((t, LANES), lambda p, i: (i, p)), pl.BlockSpec((rows, LANES), lambda p, i: (0, p)),
                  pl.BlockSpec((rows, LANES), lambda p, i: (0, p)),
                  pl.BlockSpec((B_HEADS, nb, t), lambda p, i: (0, 0, 0))],
        out_specs=[pl.BlockSpec((t, LANES), lambda p, i: (i, p)),
                   pl.BlockSpec((2, 1, 1, t), lambda p, i: (p, i, 0, 0))],
        out_shape=[jax.ShapeDtypeStruct((rows, B_WIDTH), F32), jax.ShapeDtypeStruct((B_HEADS, nb, 1, t), F32)],
        name=name, compiler_params=_cparams(("parallel", "parallel")),
    )(q, k, v, nck_rows)


def _flash_bwd_dq(name, q, k, v, nck_rows, o, do, lse_rows):
    rows = q.shape[0]
    t = min(_ATT_T, rows)
    nb = rows // t

    def body(q_ref, k_ref, v_ref, nck_ref, o_ref, do_ref, lse_ref, dq_ref, delta_ref):
        pair, i = pl.program_id(0), pl.program_id(1)
        q2 = q_ref[...]
        do2 = do_ref[...]
        od = o_ref[...] * do2
        do_b = do2.astype(BF16)
        row, col = _causal(t)
        masks = _head_masks()
        qh = [jnp.where(hm, q2, jnp.zeros_like(q2)) for hm in masks]
        doh = [jnp.where(hm, do_b, jnp.zeros_like(do_b)) for hm in masks]
        delta = [jnp.sum(jnp.where(hm, od, 0.0), axis=1, keepdims=True) for hm in masks]
        lse = [_col_from_row(lse_ref[2 * pair + hh, pl.ds(i, 1), :]) for hh in range(2)]

        def step(j, carry, diag):
            acc, rowsum = carry
            start = pl.multiple_of(j * t, t)
            kb = k_ref[pl.ds(start, t), :]
            vb = v_ref[pl.ds(start, t), :]
            new_rowsum = []
            for hh, hm in enumerate(masks):
                s = lax.dot_general(qh[hh], kb, _DOT_DIMS["nt"], preferred_element_type=F32)
                s = s + nck_ref[2 * pair + hh, pl.ds(j, 1), :]
                p = jnp.exp(s - lse[hh])
                if diag:
                    p = jnp.where(col <= row, p, 0.0)
                dp = lax.dot_general(doh[hh], vb, _DOT_DIMS["nt"], preferred_element_type=F32)
                ds = p * (dp - delta[hh])
                new_rowsum.append(rowsum[hh] + jnp.sum(ds, axis=1, keepdims=True))
                acc = acc + jnp.dot(ds.astype(BF16), jnp.where(hm, kb, jnp.zeros_like(kb)),
                                    preferred_element_type=F32)
            return acc, tuple(new_rowsum)

        zcol = jnp.zeros((t, 1), F32)
        carry = lax.fori_loop(0, i, lambda j, c: step(j, c, False), (jnp.zeros((t, LANES), F32), (zcol, zcol)))
        acc, rowsum = step(i, carry, True)
        dq_ref[...] = acc
        for hh in range(2):
            delta_ref[hh, 0] = _row_from_col(delta[hh] + rowsum[hh])

    tile = pl.BlockSpec((t, LANES), lambda p, i: (i, p))
    full = pl.BlockSpec((rows, LANES), lambda p, i: (0, p))
    rowspec = pl.BlockSpec((B_HEADS, nb, t), lambda p, i: (0, 0, 0))
    return pl.pallas_call(
        body, grid=(B_HEADS // 2, nb),
        in_specs=[tile, full, full, rowspec, tile, tile, rowspec],
        out_specs=[tile, pl.BlockSpec((2, 1, 1, t), lambda p, i: (p, i, 0, 0))],
        out_shape=[jax.ShapeDtypeStruct((rows, B_WIDTH), F32), jax.ShapeDtypeStruct((B_HEADS, nb, 1, t), F32)],
        name=name, compiler_params=_cparams(("parallel", "parallel")),
    )(q, k, v, nck_rows, o, do, lse_rows)


def _flash_bwd_dkv(name, q, k, v, nck_rows, do, lse_rows, delta_rows):
    rows = q.shape[0]
    t = min(_ATT_T, rows)
    nb = rows // t

    def body(k_ref, v_ref, q_ref, do_ref, nck_ref, lse_ref, delta_ref, dk_ref, dv_ref, dn_ref):
        pair, j = pl.program_id(0), pl.program_id(1)
        k2 = k_ref[...]
        v2 = v_ref[...]
        row, col = _causal(t)
        masks = _head_masks()
        kh = [jnp.where(hm, k2, jnp.zeros_like(k2)) for hm in masks]
        vh = [jnp.where(hm, v2, jnp.zeros_like(v2)) for hm in masks]
        nck = [_col_from_row(nck_ref[2 * pair + hh, pl.ds(j, 1), :]) for hh in range(2)]

        def step(i, carry, diag):
            dk, dv, dn = carry
            start = pl.multiple_of(i * t, t)
            qb = q_ref[pl.ds(start, t), :]
            dob = do_ref[pl.ds(start, t), :].astype(BF16)
            dn_new = []
            for hh, hm in enumerate(masks):
                head = 2 * pair + hh
                st = lax.dot_general(kh[hh], qb, _DOT_DIMS["nt"], preferred_element_type=F32) + nck[hh]
                pt = jnp.exp(st - lse_ref[head, pl.ds(i, 1), :])
                if diag:
                    pt = jnp.where(row <= col, pt, 0.0)
                dpt = lax.dot_general(vh[hh], dob, _DOT_DIMS["nt"], preferred_element_type=F32)
                dst = pt * (dpt - delta_ref[head, pl.ds(i, 1), :])
                dv = dv + jnp.dot(pt.astype(BF16), jnp.where(hm, dob, jnp.zeros_like(dob)),
                                  preferred_element_type=F32)
                dk = dk + jnp.dot(dst.astype(BF16), jnp.where(hm, qb, jnp.zeros_like(qb)),
                                  preferred_element_type=F32)
                dn_new.append(dn[hh] + jnp.sum(dst, axis=1, keepdims=True))
            return dk, dv, tuple(dn_new)

        zero = jnp.zeros((t, LANES), F32)
        zcol = jnp.zeros((t, 1), F32)
        carry = step(j, (zero, zero, (zcol, zcol)), True)
        dk, dv, dn = lax.fori_loop(j + 1, nb, lambda i, c: step(i, c, False), carry)
        dk_ref[...] = dk
        dv_ref[...] = dv
        for hh in range(2):
            dn_ref[hh, 0] = _row_from_col(dn[hh])

    tile = pl.BlockSpec((t, LANES), lambda p, j: (j, p))
    full = pl.BlockSpec((rows, LANES), lambda p, j: (0, p))
    rowspec = pl.BlockSpec((B_HEADS, nb, t), lambda p, j: (0, 0, 0))
    big = jax.ShapeDtypeStruct((rows, B_WIDTH), F32)
    return pl.pallas_call(
        body, grid=(B_HEADS // 2, nb),
        in_specs=[tile, tile, full, full, rowspec, rowspec, rowspec],
        out_specs=[tile, tile, pl.BlockSpec((2, 1, 1, t), lambda p, j: (p, j, 0, 0))],
        out_shape=[big, big, jax.ShapeDtypeStruct((B_HEADS, nb, 1, t), F32)],
        name=name, compiler_params=_cparams(("parallel", "parallel")),
    )(k, v, q, do, nck_rows, lse_rows, delta_rows)


_S5_ROWS = 256


def _s5_discretize(a_re, a_im, log_dt, b_re, b_im):
    dt = jnp.exp(log_dt)[:, None]
    mag = jnp.exp(a_re * dt)
    ab_re, ab_im = mag * jnp.cos(a_im * dt), mag * jnp.sin(a_im * dt)
    den = a_re * a_re + a_im * a_im
    nr, ni = ab_re - 1.0, ab_im
    cr = (nr * a_re + ni * a_im) / den
    ci = (ni * a_re - nr * a_im) / den
    bb_re = cr[..., None] * b_re - ci[..., None] * b_im
    bb_im = cr[..., None] * b_im + ci[..., None] * b_re
    return ab_re, ab_im, bb_re, bb_im


def _s5_block_diag(m):
    g, r, c = m.shape
    mb = m.reshape(S5_BLOCKS, 8, r, c)
    eye = jnp.eye(8, dtype=m.dtype)
    return jnp.einsum("bgrc,gh->bgrhc", mb, eye).reshape(S5_BLOCKS, 8 * r, 8 * c)


def _s5_block_diag_extract(m, r, c):
    mb = m.reshape(S5_BLOCKS, 8, r, 8, c)
    return jnp.einsum("bgrhc,gh->bgrc", mb, jnp.eye(8, dtype=m.dtype)).reshape(S5_GROUPS, r, c)


def _s5_tables(ab_re, ab_im, r):
    ar = jnp.broadcast_to(ab_re.reshape(1, -1), (r, S5_GROUPS * S5_STATE))
    ai = jnp.broadcast_to(ab_im.reshape(1, -1), (r, S5_GROUPS * S5_STATE))

    def mul(x, y):
        return x[0] * y[0] - x[1] * y[1], x[0] * y[1] + x[1] * y[0]

    return lax.associative_scan(mul, (ar, ai), axis=0)


def _scan_step(xr, xi, ar, ai, s, row, up):
    r = xr.shape[0]
    if up:
        ai = -ai
    if s < 8:
        if up:
            sr = jnp.where(row < r - s, pltpu.roll(xr, r - s, 0), 0.0)
            si = jnp.where(row < r - s, pltpu.roll(xi, r - s, 0), 0.0)
        else:
            sr = jnp.where(row >= s, pltpu.roll(xr, s, 0), 0.0)
            si = jnp.where(row >= s, pltpu.roll(xi, s, 0), 0.0)
        return xr + (ar * sr - ai * si), xi + (ar * si + ai * sr)
    if up:
        (dr, di), (sr, si) = (xr[:r - s], xi[:r - s]), (xr[s:], xi[s:])
        nr, ni = dr + (ar * sr - ai * si), di + (ar * si + ai * sr)
        return jnp.concatenate([nr, xr[r - s:]], axis=0), jnp.concatenate([ni, xi[r - s:]], axis=0)
    (dr, di), (sr, si) = (xr[s:], xi[s:]), (xr[:r - s], xi[:r - s])
    nr, ni = dr + (ar * sr - ai * si), di + (ar * si + ai * sr)
    return jnp.concatenate([xr[:s], nr], axis=0), jnp.concatenate([xi[:s], ni], axis=0)


_S5_CHUNK = 16


def _scan_tile(xr, xi, pr_ref, pi_ref, tr_ref, ti_ref, edge_ref, up):
    r, nl = xr.shape
    ch = _S5_CHUNK
    nch = r // ch
    sub = lax.broadcasted_iota(jnp.int32, (r, 1), 0) & (ch - 1)
    s = 1
    while s < ch:
        ar, ai = pr_ref[s - 1:s, :], pi_ref[s - 1:s, :]
        if up:
            ai, keep, shift = -ai, sub < ch - s, r - s
        else:
            keep, shift = sub >= s, s
        sr = jnp.where(keep, pltpu.roll(xr, shift, 0), 0.0)
        si = jnp.where(keep, pltpu.roll(xi, shift, 0), 0.0)
        xr, xi = xr + (ar * sr - ai * si), xi + (ar * si + ai * sr)
        s *= 2
    nb = nl // LANES
    for k in range(nb):
        edge_ref[k] = xr[:, k * LANES:(k + 1) * LANES]
        edge_ref[nb + k] = xi[:, k * LANES:(k + 1) * LANES]
    e0 = 0 if up else ch - 1
    er = jnp.concatenate([edge_ref[k, pl.ds(e0, nch, stride=ch), :] for k in range(nb)], axis=1)
    ei = jnp.concatenate([edge_ref[nb + k, pl.ds(e0, nch, stride=ch), :] for k in range(nb)], axis=1)
    rowc = lax.broadcasted_iota(jnp.int32, (nch, 1), 0)
    s = 1
    while s < nch:
        er, ei = _scan_step(er, ei, pr_ref[ch * s - 1:ch * s, :], pi_ref[ch * s - 1:ch * s, :], s, rowc, up)
        s *= 2
    if up:
        nr = jnp.where(rowc < nch - 1, pltpu.roll(er, nch - 1, 0), 0.0)
        ni = jnp.where(rowc < nch - 1, pltpu.roll(ei, nch - 1, 0), 0.0)
    else:
        nr = jnp.where(rowc >= 1, pltpu.roll(er, 1, 0), 0.0)
        ni = jnp.where(rowc >= 1, pltpu.roll(ei, 1, 0), 0.0)
    br = jnp.concatenate([jnp.broadcast_to(nr[n:n + 1, :], (ch, nl)) for n in range(nch)], axis=0)
    bi = jnp.concatenate([jnp.broadcast_to(ni[n:n + 1, :], (ch, nl)) for n in range(nch)], axis=0)
    tr, ti = tr_ref[...], ti_ref[...]
    if up:
        ti = -ti
    return xr + (tr * br - ti * bi), xi + (tr * bi + ti * br)


def _s5_scan_tile(u_ref, bcat_ref, pr_ref, pi_ref, tr_ref, ti_ref, edge_ref, cin_r, cin_i):
    bu = jnp.dot(u_ref[...], bcat_ref[...], preferred_element_type=F32)
    xr, xi = bu[:, :S5_LANES], bu[:, S5_LANES:]
    ar, ai = pr_ref[0:1, :], pi_ref[0:1, :]
    first = lax.broadcasted_iota(jnp.int32, (8, 1), 0) == 0
    xr = jnp.concatenate([xr[:8] + jnp.where(first, ar * cin_r - ai * cin_i, 0.0), xr[8:]], axis=0)
    xi = jnp.concatenate([xi[:8] + jnp.where(first, ar * cin_i + ai * cin_r, 0.0), xi[8:]], axis=0)
    return _scan_tile(xr, xi, pr_ref, pi_ref, tr_ref, ti_ref, edge_ref, False)


def _s5_fwd(name, u, bcat, ccat, pw_re, pw_im, pt_re, pt_im):
    rows = u.shape[0]
    r = pw_re.shape[0]
    nt = rows // r

    def body(u_ref, bcat_ref, ccat_ref, pr_ref, pi_ref, tr_ref, ti_ref, y_ref, xin_ref, carry, edge):
        @pl.when(pl.program_id(1) == 0)
        def _():
            carry[...] = jnp.zeros(carry.shape, carry.dtype)

        xin_ref[...] = carry[...]
        xr, xi = _s5_scan_tile(u_ref, bcat_ref, pr_ref, pi_ref, tr_ref, ti_ref, edge,
                               carry[0:1, :S5_LANES], carry[0:1, S5_LANES:])
        xcat = jnp.concatenate([xr, xi], axis=1)
        carry[...] = jnp.broadcast_to(xcat[r - 1:r, :], carry.shape)
        y_ref[...] = jnp.dot(xcat.astype(BF16), ccat_ref[...], preferred_element_type=F32)

    tab = pl.BlockSpec((r, S5_LANES), lambda b, i: (0, b))
    return pl.pallas_call(
        body, grid=(S5_BLOCKS, nt),
        in_specs=[pl.BlockSpec((r, LANES), lambda b, i: (i, b)),
                  pl.BlockSpec((None, LANES, 2 * S5_LANES), lambda b, i: (b, 0, 0)),
                  pl.BlockSpec((None, 2 * S5_LANES, LANES), lambda b, i: (b, 0, 0)), tab, tab, tab, tab],
        out_specs=[pl.BlockSpec((r, LANES), lambda b, i: (i, b)),
                   pl.BlockSpec((None, 8, 2 * S5_LANES), lambda b, i: (b, i, 0))],
        out_shape=[jax.ShapeDtypeStruct((rows, D_MODEL), F32),
                   jax.ShapeDtypeStruct((S5_BLOCKS, 8 * nt, 2 * S5_LANES), F32)],
        scratch_shapes=[pltpu.VMEM((8, 2 * S5_LANES), F32), pltpu.VMEM((2 * S5_LANES // LANES, r, LANES), F32)], name=name,
        compiler_params=_cparams(("parallel", "arbitrary")),
    )(u, bcat, ccat, pw_re, pw_im, pt_re, pt_im)


def _s5_bwd(name, u, dy, xin, bcat, ccat, pw_re, pw_im, pt_re, pt_im, ptu_re, ptu_im):
    rows = u.shape[0]
    r = pw_re.shape[0]
    nt = rows // r

    def body(u_ref, dy_ref, xin_ref, bcat_ref, ccat_ref, pr_ref, pi_ref, tr_ref, ti_ref, ur_ref, ui_ref,
             du_ref, db_ref, dc_ref, dar_ref, dai_ref, carry, edge):
        @pl.when(pl.program_id(1) == 0)
        def _():
            carry[...] = jnp.zeros(carry.shape, carry.dtype)
            for ref in (db_ref, dc_ref, dar_ref, dai_ref):
                ref[...] = jnp.zeros(ref.shape, ref.dtype)

        row = lax.broadcasted_iota(jnp.int32, (r, 1), 0)
        cin_r, cin_i = xin_ref[0:1, :S5_LANES], xin_ref[0:1, S5_LANES:]
        xr, xi = _s5_scan_tile(u_ref, bcat_ref, pr_ref, pi_ref, tr_ref, ti_ref, edge, cin_r, cin_i)
        dy_b = dy_ref[...].astype(BF16)
        xcat = jnp.concatenate([xr, xi], axis=1).astype(BF16)
        dc_ref[...] += lax.dot_general(xcat, dy_b, _DOT_DIMS["tn"], preferred_element_type=F32)
        g = lax.dot_general(dy_b, ccat_ref[...], _DOT_DIMS["nt"], preferred_element_type=F32)
        lr, li = g[:, :S5_LANES], g[:, S5_LANES:]
        nr, ni = carry[0:1, :S5_LANES], carry[0:1, S5_LANES:]
        ar, ai = pr_ref[0:1, :], pi_ref[0:1, :]
        final = lax.broadcasted_iota(jnp.int32, (8, 1), 0) == 7
        lr = jnp.concatenate([lr[:r - 8], lr[r - 8:] + jnp.where(final, ar * nr + ai * ni, 0.0)], axis=0)
        li = jnp.concatenate([li[:r - 8], li[r - 8:] + jnp.where(final, ar * ni - ai * nr, 0.0)], axis=0)
        lr, li = _scan_tile(lr, li, pr_ref, pi_ref, ur_ref, ui_ref, edge, True)
        carry[...] = jnp.broadcast_to(jnp.concatenate([lr[0:1, :], li[0:1, :]], axis=1), carry.shape)
        lcat = jnp.concatenate([lr, li], axis=1).astype(BF16)
        du_ref[...] = lax.dot_general(lcat, bcat_ref[...], _DOT_DIMS["nt"], preferred_element_type=F32)
        db_ref[...] += lax.dot_general(u_ref[...], lcat, _DOT_DIMS["tn"], preferred_element_type=F32)
        pxr = jnp.where(row == 0, cin_r, pltpu.roll(xr, 1, 0))
        pxi = jnp.where(row == 0, cin_i, pltpu.roll(xi, 1, 0))
        dar_ref[...] += jnp.sum((lr * pxr + li * pxi).reshape(r // 8, 8, S5_LANES), axis=0)
        dai_ref[...] += jnp.sum((li * pxr - lr * pxi).reshape(r // 8, 8, S5_LANES), axis=0)

    rev = lambda b, i: (nt - 1 - i, b)
    tab = pl.BlockSpec((r, S5_LANES), lambda b, i: (0, b))
    return pl.pallas_call(
        body, grid=(S5_BLOCKS, nt),
        in_specs=[pl.BlockSpec((r, LANES), rev), pl.BlockSpec((r, LANES), rev),
                  pl.BlockSpec((None, 8, 2 * S5_LANES), lambda b, i: (b, nt - 1 - i, 0)),
                  pl.BlockSpec((None, LANES, 2 * S5_LANES), lambda b, i: (b, 0, 0)),
                  pl.BlockSpec((None, 2 * S5_LANES, LANES), lambda b, i: (b, 0, 0)), tab, tab, tab, tab, tab, tab],
        out_specs=[pl.BlockSpec((r, LANES), rev),
                   pl.BlockSpec((None, LANES, 2 * S5_LANES), lambda b, i: (b, 0, 0)),
                   pl.BlockSpec((None, 2 * S5_LANES, LANES), lambda b, i: (b, 0, 0)),
                   pl.BlockSpec((None, 8, S5_LANES), lambda b, i: (b, 0, 0)),
                   pl.BlockSpec((None, 8, S5_LANES), lambda b, i: (b, 0, 0))],
        out_shape=[jax.ShapeDtypeStruct((rows, D_MODEL), F32),
                   jax.ShapeDtypeStruct((S5_BLOCKS, LANES, 2 * S5_LANES), F32),
                   jax.ShapeDtypeStruct((S5_BLOCKS, 2 * S5_LANES, LANES), F32),
                   jax.ShapeDtypeStruct((S5_BLOCKS, 8, S5_LANES), F32),
                   jax.ShapeDtypeStruct((S5_BLOCKS, 8, S5_LANES), F32)],
        scratch_shapes=[pltpu.VMEM((8, 2 * S5_LANES), F32), pltpu.VMEM((2 * S5_LANES // LANES, r, LANES), F32)], name=name,
        compiler_params=_cparams(("parallel", "arbitrary")),
    )(u, dy, xin, bcat, ccat, pw_re, pw_im, pt_re, pt_im, ptu_re, ptu_im)


def _ones_gain():
    return jnp.ones((1, D_MODEL), F32)


def _channel_fwd(i, x1, p_i, w, rp, hn=None, next_norm=None):
    if hn is None:
        hn, = _rmsnorm_fwd(f"ffn_norm_{i}", x1, rp["norm_ffn"][i][None], [BF16])
    hg = _mm(f"ffn_up_g_{i}", hn, w["up_g"], tn=1408)
    hu = _mm(f"ffn_up_u_{i}", hn, w["up_u"], tn=1408)
    a = _convffn_fwd(f"ffn_conv_{i}", hg, hu, w["cw_g"], w["cw_u"], w["cb_g"], w["cb_u"])
    x2, r = _mm(f"ffn_down_{i}", a, w["down"], res=x1, tk=1408, norm_gain=_ones_gain())
    zg = _mm(f"ple_gate_{i}", r, w["ple_gate"])
    pp = _mm(f"ple_proj_{i}", p_i, w["ple_proj"])
    saved = dict(x1=x1, hn=hn, hg=hg, hu=hu, a=a, x2=x2, r=r, zg=zg, pp=pp, p_i=p_i)
    if next_norm is None:
        x3, = _rows(f"ple_out_{i}", lambda xv, zv, pv: (xv + _sigmoid(zv) * pv,), [x2, zg, pp], [], [(D_MODEL, F32)])
        return x3, None, saved
    gain, dtypes = next_norm

    def ple_out_norm(xv, zv, pv, gv):
        x3v = xv + _sigmoid(zv) * pv
        h = (x3v * _rstd(x3v)) * gv
        return (x3v,) + tuple(h for _ in dtypes)

    x3, *h_next = _rows(f"ple_out_{i}", ple_out_norm, [x2, zg, pp], [gain],
                        [(D_MODEL, F32)] + [(D_MODEL, dt) for dt in dtypes])
    return x3, h_next, saved


def _channel_bwd(i, dx3, sv, w, rp):
    def ple_bwd(dv, zv, pv):
        gate = _sigmoid(zv)
        return dv * gate, (dv * pv) * (gate * (1.0 - gate))

    dpp, dzg = _rows(f"ple_out_bwd_{i}", ple_bwd, [dx3, sv["zg"], sv["pp"]], [], [(D_MODEL, BF16), (D_MODEL, BF16)])
    g = {}
    g["ple_proj"] = _mm(f"ple_proj_dw_{i}", sv["p_i"], dpp, "tn")
    g["ple_gate"] = _mm(f"ple_gate_dw_{i}", sv["r"], dzg, "tn")
    dx2, _ = _mm_norm_bwd(f"ple_gate_dx_{i}", dzg, w["ple_gate"], "nt", sv["x2"], dx3, _ones_gain())
    da = _mm(f"ffn_down_dx_{i}", dx2, w["down"], "nt", tn=1408)
    g["down"] = _mm(f"ffn_down_dw_{i}", sv["a"], dx2, "tn", tm=1408)
    dhg, dhu, g["cw_g"], g["cw_u"], dbg, dbu = _convffn_bwd(
        f"ffn_conv_bwd_{i}", da, sv["hg"], sv["hu"], w["cw_g"], w["cw_u"], w["cb_g"], w["cb_u"])
    g["conv_b"] = jnp.concatenate([dbg, dbu], axis=1)[0]
    g["up_g"] = _mm(f"ffn_up_g_dw_{i}", sv["hn"], dhg, "tn", tn=1408)
    g["up_u"] = _mm(f"ffn_up_u_dw_{i}", sv["hn"], dhu, "tn", tn=1408)
    dhn = _mm(f"ffn_up_g_dx_{i}", dhg, w["up_g"], "nt", tk=1408)
    dx1, dgf = _mm_norm_bwd(f"ffn_up_u_dx_{i}", dhu, w["up_u"], "nt", sv["x1"], dx2, rp["norm_ffn"][i][None],
                            res=dhn, tk=1408)
    g["norm_ffn"] = dgf[0]
    return dx1, g


def _even_consts(e, rp):
    tri = jnp.tril(jnp.ones((A_CHUNK, A_CHUNK), dtype=bool))
    w_tril = jnp.where(tri[None], rp["ev_w_spatial"][e], 0.0).astype(BF16)
    b_exp = jnp.broadcast_to(rp["ev_b_spatial"][e][:, :, None], (A_GROUPS, A_CHUNK, LANES))
    seg = np.arange(B_WIDTH) // B_HEAD_DIM
    bd = jnp.asarray((seg[:, None] == seg[None, :]).astype(np.float32)).astype(BF16)
    return dict(
        tri=tri, w_tril=w_tril, w_tril_t=jnp.swapaxes(w_tril, 1, 2), b_exp=b_exp, bd=bd,
        v_gain=rp["ev_v_norm"][e][None], qg=jnp.tile(rp["ev_q_norm"][e], B_HEADS)[None],
        kg=jnp.tile(rp["ev_k_norm"][e], B_HEADS)[None],
        bf=jnp.pad(rp["ev_b_fgate"][e], (0, LANES - B_HEADS))[None])


def _even_fwd(i, x, w, rp, h_in=None):
    e = i // 2
    c = _even_consts(e, rp)
    rows = x.shape[0]
    t = min(_ATT_T, rows)
    h, = h_in if h_in is not None else _rmsnorm_fwd(f"mix_norm_{i}", x, rp["norm_mix"][i][None], [BF16])
    zuv = _mm(f"in_uv_{i}", h, w["in_uv"])
    zqkv = _mm(f"in_qkv_{i}", h, w["in_qkv"], tn=768)
    zf = _mm(f"in_f_{i}", h, w["in_f"])
    ya = _gmlp_fwd(f"gmlp_{i}", zuv, c["v_gain"], c["w_tril"], c["b_exp"])
    q, k, v, ls = _qkv_prep_fwd(f"qkv_prep_{i}", zqkv, zf, c["qg"], c["kg"], c["bf"], c["bd"])
    csum = _cumsum_rows(f"forget_cumsum_{i}", ls)
    nck = -csum[:, :B_HEADS].T
    nck_rows = nck.reshape(B_HEADS, rows // t, t)
    o, lse = _flash_fwd(f"attn_{i}", q, k, v, nck_rows)
    x1 = _mm(f"out_a_{i}", ya, w["out_a"], res=x)
    x1, hn = _mm(f"out_b_{i}", o, w["out_b"], res=x1, norm_gain=rp["norm_ffn"][i][None])
    return x1, hn, dict(x=x, h=h, zuv=zuv, zqkv=zqkv, zf=zf, ya=ya, q=q, k=k, v=v, nck_rows=nck_rows, o=o,
                          lse_rows=lse.reshape(B_HEADS, rows // t, t))


def _even_bwd(i, dx1, sv, w, rp):
    e = i // 2
    c = _even_consts(e, rp)
    rows = dx1.shape[0]
    t = min(_ATT_T, rows)
    nb = rows // t
    g = {}
    dya = _mm(f"out_a_dx_{i}", dx1, w["out_a"], "nt")
    do = _mm(f"out_b_dx_{i}", dx1, w["out_b"], "nt")
    g["out_a"] = _mm(f"out_a_dw_{i}", sv["ya"], dx1, "tn")
    g["out_b"] = _mm(f"out_b_dw_{i}", sv["o"], dx1, "tn")
    dq, delta = _flash_bwd_dq(f"attn_dq_{i}", sv["q"], sv["k"], sv["v"], sv["nck_rows"], sv["o"], do,
                              sv["lse_rows"])
    dk, dv, dn = _flash_bwd_dkv(f"attn_dkv_{i}", sv["q"], sv["k"], sv["v"], sv["nck_rows"], do, sv["lse_rows"],
                                delta.reshape(B_HEADS, nb, t))
    dcs = jnp.pad(-dn.reshape(B_HEADS, rows).T, ((0, 0), (0, LANES - B_HEADS)))
    dls = _cumsum_rows(f"forget_cumsum_bwd_{i}", dcs, reverse=True)
    dzqkv, dzf, dqg, dkg, dbf = _qkv_prep_bwd(f"qkv_prep_bwd_{i}", sv["zqkv"], sv["zf"], dq, dk, dv, dls,
                                              c["qg"], c["kg"], c["bf"], c["bd"])
    dzuv, dws, dbs, dvg = _gmlp_bwd(f"gmlp_bwd_{i}", sv["zuv"], dya, c["v_gain"], c["w_tril"], c["w_tril_t"],
                                    c["b_exp"])
    g["in_uv"] = _mm(f"in_uv_dw_{i}", sv["h"], dzuv, "tn")
    g["in_qkv"] = _mm(f"in_qkv_dw_{i}", sv["h"], dzqkv, "tn", tn=768)
    g["in_f"] = _mm(f"in_f_dw_{i}", sv["h"], dzf, "tn")
    dh = _mm(f"in_uv_dx_{i}", dzuv, w["in_uv"], "nt")
    dh = _mm(f"in_qkv_dx_{i}", dzqkv, w["in_qkv"], "nt", res=dh, tk=768)
    dx, dgm = _mm_norm_bwd(f"in_f_dx_{i}", dzf, w["in_f"], "nt", sv["x"], dx1, rp["norm_mix"][i][None], res=dh)
    g["norm_mix"] = dgm[0]
    g["ev_b_fgate"] = dbf[0, :B_HEADS]
    g["ev_q_norm"] = dqg.reshape(B_HEADS, B_HEAD_DIM).sum(axis=0)
    g["ev_k_norm"] = dkg.reshape(B_HEADS, B_HEAD_DIM).sum(axis=0)
    g["ev_v_norm"] = dvg[0]
    g["ev_w_spatial"] = jnp.where(c["tri"][None], dws, 0.0)
    g["ev_b_spatial"] = dbs.sum(axis=-1)
    return dx, g


def _s5_consts(o, rp, r):
    prm = (rp["od_a_re"][o], rp["od_a_im"][o], rp["od_log_dt"][o], rp["od_b_re"][o], rp["od_b_im"][o])
    (ab_re, ab_im, bb_re, bb_im), vjp = jax.vjp(_s5_discretize, *prm)
    bcat = jnp.concatenate([_s5_block_diag(bb_re.transpose(0, 2, 1)), _s5_block_diag(bb_im.transpose(0, 2, 1))], axis=2)
    c_re, c_im = rp["od_c_re"][o], rp["od_c_im"][o]
    ccat = jnp.concatenate([_s5_block_diag(c_re.transpose(0, 2, 1)), _s5_block_diag(-c_im.transpose(0, 2, 1))], axis=1)
    pw = tuple(_s5_tables(ab_re, ab_im, r))
    reps = (r // _S5_CHUNK, 1)
    down = tuple(jnp.tile(t[:_S5_CHUNK], reps) for t in pw)
    up = tuple(jnp.tile(jnp.flip(t[:_S5_CHUNK], axis=0), reps) for t in pw)
    return dict(vjp=vjp, bcat=bcat.astype(BF16), ccat=ccat.astype(BF16), fwd_tabs=pw + down, bwd_tabs=pw + down + up)


def _odd_fwd(i, x, w, rp, h_in=None):
    o = i // 2
    rows = x.shape[0]
    c = _s5_consts(o, rp, min(_S5_ROWS, rows))
    hb, hf = h_in if h_in is not None else _rmsnorm_fwd(f"mix_norm_{i}", x, rp["norm_mix"][i][None], [BF16, F32])
    ys, xin = _s5_fwd(f"s5_{i}", hb, c["bcat"], c["ccat"], *c["fwd_tabs"])

    def skip_gelu(yv, hv, dv):
        y = yv + dv * hv
        return y, _gelu(y)

    y, ge = _rows(f"s5_skip_gelu_{i}", skip_gelu, [ys, hf], [w["od_d"]], [(D_MODEL, F32), (D_MODEL, BF16)])
    gl = _mm(f"glu_{i}", ge, w["glu"])

    def glu_out(xv, gv, nv):
        x1v = xv + gv[:, :D_MODEL] * _sigmoid(gv[:, D_MODEL:])
        return x1v, (x1v * _rstd(x1v)) * nv

    x1, hn = _rows(f"glu_out_{i}", glu_out, [x, gl], [rp["norm_ffn"][i][None]], [(D_MODEL, F32), (D_MODEL, BF16)])
    return x1, hn, dict(x=x, hb=hb, hf=hf, xin=xin, y=y, ge=ge, gl=gl, c=c)


def _odd_bwd(i, dx1, sv, w, rp):
    o = i // 2
    c = sv["c"]
    g = {}

    def glu_bwd(dv, gv):
        ga, gb = gv[:, :D_MODEL], gv[:, D_MODEL:]
        sg = _sigmoid(gb)
        return (jnp.concatenate([dv * sg, (dv * ga) * (sg * (1.0 - sg))], axis=1),)

    dgl, = _rows(f"glu_out_bwd_{i}", glu_bwd, [dx1, sv["gl"]], [], [(2 * D_MODEL, BF16)])
    g["glu"] = _mm(f"glu_dw_{i}", sv["ge"], dgl, "tn")
    dge = _mm(f"glu_dx_{i}", dgl, w["glu"], "nt")

    def gelu_bwd(dv, yv, hv):
        dy = dv * _gelu_grad(yv)
        return dy, jnp.sum(dy * hv, axis=0, keepdims=True)

    dy, dd = _rows(f"s5_skip_gelu_bwd_{i}", gelu_bwd, [dge, sv["y"], sv["hf"]], [], [(D_MODEL, F32)],
                   accs=[(1, D_MODEL)])
    g["od_d"] = dd[0]
    du, db, dc, dar, dai = _s5_bwd(f"s5_bwd_{i}", sv["hb"], dy, sv["xin"], c["bcat"], c["ccat"], *c["bwd_tabs"])
    dab_re = dar.sum(axis=1).reshape(S5_GROUPS, S5_STATE)
    dab_im = dai.sum(axis=1).reshape(S5_GROUPS, S5_STATE)
    dbb_re = _s5_block_diag_extract(db[:, :, :S5_LANES], S5_GROUP_CH, S5_STATE).transpose(0, 2, 1)
    dbb_im = _s5_block_diag_extract(db[:, :, S5_LANES:], S5_GROUP_CH, S5_STATE).transpose(0, 2, 1)
    g["od_a_re"], g["od_a_im"], g["od_log_dt"], g["od_b_re"], g["od_b_im"] = c["vjp"]((dab_re, dab_im, dbb_re, dbb_im))
    g["od_c_re"] = _s5_block_diag_extract(dc[:, :S5_LANES, :], S5_STATE, S5_GROUP_CH).transpose(0, 2, 1)
    g["od_c_im"] = -_s5_block_diag_extract(dc[:, S5_LANES:, :], S5_STATE, S5_GROUP_CH).transpose(0, 2, 1)

    def norm_bwd(xv, duv, dyv, drv, gv, dv):
        dh = duv + dv * dyv
        r = _rstd(xv)
        xh = xv * r
        dhg = dh * gv
        dx = drv + r * (dhg - xh * jnp.mean(dhg * xh, axis=-1, keepdims=True))
        return dx, jnp.sum(dh * xh, axis=0, keepdims=True)

    dx, dgm = _rows(f"mix_norm_bwd_{i}", norm_bwd, [sv["x"], du, dy, dx1], [rp["norm_mix"][i][None], w["od_d"]],
                    [(D_MODEL, F32)], accs=[(1, D_MODEL)])
    g["norm_mix"] = dgm[0]
    return dx, g


def _local_step(x, p, target, lw, rp):
    saved = []
    h_next = None
    for i in range(DEPTH):
        x, hn, s_mix = (_even_fwd if i % 2 == 0 else _odd_fwd)(i, x, lw[i], rp, h_next)
        nxt = None
        if i + 1 < DEPTH:
            nxt = (rp["norm_mix"][i + 1][None], [BF16, F32] if (i + 1) % 2 else [BF16])
        x, h_next, s_ch = _channel_fwd(i, x, p[i], lw[i], rp, hn, nxt)
        saved.append((s_mix, s_ch))

    def loss_fn(yv, tv):
        diff = yv - tv
        return diff * (1.0 / D_MODEL), jnp.sum(diff * diff, axis=0, keepdims=True)

    dx, sq = _rows("loss", loss_fn, [x, target], [], [(D_MODEL, F32)], accs=[(1, D_MODEL)])
    loss = 0.5 * jnp.sum(sq) / D_MODEL
    grads = [None] * DEPTH
    for i in reversed(range(DEPTH)):
        s_mix, s_ch = saved[i]
        dx, g_ch = _channel_bwd(i, dx, s_ch, lw[i], rp)
        dx, g_mix = (_even_bwd if i % 2 == 0 else _odd_bwd)(i, dx, s_mix, lw[i], rp)
        grads[i] = {**g_ch, **g_mix}
    return loss, dx, grads


WEIGHT_ORDER = ["norm_mix", "norm_ffn", "ev_w_in", "ev_b_fgate", "ev_q_norm", "ev_k_norm", "ev_v_norm", "ev_w_spatial",
                "ev_b_spatial", "ev_w_out", "od_a_re", "od_a_im", "od_log_dt", "od_b_re", "od_b_im", "od_c_re",
                "od_c_im", "od_d", "od_w_glu", "ffn_w_up", "ffn_conv_w", "ffn_conv_b", "ffn_w_down", "ple_w_proj",
                "ple_w_gate"]
SHARD_AXIS = {"ev_w_in": 2, "ev_w_out": 1, "od_d": 1, "od_w_glu": 2, "ffn_w_up": 2, "ffn_conv_w": 2, "ffn_w_down": 1,
              "ple_w_proj": 2, "ple_w_gate": 1}
BIG_WEIGHTS = [n for n in WEIGHT_ORDER if n in SHARD_AXIS]
SMALL_WEIGHTS = [n for n in WEIGHT_ORDER if n not in SHARD_AXIS]
KEPT_F32 = ("od_d", "ffn_conv_w")
IN_UV, IN_QKV_END, IN_COLS = 2 * A_WIDTH, 2 * A_WIDTH + 3 * B_WIDTH, 2 * A_WIDTH + 3 * B_WIDTH + B_HEADS


def _layer_weights(i, full, rp):
    w = {}
    up, cw, cb = full["ffn_w_up"][i], full["ffn_conv_w"][i], rp["ffn_conv_b"][i][None]
    w["up_g"], w["up_u"] = up[:, :D_FF], up[:, D_FF:]
    w["cw_g"], w["cw_u"] = cw[:, :D_FF], cw[:, D_FF:]
    w["cb_g"], w["cb_u"] = cb[:, :D_FF], cb[:, D_FF:]
    w["down"], w["ple_proj"], w["ple_gate"] = full["ffn_w_down"][i], full["ple_w_proj"][i], full["ple_w_gate"][i]
    if i % 2 == 0:
        win, wout = full["ev_w_in"][i // 2], full["ev_w_out"][i // 2]
        w["in_uv"], w["in_qkv"] = win[:, :IN_UV], win[:, IN_UV:IN_QKV_END]
        w["in_f"] = jnp.pad(win[:, IN_QKV_END:], ((0, 0), (0, LANES - B_HEADS)))
        w["out_a"], w["out_b"] = wout[:A_WIDTH], wout[A_WIDTH:]
    else:
        w["od_d"], w["glu"] = full["od_d"][i // 2][None], full["od_w_glu"][i // 2]
    return w


def _full_grads(grads):
    ev, od = [grads[i] for i in range(0, DEPTH, 2)], [grads[i] for i in range(1, DEPTH, 2)]
    out = {
        "norm_mix": jnp.stack([g["norm_mix"] for g in grads]), "norm_ffn": jnp.stack([g["norm_ffn"] for g in grads]),
        "ev_w_in": jnp.stack([jnp.concatenate([g["in_uv"], g["in_qkv"], g["in_f"][:, :B_HEADS]], axis=1) for g in ev]),
        "ev_w_out": jnp.stack([jnp.concatenate([g["out_a"], g["out_b"]], axis=0) for g in ev]),
        "od_w_glu": jnp.stack([g["glu"] for g in od]),
        "ffn_w_up": jnp.stack([jnp.concatenate([g["up_g"], g["up_u"]], axis=1) for g in grads]),
        "ffn_conv_w": jnp.stack([jnp.concatenate([g["cw_g"], g["cw_u"]], axis=1) for g in grads]),
        "ffn_conv_b": jnp.stack([g["conv_b"] for g in grads]),
        "ffn_w_down": jnp.stack([g["down"] for g in grads]),
        "ple_w_proj": jnp.stack([g["ple_proj"] for g in grads]),
        "ple_w_gate": jnp.stack([g["ple_gate"] for g in grads]),
    }
    for n in ("ev_b_fgate", "ev_q_norm", "ev_k_norm", "ev_v_norm", "ev_w_spatial", "ev_b_spatial"):
        out[n] = jnp.stack([g[n] for g in ev])
    for n in ("od_a_re", "od_a_im", "od_log_dt", "od_b_re", "od_b_im", "od_c_re", "od_c_im", "od_d"):
        out[n] = jnp.stack([g[n] for g in od])
    return out


def _pack(arrs, row_multiple):
    flat = jnp.concatenate([a.reshape(-1) for a in arrs])
    rows = -(-flat.shape[0] // (PACK_W * row_multiple)) * row_multiple
    return jnp.pad(flat, (0, rows * PACK_W - flat.shape[0])).reshape(rows, PACK_W)


def _unpack(buf, shapes):
    flat = buf.reshape(-1)
    out, at = [], 0
    for s in shapes:
        n = int(np.prod(s))
        out.append(flat[at:at + n].reshape(s))
        at += n
    return out


def _shard(name, a, k):
    ax = SHARD_AXIS[name]
    n = a.shape[ax] // N_CHIPS
    return lax.slice_in_dim(a, k * n, (k + 1) * n, axis=ax)


_ANY = pl.BlockSpec(memory_space=pl.ANY)


def _mesh_pos():
    return lax.axis_index("x"), lax.axis_index("y"), lax.axis_index("c")


def _other_chips(x, y):
    return [(1 - x, y), (x, 1 - y), (1 - x, 1 - y)]


def _gather_shards(name, shards):
    n = len(shards)

    def body(*refs):
        ins, outs = refs[:n], refs[n:2 * n]
        send_sems, recv_sems, local_sems = refs[2 * n:]
        x, y, c = _mesh_pos()
        sibling = (x, y, 1 - c)
        chips = _other_chips(x, y)

        def part(a, k, hc):
            half = shards[a].shape[0] // 2
            return outs[a].at[k, pl.ds(hc * half, half), :]

        def copy(sem, src, dst, to):
            return pltpu.make_async_remote_copy(src_ref=src, dst_ref=dst, send_sem=send_sems.at[sem],
                                                recv_sem=recv_sems.at[sem], device_id=to, device_id_type=MESH)

        local, sent, passed = [], [], []
        for a in range(n):
            half = shards[a].shape[0] // 2
            local.append(pltpu.make_async_copy(ins[a], outs[a].at[2 * x + y], local_sems.at[a]))
            local[-1].start()
            for j, (cx, cy) in enumerate(chips):
                sent.append(copy(6 * a + j, ins[a].at[pl.ds(c * half, half), :], part(a, 2 * x + y, c), (cx, cy, c)))
                sent[-1].start()
        for a in range(n):
            for j, (cx, cy) in enumerate(chips):
                blk = part(a, 2 * cx + cy, c)
                copy(6 * a + j, blk, blk, (cx, cy, c)).wait_recv()
                passed.append(copy(6 * a + 3 + j, blk, blk, sibling))
                passed[-1].start()
        for a in range(n):
            for j, (cx, cy) in enumerate(chips):
                blk = part(a, 2 * cx + cy, 1 - c)
                copy(6 * a + 3 + j, blk, blk, sibling).wait_recv()
        for cp in sent + passed:
            cp.wait_send()
        for cp in local:
            cp.wait()

    return pl.pallas_call(
        body, out_shape=[jax.ShapeDtypeStruct((N_CHIPS,) + s.shape, s.dtype) for s in shards],
        in_specs=[_ANY] * n, out_specs=[_ANY] * n,
        scratch_shapes=[pltpu.SemaphoreType.DMA((6 * n,)), pltpu.SemaphoreType.DMA((6 * n,)),
                        pltpu.SemaphoreType.DMA((n,))],
        name=name,
    )(*shards)


def _swap_halves(name, arrs):
    n = len(arrs)

    def body(*refs):
        ins, outs = refs[:n], refs[n:2 * n]
        send_sems, recv_sems = refs[2 * n:]
        x, y, c = _mesh_pos()
        cps = []
        for a in range(n):
            half = arrs[a].shape[1] // 2
            cps.append(pltpu.make_async_remote_copy(
                src_ref=ins[a].at[:, pl.ds((1 - c) * half, half), :], dst_ref=outs[a], send_sem=send_sems.at[a],
                recv_sem=recv_sems.at[a], device_id=(x, y, 1 - c), device_id_type=MESH))
            cps[-1].start()
        for cp in cps:
            cp.wait()

    return pl.pallas_call(
        body, out_shape=[jax.ShapeDtypeStruct((a.shape[0], a.shape[1] // 2, a.shape[2]), a.dtype) for a in arrs],
        in_specs=[_ANY] * n, out_specs=[_ANY] * n,
        scratch_shapes=[pltpu.SemaphoreType.DMA((n,)), pltpu.SemaphoreType.DMA((n,))], name=name,
    )(*arrs)


def _send_to_owner_chips(name, arrs):
    n = len(arrs)

    def body(*refs):
        ins, outs = refs[:n], refs[n:2 * n]
        send_sems, recv_sems = refs[2 * n:]
        x, y, c = _mesh_pos()
        cps = []
        for a in range(n):
            for j, (cx, cy) in enumerate(_other_chips(x, y)):
                cps.append(pltpu.make_async_remote_copy(
                    src_ref=ins[a].at[2 * cx + cy], dst_ref=outs[a].at[j], send_sem=send_sems.at[3 * a + j],
                    recv_sem=recv_sems.at[3 * a + j], device_id=(cx, cy, c), device_id_type=MESH))
                cps[-1].start()
        for cp in cps:
            cp.wait()

    return pl.pallas_call(
        body, out_shape=[jax.ShapeDtypeStruct((3,) + a.shape[1:], a.dtype) for a in arrs],
        in_specs=[_ANY] * n, out_specs=[_ANY] * n,
        scratch_shapes=[pltpu.SemaphoreType.DMA((3 * n,)), pltpu.SemaphoreType.DMA((3 * n,))], name=name,
    )(*arrs)


def _swap_with_sibling(name, arrs):
    n = len(arrs)

    def body(*refs):
        ins, outs = refs[:n], refs[n:2 * n]
        send_sems, recv_sems = refs[2 * n:]
        x, y, c = _mesh_pos()
        cps = []
        for a in range(n):
            cps.append(pltpu.make_async_remote_copy(
                src_ref=ins[a], dst_ref=outs[a], send_sem=send_sems.at[a], recv_sem=recv_sems.at[a],
                device_id=(x, y, 1 - c), device_id_type=MESH))
            cps[-1].start()
        for cp in cps:
            cp.wait()

    return pl.pallas_call(
        body, out_shape=[jax.ShapeDtypeStruct(a.shape, a.dtype) for a in arrs],
        in_specs=[_ANY] * n, out_specs=[_ANY] * n,
        scratch_shapes=[pltpu.SemaphoreType.DMA((n,)), pltpu.SemaphoreType.DMA((n,))], name=name,
    )(*arrs)


def _all_gather_devices(name, a):
    rows, w = a.shape

    def body(a_ref, out_ref, send_sems, recv_sems, local_sem):
        x, y, c = _mesh_pos()
        me, sibling = (x, y, c), (x, y, 1 - c)
        chips = _other_chips(x, y)

        def slot(px, py, pc):
            return out_ref.at[4 * px + 2 * py + pc]

        def copy(sem, block, to, src=None):
            return pltpu.make_async_remote_copy(src_ref=slot(*block) if src is None else src, dst_ref=slot(*block),
                                                send_sem=send_sems.at[sem], recv_sem=recv_sems.at[sem], device_id=to,
                                                device_id_type=MESH)

        mine = pltpu.make_async_copy(a_ref, slot(*me), local_sem)
        mine.start()
        first = [copy(0, me, sibling, src=a_ref)]
        first += [copy(1 + j, me, (*chip, c), src=a_ref) for j, chip in enumerate(chips)]
        for cp in first:
            cp.start()
        passed = [copy(4 + j, (*chip, c), sibling) for j, chip in enumerate(chips)]
        for j, chip in enumerate(chips):
            copy(1 + j, (*chip, c), me).wait_recv()
            passed[j].start()
        copy(0, sibling, me).wait_recv()
        for j, chip in enumerate(chips):
            copy(4 + j, (*chip, 1 - c), me).wait_recv()
        for cp in first + passed:
            cp.wait_send()
        mine.wait()

    return pl.pallas_call(
        body, out_shape=jax.ShapeDtypeStruct((8, rows, w), a.dtype), in_specs=[_ANY], out_specs=_ANY,
        scratch_shapes=[pltpu.SemaphoreType.DMA((7,)), pltpu.SemaphoreType.DMA((7,)), pltpu.SemaphoreType.DMA],
        name=name,
    )(a)


_PACK_TILE = 256


def _sum_rows(name, arrs):
    def fn(*vals):
        tot = vals[0]
        for v in vals[1:]:
            tot = tot + v
        return (tot,)

    return _rows(name, fn, list(arrs), [], [(arrs[0].shape[1], F32)], tile=_PACK_TILE)[0]


def _adam_math(wv, gv, mv, vv):
    m2 = ADAM_B1 * mv + (1.0 - ADAM_B1) * gv
    v2 = ADAM_B2 * vv + (1.0 - ADAM_B2) * (gv * gv)
    m_hat = m2 / (1.0 - ADAM_B1 ** ADAM_STEP)
    v_hat = v2 / (1.0 - ADAM_B2 ** ADAM_STEP)
    delta = -ADAM_LR * (m_hat / (jnp.sqrt(v_hat) + ADAM_EPS) + ADAM_WD * wv)
    return delta, m2, v2


def _adamw(name, w, g, m, v):
    return _rows(name, _adam_math, [w, g, m, v], [], [(w.shape[1], F32)] * 3, tile=_PACK_TILE)


_INPUT_ORDER = (["x", "p"] + WEIGHT_ORDER + ["loss_target"] + ["m_" + n for n in WEIGHT_ORDER]
                + ["v_" + n for n in WEIGHT_ORDER])


_SUM_ROWS = 128


def _pair_sum(name, g, got, core):
    nk, rows, w = g.shape
    half = rows // 2
    nt = half // _SUM_ROWS

    def body(c_ref, g_ref, got_ref, o_ref, ob_ref):
        tot = g_ref[...] + got_ref[...]
        o_ref[...] = tot
        ob_ref[...] = tot.astype(BF16)

    spec = pl.BlockSpec((None, _SUM_ROWS, w), lambda k, i, c: (k, i, 0))
    grid_spec = pltpu.PrefetchScalarGridSpec(
        num_scalar_prefetch=1, grid=(nk, nt),
        in_specs=[pl.BlockSpec((None, _SUM_ROWS, w), lambda k, i, c: (k, c[0] * nt + i, 0)), spec],
        out_specs=[spec, spec])
    return pl.pallas_call(
        body, grid_spec=grid_spec, name=name, compiler_params=_cparams(("parallel", "parallel")),
        out_shape=[jax.ShapeDtypeStruct((nk, half, w), F32), jax.ShapeDtypeStruct((nk, half, w), BF16)])(core, g, got)


def _owner_sum(name, pair, owed, chip):
    _, half, w = pair.shape

    def body(k_ref, p_ref, a_ref, b_ref, c_ref, o_ref):
        o_ref[...] = ((p_ref[...] + a_ref[...].astype(F32)) + b_ref[...].astype(F32)) + c_ref[...].astype(F32)

    def owed_spec(j):
        return pl.BlockSpec((None, _SUM_ROWS, w), lambda i, k: (j, i, 0))

    grid_spec = pltpu.PrefetchScalarGridSpec(
        num_scalar_prefetch=1, grid=(half // _SUM_ROWS,),
        in_specs=[pl.BlockSpec((None, _SUM_ROWS, w), lambda i, k: (k[0], i, 0)), owed_spec(0), owed_spec(1),
                  owed_spec(2)],
        out_specs=pl.BlockSpec((_SUM_ROWS, w), lambda i, k: (i, 0)))
    return pl.pallas_call(body, grid_spec=grid_spec, out_shape=jax.ShapeDtypeStruct((half, w), F32), name=name,
                          compiler_params=_cparams(("parallel",)))(chip, pair, owed, owed, owed)


def _adamw_halves(name, w, mine, other, m, v, core):
    rows, wd = w.shape
    nh = (rows // 2) // _SUM_ROWS

    def body(c_ref, w_ref, a_ref, b_ref, m_ref, v_ref, g_ref, d_ref, m2_ref, v2_ref):
        own = (pl.program_id(0) // nh) == c_ref[0]
        g = jnp.where(own, a_ref[...], b_ref[...])
        g_ref[...] = g
        d_ref[...], m2_ref[...], v2_ref[...] = _adam_math(w_ref[...], g, m_ref[...], v_ref[...])

    full = pl.BlockSpec((_SUM_ROWS, wd), lambda i, c: (i, 0))
    part = pl.BlockSpec((_SUM_ROWS, wd), lambda i, c: (lax.rem(i, nh), 0))
    grid_spec = pltpu.PrefetchScalarGridSpec(num_scalar_prefetch=1, grid=(2 * nh,),
                                             in_specs=[full, part, part, full, full], out_specs=[full] * 4)
    return pl.pallas_call(body, grid_spec=grid_spec, out_shape=[jax.ShapeDtypeStruct((rows, wd), F32)] * 4, name=name,
                          compiler_params=_cparams(("parallel",)))(core, w, mine, other, m, v)


MATRIX_WEIGHTS = [n for n in BIG_WEIGHTS if n not in KEPT_F32]
TINY_SHARDED = [n for n in BIG_WEIGHTS if n in KEPT_F32]


def _as_rows(a):
    return a.reshape(-1, a.shape[-1])


def _owner_major(grads):
    ev, od = [grads[i] for i in range(0, DEPTH, 2)], [grads[i] for i in range(1, DEPTH, 2)]

    def cols(m, k, n):
        w = m.shape[1] // n
        return m[:, k * w:(k + 1) * w]

    def rows(m, k, n):
        r = m.shape[0] // n
        return m[k * r:(k + 1) * r]

    w_in = [jnp.concatenate([g["in_uv"], g["in_qkv"], g["in_f"][:, :B_HEADS]], axis=1) for g in ev]
    per_chip = {
        "ev_w_in": lambda k: [cols(m, k, N_CHIPS) for m in w_in],
        "ev_w_out": lambda k: [rows(g["out_a"] if k < 2 else g["out_b"], k % 2, 2) for g in ev],
        "od_w_glu": lambda k: [cols(g["glu"], k, N_CHIPS) for g in od],
        "ffn_w_up": lambda k: [cols(g["up_g"] if k < 2 else g["up_u"], k % 2, 2) for g in grads],
        "ffn_w_down": lambda k: [rows(g["down"], k, N_CHIPS) for g in grads],
        "ple_w_proj": lambda k: [cols(g["ple_proj"], k, N_CHIPS) for g in grads],
        "ple_w_gate": lambda k: [rows(g["ple_gate"], k, N_CHIPS) for g in grads],
    }
    return {n: jnp.stack([jnp.concatenate(per_chip[n](k), axis=0) for k in range(N_CHIPS)]) for n in MATRIX_WEIGHTS}


def _small_grads(grads):
    ev, od = [grads[i] for i in range(0, DEPTH, 2)], [grads[i] for i in range(1, DEPTH, 2)]
    out = {"norm_mix": jnp.stack([g["norm_mix"] for g in grads]), "norm_ffn": jnp.stack([g["norm_ffn"] for g in grads]),
           "ffn_conv_w": jnp.stack([jnp.concatenate([g["cw_g"], g["cw_u"]], axis=1) for g in grads]),
           "ffn_conv_b": jnp.stack([g["conv_b"] for g in grads])}
    for n in ("ev_b_fgate", "ev_q_norm", "ev_k_norm", "ev_v_norm", "ev_w_spatial", "ev_b_spatial"):
        out[n] = jnp.stack([g[n] for g in ev])
    for n in ("od_a_re", "od_a_im", "od_log_dt", "od_b_re", "od_b_im", "od_c_re", "od_c_im", "od_d"):
        out[n] = jnp.stack([g[n] for g in od])
    return out


def _step(a):
    xi, yi, ci = _mesh_pos()
    chip = 2 * xi + yi
    core_arr, chip_arr = ci.astype(jnp.int32).reshape(1), chip.astype(jnp.int32).reshape(1)
    rp = {n: a[n] for n in SMALL_WEIGHTS}

    tiny = _pack([a[n] for n in TINY_SHARDED], 32)
    gathered = _gather_shards("gather_weights", [_as_rows(a[n]).astype(BF16) for n in MATRIX_WEIGHTS] + [tiny])
    full = {}
    for n, g in zip(MATRIX_WEIGHTS, gathered):
        full[n] = jnp.concatenate([g[k].reshape(a[n].shape) for k in range(N_CHIPS)], axis=SHARD_AXIS[n])
    tiny_parts = [_unpack(gathered[-1][k], [a[n].shape for n in TINY_SHARDED]) for k in range(N_CHIPS)]
    for idx, n in enumerate(TINY_SHARDED):
        full[n] = jnp.concatenate([tiny_parts[k][idx] for k in range(N_CHIPS)], axis=SHARD_AXIS[n])
    lw = [_layer_weights(i, full, rp) for i in range(DEPTH)]

    loss_local, grad_x, grads = _local_step(a["x"][0], a["p"][:, 0], a["loss_target"][0], lw, rp)
    loss = lax.psum(loss_local, ("x", "y", "c"))

    contrib = _owner_major(grads)
    mats = [contrib[n] for n in MATRIX_WEIGHTS]
    got = _swap_halves("grad_pair_swap", mats)
    pair = [_pair_sum(f"grad_pair_sum_{n}", g, h, core_arr) for n, g, h in zip(MATRIX_WEIGHTS, mats, got)]
    owed = _send_to_owner_chips("grad_to_owner", [pb for _, pb in pair])
    mine = [_owner_sum(f"grad_owner_sum_{n}", p, o, chip_arr) for n, (p, _), o in zip(MATRIX_WEIGHTS, pair, owed)]
    theirs = _swap_with_sibling("grad_half_swap", mine)

    small_names = SMALL_WEIGHTS + TINY_SHARDED
    sg = _small_grads(grads)
    everyone = _all_gather_devices("small_grad_gather", _pack([sg[n] for n in small_names], _PACK_TILE))
    g_small = _sum_rows("small_grad_sum", [everyone[d] for d in range(8)])
    small_full = dict(zip(small_names, _unpack(g_small, [sg[n].shape for n in small_names])))

    out = {}
    for n, own_half, other_half in zip(MATRIX_WEIGHTS, mine, theirs):
        shape = a[n].shape
        g2d, delta, m2, v2 = _adamw_halves(f"adamw_{n}", _as_rows(a[n]), own_half, other_half,
                                           _as_rows(a["m_" + n]), _as_rows(a["v_" + n]), core_arr)
        for kind, val in (("grad", g2d), ("delta", delta), ("new_m", m2), ("new_v", v2)):
            out[kind + "_" + n] = val.reshape(shape)
    g_sm = {n: small_full[n] for n in SMALL_WEIGHTS}
    for n in TINY_SHARDED:
        width = a[n].shape[SHARD_AXIS[n]]
        g_sm[n] = lax.dynamic_slice_in_dim(small_full[n], chip * width, width, axis=SHARD_AXIS[n])
    shapes = [a[n].shape for n in small_names]
    w, m, v = (_pack([a[pre + n] for n in small_names], _PACK_TILE) for pre in ("", "m_", "v_"))
    g = _pack([g_sm[n] for n in small_names], _PACK_TILE)
    delta, m2, v2 = _adamw("adamw_small", w, g, m, v)
    for kind, buf in (("delta", delta), ("new_m", m2), ("new_v", v2)):
        for n, val in zip(small_names, _unpack(buf, shapes)):
            out[kind + "_" + n] = val
    for n in small_names:
        out["grad_" + n] = g_sm[n]
    res = [loss, grad_x[None]]
    for kind in ("grad", "delta", "new_m", "new_v"):
        res += [out[kind + "_" + n] for n in WEIGHT_ORDER]
    return tuple(res)


def kernel(x, p, norm_mix, norm_ffn, ev_w_in, ev_b_fgate, ev_q_norm, ev_k_norm, ev_v_norm, ev_w_spatial, ev_b_spatial, ev_w_out, od_a_re, od_a_im, od_log_dt, od_b_re, od_b_im, od_c_re, od_c_im, od_d, od_w_glu, ffn_w_up, ffn_conv_w, ffn_conv_b, ffn_w_down, ple_w_proj, ple_w_gate, loss_target, m_norm_mix, m_norm_ffn, m_ev_w_in, m_ev_b_fgate, m_ev_q_norm, m_ev_k_norm, m_ev_v_norm, m_ev_w_spatial, m_ev_b_spatial, m_ev_w_out, m_od_a_re, m_od_a_im, m_od_log_dt, m_od_b_re, m_od_b_im, m_od_c_re, m_od_c_im, m_od_d, m_od_w_glu, m_ffn_w_up, m_ffn_conv_w, m_ffn_conv_b, m_ffn_w_down, m_ple_w_proj, m_ple_w_gate, v_norm_mix, v_norm_ffn, v_ev_w_in, v_ev_b_fgate, v_ev_q_norm, v_ev_k_norm, v_ev_v_norm, v_ev_w_spatial, v_ev_b_spatial, v_ev_w_out, v_od_a_re, v_od_a_im, v_od_log_dt, v_od_b_re, v_od_b_im, v_od_c_re, v_od_c_im, v_od_d, v_od_w_glu, v_ffn_w_up, v_ffn_conv_w, v_ffn_conv_b, v_ffn_w_down, v_ple_w_proj, v_ple_w_gate):
    args = (x, p, norm_mix, norm_ffn, ev_w_in, ev_b_fgate, ev_q_norm, ev_k_norm, ev_v_norm, ev_w_spatial, ev_b_spatial, ev_w_out, od_a_re, od_a_im, od_log_dt, od_b_re, od_b_im, od_c_re, od_c_im, od_d, od_w_glu, ffn_w_up, ffn_conv_w, ffn_conv_b, ffn_w_down, ple_w_proj, ple_w_gate, loss_target, m_norm_mix, m_norm_ffn, m_ev_w_in, m_ev_b_fgate, m_ev_q_norm, m_ev_k_norm, m_ev_v_norm, m_ev_w_spatial, m_ev_b_spatial, m_ev_w_out, m_od_a_re, m_od_a_im, m_od_log_dt, m_od_b_re, m_od_b_im, m_od_c_re, m_od_c_im, m_od_d, m_od_w_glu, m_ffn_w_up, m_ffn_conv_w, m_ffn_conv_b, m_ffn_w_down, m_ple_w_proj, m_ple_w_gate, v_norm_mix, v_norm_ffn, v_ev_w_in, v_ev_b_fgate, v_ev_q_norm, v_ev_k_norm, v_ev_v_norm, v_ev_w_spatial, v_ev_b_spatial, v_ev_w_out, v_od_a_re, v_od_a_im, v_od_log_dt, v_od_b_re, v_od_b_im, v_od_c_re, v_od_c_im, v_od_d, v_od_w_glu, v_ffn_w_up, v_ffn_conv_w, v_ffn_conv_b, v_ffn_w_down, v_ple_w_proj, v_ple_w_gate)
    return _step(dict(zip(_INPUT_ORDER, args)))
```

```python
import functools
import math

import jax
import jax.numpy as jnp
import numpy as np
from jax import lax
from jax.experimental import pallas as pl
from jax.experimental.pallas import tpu as pltpu

F32 = jnp.float32
BF16 = jnp.bfloat16
MESH = pl.DeviceIdType.MESH

V7X_VMEM_LIMIT_BYTES = 56 * 1024 * 1024
LANES = 128

D_MODEL = 1024
DEPTH = 4
A_GROUPS = 4
A_CHUNK = 128
A_WIDTH = 512
B_HEADS = 8
B_HEAD_DIM = 64
B_WIDTH = 512
S5_GROUP_CH = 16
S5_GROUPS = 64
S5_STATE = 64
S5_BLOCKS = 8
S5_LANES = 512
D_FF = 2816
PLE_DIM = 256
EPS = 1e-6
NEG_INF = -1e30

ADAM_LR = 0.001
ADAM_B1 = 0.9
ADAM_B2 = 0.999
ADAM_EPS = 1e-08
ADAM_WD = 0.01
ADAM_STEP = 10

N_CHIPS = 4
PACK_W = 1024


def _cparams(sem):
    return pltpu.CompilerParams(dimension_semantics=sem, vmem_limit_bytes=V7X_VMEM_LIMIT_BYTES)


def _pick(n, target):
    if n <= target:
        return n
    t = (target // LANES) * LANES
    while t >= LANES:
        if n % t == 0:
            return t
        t -= LANES
    return n


_GELU_K = 0.7978845608028654
_GELU_C = 0.044715


def _gelu(x):
    return x * (0.5 * (1.0 + jnp.tanh(_GELU_K * (x + _GELU_C * (x * x * x)))))


def _gelu_grad(x):
    x2 = x * x
    t = jnp.tanh(_GELU_K * (x + _GELU_C * (x * x2)))
    return 0.5 * (1.0 + t) + (0.5 * x) * (1.0 - t * t) * (_GELU_K * (1.0 + (3.0 * _GELU_C) * x2))


def _sigmoid(x):
    return 0.5 * jnp.tanh(0.5 * x) + 0.5


def _log_sigmoid(x):
    return -(jnp.maximum(-x, 0.0) + jnp.log(1.0 + jnp.exp(-jnp.abs(x))))


def _rstd(x):
    return lax.rsqrt(jnp.mean(x * x, axis=-1, keepdims=True) + EPS)


def _rows(name, fn, row_ins, full_ins, outs, accs=(), tile=256):
    rows = row_ins[0].shape[0]
    r = min(tile, rows)
    n = rows // r
    n_in = len(row_ins) + len(full_ins)
    n_out = len(outs)

    def body(*refs):
        res = fn(*[ref[...] for ref in refs[:n_in]])
        for ref, v in zip(refs[n_in:n_in + n_out], res[:n_out]):
            ref[...] = v.astype(ref.dtype)
        acc_refs = refs[n_in + n_out:]
        if acc_refs:
            @pl.when(pl.program_id(0) == 0)
            def _():
                for ref in acc_refs:
                    ref[...] = jnp.zeros(ref.shape, ref.dtype)

            for ref, v in zip(acc_refs, res[n_out:]):
                ref[...] += v

    in_specs = [pl.BlockSpec((r, a.shape[1]), lambda i: (i, 0)) for a in row_ins]
    in_specs += [pl.BlockSpec(a.shape, lambda i, nd=a.ndim: (0,) * nd) for a in full_ins]
    out_shape = [jax.ShapeDtypeStruct((rows, w), dt) for (w, dt) in outs]
    out_shape += [jax.ShapeDtypeStruct(s, F32) for s in accs]
    out_specs = [pl.BlockSpec((r, w), lambda i: (i, 0)) for (w, dt) in outs]
    out_specs += [pl.BlockSpec(s, lambda i, nd=len(s): (0,) * nd) for s in accs]
    return pl.pallas_call(
        body, grid=(n,), in_specs=in_specs, out_specs=out_specs, out_shape=out_shape, name=name,
        compiler_params=_cparams(("arbitrary",) if accs else ("parallel",)),
    )(*row_ins, *full_ins)


_DOT_DIMS = {"nn": (((1,), (0,)), ((), ())), "nt": (((1,), (1,)), ((), ())), "tn": (((0,), (0,)), ((), ()))}


def _mm(name, a, b, mode="nn", out_dtype=F32, res=None, tm=1024, tn=1024, tk=1024, norm_gain=None):
    if mode == "nn":
        (m, k), (k2, n) = a.shape, b.shape
    elif mode == "nt":
        (m, k), (n, k2) = a.shape, b.shape
    else:
        (k, m), (k2, n) = a.shape, b.shape
    assert k == k2, (name, a.shape, b.shape, mode)
    tm, tn, tk = _pick(m, tm), _pick(n, tn), _pick(k, tk)
    nk = k // tk
    dims = _DOT_DIMS[mode]
    has_res = res is not None
    has_norm = norm_gain is not None
    assert not has_norm or tn == n, (name, tn, n)
    n_in = 2 + has_res + has_norm

    def body(*refs):
        a_ref, b_ref = refs[0], refs[1]
        res_ref = refs[2] if has_res else None
        gain_ref = refs[n_in - 1] if has_norm else None
        o_ref = refs[n_in]
        h_ref = refs[n_in + 1] if has_norm else None

        def finish(tot):
            if has_res:
                tot = res_ref[...] + tot
            o_ref[...] = tot.astype(o_ref.dtype)
            if has_norm:
                h_ref[...] = ((tot * _rstd(tot)) * gain_ref[...]).astype(h_ref.dtype)

        prod = lax.dot_general(a_ref[...].astype(BF16), b_ref[...].astype(BF16), dims, preferred_element_type=F32)
        if nk == 1:
            finish(prod)
            return
        acc = refs[-1]
        kk = pl.program_id(2)

        @pl.when(kk == 0)
        def _():
            acc[...] = prod

        @pl.when(kk > 0)
        def _():
            acc[...] += prod

        @pl.when(kk == nk - 1)
        def _():
            finish(acc[...])

    if mode == "nn":
        a_spec = pl.BlockSpec((tm, tk), lambda i, j, kk: (i, kk))
        b_spec = pl.BlockSpec((tk, tn), lambda i, j, kk: (kk, j))
    elif mode == "nt":
        a_spec = pl.BlockSpec((tm, tk), lambda i, j, kk: (i, kk))
        b_spec = pl.BlockSpec((tn, tk), lambda i, j, kk: (j, kk))
    else:
        a_spec = pl.BlockSpec((tk, tm), lambda i, j, kk: (kk, i))
        b_spec = pl.BlockSpec((tk, tn), lambda i, j, kk: (kk, j))
    o_spec = pl.BlockSpec((tm, tn), lambda i, j, kk: (i, j))
    in_specs = [a_spec, b_spec] + ([o_spec] if has_res else [])
    in_specs += [pl.BlockSpec((1, tn), lambda i, j, kk: (0, j))] if has_norm else []
    args = (a, b) + ((res,) if has_res else ()) + ((norm_gain,) if has_norm else ())
    out_shape = jax.ShapeDtypeStruct((m, n), out_dtype)
    return pl.pallas_call(
        body, grid=(m // tm, n // tn, nk), in_specs=in_specs, out_specs=[o_spec, o_spec] if has_norm else o_spec,
        out_shape=[out_shape, jax.ShapeDtypeStruct((m, n), BF16)] if has_norm else out_shape, name=name,
        scratch_shapes=[pltpu.VMEM((tm, tn), F32)] if nk > 1 else [],
        compiler_params=_cparams(("parallel", "parallel", "arbitrary")),
    )(*args)


def _mm_norm_bwd(name, a, b, mode, x, dres, gain, res=None, tm=512, tk=1024):
    if mode == "nn":
        (m, k), (k2, n) = a.shape, b.shape
    else:
        (m, k), (n, k2) = a.shape, b.shape
    assert k == k2, (name, a.shape, b.shape, mode)
    tm, tk = _pick(m, tm), _pick(k, tk)
    nk = k // tk
    dims = _DOT_DIMS[mode]
    has_res = res is not None
    n_in = 5 + has_res

    def body(*refs):
        a_ref, b_ref = refs[0], refs[1]
        res_ref = refs[2] if has_res else None
        x_ref, dres_ref, gain_ref = refs[n_in - 3:n_in]
        o_ref, dg_ref = refs[n_in], refs[n_in + 1]

        def finish(d):
            if has_res:
                d = res_ref[...] + d
            xv = x_ref[...]
            r = _rstd(xv)
            xh = xv * r
            dyg = d * gain_ref[...]
            o_ref[...] = dres_ref[...] + r * (dyg - xh * jnp.mean(dyg * xh, axis=-1, keepdims=True))
            part = jnp.sum(d * xh, axis=0, keepdims=True)

            @pl.when(pl.program_id(0) == 0)
            def _():
                dg_ref[...] = part

            @pl.when(pl.program_id(0) > 0)
            def _():
                dg_ref[...] += part

        prod = lax.dot_general(a_ref[...].astype(BF16), b_ref[...].astype(BF16), dims, preferred_element_type=F32)
        if nk == 1:
            finish(prod)
            return
        acc = refs[-1]
        kk = pl.program_id(1)

        @pl.when(kk == 0)
        def _():
            acc[...] = prod

        @pl.when(kk > 0)
        def _():
            acc[...] += prod

        @pl.when(kk == nk - 1)
        def _():
            finish(acc[...])

    a_spec = pl.BlockSpec((tm, tk), lambda i, kk: (i, kk))
    b_spec = pl.BlockSpec((tk, n), lambda i, kk: (kk, 0)) if mode == "nn" else pl.BlockSpec((n, tk), lambda i, kk: (0, kk))
    row_spec = pl.BlockSpec((tm, n), lambda i, kk: (i, 0))
    vec_spec = pl.BlockSpec((1, n), lambda i, kk: (0, 0))
    in_specs = [a_spec, b_spec] + ([row_spec] if has_res else []) + [row_spec, row_spec, vec_spec]
    args = (a, b) + ((res,) if has_res else ()) + (x, dres, gain)
    return pl.pallas_call(
        body, grid=(m // tm, nk), in_specs=in_specs, out_specs=[row_spec, vec_spec],
        out_shape=[jax.ShapeDtypeStruct((m, n), F32), jax.ShapeDtypeStruct((1, n), F32)], name=name,
        scratch_shapes=[pltpu.VMEM((tm, n), F32)] if nk > 1 else [],
        compiler_params=_cparams(("arbitrary", "arbitrary")),
    )(*args)


def _rmsnorm_fwd(name, x, g, outs):
    def fn(xv, gv):
        y = (xv * _rstd(xv)) * gv
        return tuple(y for _ in outs)

    return _rows(name, fn, [x], [g], [(x.shape[1], dt) for dt in outs])


_CONV_ROWS = 256
_CONV_COLS = 1408


def _conv_taps(h_ref, halo_ref, first):
    h = h_ref[...]
    rows = h.shape[0]
    row = lax.broadcasted_iota(jnp.int32, (rows, 1), 0)
    keep = jnp.where(first, 0.0, 1.0)
    m1 = halo_ref[7:8, :] * keep
    m2 = halo_ref[6:7, :] * keep
    p1 = jnp.where(row == 0, m1, pltpu.roll(h, 1, 0))
    p2 = jnp.where(row == 0, m2, jnp.where(row == 1, m1, pltpu.roll(h, 2, 0)))
    return h, p1, p2


def _conv_specs(rows, r, cw):
    tile = pl.BlockSpec((r, cw), lambda j, i: (i, j))
    halo = pl.BlockSpec((8, cw), lambda j, i: (jnp.maximum(i * (r // 8) - 1, 0), j))
    vec3 = pl.BlockSpec((3, cw), lambda j, i: (0, j))
    vec1 = pl.BlockSpec((1, cw), lambda j, i: (0, j))
    return tile, halo, vec3, vec1


def _convffn_fwd(name, hg, hu, wg, wu, bg, bu):
    rows, f = hg.shape
    r, cw = min(_CONV_ROWS, rows), _pick(f, _CONV_COLS)

    def body(hg_ref, hgh_ref, hu_ref, huh_ref, wg_ref, wu_ref, bg_ref, bu_ref, o_ref):
        first = pl.program_id(1) == 0
        h, p1, p2 = _conv_taps(hg_ref, hgh_ref, first)
        g = bg_ref[...] + wg_ref[0:1, :] * p2 + wg_ref[1:2, :] * p1 + wg_ref[2:3, :] * h
        h, p1, p2 = _conv_taps(hu_ref, huh_ref, first)
        u = bu_ref[...] + wu_ref[0:1, :] * p2 + wu_ref[1:2, :] * p1 + wu_ref[2:3, :] * h
        o_ref[...] = ((g * _sigmoid(g)) * u).astype(o_ref.dtype)

    tile, halo, vec3, vec1 = _conv_specs(rows, r, cw)
    return pl.pallas_call(
        body, grid=(f // cw, rows // r), in_specs=[tile, halo, tile, halo, vec3, vec3, vec1, vec1], out_specs=tile,
        out_shape=jax.ShapeDtypeStruct((rows, f), BF16), name=name, compiler_params=_cparams(("parallel", "parallel")),
    )(hg, hg, hu, hu, wg, wu, bg, bu)


def _gate_grads(da, g, u):
    sg = _sigmoid(g)
    return da * u * (sg * (1.0 + g * (1.0 - sg))), da * (g * sg)


def _conv_back(dc, dc_next, w_ref, last):
    r = dc.shape[0]
    row = lax.broadcasted_iota(jnp.int32, (r, 1), 0)
    keep = jnp.where(last, 0.0, 1.0)
    n0 = dc_next[0:1, :] * keep
    n1 = dc_next[1:2, :] * keep
    f1 = jnp.where(row == r - 1, n0, pltpu.roll(dc, r - 1, 0))
    f2 = jnp.where(row == r - 1, n1, jnp.where(row == r - 2, n0, pltpu.roll(dc, r - 2, 0)))
    return w_ref[2:3, :] * dc + w_ref[1:2, :] * f1 + w_ref[0:1, :] * f2


def _conv_next_rows(h, nxt_ref, w_ref, b_ref):
    r = h.shape[0]
    hn = nxt_ref[...]
    row = lax.broadcasted_iota(jnp.int32, (8, 1), 0)
    m1, m2 = h[r - 1:r, :], h[r - 2:r - 1, :]
    p1 = jnp.where(row == 0, m1, pltpu.roll(hn, 1, 0))
    p2 = jnp.where(row == 0, m2, jnp.where(row == 1, m1, pltpu.roll(hn, 2, 0)))
    return b_ref[...] + w_ref[0:1, :] * p2 + w_ref[1:2, :] * p1 + w_ref[2:3, :] * hn


def _convffn_bwd(name, da, hg, hu, wg, wu, bg, bu):
    rows, f = hg.shape
    r, cw = min(_CONV_ROWS, rows), _pick(f, _CONV_COLS)
    nrt = rows // r

    def body(da_ref, dan_ref, hg_ref, hgh_ref, hgn_ref, hu_ref, huh_ref, hun_ref, wg_ref, wu_ref, bg_ref, bu_ref,
             dhg_ref, dhu_ref, dwg_ref, dwu_ref, dbg_ref, dbu_ref):
        first = pl.program_id(1) == 0
        last = pl.program_id(1) == nrt - 1
        hgv, g1, g2 = _conv_taps(hg_ref, hgh_ref, first)
        g = bg_ref[...] + wg_ref[0:1, :] * g2 + wg_ref[1:2, :] * g1 + wg_ref[2:3, :] * hgv
        huv, u1, u2 = _conv_taps(hu_ref, huh_ref, first)
        u = bu_ref[...] + wu_ref[0:1, :] * u2 + wu_ref[1:2, :] * u1 + wu_ref[2:3, :] * huv
        dcg, dcu = _gate_grads(da_ref[...], g, u)
        dcg_n, dcu_n = _gate_grads(dan_ref[...], _conv_next_rows(hgv, hgn_ref, wg_ref, bg_ref),
                                   _conv_next_rows(huv, hun_ref, wu_ref, bu_ref))
        dhg_ref[...] = _conv_back(dcg, dcg_n, wg_ref, last).astype(dhg_ref.dtype)
        dhu_ref[...] = _conv_back(dcu, dcu_n, wu_ref, last).astype(dhu_ref.dtype)

        @pl.when(first)
        def _():
            for ref in (dwg_ref, dwu_ref, dbg_ref, dbu_ref):
                ref[...] = jnp.zeros(ref.shape, ref.dtype)

        def colsum(v):
            return jnp.sum(v, axis=0, keepdims=True)

        dwg_ref[0:1, :] += colsum(dcg * g2)
        dwg_ref[1:2, :] += colsum(dcg * g1)
        dwg_ref[2:3, :] += colsum(dcg * hgv)
        dwu_ref[0:1, :] += colsum(dcu * u2)
        dwu_ref[1:2, :] += colsum(dcu * u1)
        dwu_ref[2:3, :] += colsum(dcu * huv)
        dbg_ref[...] += colsum(dcg)
        dbu_ref[...] += colsum(dcu)

    tile, halo, vec3, vec1 = _conv_specs(rows, r, cw)
    nxt = pl.BlockSpec((8, cw), lambda j, i: (jnp.minimum((i + 1) * (r // 8), rows // 8 - 1), j))
    big = jax.ShapeDtypeStruct((rows, f), BF16)
    return pl.pallas_call(
        body, grid=(f // cw, nrt),
        in_specs=[tile, nxt, tile, halo, nxt, tile, halo, nxt, vec3, vec3, vec1, vec1],
        out_specs=[tile, tile, vec3, vec3, vec1, vec1],
        out_shape=[big, big, jax.ShapeDtypeStruct((3, f), F32), jax.ShapeDtypeStruct((3, f), F32),
                   jax.ShapeDtypeStruct((1, f), F32), jax.ShapeDtypeStruct((1, f), F32)],
        name=name, compiler_params=_cparams(("parallel", "arbitrary")),
    )(da, da, hg, hg, hg, hu, hu, hu, wg, wu, bg, bu)


_GMLP_ROWS = 256


def _gmlp_group_norm(vg, gain):
    r = lax.rsqrt(jnp.mean(vg * vg, axis=-1, keepdims=True) + EPS)
    vh = vg * r
    return vh, r, vh * gain


def _gmlp_fwd(name, zuv, v_gain, w_tril, b_exp):
    rows = zuv.shape[0]
    r = min(_GMLP_ROWS, rows)

    def body(z_ref, gain_ref, w_ref, b_ref, o_ref):
        for ch in range(r // A_CHUNK):
            lo = ch * A_CHUNK
            for g in range(A_GROUPS):
                c0 = g * LANES
                u = _gelu(z_ref[lo:lo + A_CHUNK, c0:c0 + LANES])
                v = _gelu(z_ref[lo:lo + A_CHUNK, A_WIDTH + c0:A_WIDTH + c0 + LANES])
                _, _, vn = _gmlp_group_norm(v, gain_ref[:, c0:c0 + LANES])
                sv = jnp.dot(w_ref[g], vn.astype(BF16), preferred_element_type=F32) + b_ref[g]
                o_ref[lo:lo + A_CHUNK, c0:c0 + LANES] = (u * sv).astype(o_ref.dtype)

    return pl.pallas_call(
        body, grid=(rows // r,),
        in_specs=[pl.BlockSpec((r, 2 * A_WIDTH), lambda i: (i, 0)), pl.BlockSpec((1, A_WIDTH), lambda i: (0, 0)),
                  pl.BlockSpec((A_GROUPS, A_CHUNK, A_CHUNK), lambda i: (0, 0, 0)),
                  pl.BlockSpec((A_GROUPS, A_CHUNK, LANES), lambda i: (0, 0, 0))],
        out_specs=pl.BlockSpec((r, A_WIDTH), lambda i: (i, 0)),
        out_shape=jax.ShapeDtypeStruct((rows, A_WIDTH), BF16), name=name, compiler_params=_cparams(("parallel",)),
    )(zuv, v_gain, w_tril, b_exp)


def _gmlp_bwd(name, zuv, dya, v_gain, w_tril, w_tril_t, b_exp):
    rows = zuv.shape[0]
    r = min(_GMLP_ROWS, rows)

    def body(z_ref, dy_ref, gain_ref, w_ref, wt_ref, b_ref, dz_ref, dw_ref, db_ref, dgain_ref):
        @pl.when(pl.program_id(0) == 0)
        def _():
            for ref in (dw_ref, db_ref, dgain_ref):
                ref[...] = jnp.zeros(ref.shape, ref.dtype)

        for ch in range(r // A_CHUNK):
            lo = ch * A_CHUNK
            for g in range(A_GROUPS):
                c0 = g * LANES
                zu = z_ref[lo:lo + A_CHUNK, c0:c0 + LANES]
                zv = z_ref[lo:lo + A_CHUNK, A_WIDTH + c0:A_WIDTH + c0 + LANES]
                gain = gain_ref[:, c0:c0 + LANES]
                u = _gelu(zu)
                v = _gelu(zv)
                vh, rr, vn = _gmlp_group_norm(v, gain)
                vn_b = vn.astype(BF16)
                sv = jnp.dot(w_ref[g], vn_b, preferred_element_type=F32) + b_ref[g]
                dy = dy_ref[lo:lo + A_CHUNK, c0:c0 + LANES]
                dsv = dy * u
                dsv_b = dsv.astype(BF16)
                dz_ref[lo:lo + A_CHUNK, c0:c0 + LANES] = ((dy * sv) * _gelu_grad(zu)).astype(dz_ref.dtype)
                dw_ref[g] += lax.dot_general(dsv_b, vn_b, _DOT_DIMS["nt"], preferred_element_type=F32)
                db_ref[g] += dsv
                dvn = jnp.dot(wt_ref[g], dsv_b, preferred_element_type=F32)
                dgain_ref[:, c0:c0 + LANES] += jnp.sum(dvn * vh, axis=0, keepdims=True)
                dvh = dvn * gain
                dv = rr * (dvh - vh * jnp.mean(dvh * vh, axis=-1, keepdims=True))
                dz_ref[lo:lo + A_CHUNK, A_WIDTH + c0:A_WIDTH + c0 + LANES] = (dv * _gelu_grad(zv)).astype(dz_ref.dtype)

    wspec = pl.BlockSpec((A_GROUPS, A_CHUNK, A_CHUNK), lambda i: (0, 0, 0))
    bspec = pl.BlockSpec((A_GROUPS, A_CHUNK, LANES), lambda i: (0, 0, 0))
    gspec = pl.BlockSpec((1, A_WIDTH), lambda i: (0, 0))
    return pl.pallas_call(
        body, grid=(rows // r,),
        in_specs=[pl.BlockSpec((r, 2 * A_WIDTH), lambda i: (i, 0)), pl.BlockSpec((r, A_WIDTH), lambda i: (i, 0)),
                  gspec, wspec, wspec, bspec],
        out_specs=[pl.BlockSpec((r, 2 * A_WIDTH), lambda i: (i, 0)), wspec, bspec, gspec],
        out_shape=[jax.ShapeDtypeStruct((rows, 2 * A_WIDTH), BF16),
                   jax.ShapeDtypeStruct((A_GROUPS, A_CHUNK, A_CHUNK), F32),
                   jax.ShapeDtypeStruct((A_GROUPS, A_CHUNK, LANES), F32), jax.ShapeDtypeStruct((1, A_WIDTH), F32)],
        name=name, compiler_params=_cparams(("arbitrary",)),
    )(zuv, dya, v_gain, w_tril, w_tril_t, b_exp)


_ATT_T = 512
_Q_SCALE = B_HEAD_DIM ** -0.5


def _head_mean(v, bd):
    hi = v.astype(BF16)
    lo = (v - hi.astype(F32)).astype(BF16)
    tot = jnp.dot(hi, bd, preferred_element_type=F32) + jnp.dot(lo, bd, preferred_element_type=F32)
    return tot * (1.0 / B_HEAD_DIM)


def _qkv_prep_fwd(name, zqkv, zf, qg, kg, bf, bd):
    def fn(z, f, qg_v, kg_v, bf_v, bd_v):
        zq, zk, zv = z[:, :B_WIDTH], z[:, B_WIDTH:2 * B_WIDTH], z[:, 2 * B_WIDTH:]
        q = (zq * lax.rsqrt(_head_mean(zq * zq, bd_v) + EPS)) * qg_v * _Q_SCALE
        k = (zk * lax.rsqrt(_head_mean(zk * zk, bd_v) + EPS)) * kg_v
        return q, k, zv, _log_sigmoid(f + bf_v)

    return _rows(name, fn, [zqkv, zf], [qg, kg, bf, bd],
                 [(B_WIDTH, BF16), (B_WIDTH, BF16), (B_WIDTH, BF16), (LANES, F32)])


def _qkv_prep_bwd(name, zqkv, zf, dq, dk, dv, dls, qg, kg, bf, bd):
    def fn(z, f, dq_v, dk_v, dv_v, dls_v, qg_v, kg_v, bf_v, bd_v):
        zq, zk = z[:, :B_WIDTH], z[:, B_WIDTH:2 * B_WIDTH]

        def norm_bwd(x, dy, gain):
            r = lax.rsqrt(_head_mean(x * x, bd_v) + EPS)
            xh = x * r
            dxh = dy * gain
            dx = r * (dxh - xh * _head_mean(dxh * xh, bd_v))
            return dx, jnp.sum(dy * xh, axis=0, keepdims=True)

        dzq, dqg = norm_bwd(zq, dq_v * _Q_SCALE, qg_v)
        dzk, dkg = norm_bwd(zk, dk_v, kg_v)
        dzf = dls_v * (1.0 - _sigmoid(f + bf_v))
        return jnp.concatenate([dzq, dzk, dv_v], axis=1), dzf, dqg, dkg, jnp.sum(dzf, axis=0, keepdims=True)

    return _rows(name, fn, [zqkv, zf, dq, dk, dv, dls], [qg, kg, bf, bd],
                 [(3 * B_WIDTH, BF16), (LANES, BF16)], accs=[(1, B_WIDTH), (1, B_WIDTH), (1, LANES)])


def _cumsum_rows(name, a, reverse=False, tile=512):
    rows, w = a.shape
    r = min(tile, rows)
    n = rows // r

    def body(a_ref, o_ref, carry):
        @pl.when(pl.program_id(0) == 0)
        def _():
            carry[...] = jnp.zeros(carry.shape, carry.dtype)

        x = a_ref[...]
        row = lax.broadcasted_iota(jnp.int32, (r, 1), 0)
        s = 1
        while s < r:
            if reverse:
                x = x + jnp.where(row < r - s, pltpu.roll(x, r - s, 0), 0.0)
            else:
                x = x + jnp.where(row >= s, pltpu.roll(x, s, 0), 0.0)
            s *= 2
        x = x + carry[0:1, :]
        o_ref[...] = x
        edge = x[0:1, :] if reverse else x[r - 1:r, :]
        carry[...] = jnp.broadcast_to(edge, carry.shape)

    idx = (lambda i: (n - 1 - i, 0)) if reverse else (lambda i: (i, 0))
    return pl.pallas_call(
        body, grid=(n,), in_specs=[pl.BlockSpec((r, w), idx)], out_specs=pl.BlockSpec((r, w), idx),
        out_shape=jax.ShapeDtypeStruct((rows, w), F32), scratch_shapes=[pltpu.VMEM((8, w), F32)], name=name,
        compiler_params=_cparams(("arbitrary",)),
    )(a)


def _head_masks():
    lane = lax.broadcasted_iota(jnp.int32, (1, LANES), 1)
    return [lane < B_HEAD_DIM, lane >= B_HEAD_DIM]


def _causal(t):
    row = lax.broadcasted_iota(jnp.int32, (t, t), 0)
    col = lax.broadcasted_iota(jnp.int32, (t, t), 1)
    return row, col


def _col_from_row(row_vec):
    return jnp.transpose(jnp.broadcast_to(row_vec, (LANES, row_vec.shape[1])))[:, 0:1]


def _row_from_col(col):
    return jnp.transpose(jnp.broadcast_to(col, (col.shape[0], LANES)))[0:1, :]


def _flash_fwd(name, q, k, v, nck_rows):
    rows = q.shape[0]
    t = min(_ATT_T, rows)
    nb = rows // t

    def body(q_ref, k_ref, v_ref, nck_ref, o_ref, lse_ref):
        pair, i = pl.program_id(0), pl.program_id(1)
        q2 = q_ref[...]
        row, col = _causal(t)
        masks = _head_masks()
        qh = [jnp.where(hm, q2, jnp.zeros_like(q2)) for hm in masks]

        def step(j, carry, diag):
            ml, acc = carry
            start = pl.multiple_of(j * t, t)
            kb = k_ref[pl.ds(start, t), :]
            vb = v_ref[pl.ds(start, t), :]
            new_ml = []
            for hh, hm in enumerate(masks):
                m, l = ml[hh]
                s = lax.dot_general(qh[hh], kb, _DOT_DIMS["nt"], preferred_element_type=F32)
                s = s + nck_ref[2 * pair + hh, pl.ds(j, 1), :]
                if diag:
                    s = jnp.where(col <= row, s, NEG_INF)
                m_new = jnp.maximum(m, jnp.max(s, axis=1, keepdims=True))
                p = jnp.exp(s - m_new)
                alpha = jnp.exp(m - m_new)
                new_ml.append((m_new, alpha * l + jnp.sum(p, axis=1, keepdims=True)))
                pv = jnp.dot(p.astype(BF16), jnp.where(hm, vb, jnp.zeros_like(vb)), preferred_element_type=F32)
                acc = acc * jnp.where(hm, alpha, 1.0) + pv
            return tuple(new_ml), acc

        def init_ml():
            return (jnp.full((t, 1), NEG_INF, F32), jnp.zeros((t, 1), F32))

        init = ((init_ml(), init_ml()), jnp.zeros((t, LANES), F32))
        carry = lax.fori_loop(0, i, lambda j, c: step(j, c, False), init)
        ml, acc = step(i, carry, True)
        o_ref[...] = acc / jnp.where(masks[0], ml[0][1], ml[1][1])
        for hh in range(2):
            lse_ref[hh, 0] = _row_from_col(ml[hh][0] + jnp.log(ml[hh][1]))

    return pl.pallas_call(
        body, grid=(B_HEADS // 2, nb),
        in_specs=[pl.BlockSpec((t, LANES), lambda p, i: (i, p)), pl.BlockSpec((rows, LANES), lambda p, i: (0, p)),
                  pl.BlockSpec((rows, LANES), lambda p, i: (0, p)),
                  pl.BlockSpec((B_HEADS, nb, t), lambda p, i: (0, 0, 0))],
        out_specs=[pl.BlockSpec((t, LANES), lambda p, i: (i, p)),
                   pl.BlockSpec((2, 1, 1, t), lambda p, i: (p, i, 0, 0))],
        out_shape=[jax.ShapeDtypeStruct((rows, B_WIDTH), F32), jax.ShapeDtypeStruct((B_HEADS, nb, 1, t), F32)],
        name=name, compiler_params=_cparams(("parallel", "parallel")),
    )(q, k, v, nck_rows)


def _flash_bwd_dq(name, q, k, v, nck_rows, o, do, lse_rows):
    rows = q.shape[0]
    t = min(_ATT_T, rows)
    nb = rows // t

    def body(q_ref, k_ref, v_ref, nck_ref, o_ref, do_ref, lse_ref, dq_ref, delta_ref):
        pair, i = pl.program_id(0), pl.program_id(1)
        q2 = q_ref[...]
        do2 = do_ref[...]
        od = o_ref[...] * do2
        do_b = do2.astype(BF16)
        row, col = _causal(t)
        masks = _head_masks()
        qh = [jnp.where(hm, q2, jnp.zeros_like(q2)) for hm in masks]
        doh = [jnp.where(hm, do_b, jnp.zeros_like(do_b)) for hm in masks]
        delta = [jnp.sum(jnp.where(hm, od, 0.0), axis=1, keepdims=True) for hm in masks]
        lse = [_col_from_row(lse_ref[2 * pair + hh, pl.ds(i, 1), :]) for hh in range(2)]

        def step(j, carry, diag):
            acc, rowsum = carry
            start = pl.multiple_of(j * t, t)
            kb = k_ref[pl.ds(start, t), :]
            vb = v_ref[pl.ds(start, t), :]
            new_rowsum = []
            for hh, hm in enumerate(masks):
                s = lax.dot_general(qh[hh], kb, _DOT_DIMS["nt"], preferred_element_type=F32)
                s = s + nck_ref[2 * pair + hh, pl.ds(j, 1), :]
                p = jnp.exp(s - lse[hh])
                if diag:
                    p = jnp.where(col <= row, p, 0.0)
                dp = lax.dot_general(doh[hh], vb, _DOT_DIMS["nt"], preferred_element_type=F32)
                ds = p * (dp - delta[hh])
                new_rowsum.append(rowsum[hh] + jnp.sum(ds, axis=1, keepdims=True))
                acc = acc + jnp.dot(ds.astype(BF16), jnp.where(hm, kb, jnp.zeros_like(kb)),
                                    preferred_element_type=F32)
            return acc, tuple(new_rowsum)

        zcol = jnp.zeros((t, 1), F32)
        carry = lax.fori_loop(0, i, lambda j, c: step(j, c, False), (jnp.zeros((t, LANES), F32), (zcol, zcol)))
        acc, rowsum = step(i, carry, True)
        dq_ref[...] = acc
        for hh in range(2):
            delta_ref[hh, 0] = _row_from_col(delta[hh] + rowsum[hh])

    tile = pl.BlockSpec((t, LANES), lambda p, i: (i, p))
    full = pl.BlockSpec((rows, LANES), lambda p, i: (0, p))
    rowspec = pl.BlockSpec((B_HEADS, nb, t), lambda p, i: (0, 0, 0))
    return pl.pallas_call(
        body, grid=(B_HEADS // 2, nb),
        in_specs=[tile, full, full, rowspec, tile, tile, rowspec],
        out_specs=[tile, pl.BlockSpec((2, 1, 1, t), lambda p, i: (p, i, 0, 0))],
        out_shape=[jax.ShapeDtypeStruct((rows, B_WIDTH), F32), jax.ShapeDtypeStruct((B_HEADS, nb, 1, t), F32)],
        name=name, compiler_params=_cparams(("parallel", "parallel")),
    )(q, k, v, nck_rows, o, do, lse_rows)


def _flash_bwd_dkv(name, q, k, v, nck_rows, do, lse_rows, delta_rows):
    rows = q.shape[0]
    t = min(_ATT_T, rows)
    nb = rows // t

    def body(k_ref, v_ref, q_ref, do_ref, nck_ref, lse_ref, delta_ref, dk_ref, dv_ref, dn_ref):
        pair, j = pl.program_id(0), pl.program_id(1)
        k2 = k_ref[...]
        v2 = v_ref[...]
        row, col = _causal(t)
        masks = _head_masks()
        kh = [jnp.where(hm, k2, jnp.zeros_like(k2)) for hm in masks]
        vh = [jnp.where(hm, v2, jnp.zeros_like(v2)) for hm in masks]
        nck = [_col_from_row(nck_ref[2 * pair + hh, pl.ds(j, 1), :]) for hh in range(2)]

        def step(i, carry, diag):
            dk, dv, dn = carry
            start = pl.multiple_of(i * t, t)
            qb = q_ref[pl.ds(start, t), :]
            dob = do_ref[pl.ds(start, t), :].astype(BF16)
            dn_new = []
            for hh, hm in enumerate(masks):
                head = 2 * pair + hh
                st = lax.dot_general(kh[hh], qb, _DOT_DIMS["nt"], preferred_element_type=F32) + nck[hh]
                pt = jnp.exp(st - lse_ref[head, pl.ds(i, 1), :])
                if diag:
                    pt = jnp.where(row <= col, pt, 0.0)
                dpt = lax.dot_general(vh[hh], dob, _DOT_DIMS["nt"], preferred_element_type=F32)
                dst = pt * (dpt - delta_ref[head, pl.ds(i, 1), :])
                dv = dv + jnp.dot(pt.astype(BF16), jnp.where(hm, dob, jnp.zeros_like(dob)),
                                  preferred_element_type=F32)
                dk = dk + jnp.dot(dst.astype(BF16), jnp.where(hm, qb, jnp.zeros_like(qb)),
                                  preferred_element_type=F32)
                dn_new.append(dn[hh] + jnp.sum(dst, axis=1, keepdims=True))
            return dk, dv, tuple(dn_new)

        zero = jnp.zeros((t, LANES), F32)
        zcol = jnp.zeros((t, 1), F32)
        carry = step(j, (zero, zero, (zcol, zcol)), True)
        dk, dv, dn = lax.fori_loop(j + 1, nb, lambda i, c: step(i, c, False), carry)
        dk_ref[...] = dk
        dv_ref[...] = dv
        for hh in range(2):
            dn_ref[hh, 0] = _row_from_col(dn[hh])

    tile = pl.BlockSpec((t, LANES), lambda p, j: (j, p))
    full = pl.BlockSpec((rows, LANES), lambda p, j: (0, p))
    rowspec = pl.BlockSpec((B_HEADS, nb, t), lambda p, j: (0, 0, 0))
    big = jax.ShapeDtypeStruct((rows, B_WIDTH), F32)
    return pl.pallas_call(
        body, grid=(B_HEADS // 2, nb),
        in_specs=[tile, tile, full, full, rowspec, rowspec, rowspec],
        out_specs=[tile, tile, pl.BlockSpec((2, 1, 1, t), lambda p, j: (p, j, 0, 0))],
        out_shape=[big, big, jax.ShapeDtypeStruct((B_HEADS, nb, 1, t), F32)],
        name=name, compiler_params=_cparams(("parallel", "parallel")),
    )(k, v, q, do, nck_rows, lse_rows, delta_rows)


_S5_ROWS = 512


def _s5_discretize(a_re, a_im, log_dt, b_re, b_im):
    dt = jnp.exp(log_dt)[:, None]
    mag = jnp.exp(a_re * dt)
    ab_re, ab_im = mag * jnp.cos(a_im * dt), mag * jnp.sin(a_im * dt)
    den = a_re * a_re + a_im * a_im
    nr, ni = ab_re - 1.0, ab_im
    cr = (nr * a_re + ni * a_im) / den
    ci = (ni * a_re - nr * a_im) / den
    bb_re = cr[..., None] * b_re - ci[..., None] * b_im
    bb_im = cr[..., None] * b_im + ci[..., None] * b_re
    return ab_re, ab_im, bb_re, bb_im


def _s5_block_diag(m):
    g, r, c = m.shape
    mb = m.reshape(S5_BLOCKS, 8, r, c)
    eye = jnp.eye(8, dtype=m.dtype)
    return jnp.einsum("bgrc,gh->bgrhc", mb, eye).reshape(S5_BLOCKS, 8 * r, 8 * c)


def _s5_block_diag_extract(m, r, c):
    mb = m.reshape(S5_BLOCKS, 8, r, 8, c)
    return jnp.einsum("bgrhc,gh->bgrc", mb, jnp.eye(8, dtype=m.dtype)).reshape(S5_GROUPS, r, c)


def _s5_tables(ab_re, ab_im, r):
    ar = jnp.broadcast_to(ab_re.reshape(1, -1), (r, S5_GROUPS * S5_STATE))
    ai = jnp.broadcast_to(ab_im.reshape(1, -1), (r, S5_GROUPS * S5_STATE))

    def mul(x, y):
        return x[0] * y[0] - x[1] * y[1], x[0] * y[1] + x[1] * y[0]

    return lax.associative_scan(mul, (ar, ai), axis=0)


def _scan_step(xr, xi, ar, ai, s, row, up):
    r = xr.shape[0]
    if up:
        ai = -ai
    if s < 8:
        if up:
            sr = jnp.where(row < r - s, pltpu.roll(xr, r - s, 0), 0.0)
            si = jnp.where(row < r - s, pltpu.roll(xi, r - s, 0), 0.0)
        else:
            sr = jnp.where(row >= s, pltpu.roll(xr, s, 0), 0.0)
            si = jnp.where(row >= s, pltpu.roll(xi, s, 0), 0.0)
        return xr + (ar * sr - ai * si), xi + (ar * si + ai * sr)
    if up:
        (dr, di), (sr, si) = (xr[:r - s], xi[:r - s]), (xr[s:], xi[s:])
        nr, ni = dr + (ar * sr - ai * si), di + (ar * si + ai * sr)
        return jnp.concatenate([nr, xr[r - s:]], axis=0), jnp.concatenate([ni, xi[r - s:]], axis=0)
    (dr, di), (sr, si) = (xr[s:], xi[s:]), (xr[:r - s], xi[:r - s])
    nr, ni = dr + (ar * sr - ai * si), di + (ar * si + ai * sr)
    return jnp.concatenate([xr[:s], nr], axis=0), jnp.concatenate([xi[:s], ni], axis=0)


_S5_CHUNK = 16


def _scan_tile(xr, xi, pr_ref, pi_ref, tr_ref, ti_ref, edge_ref, up):
    r, nl = xr.shape
    ch = _S5_CHUNK
    nch = r // ch
    sub = lax.broadcasted_iota(jnp.int32, (r, 1), 0) & (ch - 1)
    s = 1
    while s < ch:
        ar, ai = pr_ref[s - 1:s, :], pi_ref[s - 1:s, :]
        if up:
            ai, keep, shift = -ai, sub < ch - s, r - s
        else:
            keep, shift = sub >= s, s
        sr = jnp.where(keep, pltpu.roll(xr, shift, 0), 0.0)
        si = jnp.where(keep, pltpu.roll(xi, shift, 0), 0.0)
        xr, xi = xr + (ar * sr - ai * si), xi + (ar * si + ai * sr)
        s *= 2
    nb = nl // LANES
    for k in range(nb):
        edge_ref[k] = xr[:, k * LANES:(k + 1) * LANES]
        edge_ref[nb + k] = xi[:, k * LANES:(k + 1) * LANES]
    e0 = 0 if up else ch - 1
    er = jnp.concatenate([edge_ref[k, pl.ds(e0, nch, stride=ch), :] for k in range(nb)], axis=1)
    ei = jnp.concatenate([edge_ref[nb + k, pl.ds(e0, nch, stride=ch), :] for k in range(nb)], axis=1)
    rowc = lax.broadcasted_iota(jnp.int32, (nch, 1), 0)
    s = 1
    while s < nch:
        er, ei = _scan_step(er, ei, pr_ref[ch * s - 1:ch * s, :], pi_ref[ch * s - 1:ch * s, :], s, rowc, up)
        s *= 2
    if up:
        nr = jnp.where(rowc < nch - 1, pltpu.roll(er, nch - 1, 0), 0.0)
        ni = jnp.where(rowc < nch - 1, pltpu.roll(ei, nch - 1, 0), 0.0)
    else:
        nr = jnp.where(rowc >= 1, pltpu.roll(er, 1, 0), 0.0)
        ni = jnp.where(rowc >= 1, pltpu.roll(ei, 1, 0), 0.0)
    br = jnp.concatenate([jnp.broadcast_to(nr[n:n + 1, :], (ch, nl)) for n in range(nch)], axis=0)
    bi = jnp.concatenate([jnp.broadcast_to(ni[n:n + 1, :], (ch, nl)) for n in range(nch)], axis=0)
    tr, ti = tr_ref[...], ti_ref[...]
    if up:
        ti = -ti
    return xr + (tr * br - ti * bi), xi + (tr * bi + ti * br)


def _s5_scan_tile(u_ref, bcat_ref, pr_ref, pi_ref, tr_ref, ti_ref, edge_ref, cin_r, cin_i):
    bu = jnp.dot(u_ref[...], bcat_ref[...], preferred_element_type=F32)
    xr, xi = bu[:, :S5_LANES], bu[:, S5_LANES:]
    ar, ai = pr_ref[0:1, :], pi_ref[0:1, :]
    first = lax.broadcasted_iota(jnp.int32, (8, 1), 0) == 0
    xr = jnp.concatenate([xr[:8] + jnp.where(first, ar * cin_r - ai * cin_i, 0.0), xr[8:]], axis=0)
    xi = jnp.concatenate([xi[:8] + jnp.where(first, ar * cin_i + ai * cin_r, 0.0), xi[8:]], axis=0)
    return _scan_tile(xr, xi, pr_ref, pi_ref, tr_ref, ti_ref, edge_ref, False)


def _s5_fwd(name, u, bcat, ccat, pw_re, pw_im, pt_re, pt_im):
    rows = u.shape[0]
    r = pw_re.shape[0]
    nt = rows // r

    def body(u_ref, bcat_ref, ccat_ref, pr_ref, pi_ref, tr_ref, ti_ref, y_ref, xin_ref, carry, edge):
        @pl.when(pl.program_id(1) == 0)
        def _():
            carry[...] = jnp.zeros(carry.shape, carry.dtype)

        xin_ref[...] = carry[...]
        xr, xi = _s5_scan_tile(u_ref, bcat_ref, pr_ref, pi_ref, tr_ref, ti_ref, edge,
                               carry[0:1, :S5_LANES], carry[0:1, S5_LANES:])
        xcat = jnp.concatenate([xr, xi], axis=1)
        carry[...] = jnp.broadcast_to(xcat[r - 1:r, :], carry.shape)
        y_ref[...] = jnp.dot(xcat.astype(BF16), ccat_ref[...], preferred_element_type=F32)

    tab = pl.BlockSpec((r, S5_LANES), lambda b, i: (0, b))
    return pl.pallas_call(
        body, grid=(S5_BLOCKS, nt),
        in_specs=[pl.BlockSpec((r, LANES), lambda b, i: (i, b)),
                  pl.BlockSpec((None, LANES, 2 * S5_LANES), lambda b, i: (b, 0, 0)),
                  pl.BlockSpec((None, 2 * S5_LANES, LANES), lambda b, i: (b, 0, 0)), tab, tab, tab, tab],
        out_specs=[pl.BlockSpec((r, LANES), lambda b, i: (i, b)),
                   pl.BlockSpec((None, 8, 2 * S5_LANES), lambda b, i: (b, i, 0))],
        out_shape=[jax.ShapeDtypeStruct((rows, D_MODEL), F32),
                   jax.ShapeDtypeStruct((S5_BLOCKS, 8 * nt, 2 * S5_LANES), F32)],
        scratch_shapes=[pltpu.VMEM((8, 2 * S5_LANES), F32), pltpu.VMEM((2 * S5_LANES // LANES, r, LANES), F32)], name=name,
        compiler_params=_cparams(("parallel", "arbitrary")),
    )(u, bcat, ccat, pw_re, pw_im, pt_re, pt_im)


def _s5_bwd(name, u, dy, xin, bcat, ccat, pw_re, pw_im, pt_re, pt_im, ptu_re, ptu_im):
    rows = u.shape[0]
    r = pw_re.shape[0]
    nt = rows // r

    def body(u_ref, dy_ref, xin_ref, bcat_ref, ccat_ref, pr_ref, pi_ref, tr_ref, ti_ref, ur_ref, ui_ref,
             du_ref, db_ref, dc_ref, dar_ref, dai_ref, carry, edge):
        @pl.when(pl.program_id(1) == 0)
        def _():
            carry[...] = jnp.zeros(carry.shape, carry.dtype)
            for ref in (db_ref, dc_ref, dar_ref, dai_ref):
                ref[...] = jnp.zeros(ref.shape, ref.dtype)

        row = lax.broadcasted_iota(jnp.int32, (r, 1), 0)
        cin_r, cin_i = xin_ref[0:1, :S5_LANES], xin_ref[0:1, S5_LANES:]
        xr, xi = _s5_scan_tile(u_ref, bcat_ref, pr_ref, pi_ref, tr_ref, ti_ref, edge, cin_r, cin_i)
        dy_b = dy_ref[...].astype(BF16)
        xcat = jnp.concatenate([xr, xi], axis=1).astype(BF16)
        dc_ref[...] += lax.dot_general(xcat, dy_b, _DOT_DIMS["tn"], preferred_element_type=F32)
        g = lax.dot_general(dy_b, ccat_ref[...], _DOT_DIMS["nt"], preferred_element_type=F32)
        lr, li = g[:, :S5_LANES], g[:, S5_LANES:]
        nr, ni = carry[0:1, :S5_LANES], carry[0:1, S5_LANES:]
        ar, ai = pr_ref[0:1, :], pi_ref[0:1, :]
        final = lax.broadcasted_iota(jnp.int32, (8, 1), 0) == 7
        lr = jnp.concatenate([lr[:r - 8], lr[r - 8:] + jnp.where(final, ar * nr + ai * ni, 0.0)], axis=0)
        li = jnp.concatenate([li[:r - 8], li[r - 8:] + jnp.where(final, ar * ni - ai * nr, 0.0)], axis=0)
        lr, li = _scan_tile(lr, li, pr_ref, pi_ref, ur_ref, ui_ref, edge, True)
        carry[...] = jnp.broadcast_to(jnp.concatenate([lr[0:1, :], li[0:1, :]], axis=1), carry.shape)
        lcat = jnp.concatenate([lr, li], axis=1).astype(BF16)
        du_ref[...] = lax.dot_general(lcat, bcat_ref[...], _DOT_DIMS["nt"], preferred_element_type=F32)
        db_ref[...] += lax.dot_general(u_ref[...], lcat, _DOT_DIMS["tn"], preferred_element_type=F32)
        pxr = jnp.where(row == 0, cin_r, pltpu.roll(xr, 1, 0))
        pxi = jnp.where(row == 0, cin_i, pltpu.roll(xi, 1, 0))
        dar_ref[...] += jnp.sum((lr * pxr + li * pxi).reshape(r // 8, 8, S5_LANES), axis=0)
        dai_ref[...] += jnp.sum((li * pxr - lr * pxi).reshape(r // 8, 8, S5_LANES), axis=0)

    rev = lambda b, i: (nt - 1 - i, b)
    tab = pl.BlockSpec((r, S5_LANES), lambda b, i: (0, b))
    return pl.pallas_call(
        body, grid=(S5_BLOCKS, nt),
        in_specs=[pl.BlockSpec((r, LANES), rev), pl.BlockSpec((r, LANES), rev),
                  pl.BlockSpec((None, 8, 2 * S5_LANES), lambda b, i: (b, nt - 1 - i, 0)),
                  pl.BlockSpec((None, LANES, 2 * S5_LANES), lambda b, i: (b, 0, 0)),
                  pl.BlockSpec((None, 2 * S5_LANES, LANES), lambda b, i: (b, 0, 0)), tab, tab, tab, tab, tab, tab],
        out_specs=[pl.BlockSpec((r, LANES), rev),
                   pl.BlockSpec((None, LANES, 2 * S5_LANES), lambda b, i: (b, 0, 0)),
                   pl.BlockSpec((None, 2 * S5_LANES, LANES), lambda b, i: (b, 0, 0)),
                   pl.BlockSpec((None, 8, S5_LANES), lambda b, i: (b, 0, 0)),
                   pl.BlockSpec((None, 8, S5_LANES), lambda b, i: (b, 0, 0))],
        out_shape=[jax.ShapeDtypeStruct((rows, D_MODEL), F32),
                   jax.ShapeDtypeStruct((S5_BLOCKS, LANES, 2 * S5_LANES), F32),
                   jax.ShapeDtypeStruct((S5_BLOCKS, 2 * S5_LANES, LANES), F32),
                   jax.ShapeDtypeStruct((S5_BLOCKS, 8, S5_LANES), F32),
                   jax.ShapeDtypeStruct((S5_BLOCKS, 8, S5_LANES), F32)],
        scratch_shapes=[pltpu.VMEM((8, 2 * S5_LANES), F32), pltpu.VMEM((2 * S5_LANES // LANES, r, LANES), F32)], name=name,
        compiler_params=_cparams(("parallel", "arbitrary")),
    )(u, dy, xin, bcat, ccat, pw_re, pw_im, pt_re, pt_im, ptu_re, ptu_im)


def _ones_gain():
    return jnp.ones((1, D_MODEL), F32)


def _channel_fwd(i, x1, p_i, w, rp, hn=None, next_norm=None):
    if hn is None:
        hn, = _rmsnorm_fwd(f"ffn_norm_{i}", x1, rp["norm_ffn"][i][None], [BF16])
    hg = _mm(f"ffn_up_g_{i}", hn, w["up_g"], tn=1408)
    hu = _mm(f"ffn_up_u_{i}", hn, w["up_u"], tn=1408)
    a = _convffn_fwd(f"ffn_conv_{i}", hg, hu, w["cw_g"], w["cw_u"], w["cb_g"], w["cb_u"])
    x2, r = _mm(f"ffn_down_{i}", a, w["down"], res=x1, tk=1408, norm_gain=_ones_gain())
    zg = _mm(f"ple_gate_{i}", r, w["ple_gate"])
    pp = _mm(f"ple_proj_{i}", p_i, w["ple_proj"])
    saved = dict(x1=x1, hn=hn, hg=hg, hu=hu, a=a, x2=x2, r=r, zg=zg, pp=pp, p_i=p_i)
    if next_norm is None:
        x3, = _rows(f"ple_out_{i}", lambda xv, zv, pv: (xv + _sigmoid(zv) * pv,), [x2, zg, pp], [], [(D_MODEL, F32)])
        return x3, None, saved
    gain, dtypes = next_norm

    def ple_out_norm(xv, zv, pv, gv):
        x3v = xv + _sigmoid(zv) * pv
        h = (x3v * _rstd(x3v)) * gv
        return (x3v,) + tuple(h for _ in dtypes)

    x3, *h_next = _rows(f"ple_out_{i}", ple_out_norm, [x2, zg, pp], [gain],
                        [(D_MODEL, F32)] + [(D_MODEL, dt) for dt in dtypes])
    return x3, h_next, saved


def _channel_bwd(i, dx3, sv, w, rp):
    def ple_bwd(dv, zv, pv):
        gate = _sigmoid(zv)
        return dv * gate, (dv * pv) * (gate * (1.0 - gate))

    dpp, dzg = _rows(f"ple_out_bwd_{i}", ple_bwd, [dx3, sv["zg"], sv["pp"]], [], [(D_MODEL, BF16), (D_MODEL, BF16)])
    g = {}
    g["ple_proj"] = _mm(f"ple_proj_dw_{i}", sv["p_i"], dpp, "tn")
    g["ple_gate"] = _mm(f"ple_gate_dw_{i}", sv["r"], dzg, "tn")
    dx2, _ = _mm_norm_bwd(f"ple_gate_dx_{i}", dzg, w["ple_gate"], "nt", sv["x2"], dx3, _ones_gain())
    da = _mm(f"ffn_down_dx_{i}", dx2, w["down"], "nt", tn=1408)
    g["down"] = _mm(f"ffn_down_dw_{i}", sv["a"], dx2, "tn", tm=1408)
    dhg, dhu, g["cw_g"], g["cw_u"], dbg, dbu = _convffn_bwd(
        f"ffn_conv_bwd_{i}", da, sv["hg"], sv["hu"], w["cw_g"], w["cw_u"], w["cb_g"], w["cb_u"])
    g["conv_b"] = jnp.concatenate([dbg, dbu], axis=1)[0]
    g["up_g"] = _mm(f"ffn_up_g_dw_{i}", sv["hn"], dhg, "tn", tn=1408)
    g["up_u"] = _mm(f"ffn_up_u_dw_{i}", sv["hn"], dhu, "tn", tn=1408)
    dhn = _mm(f"ffn_up_g_dx_{i}", dhg, w["up_g"], "nt", tk=1408)
    dx1, dgf = _mm_norm_bwd(f"ffn_up_u_dx_{i}", dhu, w["up_u"], "nt", sv["x1"], dx2, rp["norm_ffn"][i][None],
                            res=dhn, tk=1408)
    g["norm_ffn"] = dgf[0]
    return dx1, g


def _even_consts(e, rp):
    tri = jnp.tril(jnp.ones((A_CHUNK, A_CHUNK), dtype=bool))
    w_tril = jnp.where(tri[None], rp["ev_w_spatial"][e], 0.0).astype(BF16)
    b_exp = jnp.broadcast_to(rp["ev_b_spatial"][e][:, :, None], (A_GROUPS, A_CHUNK, LANES))
    seg = np.arange(B_WIDTH) // B_HEAD_DIM
    bd = jnp.asarray((seg[:, None] == seg[None, :]).astype(np.float32)).astype(BF16)
    return dict(
        tri=tri, w_tril=w_tril, w_tril_t=jnp.swapaxes(w_tril, 1, 2), b_exp=b_exp, bd=bd,
        v_gain=rp["ev_v_norm"][e][None], qg=jnp.tile(rp["ev_q_norm"][e], B_HEADS)[None],
        kg=jnp.tile(rp["ev_k_norm"][e], B_HEADS)[None],
        bf=jnp.pad(rp["ev_b_fgate"][e], (0, LANES - B_HEADS))[None])


def _even_fwd(i, x, w, rp, h_in=None):
    e = i // 2
    c = _even_consts(e, rp)
    rows = x.shape[0]
    t = min(_ATT_T, rows)
    h, = h_in if h_in is not None else _rmsnorm_fwd(f"mix_norm_{i}", x, rp["norm_mix"][i][None], [BF16])
    zuv = _mm(f"in_uv_{i}", h, w["in_uv"])
    zqkv = _mm(f"in_qkv_{i}", h, w["in_qkv"], tn=768)
    zf = _mm(f"in_f_{i}", h, w["in_f"])
    ya = _gmlp_fwd(f"gmlp_{i}", zuv, c["v_gain"], c["w_tril"], c["b_exp"])
    q, k, v, ls = _qkv_prep_fwd(f"qkv_prep_{i}", zqkv, zf, c["qg"], c["kg"], c["bf"], c["bd"])
    csum = _cumsum_rows(f"forget_cumsum_{i}", ls)
    nck = -csum[:, :B_HEADS].T
    nck_rows = nck.reshape(B_HEADS, rows // t, t)
    o, lse = _flash_fwd(f"attn_{i}", q, k, v, nck_rows)
    x1 = _mm(f"out_a_{i}", ya, w["out_a"], res=x)
    x1, hn = _mm(f"out_b_{i}", o, w["out_b"], res=x1, norm_gain=rp["norm_ffn"][i][None])
    return x1, hn, dict(x=x, h=h, zuv=zuv, zqkv=zqkv, zf=zf, ya=ya, q=q, k=k, v=v, nck_rows=nck_rows, o=o,
                          lse_rows=lse.reshape(B_HEADS, rows // t, t))


def _even_bwd(i, dx1, sv, w, rp):
    e = i // 2
    c = _even_consts(e, rp)
    rows = dx1.shape[0]
    t = min(_ATT_T, rows)
    nb = rows // t
    g = {}
    dya = _mm(f"out_a_dx_{i}", dx1, w["out_a"], "nt")
    do = _mm(f"out_b_dx_{i}", dx1, w["out_b"], "nt")
    g["out_a"] = _mm(f"out_a_dw_{i}", sv["ya"], dx1, "tn")
    g["out_b"] = _mm(f"out_b_dw_{i}", sv["o"], dx1, "tn")
    dq, delta = _flash_bwd_dq(f"attn_dq_{i}", sv["q"], sv["k"], sv["v"], sv["nck_rows"], sv["o"], do,
                              sv["lse_rows"])
    dk, dv, dn = _flash_bwd_dkv(f"attn_dkv_{i}", sv["q"], sv["k"], sv["v"], sv["nck_rows"], do, sv["lse_rows"],
                                delta.reshape(B_HEADS, nb, t))
    dcs = jnp.pad(-dn.reshape(B_HEADS, rows).T, ((0, 0), (0, LANES - B_HEADS)))
    dls = _cumsum_rows(f"forget_cumsum_bwd_{i}", dcs, reverse=True)
    dzqkv, dzf, dqg, dkg, dbf = _qkv_prep_bwd(f"qkv_prep_bwd_{i}", sv["zqkv"], sv["zf"], dq, dk, dv, dls,
                                              c["qg"], c["kg"], c["bf"], c["bd"])
    dzuv, dws, dbs, dvg = _gmlp_bwd(f"gmlp_bwd_{i}", sv["zuv"], dya, c["v_gain"], c["w_tril"], c["w_tril_t"],
                                    c["b_exp"])
    g["in_uv"] = _mm(f"in_uv_dw_{i}", sv["h"], dzuv, "tn")
    g["in_qkv"] = _mm(f"in_qkv_dw_{i}", sv["h"], dzqkv, "tn", tn=768)
    g["in_f"] = _mm(f"in_f_dw_{i}", sv["h"], dzf, "tn")
    dh = _mm(f"in_uv_dx_{i}", dzuv, w["in_uv"], "nt")
    dh = _mm(f"in_qkv_dx_{i}", dzqkv, w["in_qkv"], "nt", res=dh, tk=768)
    dx, dgm = _mm_norm_bwd(f"in_f_dx_{i}", dzf, w["in_f"], "nt", sv["x"], dx1, rp["norm_mix"][i][None], res=dh)
    g["norm_mix"] = dgm[0]
    g["ev_b_fgate"] = dbf[0, :B_HEADS]
    g["ev_q_norm"] = dqg.reshape(B_HEADS, B_HEAD_DIM).sum(axis=0)
    g["ev_k_norm"] = dkg.reshape(B_HEADS, B_HEAD_DIM).sum(axis=0)
    g["ev_v_norm"] = dvg[0]
    g["ev_w_spatial"] = jnp.where(c["tri"][None], dws, 0.0)
    g["ev_b_spatial"] = dbs.sum(axis=-1)
    return dx, g


def _s5_consts(o, rp, r):
    prm = (rp["od_a_re"][o], rp["od_a_im"][o], rp["od_log_dt"][o], rp["od_b_re"][o], rp["od_b_im"][o])
    (ab_re, ab_im, bb_re, bb_im), vjp = jax.vjp(_s5_discretize, *prm)
    bcat = jnp.concatenate([_s5_block_diag(bb_re.transpose(0, 2, 1)), _s5_block_diag(bb_im.transpose(0, 2, 1))], axis=2)
    c_re, c_im = rp["od_c_re"][o], rp["od_c_im"][o]
    ccat = jnp.concatenate([_s5_block_diag(c_re.transpose(0, 2, 1)), _s5_block_diag(-c_im.transpose(0, 2, 1))], axis=1)
    pw = tuple(_s5_tables(ab_re, ab_im, r))
    reps = (r // _S5_CHUNK, 1)
    down = tuple(jnp.tile(t[:_S5_CHUNK], reps) for t in pw)
    up = tuple(jnp.tile(jnp.flip(t[:_S5_CHUNK], axis=0), reps) for t in pw)
    return dict(vjp=vjp, bcat=bcat.astype(BF16), ccat=ccat.astype(BF16), fwd_tabs=pw + down, bwd_tabs=pw + down + up)


def _odd_fwd(i, x, w, rp, h_in=None):
    o = i // 2
    rows = x.shape[0]
    c = _s5_consts(o, rp, min(_S5_ROWS, rows))
    hb, hf = h_in if h_in is not None else _rmsnorm_fwd(f"mix_norm_{i}", x, rp["norm_mix"][i][None], [BF16, F32])
    ys, xin = _s5_fwd(f"s5_{i}", hb, c["bcat"], c["ccat"], *c["fwd_tabs"])

    def skip_gelu(yv, hv, dv):
        y = yv + dv * hv
        return y, _gelu(y)

    y, ge = _rows(f"s5_skip_gelu_{i}", skip_gelu, [ys, hf], [w["od_d"]], [(D_MODEL, F32), (D_MODEL, BF16)])
    gl = _mm(f"glu_{i}", ge, w["glu"])

    def glu_out(xv, gv, nv):
        x1v = xv + gv[:, :D_MODEL] * _sigmoid(gv[:, D_MODEL:])
        return x1v, (x1v * _rstd(x1v)) * nv

    x1, hn = _rows(f"glu_out_{i}", glu_out, [x, gl], [rp["norm_ffn"][i][None]], [(D_MODEL, F32), (D_MODEL, BF16)])
    return x1, hn, dict(x=x, hb=hb, hf=hf, xin=xin, y=y, ge=ge, gl=gl, c=c)


def _odd_bwd(i, dx1, sv, w, rp):
    o = i // 2
    c = sv["c"]
    g = {}

    def glu_bwd(dv, gv):
        ga, gb = gv[:, :D_MODEL], gv[:, D_MODEL:]
        sg = _sigmoid(gb)
        return (jnp.concatenate([dv * sg, (dv * ga) * (sg * (1.0 - sg))], axis=1),)

    dgl, = _rows(f"glu_out_bwd_{i}", glu_bwd, [dx1, sv["gl"]], [], [(2 * D_MODEL, BF16)])
    g["glu"] = _mm(f"glu_dw_{i}", sv["ge"], dgl, "tn")
    dge = _mm(f"glu_dx_{i}", dgl, w["glu"], "nt")

    def gelu_bwd(dv, yv, hv):
        dy = dv * _gelu_grad(yv)
        return dy, jnp.sum(dy * hv, axis=0, keepdims=True)

    dy, dd = _rows(f"s5_skip_gelu_bwd_{i}", gelu_bwd, [dge, sv["y"], sv["hf"]], [], [(D_MODEL, F32)],
                   accs=[(1, D_MODEL)])
    g["od_d"] = dd[0]
    du, db, dc, dar, dai = _s5_bwd(f"s5_bwd_{i}", sv["hb"], dy, sv["xin"], c["bcat"], c["ccat"], *c["bwd_tabs"])
    dab_re = dar.sum(axis=1).reshape(S5_GROUPS, S5_STATE)
    dab_im = dai.sum(axis=1).reshape(S5_GROUPS, S5_STATE)
    dbb_re = _s5_block_diag_extract(db[:, :, :S5_LANES], S5_GROUP_CH, S5_STATE).transpose(0, 2, 1)
    dbb_im = _s5_block_diag_extract(db[:, :, S5_LANES:], S5_GROUP_CH, S5_STATE).transpose(0, 2, 1)
    g["od_a_re"], g["od_a_im"], g["od_log_dt"], g["od_b_re"], g["od_b_im"] = c["vjp"]((dab_re, dab_im, dbb_re, dbb_im))
    g["od_c_re"] = _s5_block_diag_extract(dc[:, :S5_LANES, :], S5_STATE, S5_GROUP_CH).transpose(0, 2, 1)
    g["od_c_im"] = -_s5_block_diag_extract(dc[:, S5_LANES:, :], S5_STATE, S5_GROUP_CH).transpose(0, 2, 1)

    def norm_bwd(xv, duv, dyv, drv, gv, dv):
        dh = duv + dv * dyv
        r = _rstd(xv)
        xh = xv * r
        dhg = dh * gv
        dx = drv + r * (dhg - xh * jnp.mean(dhg * xh, axis=-1, keepdims=True))
        return dx, jnp.sum(dh * xh, axis=0, keepdims=True)

    dx, dgm = _rows(f"mix_norm_bwd_{i}", norm_bwd, [sv["x"], du, dy, dx1], [rp["norm_mix"][i][None], w["od_d"]],
                    [(D_MODEL, F32)], accs=[(1, D_MODEL)])
    g["norm_mix"] = dgm[0]
    return dx, g


def _local_step(x, p, target, lw, rp):
    saved = []
    h_next = None
    for i in range(DEPTH):
        x, hn, s_mix = (_even_fwd if i % 2 == 0 else _odd_fwd)(i, x, lw[i], rp, h_next)
        nxt = None
        if i + 1 < DEPTH:
            nxt = (rp["norm_mix"][i + 1][None], [BF16, F32] if (i + 1) % 2 else [BF16])
        x, h_next, s_ch = _channel_fwd(i, x, p[i], lw[i], rp, hn, nxt)
        saved.append((s_mix, s_ch))

    def loss_fn(yv, tv):
        diff = yv - tv
        return diff * (1.0 / D_MODEL), jnp.sum(diff * diff, axis=0, keepdims=True)

    dx, sq = _rows("loss", loss_fn, [x, target], [], [(D_MODEL, F32)], accs=[(1, D_MODEL)])
    loss = 0.5 * jnp.sum(sq) / D_MODEL
    grads = [None] * DEPTH
    for i in reversed(range(DEPTH)):
        s_mix, s_ch = saved[i]
        dx, g_ch = _channel_bwd(i, dx, s_ch, lw[i], rp)
        dx, g_mix = (_even_bwd if i % 2 == 0 else _odd_bwd)(i, dx, s_mix, lw[i], rp)
        grads[i] = {**g_ch, **g_mix}
    return loss, dx, grads


WEIGHT_ORDER = ["norm_mix", "norm_ffn", "ev_w_in", "ev_b_fgate", "ev_q_norm", "ev_k_norm", "ev_v_norm", "ev_w_spatial",
                "ev_b_spatial", "ev_w_out", "od_a_re", "od_a_im", "od_log_dt", "od_b_re", "od_b_im", "od_c_re",
                "od_c_im", "od_d", "od_w_glu", "ffn_w_up", "ffn_conv_w", "ffn_conv_b", "ffn_w_down", "ple_w_proj",
                "ple_w_gate"]
SHARD_AXIS = {"ev_w_in": 2, "ev_w_out": 1, "od_d": 1, "od_w_glu": 2, "ffn_w_up": 2, "ffn_conv_w": 2, "ffn_w_down": 1,
              "ple_w_proj": 2, "ple_w_gate": 1}
BIG_WEIGHTS = [n for n in WEIGHT_ORDER if n in SHARD_AXIS]
SMALL_WEIGHTS = [n for n in WEIGHT_ORDER if n not in SHARD_AXIS]
KEPT_F32 = ("od_d", "ffn_conv_w")
IN_UV, IN_QKV_END, IN_COLS = 2 * A_WIDTH, 2 * A_WIDTH + 3 * B_WIDTH, 2 * A_WIDTH + 3 * B_WIDTH + B_HEADS


def _layer_weights(i, full, rp):
    w = {}
    up, cw, cb = full["ffn_w_up"][i], full["ffn_conv_w"][i], rp["ffn_conv_b"][i][None]
    w["up_g"], w["up_u"] = up[:, :D_FF], up[:, D_FF:]
    w["cw_g"], w["cw_u"] = cw[:, :D_FF], cw[:, D_FF:]
    w["cb_g"], w["cb_u"] = cb[:, :D_FF], cb[:, D_FF:]
    w["down"], w["ple_proj"], w["ple_gate"] = full["ffn_w_down"][i], full["ple_w_proj"][i], full["ple_w_gate"][i]
    if i % 2 == 0:
        win, wout = full["ev_w_in"][i // 2], full["ev_w_out"][i // 2]
        w["in_uv"], w["in_qkv"] = win[:, :IN_UV], win[:, IN_UV:IN_QKV_END]
        w["in_f"] = jnp.pad(win[:, IN_QKV_END:], ((0, 0), (0, LANES - B_HEADS)))
        w["out_a"], w["out_b"] = wout[:A_WIDTH], wout[A_WIDTH:]
    else:
        w["od_d"], w["glu"] = full["od_d"][i // 2][None], full["od_w_glu"][i // 2]
    return w


def _full_grads(grads):
    ev, od = [grads[i] for i in range(0, DEPTH, 2)], [grads[i] for i in range(1, DEPTH, 2)]
    out = {
        "norm_mix": jnp.stack([g["norm_mix"] for g in grads]), "norm_ffn": jnp.stack([g["norm_ffn"] for g in grads]),
        "ev_w_in": jnp.stack([jnp.concatenate([g["in_uv"], g["in_qkv"], g["in_f"][:, :B_HEADS]], axis=1) for g in ev]),
        "ev_w_out": jnp.stack([jnp.concatenate([g["out_a"], g["out_b"]], axis=0) for g in ev]),
        "od_w_glu": jnp.stack([g["glu"] for g in od]),
        "ffn_w_up": jnp.stack([jnp.concatenate([g["up_g"], g["up_u"]], axis=1) for g in grads]),
        "ffn_conv_w": jnp.stack([jnp.concatenate([g["cw_g"], g["cw_u"]], axis=1) for g in grads]),
        "ffn_conv_b": jnp.stack([g["conv_b"] for g in grads]),
        "ffn_w_down": jnp.stack([g["down"] for g in grads]),
        "ple_w_proj": jnp.stack([g["ple_proj"] for g in grads]),
        "ple_w_gate": jnp.stack([g["ple_gate"] for g in grads]),
    }
    for n in ("ev_b_fgate", "ev_q_norm", "ev_k_norm", "ev_v_norm", "ev_w_spatial", "ev_b_spatial"):
        out[n] = jnp.stack([g[n] for g in ev])
    for n in ("od_a_re", "od_a_im", "od_log_dt", "od_b_re", "od_b_im", "od_c_re", "od_c_im", "od_d"):
        out[n] = jnp.stack([g[n] for g in od])
    return out


def _pack(arrs, row_multiple):
    flat = jnp.concatenate([a.reshape(-1) for a in arrs])
    rows = -(-flat.shape[0] // (PACK_W * row_multiple)) * row_multiple
    return jnp.pad(flat, (0, rows * PACK_W - flat.shape[0])).reshape(rows, PACK_W)


def _unpack(buf, shapes):
    flat = buf.reshape(-1)
    out, at = [], 0
    for s in shapes:
        n = int(np.prod(s))
        out.append(flat[at:at + n].reshape(s))
        at += n
    return out


def _shard(name, a, k):
    ax = SHARD_AXIS[name]
    n = a.shape[ax] // N_CHIPS
    return lax.slice_in_dim(a, k * n, (k + 1) * n, axis=ax)


_ANY = pl.BlockSpec(memory_space=pl.ANY)


def _mesh_pos():
    return lax.axis_index("x"), lax.axis_index("y"), lax.axis_index("c")


def _other_chips(x, y):
    return [(1 - x, y), (x, 1 - y), (1 - x, 1 - y)]


def _gather_shards(name, shards):
    n = len(shards)

    def body(*refs):
        ins, outs = refs[:n], refs[n:2 * n]
        send_sems, recv_sems, local_sems = refs[2 * n:]
        x, y, c = _mesh_pos()
        sibling = (x, y, 1 - c)
        chips = _other_chips(x, y)

        def part(a, k, hc):
            half = shards[a].shape[0] // 2
            return outs[a].at[k, pl.ds(hc * half, half), :]

        def copy(sem, src, dst, to):
            return pltpu.make_async_remote_copy(src_ref=src, dst_ref=dst, send_sem=send_sems.at[sem],
                                                recv_sem=recv_sems.at[sem], device_id=to, device_id_type=MESH)

        local, sent, passed = [], [], []
        for a in range(n):
            half = shards[a].shape[0] // 2
            local.append(pltpu.make_async_copy(ins[a], outs[a].at[2 * x + y], local_sems.at[a]))
            local[-1].start()
            for j, (cx, cy) in enumerate(chips):
                sent.append(copy(6 * a + j, ins[a].at[pl.ds(c * half, half), :], part(a, 2 * x + y, c), (cx, cy, c)))
                sent[-1].start()
        for a in range(n):
            for j, (cx, cy) in enumerate(chips):
                blk = part(a, 2 * cx + cy, c)
                copy(6 * a + j, blk, blk, (cx, cy, c)).wait_recv()
                passed.append(copy(6 * a + 3 + j, blk, blk, sibling))
                passed[-1].start()
        for a in range(n):
            for j, (cx, cy) in enumerate(chips):
                blk = part(a, 2 * cx + cy, 1 - c)
                copy(6 * a + 3 + j, blk, blk, sibling).wait_recv()
        for cp in sent + passed:
            cp.wait_send()
        for cp in local:
            cp.wait()

    return pl.pallas_call(
        body, out_shape=[jax.ShapeDtypeStruct((N_CHIPS,) + s.shape, s.dtype) for s in shards],
        in_specs=[_ANY] * n, out_specs=[_ANY] * n,
        scratch_shapes=[pltpu.SemaphoreType.DMA((6 * n,)), pltpu.SemaphoreType.DMA((6 * n,)),
                        pltpu.SemaphoreType.DMA((n,))],
        name=name,
    )(*shards)


def _swap_halves(name, arrs):
    n = len(arrs)

    def body(*refs):
        ins, outs = refs[:n], refs[n:2 * n]
        send_sems, recv_sems = refs[2 * n:]
        x, y, c = _mesh_pos()
        cps = []
        for a in range(n):
            half = arrs[a].shape[1] // 2
            cps.append(pltpu.make_async_remote_copy(
                src_ref=ins[a].at[:, pl.ds((1 - c) * half, half), :], dst_ref=outs[a], send_sem=send_sems.at[a],
                recv_sem=recv_sems.at[a], device_id=(x, y, 1 - c), device_id_type=MESH))
            cps[-1].start()
        for cp in cps:
            cp.wait()

    return pl.pallas_call(
        body, out_shape=[jax.ShapeDtypeStruct((a.shape[0], a.shape[1] // 2, a.shape[2]), a.dtype) for a in arrs],
        in_specs=[_ANY] * n, out_specs=[_ANY] * n,
        scratch_shapes=[pltpu.SemaphoreType.DMA((n,)), pltpu.SemaphoreType.DMA((n,))], name=name,
    )(*arrs)


def _send_to_owner_chips(name, arrs):
    n = len(arrs)

    def body(*refs):
        ins, outs = refs[:n], refs[n:2 * n]
        send_sems, recv_sems = refs[2 * n:]
        x, y, c = _mesh_pos()
        cps = []
        for a in range(n):
            for j, (cx, cy) in enumerate(_other_chips(x, y)):
                cps.append(pltpu.make_async_remote_copy(
                    src_ref=ins[a].at[2 * cx + cy], dst_ref=outs[a].at[j], send_sem=send_sems.at[3 * a + j],
                    recv_sem=recv_sems.at[3 * a + j], device_id=(cx, cy, c), device_id_type=MESH))
                cps[-1].start()
        for cp in cps:
            cp.wait()

    return pl.pallas_call(
        body, out_shape=[jax.ShapeDtypeStruct((3,) + a.shape[1:], a.dtype) for a in arrs],
        in_specs=[_ANY] * n, out_specs=[_ANY] * n,
        scratch_shapes=[pltpu.SemaphoreType.DMA((3 * n,)), pltpu.SemaphoreType.DMA((3 * n,))], name=name,
    )(*arrs)


def _swap_with_sibling(name, arrs):
    n = len(arrs)

    def body(*refs):
        ins, outs = refs[:n], refs[n:2 * n]
        send_sems, recv_sems = refs[2 * n:]
        x, y, c = _mesh_pos()
        cps = []
        for a in range(n):
            cps.append(pltpu.make_async_remote_copy(
                src_ref=ins[a], dst_ref=outs[a], send_sem=send_sems.at[a], recv_sem=recv_sems.at[a],
                device_id=(x, y, 1 - c), device_id_type=MESH))
            cps[-1].start()
        for cp in cps:
            cp.wait()

    return pl.pallas_call(
        body, out_shape=[jax.ShapeDtypeStruct(a.shape, a.dtype) for a in arrs],
        in_specs=[_ANY] * n, out_specs=[_ANY] * n,
        scratch_shapes=[pltpu.SemaphoreType.DMA((n,)), pltpu.SemaphoreType.DMA((n,))], name=name,
    )(*arrs)


def _all_gather_devices(name, a):
    rows, w = a.shape

    def body(a_ref, out_ref, send_sems, recv_sems, local_sem):
        x, y, c = _mesh_pos()
        me, sibling = (x, y, c), (x, y, 1 - c)
        chips = _other_chips(x, y)

        def slot(px, py, pc):
            return out_ref.at[4 * px + 2 * py + pc]

        def copy(sem, block, to, src=None):
            return pltpu.make_async_remote_copy(src_ref=slot(*block) if src is None else src, dst_ref=slot(*block),
                                                send_sem=send_sems.at[sem], recv_sem=recv_sems.at[sem], device_id=to,
                                                device_id_type=MESH)

        mine = pltpu.make_async_copy(a_ref, slot(*me), local_sem)
        mine.start()
        first = [copy(0, me, sibling, src=a_ref)]
        first += [copy(1 + j, me, (*chip, c), src=a_ref) for j, chip in enumerate(chips)]
        for cp in first:
            cp.start()
        passed = [copy(4 + j, (*chip, c), sibling) for j, chip in enumerate(chips)]
        for j, chip in enumerate(chips):
            copy(1 + j, (*chip, c), me).wait_recv()
            passed[j].start()
        copy(0, sibling, me).wait_recv()
        for j, chip in enumerate(chips):
            copy(4 + j, (*chip, 1 - c), me).wait_recv()
        for cp in first + passed:
            cp.wait_send()
        mine.wait()

    return pl.pallas_call(
        body, out_shape=jax.ShapeDtypeStruct((8, rows, w), a.dtype), in_specs=[_ANY], out_specs=_ANY,
        scratch_shapes=[pltpu.SemaphoreType.DMA((7,)), pltpu.SemaphoreType.DMA((7,)), pltpu.SemaphoreType.DMA],
        name=name,
    )(a)


_PACK_TILE = 256


def _sum_rows(name, arrs):
    def fn(*vals):
        tot = vals[0]
        for v in vals[1:]:
            tot = tot + v
        return (tot,)

    return _rows(name, fn, list(arrs), [], [(arrs[0].shape[1], F32)], tile=_PACK_TILE)[0]


def _adam_math(wv, gv, mv, vv):
    m2 = ADAM_B1 * mv + (1.0 - ADAM_B1) * gv
    v2 = ADAM_B2 * vv + (1.0 - ADAM_B2) * (gv * gv)
    m_hat = m2 / (1.0 - ADAM_B1 ** ADAM_STEP)
    v_hat = v2 / (1.0 - ADAM_B2 ** ADAM_STEP)
    delta = -ADAM_LR * (m_hat / (jnp.sqrt(v_hat) + ADAM_EPS) + ADAM_WD * wv)
    return delta, m2, v2


def _adamw(name, w, g, m, v):
    return _rows(name, _adam_math, [w, g, m, v], [], [(w.shape[1], F32)] * 3, tile=_PACK_TILE)


_INPUT_ORDER = (["x", "p"] + WEIGHT_ORDER + ["loss_target"] + ["m_" + n for n in WEIGHT_ORDER]
                + ["v_" + n for n in WEIGHT_ORDER])


_SUM_ROWS = 128


def _pair_sum(name, g, got, core):
    nk, rows, w = g.shape
    half = rows // 2
    nt = half // _SUM_ROWS

    def body(c_ref, g_ref, got_ref, o_ref, ob_ref):
        tot = g_ref[...] + got_ref[...]
        o_ref[...] = tot
        ob_ref[...] = tot.astype(BF16)

    spec = pl.BlockSpec((None, _SUM_ROWS, w), lambda k, i, c: (k, i, 0))
    grid_spec = pltpu.PrefetchScalarGridSpec(
        num_scalar_prefetch=1, grid=(nk, nt),
        in_specs=[pl.BlockSpec((None, _SUM_ROWS, w), lambda k, i, c: (k, c[0] * nt + i, 0)), spec],
        out_specs=[spec, spec])
    return pl.pallas_call(
        body, grid_spec=grid_spec, name=name, compiler_params=_cparams(("parallel", "parallel")),
        out_shape=[jax.ShapeDtypeStruct((nk, half, w), F32), jax.ShapeDtypeStruct((nk, half, w), BF16)])(core, g, got)


def _owner_sum(name, pair, owed, chip):
    _, half, w = pair.shape

    def body(k_ref, p_ref, a_ref, b_ref, c_ref, o_ref):
        o_ref[...] = ((p_ref[...] + a_ref[...].astype(F32)) + b_ref[...].astype(F32)) + c_ref[...].astype(F32)

    def owed_spec(j):
        return pl.BlockSpec((None, _SUM_ROWS, w), lambda i, k: (j, i, 0))

    grid_spec = pltpu.PrefetchScalarGridSpec(
        num_scalar_prefetch=1, grid=(half // _SUM_ROWS,),
        in_specs=[pl.BlockSpec((None, _SUM_ROWS, w), lambda i, k: (k[0], i, 0)), owed_spec(0), owed_spec(1),
                  owed_spec(2)],
        out_specs=pl.BlockSpec((_SUM_ROWS, w), lambda i, k: (i, 0)))
    return pl.pallas_call(body, grid_spec=grid_spec, out_shape=jax.ShapeDtypeStruct((half, w), F32), name=name,
                          compiler_params=_cparams(("parallel",)))(chip, pair, owed, owed, owed)


def _adamw_halves(name, w, mine, other, m, v, core):
    rows, wd = w.shape
    nh = (rows // 2) // _SUM_ROWS

    def body(c_ref, w_ref, a_ref, b_ref, m_ref, v_ref, g_ref, d_ref, m2_ref, v2_ref):
        own = (pl.program_id(0) // nh) == c_ref[0]
        g = jnp.where(own, a_ref[...], b_ref[...])
        g_ref[...] = g
        d_ref[...], m2_ref[...], v2_ref[...] = _adam_math(w_ref[...], g, m_ref[...], v_ref[...])

    full = pl.BlockSpec((_SUM_ROWS, wd), lambda i, c: (i, 0))
    part = pl.BlockSpec((_SUM_ROWS, wd), lambda i, c: (lax.rem(i, nh), 0))
    grid_spec = pltpu.PrefetchScalarGridSpec(num_scalar_prefetch=1, grid=(2 * nh,),
                                             in_specs=[full, part, part, full, full], out_specs=[full] * 4)
    return pl.pallas_call(body, grid_spec=grid_spec, out_shape=[jax.ShapeDtypeStruct((rows, wd), F32)] * 4, name=name,
                          compiler_params=_cparams(("parallel",)))(core, w, mine, other, m, v)


MATRIX_WEIGHTS = [n for n in BIG_WEIGHTS if n not in KEPT_F32]
TINY_SHARDED = [n for n in BIG_WEIGHTS if n in KEPT_F32]


def _as_rows(a):
    return a.reshape(-1, a.shape[-1])


def _owner_major(grads):
    ev, od = [grads[i] for i in range(0, DEPTH, 2)], [grads[i] for i in range(1, DEPTH, 2)]

    def cols(m, k, n):
        w = m.shape[1] // n
        return m[:, k * w:(k + 1) * w]

    def rows(m, k, n):
        r = m.shape[0] // n
        return m[k * r:(k + 1) * r]

    w_in = [jnp.concatenate([g["in_uv"], g["in_qkv"], g["in_f"][:, :B_HEADS]], axis=1) for g in ev]
    per_chip = {
        "ev_w_in": lambda k: [cols(m, k, N_CHIPS) for m in w_in],
        "ev_w_out": lambda k: [rows(g["out_a"] if k < 2 else g["out_b"], k % 2, 2) for g in ev],
        "od_w_glu": lambda k: [cols(g["glu"], k, N_CHIPS) for g in od],
        "ffn_w_up": lambda k: [cols(g["up_g"] if k < 2 else g["up_u"], k % 2, 2) for g in grads],
        "ffn_w_down": lambda k: [rows(g["down"], k, N_CHIPS) for g in grads],
        "ple_w_proj": lambda k: [cols(g["ple_proj"], k, N_CHIPS) for g in grads],
        "ple_w_gate": lambda k: [rows(g["ple_gate"], k, N_CHIPS) for g in grads],
    }
    return {n: jnp.stack([jnp.concatenate(per_chip[n](k), axis=0) for k in range(N_CHIPS)]) for n in MATRIX_WEIGHTS}


def _small_grads(grads):
    ev, od = [grads[i] for i in range(0, DEPTH, 2)], [grads[i] for i in range(1, DEPTH, 2)]
    out = {"norm_mix": jnp.stack([g["norm_mix"] for g in grads]), "norm_ffn": jnp.stack([g["norm_ffn"] for g in grads]),
           "ffn_conv_w": jnp.stack([jnp.concatenate([g["cw_g"], g["cw_u"]], axis=1) for g in grads]),
           "ffn_conv_b": jnp.stack([g["conv_b"] for g in grads])}
    for n in ("ev_b_fgate", "ev_q_norm", "ev_k_norm", "ev_v_norm", "ev_w_spatial", "ev_b_spatial"):
        out[n] = jnp.stack([g[n] for g in ev])
    for n in ("od_a_re", "od_a_im", "od_log_dt", "od_b_re", "od_b_im", "od_c_re", "od_c_im", "od_d"):
        out[n] = jnp.stack([g[n] for g in od])
    return out


def _step(a):
    xi, yi, ci = _mesh_pos()
    chip = 2 * xi + yi
    core_arr, chip_arr = ci.astype(jnp.int32).reshape(1), chip.astype(jnp.int32).reshape(1)
    rp = {n: a[n] for n in SMALL_WEIGHTS}

    tiny = _pack([a[n] for n in TINY_SHARDED], 32)
    gathered = _gather_shards("gather_weights", [_as_rows(a[n]).astype(BF16) for n in MATRIX_WEIGHTS] + [tiny])
    full = {}
    for n, g in zip(MATRIX_WEIGHTS, gathered):
        full[n] = jnp.concatenate([g[k].reshape(a[n].shape) for k in range(N_CHIPS)], axis=SHARD_AXIS[n])
    tiny_parts = [_unpack(gathered[-1][k], [a[n].shape for n in TINY_SHARDED]) for k in range(N_CHIPS)]
    for idx, n in enumerate(TINY_SHARDED):
        full[n] = jnp.concatenate([tiny_parts[k][idx] for k in range(N_CHIPS)], axis=SHARD_AXIS[n])
    lw = [_layer_weights(i, full, rp) for i in range(DEPTH)]

    loss_local, grad_x, grads = _local_step(a["x"][0], a["p"][:, 0], a["loss_target"][0], lw, rp)
    loss = lax.psum(loss_local, ("x", "y", "c"))

    contrib = _owner_major(grads)
    mats = [contrib[n] for n in MATRIX_WEIGHTS]
    got = _swap_halves("grad_pair_swap", mats)
    pair = [_pair_sum(f"grad_pair_sum_{n}", g, h, core_arr) for n, g, h in zip(MATRIX_WEIGHTS, mats, got)]
    owed = _send_to_owner_chips("grad_to_owner", [pb for _, pb in pair])
    mine = [_owner_sum(f"grad_owner_sum_{n}", p, o, chip_arr) for n, (p, _), o in zip(MATRIX_WEIGHTS, pair, owed)]
    theirs = _swap_with_sibling("grad_half_swap", mine)

    small_names = SMALL_WEIGHTS + TINY_SHARDED
    sg = _small_grads(grads)
    everyone = _all_gather_devices("small_grad_gather", _pack([sg[n] for n in small_names], _PACK_TILE))
    g_small = _sum_rows("small_grad_sum", [everyone[d] for d in range(8)])
    small_full = dict(zip(small_names, _unpack(g_small, [sg[n].shape for n in small_names])))

    out = {}
    for n, own_half, other_half in zip(MATRIX_WEIGHTS, mine, theirs):
        shape = a[n].shape
        g2d, delta, m2, v2 = _adamw_halves(f"adamw_{n}", _as_rows(a[n]), own_half, other_half,
                                           _as_rows(a["m_" + n]), _as_rows(a["v_" + n]), core_arr)
        for kind, val in (("grad", g2d), ("delta", delta), ("new_m", m2), ("new_v", v2)):
            out[kind + "_" + n] = val.reshape(shape)
    g_sm = {n: small_full[n] for n in SMALL_WEIGHTS}
    for n in TINY_SHARDED:
        width = a[n].shape[SHARD_AXIS[n]]
        g_sm[n] = lax.dynamic_slice_in_dim(small_full[n], chip * width, width, axis=SHARD_AXIS[n])
    shapes = [a[n].shape for n in small_names]
    w, m, v = (_pack([a[pre + n] for n in small_names], _PACK_TILE) for pre in ("", "m_", "v_"))
    g = _pack([g_sm[n] for n in small_names], _PACK_TILE)
    delta, m2, v2 = _adamw("adamw_small", w, g, m, v)
    for kind, buf in (("delta", delta), ("new_m", m2), ("new_v", v2)):
        for n, val in zip(small_names, _unpack(buf, shapes)):
            out[kind + "_" + n] = val
    for n in small_names:
        out["grad_" + n] = g_sm[n]
    res = [loss, grad_x[None]]
    for kind in ("grad", "delta", "new_m", "new_v"):
        res += [out[kind + "_" + n] for n in WEIGHT_ORDER]
    return tuple(res)


def kernel(x, p, norm_mix, norm_ffn, ev_w_in, ev_b_fgate, ev_q_norm, ev_k_norm, ev_v_norm, ev_w_spatial, ev_b_spatial, ev_w_out, od_a_re, od_a_im, od_log_dt, od_b_re, od_b_im, od_c_re, od_c_im, od_d, od_w_glu, ffn_w_up, ffn_conv_w, ffn_conv_b, ffn_w_down, ple_w_proj, ple_w_gate, loss_target, m_norm_mix, m_norm_ffn, m_ev_w_in, m_ev_b_fgate, m_ev_q_norm, m_ev_k_norm, m_ev_v_norm, m_ev_w_spatial, m_ev_b_spatial, m_ev_w_out, m_od_a_re, m_od_a_im, m_od_log_dt, m_od_b_re, m_od_b_im, m_od_c_re, m_od_c_im, m_od_d, m_od_w_glu, m_ffn_w_up, m_ffn_conv_w, m_ffn_conv_b, m_ffn_w_down, m_ple_w_proj, m_ple_w_gate, v_norm_mix, v_norm_ffn, v_ev_w_in, v_ev_b_fgate, v_ev_q_norm, v_ev_k_norm, v_ev_v_norm, v_ev_w_spatial, v_ev_b_spatial, v_ev_w_out, v_od_a_re, v_od_a_im, v_od_log_dt, v_od_b_re, v_od_b_im, v_od_c_re, v_od_c_im, v_od_d, v_od_w_glu, v_ffn_w_up, v_ffn_conv_w, v_ffn_conv_b, v_ffn_w_down, v_ple_w_proj, v_ple_w_gate):
    args = (x, p, norm_mix, norm_ffn, ev_w_in, ev_b_fgate, ev_q_norm, ev_k_norm, ev_v_norm, ev_w_spatial, ev_b_spatial, ev_w_out, od_a_re, od_a_im, od_log_dt, od_b_re, od_b_im, od_c_re, od_c_im, od_d, od_w_glu, ffn_w_up, ffn_conv_w, ffn_conv_b, ffn_w_down, ple_w_proj, ple_w_gate, loss_target, m_norm_mix, m_norm_ffn, m_ev_w_in, m_ev_b_fgate, m_ev_q_norm, m_ev_k_norm, m_ev_v_norm, m_ev_w_spatial, m_ev_b_spatial, m_ev_w_out, m_od_a_re, m_od_a_im, m_od_log_dt, m_od_b_re, m_od_b_im, m_od_c_re, m_od_c_im, m_od_d, m_od_w_glu, m_ffn_w_up, m_ffn_conv_w, m_ffn_conv_b, m_ffn_w_down, m_ple_w_proj, m_ple_w_gate, v_norm_mix, v_norm_ffn, v_ev_w_in, v_ev_b_fgate, v_ev_q_norm, v_ev_k_norm, v_ev_v_norm, v_ev_w_spatial, v_ev_b_spatial, v_ev_w_out, v_od_a_re, v_od_a_im, v_od_log_dt, v_od_b_re, v_od_b_im, v_od_c_re, v_od_c_im, v_od_d, v_od_w_glu, v_ffn_w_up, v_ffn_conv_w, v_ffn_conv_b, v_ffn_w_down, v_ple_w_proj, v_ple_w_gate)
    return _step(dict(zip(_INPUT_ORDER, args)))
```

```python
import functools
import math

import jax
import jax.numpy as jnp
import numpy as np
from jax import lax
from jax.experimental import pallas as pl
from jax.experimental.pallas import tpu as pltpu

F32 = jnp.float32
BF16 = jnp.bfloat16
MESH = pl.DeviceIdType.MESH

V7X_VMEM_LIMIT_BYTES = 56 * 1024 * 1024
LANES = 128

D_MODEL = 1024
DEPTH = 4
A_GROUPS = 4
A_CHUNK = 128
A_WIDTH = 512
B_HEADS = 8
B_HEAD_DIM = 64
B_WIDTH = 512
S5_GROUP_CH = 16
S5_GROUPS = 64
S5_STATE = 64
S5_BLOCKS = 8
S5_LANES = 512
D_FF = 2816
PLE_DIM = 256
EPS = 1e-6
NEG_INF = -1e30

ADAM_LR = 0.001
ADAM_B1 = 0.9
ADAM_B2 = 0.999
ADAM_EPS = 1e-08
ADAM_WD = 0.01
ADAM_STEP = 10

N_CHIPS = 4
PACK_W = 1024


def _cparams(sem):
    return pltpu.CompilerParams(dimension_semantics=sem, vmem_limit_bytes=V7X_VMEM_LIMIT_BYTES)


def _pick(n, target):
    if n <= target:
        return n
    t = (target // LANES) * LANES
    while t >= LANES:
        if n % t == 0:
            return t
        t -= LANES
    return n


_GELU_K = 0.7978845608028654
_GELU_C = 0.044715


def _gelu(x):
    return x * (0.5 * (1.0 + jnp.tanh(_GELU_K * (x + _GELU_C * (x * x * x)))))


def _gelu_grad(x):
    x2 = x * x
    t = jnp.tanh(_GELU_K * (x + _GELU_C * (x * x2)))
    return 0.5 * (1.0 + t) + (0.5 * x) * (1.0 - t * t) * (_GELU_K * (1.0 + (3.0 * _GELU_C) * x2))


def _sigmoid(x):
    return 0.5 * jnp.tanh(0.5 * x) + 0.5


def _log_sigmoid(x):
    return -(jnp.maximum(-x, 0.0) + jnp.log(1.0 + jnp.exp(-jnp.abs(x))))


def _rstd(x):
    return lax.rsqrt(jnp.mean(x * x, axis=-1, keepdims=True) + EPS)


def _rows(name, fn, row_ins, full_ins, outs, accs=(), tile=256):
    rows = row_ins[0].shape[0]
    r = min(tile, rows)
    n = rows // r
    n_in = len(row_ins) + len(full_ins)
    n_out = len(outs)

    def body(*refs):
        res = fn(*[ref[...] for ref in refs[:n_in]])
        for ref, v in zip(refs[n_in:n_in + n_out], res[:n_out]):
            ref[...] = v.astype(ref.dtype)
        acc_refs = refs[n_in + n_out:]
        if acc_refs:
            @pl.when(pl.program_id(0) == 0)
            def _():
                for ref in acc_refs:
                    ref[...] = jnp.zeros(ref.shape, ref.dtype)

            for ref, v in zip(acc_refs, res[n_out:]):
                ref[...] += v

    in_specs = [pl.BlockSpec((r, a.shape[1]), lambda i: (i, 0)) for a in row_ins]
    in_specs += [pl.BlockSpec(a.shape, lambda i, nd=a.ndim: (0,) * nd) for a in full_ins]
    out_shape = [jax.ShapeDtypeStruct((rows, w), dt) for (w, dt) in outs]
    out_shape += [jax.ShapeDtypeStruct(s, F32) for s in accs]
    out_specs = [pl.BlockSpec((r, w), lambda i: (i, 0)) for (w, dt) in outs]
    out_specs += [pl.BlockSpec(s, lambda i, nd=len(s): (0,) * nd) for s in accs]
    return pl.pallas_call(
        body, grid=(n,), in_specs=in_specs, out_specs=out_specs, out_shape=out_shape, name=name,
        compiler_params=_cparams(("arbitrary",) if accs else ("parallel",)),
    )(*row_ins, *full_ins)


_DOT_DIMS = {"nn": (((1,), (0,)), ((), ())), "nt": (((1,), (1,)), ((), ())), "tn": (((0,), (0,)), ((), ()))}


def _mm(name, a, b, mode="nn", out_dtype=F32, res=None, tm=1024, tn=1024, tk=1024, norm_gain=None):
    if mode == "nn":
        (m, k), (k2, n) = a.shape, b.shape
    elif mode == "nt":
        (m, k), (n, k2) = a.shape, b.shape
    else:
        (k, m), (k2, n) = a.shape, b.shape
    assert k == k2, (name, a.shape, b.shape, mode)
    tm, tn, tk = _pick(m, tm), _pick(n, tn), _pick(k, tk)
    nk = k // tk
    dims = _DOT_DIMS[mode]
    has_res = res is not None
    has_norm = norm_gain is not None
    assert not has_norm or tn == n, (name, tn, n)
    n_in = 2 + has_res + has_norm

    def body(*refs):
        a_ref, b_ref = refs[0], refs[1]
        res_ref = refs[2] if has_res else None
        gain_ref = refs[n_in - 1] if has_norm else None
        o_ref = refs[n_in]
        h_ref = refs[n_in + 1] if has_norm else None

        def finish(tot):
            if has_res:
                tot = res_ref[...] + tot
            o_ref[...] = tot.astype(o_ref.dtype)
            if has_norm:
                h_ref[...] = ((tot * _rstd(tot)) * gain_ref[...]).astype(h_ref.dtype)

        prod = lax.dot_general(a_ref[...].astype(BF16), b_ref[...].astype(BF16), dims, preferred_element_type=F32)
        if nk == 1:
            finish(prod)
            return
        acc = refs[-1]
        kk = pl.program_id(2)

        @pl.when(kk == 0)
        def _():
            acc[...] = prod

        @pl.when(kk > 0)
        def _():
            acc[...] += prod

        @pl.when(kk == nk - 1)
        def _():
            finish(acc[...])

    if mode == "nn":
        a_spec = pl.BlockSpec((tm, tk), lambda i, j, kk: (i, kk))
        b_spec = pl.BlockSpec((tk, tn), lambda i, j, kk: (kk, j))
    elif mode == "nt":
        a_spec = pl.BlockSpec((tm, tk), lambda i, j, kk: (i, kk))
        b_spec = pl.BlockSpec((tn, tk), lambda i, j, kk: (j, kk))
    else:
        a_spec = pl.BlockSpec((tk, tm), lambda i, j, kk: (kk, i))
        b_spec = pl.BlockSpec((tk, tn), lambda i, j, kk: (kk, j))
    o_spec = pl.BlockSpec((tm, tn), lambda i, j, kk: (i, j))
    in_specs = [a_spec, b_spec] + ([o_spec] if has_res else [])
    in_specs += [pl.BlockSpec((1, tn), lambda i, j, kk: (0, j))] if has_norm else []
    args = (a, b) + ((res,) if has_res else ()) + ((norm_gain,) if has_norm else ())
    out_shape = jax.ShapeDtypeStruct((m, n), out_dtype)
    return pl.pallas_call(
        body, grid=(m // tm, n // tn, nk), in_specs=in_specs, out_specs=[o_spec, o_spec] if has_norm else o_spec,
        out_shape=[out_shape, jax.ShapeDtypeStruct((m, n), BF16)] if has_norm else out_shape, name=name,
        scratch_shapes=[pltpu.VMEM((tm, tn), F32)] if nk > 1 else [],
        compiler_params=_cparams(("parallel", "parallel", "arbitrary")),
    )(*args)


def _mm_norm_bwd(name, a, b, mode, x, dres, gain, res=None, tm=512, tk=1024):
    if mode == "nn":
        (m, k), (k2, n) = a.shape, b.shape
    else:
        (m, k), (n, k2) = a.shape, b.shape
    assert k == k2, (name, a.shape, b.shape, mode)
    tm, tk = _pick(m, tm), _pick(k, tk)
    nk = k // tk
    dims = _DOT_DIMS[mode]
    has_res = res is not None
    n_in = 5 + has_res

    def body(*refs):
        a_ref, b_ref = refs[0], refs[1]
        res_ref = refs[2] if has_res else None
        x_ref, dres_ref, gain_ref = refs[n_in - 3:n_in]
        o_ref, dg_ref = refs[n_in], refs[n_in + 1]

        def finish(d):
            if has_res:
                d = res_ref[...] + d
            xv = x_ref[...]
            r = _rstd(xv)
            xh = xv * r
            dyg = d * gain_ref[...]
            o_ref[...] = dres_ref[...] + r * (dyg - xh * jnp.mean(dyg * xh, axis=-1, keepdims=True))
            part = jnp.sum(d * xh, axis=0, keepdims=True)

            @pl.when(pl.program_id(0) == 0)
            def _():
                dg_ref[...] = part

            @pl.when(pl.program_id(0) > 0)
            def _():
                dg_ref[...] += part

        prod = lax.dot_general(a_ref[...].astype(BF16), b_ref[...].astype(BF16), dims, preferred_element_type=F32)
        if nk == 1:
            finish(prod)
            return
        acc = refs[-1]
        kk = pl.program_id(1)

        @pl.when(kk == 0)
        def _():
            acc[...] = prod

        @pl.when(kk > 0)
        def _():
            acc[...] += prod

        @pl.when(kk == nk - 1)
        def _():
            finish(acc[...])

    a_spec = pl.BlockSpec((tm, tk), lambda i, kk: (i, kk))
    b_spec = pl.BlockSpec((tk, n), lambda i, kk: (kk, 0)) if mode == "nn" else pl.BlockSpec((n, tk), lambda i, kk: (0, kk))
    row_spec = pl.BlockSpec((tm, n), lambda i, kk: (i, 0))
    vec_spec = pl.BlockSpec((1, n), lambda i, kk: (0, 0))
    in_specs = [a_spec, b_spec] + ([row_spec] if has_res else []) + [row_spec, row_spec, vec_spec]
    args = (a, b) + ((res,) if has_res else ()) + (x, dres, gain)
    return pl.pallas_call(
        body, grid=(m // tm, nk), in_specs=in_specs, out_specs=[row_spec, vec_spec],
        out_shape=[jax.ShapeDtypeStruct((m, n), F32), jax.ShapeDtypeStruct((1, n), F32)], name=name,
        scratch_shapes=[pltpu.VMEM((tm, n), F32)] if nk > 1 else [],
        compiler_params=_cparams(("arbitrary", "arbitrary")),
    )(*args)


def _rmsnorm_fwd(name, x, g, outs):
    def fn(xv, gv):
        y = (xv * _rstd(xv)) * gv
        return tuple(y for _ in outs)

    return _rows(name, fn, [x], [g], [(x.shape[1], dt) for dt in outs])


_CONV_ROWS = 256
_CONV_COLS = 1408


def _conv_taps(h_ref, halo_ref, first):
    h = h_ref[...]
    rows = h.shape[0]
    row = lax.broadcasted_iota(jnp.int32, (rows, 1), 0)
    keep = jnp.where(first, 0.0, 1.0)
    m1 = halo_ref[7:8, :] * keep
    m2 = halo_ref[6:7, :] * keep
    p1 = jnp.where(row == 0, m1, pltpu.roll(h, 1, 0))
    p2 = jnp.where(row == 0, m2, jnp.where(row == 1, m1, pltpu.roll(h, 2, 0)))
    return h, p1, p2


def _conv_specs(rows, r, cw):
    tile = pl.BlockSpec((r, cw), lambda j, i: (i, j))
    halo = pl.BlockSpec((8, cw), lambda j, i: (jnp.maximum(i * (r // 8) - 1, 0), j))
    vec3 = pl.BlockSpec((3, cw), lambda j, i: (0, j))
    vec1 = pl.BlockSpec((1, cw), lambda j, i: (0, j))
    return tile, halo, vec3, vec1


def _convffn_fwd(name, hg, hu, wg, wu, bg, bu):
    rows, f = hg.shape
    r, cw = min(_CONV_ROWS, rows), _pick(f, _CONV_COLS)

    def body(hg_ref, hgh_ref, hu_ref, huh_ref, wg_ref, wu_ref, bg_ref, bu_ref, o_ref):
        first = pl.program_id(1) == 0
        h, p1, p2 = _conv_taps(hg_ref, hgh_ref, first)
        g = bg_ref[...] + wg_ref[0:1, :] * p2 + wg_ref[1:2, :] * p1 + wg_ref[2:3, :] * h
        h, p1, p2 = _conv_taps(hu_ref, huh_ref, first)
        u = bu_ref[...] + wu_ref[0:1, :] * p2 + wu_ref[1:2, :] * p1 + wu_ref[2:3, :] * h
        o_ref[...] = ((g * _sigmoid(g)) * u).astype(o_ref.dtype)

    tile, halo, vec3, vec1 = _conv_specs(rows, r, cw)
    return pl.pallas_call(
        body, grid=(f // cw, rows // r), in_specs=[tile, halo, tile, halo, vec3, vec3, vec1, vec1], out_specs=tile,
        out_shape=jax.ShapeDtypeStruct((rows, f), BF16), name=name, compiler_params=_cparams(("parallel", "parallel")),
    )(hg, hg, hu, hu, wg, wu, bg, bu)


def _gate_grads(da, g, u):
    sg = _sigmoid(g)
    return da * u * (sg * (1.0 + g * (1.0 - sg))), da * (g * sg)


def _conv_back(dc, dc_next, w_ref, last):
    r = dc.shape[0]
    row = lax.broadcasted_iota(jnp.int32, (r, 1), 0)
    keep = jnp.where(last, 0.0, 1.0)
    n0 = dc_next[0:1, :] * keep
    n1 = dc_next[1:2, :] * keep
    f1 = jnp.where(row == r - 1, n0, pltpu.roll(dc, r - 1, 0))
    f2 = jnp.where(row == r - 1, n1, jnp.where(row == r - 2, n0, pltpu.roll(dc, r - 2, 0)))
    return w_ref[2:3, :] * dc + w_ref[1:2, :] * f1 + w_ref[0:1, :] * f2


def _conv_next_rows(h, nxt_ref, w_ref, b_ref):
    r = h.shape[0]
    hn = nxt_ref[...]
    row = lax.broadcasted_iota(jnp.int32, (8, 1), 0)
    m1, m2 = h[r - 1:r, :], h[r - 2:r - 1, :]
    p1 = jnp.where(row == 0, m1, pltpu.roll(hn, 1, 0))
    p2 = jnp.where(row == 0, m2, jnp.where(row == 1, m1, pltpu.roll(hn, 2, 0)))
    return b_ref[...] + w_ref[0:1, :] * p2 + w_ref[1:2, :] * p1 + w_ref[2:3, :] * hn


def _convffn_bwd(name, da, hg, hu, wg, wu, bg, bu):
    rows, f = hg.shape
    r, cw = min(_CONV_ROWS, rows), _pick(f, _CONV_COLS)
    nrt = rows // r

    def body(da_ref, dan_ref, hg_ref, hgh_ref, hgn_ref, hu_ref, huh_ref, hun_ref, wg_ref, wu_ref, bg_ref, bu_ref,
             dhg_ref, dhu_ref, dwg_ref, dwu_ref, dbg_ref, dbu_ref):
        first = pl.program_id(1) == 0
        last = pl.program_id(1) == nrt - 1
        hgv, g1, g2 = _conv_taps(hg_ref, hgh_ref, first)
        g = bg_ref[...] + wg_ref[0:1, :] * g2 + wg_ref[1:2, :] * g1 + wg_ref[2:3, :] * hgv
        huv, u1, u2 = _conv_taps(hu_ref, huh_ref, first)
        u = bu_ref[...] + wu_ref[0:1, :] * u2 + wu_ref[1:2, :] * u1 + wu_ref[2:3, :] * huv
        dcg, dcu = _gate_grads(da_ref[...], g, u)
        dcg_n, dcu_n = _gate_grads(dan_ref[...], _conv_next_rows(hgv, hgn_ref, wg_ref, bg_ref),
                                   _conv_next_rows(huv, hun_ref, wu_ref, bu_ref))
        dhg_ref[...] = _conv_back(dcg, dcg_n, wg_ref, last).astype(dhg_ref.dtype)
        dhu_ref[...] = _conv_back(dcu, dcu_n, wu_ref, last).astype(dhu_ref.dtype)

        @pl.when(first)
        def _():
            for ref in (dwg_ref, dwu_ref, dbg_ref, dbu_ref):
                ref[...] = jnp.zeros(ref.shape, ref.dtype)

        def colsum(v):
            return jnp.sum(v, axis=0, keepdims=True)

        dwg_ref[0:1, :] += colsum(dcg * g2)
        dwg_ref[1:2, :] += colsum(dcg * g1)
        dwg_ref[2:3, :] += colsum(dcg * hgv)
        dwu_ref[0:1, :] += colsum(dcu * u2)
        dwu_ref[1:2, :] += colsum(dcu * u1)
        dwu_ref[2:3, :] += colsum(dcu * huv)
        dbg_ref[...] += colsum(dcg)
        dbu_ref[...] += colsum(dcu)

    tile, halo, vec3, vec1 = _conv_specs(rows, r, cw)
    nxt = pl.BlockSpec((8, cw), lambda j, i: (jnp.minimum((i + 1) * (r // 8), rows // 8 - 1), j))
    big = jax.ShapeDtypeStruct((rows, f), BF16)
    return pl.pallas_call(
        body, grid=(f // cw, nrt),
        in_specs=[tile, nxt, tile, halo, nxt, tile, halo, nxt, vec3, vec3, vec1, vec1],
        out_specs=[tile, tile, vec3, vec3, vec1, vec1],
        out_shape=[big, big, jax.ShapeDtypeStruct((3, f), F32), jax.ShapeDtypeStruct((3, f), F32),
                   jax.ShapeDtypeStruct((1, f), F32), jax.ShapeDtypeStruct((1, f), F32)],
        name=name, compiler_params=_cparams(("parallel", "arbitrary")),
    )(da, da, hg, hg, hg, hu, hu, hu, wg, wu, bg, bu)


_GMLP_ROWS = 256


def _gmlp_group_norm(vg, gain):
    r = lax.rsqrt(jnp.mean(vg * vg, axis=-1, keepdims=True) + EPS)
    vh = vg * r
    return vh, r, vh * gain


def _gmlp_fwd(name, zuv, v_gain, w_tril, b_exp):
    rows = zuv.shape[0]
    r = min(_GMLP_ROWS, rows)

    def body(z_ref, gain_ref, w_ref, b_ref, o_ref):
        for ch in range(r // A_CHUNK):
            lo = ch * A_CHUNK
            for g in range(A_GROUPS):
                c0 = g * LANES
                u = _gelu(z_ref[lo:lo + A_CHUNK, c0:c0 + LANES])
                v = _gelu(z_ref[lo:lo + A_CHUNK, A_WIDTH + c0:A_WIDTH + c0 + LANES])
                _, _, vn = _gmlp_group_norm(v, gain_ref[:, c0:c0 + LANES])
                sv = jnp.dot(w_ref[g], vn.astype(BF16), preferred_element_type=F32) + b_ref[g]
                o_ref[lo:lo + A_CHUNK, c0:c0 + LANES] = (u * sv).astype(o_ref.dtype)

    return pl.pallas_call(
        body, grid=(rows // r,),
        in_specs=[pl.BlockSpec((r, 2 * A_WIDTH), lambda i: (i, 0)), pl.BlockSpec((1, A_WIDTH), lambda i: (0, 0)),
                  pl.BlockSpec((A_GROUPS, A_CHUNK, A_CHUNK), lambda i: (0, 0, 0)),
                  pl.BlockSpec((A_GROUPS, A_CHUNK, LANES), lambda i: (0, 0, 0))],
        out_specs=pl.BlockSpec((r, A_WIDTH), lambda i: (i, 0)),
        out_shape=jax.ShapeDtypeStruct((rows, A_WIDTH), BF16), name=name, compiler_params=_cparams(("parallel",)),
    )(zuv, v_gain, w_tril, b_exp)


def _gmlp_bwd(name, zuv, dya, v_gain, w_tril, w_tril_t, b_exp):
    rows = zuv.shape[0]
    r = min(_GMLP_ROWS, rows)

    def body(z_ref, dy_ref, gain_ref, w_ref, wt_ref, b_ref, dz_ref, dw_ref, db_ref, dgain_ref):
        @pl.when(pl.program_id(0) == 0)
        def _():
            for ref in (dw_ref, db_ref, dgain_ref):
                ref[...] = jnp.zeros(ref.shape, ref.dtype)

        for ch in range(r // A_CHUNK):
            lo = ch * A_CHUNK
            for g in range(A_GROUPS):
                c0 = g * LANES
                zu = z_ref[lo:lo + A_CHUNK, c0:c0 + LANES]
                zv = z_ref[lo:lo + A_CHUNK, A_WIDTH + c0:A_WIDTH + c0 + LANES]
                gain = gain_ref[:, c0:c0 + LANES]
                u = _gelu(zu)
                v = _gelu(zv)
                vh, rr, vn = _gmlp_group_norm(v, gain)
                vn_b = vn.astype(BF16)
                sv = jnp.dot(w_ref[g], vn_b, preferred_element_type=F32) + b_ref[g]
                dy = dy_ref[lo:lo + A_CHUNK, c0:c0 + LANES]
                dsv = dy * u
                dsv_b = dsv.astype(BF16)
                dz_ref[lo:lo + A_CHUNK, c0:c0 + LANES] = ((dy * sv) * _gelu_grad(zu)).astype(dz_ref.dtype)
                dw_ref[g] += lax.dot_general(dsv_b, vn_b, _DOT_DIMS["nt"], preferred_element_type=F32)
                db_ref[g] += dsv
                dvn = jnp.dot(wt_ref[g], dsv_b, preferred_element_type=F32)
                dgain_ref[:, c0:c0 + LANES] += jnp.sum(dvn * vh, axis=0, keepdims=True)
                dvh = dvn * gain
                dv = rr * (dvh - vh * jnp.mean(dvh * vh, axis=-1, keepdims=True))
                dz_ref[lo:lo + A_CHUNK, A_WIDTH + c0:A_WIDTH + c0 + LANES] = (dv * _gelu_grad(zv)).astype(dz_ref.dtype)

    wspec = pl.BlockSpec((A_GROUPS, A_CHUNK, A_CHUNK), lambda i: (0, 0, 0))
    bspec = pl.BlockSpec((A_GROUPS, A_CHUNK, LANES), lambda i: (0, 0, 0))
    gspec = pl.BlockSpec((1, A_WIDTH), lambda i: (0, 0))
    return pl.pallas_call(
        body, grid=(rows // r,),
        in_specs=[pl.BlockSpec((r, 2 * A_WIDTH), lambda i: (i, 0)), pl.BlockSpec((r, A_WIDTH), lambda i: (i, 0)),
                  gspec, wspec, wspec, bspec],
        out_specs=[pl.BlockSpec((r, 2 * A_WIDTH), lambda i: (i, 0)), wspec, bspec, gspec],
        out_shape=[jax.ShapeDtypeStruct((rows, 2 * A_WIDTH), BF16),
                   jax.ShapeDtypeStruct((A_GROUPS, A_CHUNK, A_CHUNK), F32),
                   jax.ShapeDtypeStruct((A_GROUPS, A_CHUNK, LANES), F32), jax.ShapeDtypeStruct((1, A_WIDTH), F32)],
        name=name, compiler_params=_cparams(("arbitrary",)),
    )(zuv, dya, v_gain, w_tril, w_tril_t, b_exp)


_ATT_T = 512
_Q_SCALE = B_HEAD_DIM ** -0.5


def _head_mean(v, bd):
    hi = v.astype(BF16)
    lo = (v - hi.astype(F32)).astype(BF16)
    tot = jnp.dot(hi, bd, preferred_element_type=F32) + jnp.dot(lo, bd, preferred_element_type=F32)
    return tot * (1.0 / B_HEAD_DIM)


def _qkv_prep_fwd(name, zqkv, zf, qg, kg, bf, bd):
    def fn(z, f, qg_v, kg_v, bf_v, bd_v):
        zq, zk, zv = z[:, :B_WIDTH], z[:, B_WIDTH:2 * B_WIDTH], z[:, 2 * B_WIDTH:]
        q = (zq * lax.rsqrt(_head_mean(zq * zq, bd_v) + EPS)) * qg_v * _Q_SCALE
        k = (zk * lax.rsqrt(_head_mean(zk * zk, bd_v) + EPS)) * kg_v
        return q, k, zv, _log_sigmoid(f + bf_v)

    return _rows(name, fn, [zqkv, zf], [qg, kg, bf, bd],
                 [(B_WIDTH, BF16), (B_WIDTH, BF16), (B_WIDTH, BF16), (LANES, F32)])


def _qkv_prep_bwd(name, zqkv, zf, dq, dk, dv, dls, qg, kg, bf, bd):
    def fn(z, f, dq_v, dk_v, dv_v, dls_v, qg_v, kg_v, bf_v, bd_v):
        zq, zk = z[:, :B_WIDTH], z[:, B_WIDTH:2 * B_WIDTH]

        def norm_bwd(x, dy, gain):
            r = lax.rsqrt(_head_mean(x * x, bd_v) + EPS)
            xh = x * r
            dxh = dy * gain
            dx = r * (dxh - xh * _head_mean(dxh * xh, bd_v))
            return dx, jnp.sum(dy * xh, axis=0, keepdims=True)

        dzq, dqg = norm_bwd(zq, dq_v * _Q_SCALE, qg_v)
        dzk, dkg = norm_bwd(zk, dk_v, kg_v)
        dzf = dls_v * (1.0 - _sigmoid(f + bf_v))
        return jnp.concatenate([dzq, dzk, dv_v], axis=1), dzf, dqg, dkg, jnp.sum(dzf, axis=0, keepdims=True)

    return _rows(name, fn, [zqkv, zf, dq, dk, dv, dls], [qg, kg, bf, bd],
                 [(3 * B_WIDTH, BF16), (LANES, BF16)], accs=[(1, B_WIDTH), (1, B_WIDTH), (1, LANES)])


def _cumsum_rows(name, a, reverse=False, tile=512):
    rows, w = a.shape
    r = min(tile, rows)
    n = rows // r

    def body(a_ref, o_ref, carry):
        @pl.when(pl.program_id(0) == 0)
        def _():
            carry[...] = jnp.zeros(carry.shape, carry.dtype)

        x = a_ref[...]
        row = lax.broadcasted_iota(jnp.int32, (r, 1), 0)
        s = 1
        while s < r:
            if reverse:
                x = x + jnp.where(row < r - s, pltpu.roll(x, r - s, 0), 0.0)
            else:
                x = x + jnp.where(row >= s, pltpu.roll(x, s, 0), 0.0)
            s *= 2
        x = x + carry[0:1, :]
        o_ref[...] = x
        edge = x[0:1, :] if reverse else x[r - 1:r, :]
        carry[...] = jnp.broadcast_to(edge, carry.shape)

    idx = (lambda i: (n - 1 - i, 0)) if reverse else (lambda i: (i, 0))
    return pl.pallas_call(
        body, grid=(n,), in_specs=[pl.BlockSpec((r, w), idx)], out_specs=pl.BlockSpec((r, w), idx),
        out_shape=jax.ShapeDtypeStruct((rows, w), F32), scratch_shapes=[pltpu.VMEM((8, w), F32)], name=name,
        compiler_params=_cparams(("arbitrary",)),
    )(a)


def _head_masks():
    lane = lax.broadcasted_iota(jnp.int32, (1, LANES), 1)
    return [lane < B_HEAD_DIM, lane >= B_HEAD_DIM]


def _causal(t):
    row = lax.broadcasted_iota(jnp.int32, (t, t), 0)
    col = lax.broadcasted_iota(jnp.int32, (t, t), 1)
    return row, col


def _col_from_row(row_vec):
    return jnp.transpose(jnp.broadcast_to(row_vec, (LANES, row_vec.shape[1])))[:, 0:1]


def _row_from_col(col):
    return jnp.transpose(jnp.broadcast_to(col, (col.shape[0], LANES)))[0:1, :]


def _flash_fwd(name, q, k, v, nck_rows):
    rows = q.shape[0]
    t = min(_ATT_T, rows)
    nb = rows // t

    def body(q_ref, k_ref, v_ref, nck_ref, o_ref, lse_ref):
        pair, i = pl.program_id(0), pl.program_id(1)
        q2 = q_ref[...]
        row, col = _causal(t)
        masks = _head_masks()
        qh = [jnp.where(hm, q2, jnp.zeros_like(q2)) for hm in masks]

        def step(j, carry, diag):
            ml, acc = carry
            start = pl.multiple_of(j * t, t)
            kb = k_ref[pl.ds(start, t), :]
            vb = v_ref[pl.ds(start, t), :]
            new_ml = []
            for hh, hm in enumerate(masks):
                m, l = ml[hh]
                s = lax.dot_general(qh[hh], kb, _DOT_DIMS["nt"], preferred_element_type=F32)
                s = s + nck_ref[2 * pair + hh, pl.ds(j, 1), :]
                if diag:
                    s = jnp.where(col <= row, s, NEG_INF)
                m_new = jnp.maximum(m, jnp.max(s, axis=1, keepdims=True))
                p = jnp.exp(s - m_new)
                alpha = jnp.exp(m - m_new)
                new_ml.append((m_new, alpha * l + jnp.sum(p, axis=1, keepdims=True)))
                pv = jnp.dot(p.astype(BF16), jnp.where(hm, vb, jnp.zeros_like(vb)), preferred_element_type=F32)
                acc = acc * jnp.where(hm, alpha, 1.0) + pv
            return tuple(new_ml), acc

        def init_ml():
            return (jnp.full((t, 1), NEG_INF, F32), jnp.zeros((t, 1), F32))

        init = ((init_ml(), init_ml()), jnp.zeros((t, LANES), F32))
        carry = lax.fori_loop(0, i, lambda j, c: step(j, c, False), init)
        ml, acc = step(i, carry, True)
        o_ref[...] = acc / jnp.where(masks[0], ml[0][1], ml[1][1])
        for hh in range(2):
            lse_ref[hh, 0] = _row_from_col(ml[hh][0] + jnp.log(ml[hh][1]))

    return pl.pallas_call(
        body, grid=(B_HEADS // 2, nb),
        in_specs=[pl.BlockSpec((t, LANES), lambda p, i: (i, p)), pl.BlockSpec((rows, LANES), lambda p, i: (0, p)),
                  pl.BlockSpec((rows, LANES), lambda p, i: (0, p)),
                  pl.BlockSpec((B_HEADS, nb, t), lambda p, i: (0, 0, 0))],
        out_specs=[pl.BlockSpec((t, LANES), lambda p, i: (i, p)),
                   pl.BlockSpec((2, 1, 1, t), lambda p, i: (p, i, 0, 0))],
        out_shape=[jax.ShapeDtypeStruct((rows, B_WIDTH), F32), jax.ShapeDtypeStruct((B_HEADS, nb, 1, t), F32)],
        name=name, compiler_params=_cparams(("parallel", "parallel")),
    )(q, k, v, nck_rows)


def _flash_bwd_dq(name, q, k, v, nck_rows, o, do, lse_rows):
    rows = q.shape[0]
    t = min(_ATT_T, rows)
    nb = rows // t

    def body(q_ref, k_ref, v_ref, nck_ref, o_ref, do_ref, lse_ref, dq_ref, delta_ref):
        pair, i = pl.program_id(0), pl.program_id(1)
        q2 = q_ref[...]
        do2 = do_ref[...]
        od = o_ref[...] * do2
        do_b = do2.astype(BF16)
        row, col = _causal(t)
        masks = _head_masks()
        qh = [jnp.where(hm, q2, jnp.zeros_like(q2)) for hm in masks]
        doh = [jnp.where(hm, do_b, jnp.zeros_like(do_b)) for hm in masks]
        delta = [jnp.sum(jnp.where(hm, od, 0.0), axis=1, keepdims=True) for hm in masks]
        lse = [_col_from_row(lse_ref[2 * pair + hh, pl.ds(i, 1), :]) for hh in range(2)]

        def step(j, carry, diag):
            acc, rowsum = carry
            start = pl.multiple_of(j * t, t)
            kb = k_ref[pl.ds(start, t), :]
            vb = v_ref[pl.ds(start, t), :]
            new_rowsum = []
            for hh, hm in enumerate(masks):
                s = lax.dot_general(qh[hh], kb, _DOT_DIMS["nt"], preferred_element_type=F32)
                s = s + nck_ref[2 * pair + hh, pl.ds(j, 1), :]
                p = jnp.exp(s - lse[hh])
                if diag:
                    p = jnp.where(col <= row, p, 0.0)
                dp = lax.dot_general(doh[hh], vb, _DOT_DIMS["nt"], preferred_element_type=F32)
                ds = p * (dp - delta[hh])
                new_rowsum.append(rowsum[hh] + jnp.sum(ds, axis=1, keepdims=True))
                acc = acc + jnp.dot(ds.astype(BF16), jnp.where(hm, kb, jnp.zeros_like(kb)),
                                    preferred_element_type=F32)
            return acc, tuple(new_rowsum)

        zcol = jnp.zeros((t, 1), F32)
        carry = lax.fori_loop(0, i, lambda j, c: step(j, c, False), (jnp.zeros((t, LANES), F32), (zcol, zcol)))
        acc, rowsum = step(i, carry, True)
        dq_ref[...] = acc
        for hh in range(2):
            delta_ref[hh, 0] = _row_from_col(delta[hh] + rowsum[hh])

    tile = pl.BlockSpec((t, LANES), lambda p, i: (i, p))
    full = pl.BlockSpec((rows, LANES), lambda p, i: (0, p))
    rowspec = pl.BlockSpec((B_HEADS, nb, t), lambda p, i: (0, 0, 0))
    return pl.pallas_call(
        body, grid=(B_HEADS // 2, nb),
        in_specs=[tile, full, full, rowspec, tile, tile, rowspec],
        out_specs=[tile, pl.BlockSpec((2, 1, 1, t), lambda p, i: (p, i, 0, 0))],
        out_shape=[jax.ShapeDtypeStruct((rows, B_WIDTH), F32), jax.ShapeDtypeStruct((B_HEADS, nb, 1, t), F32)],
        name=name, compiler_params=_cparams(("parallel", "parallel")),
    )(q, k, v, nck_rows, o, do, lse_rows)


def _flash_bwd_dkv(name, q, k, v, nck_rows, do, lse_rows, delta_rows):
    rows = q.shape[0]
    t = min(_ATT_T, rows)
    nb = rows // t

    def body(k_ref, v_ref, q_ref, do_ref, nck_ref, lse_ref, delta_ref, dk_ref, dv_ref, dn_ref):
        pair, j = pl.program_id(0), pl.program_id(1)
        k2 = k_ref[...]
        v2 = v_ref[...]
        row, col = _causal(t)
        masks = _head_masks()
        kh = [jnp.where(hm, k2, jnp.zeros_like(k2)) for hm in masks]
        vh = [jnp.where(hm, v2, jnp.zeros_like(v2)) for hm in masks]
        nck = [_col_from_row(nck_ref[2 * pair + hh, pl.ds(j, 1), :]) for hh in range(2)]

        def step(i, carry, diag):
            dk, dv, dn = carry
            start = pl.multiple_of(i * t, t)
            qb = q_ref[pl.ds(start, t), :]
            dob = do_ref[pl.ds(start, t), :].astype(BF16)
            dn_new = []
            for hh, hm in enumerate(masks):
                head = 2 * pair + hh
                st = lax.dot_general(kh[hh], qb, _DOT_DIMS["nt"], preferred_element_type=F32) + nck[hh]
                pt = jnp.exp(st - lse_ref[head, pl.ds(i, 1), :])
                if diag:
                    pt = jnp.where(row <= col, pt, 0.0)
                dpt = lax.dot_general(vh[hh], dob, _DOT_DIMS["nt"], preferred_element_type=F32)
                dst = pt * (dpt - delta_ref[head, pl.ds(i, 1), :])
                dv = dv + jnp.dot(pt.astype(BF16), jnp.where(hm, dob, jnp.zeros_like(dob)),
                                  preferred_element_type=F32)
                dk = dk + jnp.dot(dst.astype(BF16), jnp.where(hm, qb, jnp.zeros_like(qb)),
                                  preferred_element_type=F32)
                dn_new.append(dn[hh] + jnp.sum(dst, axis=1, keepdims=True))
            return dk, dv, tuple(dn_new)

        zero = jnp.zeros((t, LANES), F32)
        zcol = jnp.zeros((t, 1), F32)
        carry = step(j, (zero, zero, (zcol, zcol)), True)
        dk, dv, dn = lax.fori_loop(j + 1, nb, lambda i, c: step(i, c, False), carry)
        dk_ref[...] = dk
        dv_ref[...] = dv
        for hh in range(2):
            dn_ref[hh, 0] = _row_from_col(dn[hh])

    tile = pl.BlockSpec((t, LANES), lambda p, j: (j, p))
    full = pl.BlockSpec((rows, LANES), lambda p, j: (0, p))
    rowspec = pl.BlockSpec((B_HEADS, nb, t), lambda p, j: (0, 0, 0))
    big = jax.ShapeDtypeStruct((rows, B_WIDTH), F32)
    return pl.pallas_call(
        body, grid=(B_HEADS // 2, nb),
        in_specs=[tile, tile, full, full, rowspec, rowspec, rowspec],
        out_specs=[tile, tile, pl.BlockSpec((2, 1, 1, t), lambda p, j: (p, j, 0, 0))],
        out_shape=[big, big, jax.ShapeDtypeStruct((B_HEADS, nb, 1, t), F32)],
        name=name, compiler_params=_cparams(("parallel", "parallel")),
    )(k, v, q, do, nck_rows, lse_rows, delta_rows)


_S5_ROWS = 512


def _s5_discretize(a_re, a_im, log_dt, b_re, b_im):
    dt = jnp.exp(log_dt)[:, None]
    mag = jnp.exp(a_re * dt)
    ab_re, ab_im = mag * jnp.cos(a_im * dt), mag * jnp.sin(a_im * dt)
    den = a_re * a_re + a_im * a_im
    nr, ni = ab_re - 1.0, ab_im
    cr = (nr * a_re + ni * a_im) / den
    ci = (ni * a_re - nr * a_im) / den
    bb_re = cr[..., None] * b_re - ci[..., None] * b_im
    bb_im = cr[..., None] * b_im + ci[..., None] * b_re
    return ab_re, ab_im, bb_re, bb_im


def _s5_block_diag(m):
    g, r, c = m.shape
    mb = m.reshape(S5_BLOCKS, 8, r, c)
    eye = jnp.eye(8, dtype=m.dtype)
    return jnp.einsum("bgrc,gh->bgrhc", mb, eye).reshape(S5_BLOCKS, 8 * r, 8 * c)


def _s5_block_diag_extract(m, r, c):
    mb = m.reshape(S5_BLOCKS, 8, r, 8, c)
    return jnp.einsum("bgrhc,gh->bgrc", mb, jnp.eye(8, dtype=m.dtype)).reshape(S5_GROUPS, r, c)


def _s5_tables(ab_re, ab_im, r):
    ar = jnp.broadcast_to(ab_re.reshape(1, -1), (r, S5_GROUPS * S5_STATE))
    ai = jnp.broadcast_to(ab_im.reshape(1, -1), (r, S5_GROUPS * S5_STATE))

    def mul(x, y):
        return x[0] * y[0] - x[1] * y[1], x[0] * y[1] + x[1] * y[0]

    return lax.associative_scan(mul, (ar, ai), axis=0)


def _scan_step(xr, xi, ar, ai, s, row, up):
    r = xr.shape[0]
    if up:
        ai = -ai
    if s < 8:
        if up:
            sr = jnp.where(row < r - s, pltpu.roll(xr, r - s, 0), 0.0)
            si = jnp.where(row < r - s, pltpu.roll(xi, r - s, 0), 0.0)
        else:
            sr = jnp.where(row >= s, pltpu.roll(xr, s, 0), 0.0)
            si = jnp.where(row >= s, pltpu.roll(xi, s, 0), 0.0)
        return xr + (ar * sr - ai * si), xi + (ar * si + ai * sr)
    if up:
        (dr, di), (sr, si) = (xr[:r - s], xi[:r - s]), (xr[s:], xi[s:])
        nr, ni = dr + (ar * sr - ai * si), di + (ar * si + ai * sr)
        return jnp.concatenate([nr, xr[r - s:]], axis=0), jnp.concatenate([ni, xi[r - s:]], axis=0)
    (dr, di), (sr, si) = (xr[s:], xi[s:]), (xr[:r - s], xi[:r - s])
    nr, ni = dr + (ar * sr - ai * si), di + (ar * si + ai * sr)
    return jnp.concatenate([xr[:s], nr], axis=0), jnp.concatenate([xi[:s], ni], axis=0)


_S5_CHUNK = 16


def _scan_tile(xr, xi, pr_ref, pi_ref, tr_ref, ti_ref, edge_ref, up):
    r, nl = xr.shape
    ch = _S5_CHUNK
    nch = r // ch
    sub = lax.broadcasted_iota(jnp.int32, (r, 1), 0) & (ch - 1)
    s = 1
    while s < ch:
        ar, ai = pr_ref[s - 1:s, :], pi_ref[s - 1:s, :]
        if up:
            ai, keep, shift = -ai, sub < ch - s, r - s
        else:
            keep, shift = sub >= s, s
        sr = jnp.where(keep, pltpu.roll(xr, shift, 0), 0.0)
        si = jnp.where(keep, pltpu.roll(xi, shift, 0), 0.0)
        xr, xi = xr + (ar * sr - ai * si), xi + (ar * si + ai * sr)
        s *= 2
    nb = nl // LANES
    for k in range(nb):
        edge_ref[k] = xr[:, k * LANES:(k + 1) * LANES]
        edge_ref[nb + k] = xi[:, k * LANES:(k + 1) * LANES]
    e0 = 0 if up else ch - 1
    er = jnp.concatenate([edge_ref[k, pl.ds(e0, nch, stride=ch), :] for k in range(nb)], axis=1)
    ei = jnp.concatenate([edge_ref[nb + k, pl.ds(e0, nch, stride=ch), :] for k in range(nb)], axis=1)
    rowc = lax.broadcasted_iota(jnp.int32, (nch, 1), 0)
    s = 1
    while s < nch:
        er, ei = _scan_step(er, ei, pr_ref[ch * s - 1:ch * s, :], pi_ref[ch * s - 1:ch * s, :], s, rowc, up)
        s *= 2
    if up:
        nr = jnp.where(rowc < nch - 1, pltpu.roll(er, nch - 1, 0), 0.0)
        ni = jnp.where(rowc < nch - 1, pltpu.roll(ei, nch - 1, 0), 0.0)
    else:
        nr = jnp.where(rowc >= 1, pltpu.roll(er, 1, 0), 0.0)
        ni = jnp.where(rowc >= 1, pltpu.roll(ei, 1, 0), 0.0)
    br = jnp.concatenate([jnp.broadcast_to(nr[n:n + 1, :], (ch, nl)) for n in range(nch)], axis=0)
    bi = jnp.concatenate([jnp.broadcast_to(ni[n:n + 1, :], (ch, nl)) for n in range(nch)], axis=0)
    tr, ti = tr_ref[...], ti_ref[...]
    if up:
        ti = -ti
    return xr + (tr * br - ti * bi), xi + (tr * bi + ti * br)


def _s5_scan_tile(u_ref, bcat_ref, pr_ref, pi_ref, tr_ref, ti_ref, edge_ref, cin_r, cin_i):
    bu = jnp.dot(u_ref[...], bcat_ref[...], preferred_element_type=F32)
    xr, xi = bu[:, :S5_LANES], bu[:, S5_LANES:]
    ar, ai = pr_ref[0:1, :], pi_ref[0:1, :]
    first = lax.broadcasted_iota(jnp.int32, (8, 1), 0) == 0
    xr = jnp.concatenate([xr[:8] + jnp.where(first, ar * cin_r - ai * cin_i, 0.0), xr[8:]], axis=0)
    xi = jnp.concatenate([xi[:8] + jnp.where(first, ar * cin_i + ai * cin_r, 0.0), xi[8:]], axis=0)
    return _scan_tile(xr, xi, pr_ref, pi_ref, tr_ref, ti_ref, edge_ref, False)


def _s5_fwd(name, u, bcat, ccat, pw_re, pw_im, pt_re, pt_im):
    rows = u.shape[0]
    r = pw_re.shape[0]
    nt = rows // r

    def body(u_ref, bcat_ref, ccat_ref, pr_ref, pi_ref, tr_ref, ti_ref, y_ref, xin_ref, carry, edge):
        @pl.when(pl.program_id(1) == 0)
        def _():
            carry[...] = jnp.zeros(carry.shape, carry.dtype)

        xin_ref[...] = carry[...]
        xr, xi = _s5_scan_tile(u_ref, bcat_ref, pr_ref, pi_ref, tr_ref, ti_ref, edge,
                               carry[0:1, :S5_LANES], carry[0:1, S5_LANES:])
        xcat = jnp.concatenate([xr, xi], axis=1)
        carry[...] = jnp.broadcast_to(xcat[r - 1:r, :], carry.shape)
        y_ref[...] = jnp.dot(xcat.astype(BF16), ccat_ref[...], preferred_element_type=F32)

    tab = pl.BlockSpec((r, S5_LANES), lambda b, i: (0, b))
    return pl.pallas_call(
        body, grid=(S5_BLOCKS, nt),
        in_specs=[pl.BlockSpec((r, LANES), lambda b, i: (i, b)),
                  pl.BlockSpec((None, LANES, 2 * S5_LANES), lambda b, i: (b, 0, 0)),
                  pl.BlockSpec((None, 2 * S5_LANES, LANES), lambda b, i: (b, 0, 0)), tab, tab, tab, tab],
        out_specs=[pl.BlockSpec((r, LANES), lambda b, i: (i, b)),
                   pl.BlockSpec((None, 8, 2 * S5_LANES), lambda b, i: (b, i, 0))],
        out_shape=[jax.ShapeDtypeStruct((rows, D_MODEL), F32),
                   jax.ShapeDtypeStruct((S5_BLOCKS, 8 * nt, 2 * S5_LANES), F32)],
        scratch_shapes=[pltpu.VMEM((8, 2 * S5_LANES), F32), pltpu.VMEM((2 * S5_LANES // LANES, r, LANES), F32)], name=name,
        compiler_params=_cparams(("parallel", "arbitrary")),
    )(u, bcat, ccat, pw_re, pw_im, pt_re, pt_im)


def _s5_bwd(name, u, dy, xin, bcat, ccat, pw_re, pw_im, pt_re, pt_im, ptu_re, ptu_im):
    rows = u.shape[0]
    r = pw_re.shape[0]
    nt = rows // r

    def body(u_ref, dy_ref, xin_ref, bcat_ref, ccat_ref, pr_ref, pi_ref, tr_ref, ti_ref, ur_ref, ui_ref,
             du_ref, db_ref, dc_ref, dar_ref, dai_ref, carry, edge):
        @pl.when(pl.program_id(1) == 0)
        def _():
            carry[...] = jnp.zeros(carry.shape, carry.dtype)
            for ref in (db_ref, dc_ref, dar_ref, dai_ref):
                ref[...] = jnp.zeros(ref.shape, ref.dtype)

        row = lax.broadcasted_iota(jnp.int32, (r, 1), 0)
        cin_r, cin_i = xin_ref[0:1, :S5_LANES], xin_ref[0:1, S5_LANES:]
        xr, xi = _s5_scan_tile(u_ref, bcat_ref, pr_ref, pi_ref, tr_ref, ti_ref, edge, cin_r, cin_i)
        dy_b = dy_ref[...].astype(BF16)
        xcat = jnp.concatenate([xr, xi], axis=1).astype(BF16)
        dc_ref[...] += lax.dot_general(xcat, dy_b, _DOT_DIMS["tn"], preferred_element_type=F32)
        g = lax.dot_general(dy_b, ccat_ref[...], _DOT_DIMS["nt"], preferred_element_type=F32)
        lr, li = g[:, :S5_LANES], g[:, S5_LANES:]
        nr, ni = carry[0:1, :S5_LANES], carry[0:1, S5_LANES:]
        ar, ai = pr_ref[0:1, :], pi_ref[0:1, :]
        final = lax.broadcasted_iota(jnp.int32, (8, 1), 0) == 7
        lr = jnp.concatenate([lr[:r - 8], lr[r - 8:] + jnp.where(final, ar * nr + ai * ni, 0.0)], axis=0)
        li = jnp.concatenate([li[:r - 8], li[r - 8:] + jnp.where(final, ar * ni - ai * nr, 0.0)], axis=0)
        lr, li = _scan_tile(lr, li, pr_ref, pi_ref, ur_ref, ui_ref, edge, True)
        carry[...] = jnp.broadcast_to(jnp.concatenate([lr[0:1, :], li[0:1, :]], axis=1), carry.shape)
        lcat = jnp.concatenate([lr, li], axis=1).astype(BF16)
        du_ref[...] = lax.dot_general(lcat, bcat_ref[...], _DOT_DIMS["nt"], preferred_element_type=F32)
        db_ref[...] += lax.dot_general(u_ref[...], lcat, _DOT_DIMS["tn"], preferred_element_type=F32)
        pxr = jnp.where(row == 0, cin_r, pltpu.roll(xr, 1, 0))
        pxi = jnp.where(row == 0, cin_i, pltpu.roll(xi, 1, 0))
        dar_ref[...] += jnp.sum((lr * pxr + li * pxi).reshape(r // 8, 8, S5_LANES), axis=0)
        dai_ref[...] += jnp.sum((li * pxr - lr * pxi).reshape(r // 8, 8, S5_LANES), axis=0)

    rev = lambda b, i: (nt - 1 - i, b)
    tab = pl.BlockSpec((r, S5_LANES), lambda b, i: (0, b))
    return pl.pallas_call(
        body, grid=(S5_BLOCKS, nt),
        in_specs=[pl.BlockSpec((r, LANES), rev), pl.BlockSpec((r, LANES), rev),
                  pl.BlockSpec((None, 8, 2 * S5_LANES), lambda b, i: (b, nt - 1 - i, 0)),
                  pl.BlockSpec((None, LANES, 2 * S5_LANES), lambda b, i: (b, 0, 0)),
                  pl.BlockSpec((None, 2 * S5_LANES, LANES), lambda b, i: (b, 0, 0)), tab, tab, tab, tab, tab, tab],
        out_specs=[pl.BlockSpec((r, LANES), rev),
                   pl.BlockSpec((None, LANES, 2 * S5_LANES), lambda b, i: (b, 0, 0)),
                   pl.BlockSpec((None, 2 * S5_LANES, LANES), lambda b, i: (b, 0, 0)),
                   pl.BlockSpec((None, 8, S5_LANES), lambda b, i: (b, 0, 0)),
                   pl.BlockSpec((None, 8, S5_LANES), lambda b, i: (b, 0, 0))],
        out_shape=[jax.ShapeDtypeStruct((rows, D_MODEL), F32),
                   jax.ShapeDtypeStruct((S5_BLOCKS, LANES, 2 * S5_LANES), F32),
                   jax.ShapeDtypeStruct((S5_BLOCKS, 2 * S5_LANES, LANES), F32),
                   jax.ShapeDtypeStruct((S5_BLOCKS, 8, S5_LANES), F32),
                   jax.ShapeDtypeStruct((S5_BLOCKS, 8, S5_LANES), F32)],
        scratch_shapes=[pltpu.VMEM((8, 2 * S5_LANES), F32), pltpu.VMEM((2 * S5_LANES // LANES, r, LANES), F32)], name=name,
        compiler_params=_cparams(("parallel", "arbitrary")),
    )(u, dy, xin, bcat, ccat, pw_re, pw_im, pt_re, pt_im, ptu_re, ptu_im)


def _ones_gain():
    return jnp.ones((1, D_MODEL), F32)


def _channel_fwd(i, x1, p_i, w, rp, hn=None, next_norm=None):
    if hn is None:
        hn, = _rmsnorm_fwd(f"ffn_norm_{i}", x1, rp["norm_ffn"][i][None], [BF16])
    hg = _mm(f"ffn_up_g_{i}", hn, w["up_g"], tn=1408)
    hu = _mm(f"ffn_up_u_{i}", hn, w["up_u"], tn=1408)
    a = _convffn_fwd(f"ffn_conv_{i}", hg, hu, w["cw_g"], w["cw_u"], w["cb_g"], w["cb_u"])
    x2, r = _mm(f"ffn_down_{i}", a, w["down"], res=x1, tk=1408, norm_gain=_ones_gain())
    zg = _mm(f"ple_gate_{i}", r, w["ple_gate"])
    pp = _mm(f"ple_proj_{i}", p_i, w["ple_proj"])
    saved = dict(x1=x1, hn=hn, hg=hg, hu=hu, a=a, x2=x2, r=r, zg=zg, pp=pp, p_i=p_i)
    if next_norm is None:
        x3, = _rows(f"ple_out_{i}", lambda xv, zv, pv: (xv + _sigmoid(zv) * pv,), [x2, zg, pp], [], [(D_MODEL, F32)])
        return x3, None, saved
    gain, dtypes = next_norm

    def ple_out_norm(xv, zv, pv, gv):
        x3v = xv + _sigmoid(zv) * pv
        h = (x3v * _rstd(x3v)) * gv
        return (x3v,) + tuple(h for _ in dtypes)

    x3, *h_next = _rows(f"ple_out_{i}", ple_out_norm, [x2, zg, pp], [gain],
                        [(D_MODEL, F32)] + [(D_MODEL, dt) for dt in dtypes])
    return x3, h_next, saved


def _channel_bwd(i, dx3, sv, w, rp):
    def ple_bwd(dv, zv, pv):
        gate = _sigmoid(zv)
        return dv * gate, (dv * pv) * (gate * (1.0 - gate))

    dpp, dzg = _rows(f"ple_out_bwd_{i}", ple_bwd, [dx3, sv["zg"], sv["pp"]], [], [(D_MODEL, BF16), (D_MODEL, BF16)])
    g = {}
    g["ple_proj"] = _mm(f"ple_proj_dw_{i}", sv["p_i"], dpp, "tn")
    g["ple_gate"] = _mm(f"ple_gate_dw_{i}", sv["r"], dzg, "tn")
    dx2, _ = _mm_norm_bwd(f"ple_gate_dx_{i}", dzg, w["ple_gate"], "nt", sv["x2"], dx3, _ones_gain())
    da = _mm(f"ffn_down_dx_{i}", dx2, w["down"], "nt", tn=1408)
    g["down"] = _mm(f"ffn_down_dw_{i}", sv["a"], dx2, "tn", tm=1408)
    dhg, dhu, g["cw_g"], g["cw_u"], dbg, dbu = _convffn_bwd(
        f"ffn_conv_bwd_{i}", da, sv["hg"], sv["hu"], w["cw_g"], w["cw_u"], w["cb_g"], w["cb_u"])
    g["conv_b"] = jnp.concatenate([dbg, dbu], axis=1)[0]
    g["up_g"] = _mm(f"ffn_up_g_dw_{i}", sv["hn"], dhg, "tn", tn=1408)
    g["up_u"] = _mm(f"ffn_up_u_dw_{i}", sv["hn"], dhu, "tn", tn=1408)
    dhn = _mm(f"ffn_up_g_dx_{i}", dhg, w["up_g"], "nt", tk=1408)
    dx1, dgf = _mm_norm_bwd(f"ffn_up_u_dx_{i}", dhu, w["up_u"], "nt", sv["x1"], dx2, rp["norm_ffn"][i][None],
                            res=dhn, tk=1408)
    g["norm_ffn"] = dgf[0]
    return dx1, g


def _even_consts(e, rp):
    tri = jnp.tril(jnp.ones((A_CHUNK, A_CHUNK), dtype=bool))
    w_tril = jnp.where(tri[None], rp["ev_w_spatial"][e], 0.0).astype(BF16)
    b_exp = jnp.broadcast_to(rp["ev_b_spatial"][e][:, :, None], (A_GROUPS, A_CHUNK, LANES))
    seg = np.arange(B_WIDTH) // B_HEAD_DIM
    bd = jnp.asarray((seg[:, None] == seg[None, :]).astype(np.float32)).astype(BF16)
    return dict(
        tri=tri, w_tril=w_tril, w_tril_t=jnp.swapaxes(w_tril, 1, 2), b_exp=b_exp, bd=bd,
        v_gain=rp["ev_v_norm"][e][None], qg=jnp.tile(rp["ev_q_norm"][e], B_HEADS)[None],
        kg=jnp.tile(rp["ev_k_norm"][e], B_HEADS)[None],
        bf=jnp.pad(rp["ev_b_fgate"][e], (0, LANES - B_HEADS))[None])


def _even_fwd(i, x, w, rp, h_in=None):
    e = i // 2
    c = _even_consts(e, rp)
    rows = x.shape[0]
    t = min(_ATT_T, rows)
    h, = h_in if h_in is not None else _rmsnorm_fwd(f"mix_norm_{i}", x, rp["norm_mix"][i][None], [BF16])
    zuv = _mm(f"in_uv_{i}", h, w["in_uv"])
    zqkv = _mm(f"in_qkv_{i}", h, w["in_qkv"], tn=768)
    zf = _mm(f"in_f_{i}", h, w["in_f"])
    ya = _gmlp_fwd(f"gmlp_{i}", zuv, c["v_gain"], c["w_tril"], c["b_exp"])
    q, k, v, ls = _qkv_prep_fwd(f"qkv_prep_{i}", zqkv, zf, c["qg"], c["kg"], c["bf"], c["bd"])
    csum = _cumsum_rows(f"forget_cumsum_{i}", ls)
    nck = -csum[:, :B_HEADS].T
    nck_rows = nck.reshape(B_HEADS, rows // t, t)
    o, lse = _flash_fwd(f"attn_{i}", q, k, v, nck_rows)
    x1 = _mm(f"out_a_{i}", ya, w["out_a"], res=x)
    x1, hn = _mm(f"out_b_{i}", o, w["out_b"], res=x1, norm_gain=rp["norm_ffn"][i][None])
    return x1, hn, dict(x=x, h=h, zuv=zuv, zqkv=zqkv, zf=zf, ya=ya, q=q, k=k, v=v, nck_rows=nck_rows, o=o,
                          lse_rows=lse.reshape(B_HEADS, rows // t, t))


def _even_bwd(i, dx1, sv, w, rp):
    e = i // 2
    c = _even_consts(e, rp)
    rows = dx1.shape[0]
    t = min(_ATT_T, rows)
    nb = rows // t
    g = {}
    dya = _mm(f"out_a_dx_{i}", dx1, w["out_a"], "nt")
    do = _mm(f"out_b_dx_{i}", dx1, w["out_b"], "nt")
    g["out_a"] = _mm(f"out_a_dw_{i}", sv["ya"], dx1, "tn")
    g["out_b"] = _mm(f"out_b_dw_{i}", sv["o"], dx1, "tn")
    dq, delta = _flash_bwd_dq(f"attn_dq_{i}", sv["q"], sv["k"], sv["v"], sv["nck_rows"], sv["o"], do,
                              sv["lse_rows"])
    dk, dv, dn = _flash_bwd_dkv(f"attn_dkv_{i}", sv["q"], sv["k"], sv["v"], sv["nck_rows"], do, sv["lse_rows"],
                                delta.reshape(B_HEADS, nb, t))
    dcs = jnp.pad(-dn.reshape(B_HEADS, rows).T, ((0, 0), (0, LANES - B_HEADS)))
    dls = _cumsum_rows(f"forget_cumsum_bwd_{i}", dcs, reverse=True)
    dzqkv, dzf, dqg, dkg, dbf = _qkv_prep_bwd(f"qkv_prep_bwd_{i}", sv["zqkv"], sv["zf"], dq, dk, dv, dls,
                                              c["qg"], c["kg"], c["bf"], c["bd"])
    dzuv, dws, dbs, dvg = _gmlp_bwd(f"gmlp_bwd_{i}", sv["zuv"], dya, c["v_gain"], c["w_tril"], c["w_tril_t"],
                                    c["b_exp"])
    g["in_uv"] = _mm(f"in_uv_dw_{i}", sv["h"], dzuv, "tn")
    g["in_qkv"] = _mm(f"in_qkv_dw_{i}", sv["h"], dzqkv, "tn", tn=768)
    g["in_f"] = _mm(f"in_f_dw_{i}", sv["h"], dzf, "tn")
    dh = _mm(f"in_uv_dx_{i}", dzuv, w["in_uv"], "nt")
    dh = _mm(f"in_qkv_dx_{i}", dzqkv, w["in_qkv"], "nt", res=dh, tk=768)
    dx, dgm = _mm_norm_bwd(f"in_f_dx_{i}", dzf, w["in_f"], "nt", sv["x"], dx1, rp["norm_mix"][i][None], res=dh)
    g["norm_mix"] = dgm[0]
    g["ev_b_fgate"] = dbf[0, :B_HEADS]
    g["ev_q_norm"] = dqg.reshape(B_HEADS, B_HEAD_DIM).sum(axis=0)
    g["ev_k_norm"] = dkg.reshape(B_HEADS, B_HEAD_DIM).sum(axis=0)
    g["ev_v_norm"] = dvg[0]
    g["ev_w_spatial"] = jnp.where(c["tri"][None], dws, 0.0)
    g["ev_b_spatial"] = dbs.sum(axis=-1)
    return dx, g


def _s5_consts(o, rp, r):
    prm = (rp["od_a_re"][o], rp["od_a_im"][o], rp["od_log_dt"][o], rp["od_b_re"][o], rp["od_b_im"][o])
    (ab_re, ab_im, bb_re, bb_im), vjp = jax.vjp(_s5_discretize, *prm)
    bcat = jnp.concatenate([_s5_block_diag(bb_re.transpose(0, 2, 1)), _s5_block_diag(bb_im.transpose(0, 2, 1))], axis=2)
    c_re, c_im = rp["od_c_re"][o], rp["od_c_im"][o]
    ccat = jnp.concatenate([_s5_block_diag(c_re.transpose(0, 2, 1)), _s5_block_diag(-c_im.transpose(0, 2, 1))], axis=1)
    pw = tuple(_s5_tables(ab_re, ab_im, r))
    reps = (r // _S5_CHUNK, 1)
    down = tuple(jnp.tile(t[:_S5_CHUNK], reps) for t in pw)
    up = tuple(jnp.tile(jnp.flip(t[:_S5_CHUNK], axis=0), reps) for t in pw)
    return dict(vjp=vjp, bcat=bcat.astype(BF16), ccat=ccat.astype(BF16), fwd_tabs=pw + down, bwd_tabs=pw + down + up)


def _odd_fwd(i, x, w, rp, h_in=None):
    o = i // 2
    rows = x.shape[0]
    c = _s5_consts(o, rp, min(_S5_ROWS, rows))
    hb, hf = h_in if h_in is not None else _rmsnorm_fwd(f"mix_norm_{i}", x, rp["norm_mix"][i][None], [BF16, F32])
    ys, xin = _s5_fwd(f"s5_{i}", hb, c["bcat"], c["ccat"], *c["fwd_tabs"])

    def skip_gelu(yv, hv, dv):
        y = yv + dv * hv
        return y, _gelu(y)

    y, ge = _rows(f"s5_skip_gelu_{i}", skip_gelu, [ys, hf], [w["od_d"]], [(D_MODEL, F32), (D_MODEL, BF16)])
    gl = _mm(f"glu_{i}", ge, w["glu"])

    def glu_out(xv, gv, nv):
        x1v = xv + gv[:, :D_MODEL] * _sigmoid(gv[:, D_MODEL:])
        return x1v, (x1v * _rstd(x1v)) * nv

    x1, hn = _rows(f"glu_out_{i}", glu_out, [x, gl], [rp["norm_ffn"][i][None]], [(D_MODEL, F32), (D_MODEL, BF16)])
    return x1, hn, dict(x=x, hb=hb, hf=hf, xin=xin, y=y, ge=ge, gl=gl, c=c)


def _odd_bwd(i, dx1, sv, w, rp):
    o = i // 2
    c = sv["c"]
    g = {}

    def glu_bwd(dv, gv):
        ga, gb = gv[:, :D_MODEL], gv[:, D_MODEL:]
        sg = _sigmoid(gb)
        return (jnp.concatenate([dv * sg, (dv * ga) * (sg * (1.0 - sg))], axis=1),)

    dgl, = _rows(f"glu_out_bwd_{i}", glu_bwd, [dx1, sv["gl"]], [], [(2 * D_MODEL, BF16)])
    g["glu"] = _mm(f"glu_dw_{i}", sv["ge"], dgl, "tn")
    dge = _mm(f"glu_dx_{i}", dgl, w["glu"], "nt")

    def gelu_bwd(dv, yv, hv):
        dy = dv * _gelu_grad(yv)
        return dy, jnp.sum(dy * hv, axis=0, keepdims=True)

    dy, dd = _rows(f"s5_skip_gelu_bwd_{i}", gelu_bwd, [dge, sv["y"], sv["hf"]], [], [(D_MODEL, F32)],
                   accs=[(1, D_MODEL)])
    g["od_d"] = dd[0]
    du, db, dc, dar, dai = _s5_bwd(f"s5_bwd_{i}", sv["hb"], dy, sv["xin"], c["bcat"], c["ccat"], *c["bwd_tabs"])
    dab_re = dar.sum(axis=1).reshape(S5_GROUPS, S5_STATE)
    dab_im = dai.sum(axis=1).reshape(S5_GROUPS, S5_STATE)
    dbb_re = _s5_block_diag_extract(db[:, :, :S5_LANES], S5_GROUP_CH, S5_STATE).transpose(0, 2, 1)
    dbb_im = _s5_block_diag_extract(db[:, :, S5_LANES:], S5_GROUP_CH, S5_STATE).transpose(0, 2, 1)
    g["od_a_re"], g["od_a_im"], g["od_log_dt"], g["od_b_re"], g["od_b_im"] = c["vjp"]((dab_re, dab_im, dbb_re, dbb_im))
    g["od_c_re"] = _s5_block_diag_extract(dc[:, :S5_LANES, :], S5_STATE, S5_GROUP_CH).transpose(0, 2, 1)
    g["od_c_im"] = -_s5_block_diag_extract(dc[:, S5_LANES:, :], S5_STATE, S5_GROUP_CH).transpose(0, 2, 1)

    def norm_bwd(xv, duv, dyv, drv, gv, dv):
        dh = duv + dv * dyv
        r = _rstd(xv)
        xh = xv * r
        dhg = dh * gv
        dx = drv + r * (dhg - xh * jnp.mean(dhg * xh, axis=-1, keepdims=True))
        return dx, jnp.sum(dh * xh, axis=0, keepdims=True)

    dx, dgm = _rows(f"mix_norm_bwd_{i}", norm_bwd, [sv["x"], du, dy, dx1], [rp["norm_mix"][i][None], w["od_d"]],
                    [(D_MODEL, F32)], accs=[(1, D_MODEL)])
    g["norm_mix"] = dgm[0]
    return dx, g


def _local_step(x, p, target, lw, rp):
    saved = []
    h_next = None
    for i in range(DEPTH):
        x, hn, s_mix = (_even_fwd if i % 2 == 0 else _odd_fwd)(i, x, lw[i], rp, h_next)
        nxt = None
        if i + 1 < DEPTH:
            nxt = (rp["norm_mix"][i + 1][None], [BF16, F32] if (i + 1) % 2 else [BF16])
        x, h_next, s_ch = _channel_fwd(i, x, p[i], lw[i], rp, hn, nxt)
        saved.append((s_mix, s_ch))

    def loss_fn(yv, tv):
        diff = yv - tv
        return diff * (1.0 / D_MODEL), jnp.sum(diff * diff, axis=0, keepdims=True)

    dx, sq = _rows("loss", loss_fn, [x, target], [], [(D_MODEL, F32)], accs=[(1, D_MODEL)])
    loss = 0.5 * jnp.sum(sq) / D_MODEL
    grads = [None] * DEPTH
    for i in reversed(range(DEPTH)):
        s_mix, s_ch = saved[i]
        dx, g_ch = _channel_bwd(i, dx, s_ch, lw[i], rp)
        dx, g_mix = (_even_bwd if i % 2 == 0 else _odd_bwd)(i, dx, s_mix, lw[i], rp)
        grads[i] = {**g_ch, **g_mix}
    return loss, dx, grads


WEIGHT_ORDER = ["norm_mix", "norm_ffn", "ev_w_in", "ev_b_fgate", "ev_q_norm", "ev_k_norm", "ev_v_norm", "ev_w_spatial",
                "ev_b_spatial", "ev_w_out", "od_a_re", "od_a_im", "od_log_dt", "od_b_re", "od_b_im", "od_c_re",
                "od_c_im", "od_d", "od_w_glu", "ffn_w_up", "ffn_conv_w", "ffn_conv_b", "ffn_w_down", "ple_w_proj",
                "ple_w_gate"]
SHARD_AXIS = {"ev_w_in": 2, "ev_w_out": 1, "od_d": 1, "od_w_glu": 2, "ffn_w_up": 2, "ffn_conv_w": 2, "ffn_w_down": 1,
              "ple_w_proj": 2, "ple_w_gate": 1}
BIG_WEIGHTS = [n for n in WEIGHT_ORDER if n in SHARD_AXIS]
SMALL_WEIGHTS = [n for n in WEIGHT_ORDER if n not in SHARD_AXIS]
KEPT_F32 = ("od_d", "ffn_conv_w")
IN_UV, IN_QKV_END, IN_COLS = 2 * A_WIDTH, 2 * A_WIDTH + 3 * B_WIDTH, 2 * A_WIDTH + 3 * B_WIDTH + B_HEADS


def _layer_weights(i, full, rp):
    w = {}
    up, cw, cb = full["ffn_w_up"][i], full["ffn_conv_w"][i], rp["ffn_conv_b"][i][None]
    w["up_g"], w["up_u"] = up[:, :D_FF], up[:, D_FF:]
    w["cw_g"], w["cw_u"] = cw[:, :D_FF], cw[:, D_FF:]
    w["cb_g"], w["cb_u"] = cb[:, :D_FF], cb[:, D_FF:]
    w["down"], w["ple_proj"], w["ple_gate"] = full["ffn_w_down"][i], full["ple_w_proj"][i], full["ple_w_gate"][i]
    if i % 2 == 0:
        win, wout = full["ev_w_in"][i // 2], full["ev_w_out"][i // 2]
        w["in_uv"], w["in_qkv"] = win[:, :IN_UV], win[:, IN_UV:IN_QKV_END]
        w["in_f"] = jnp.pad(win[:, IN_QKV_END:], ((0, 0), (0, LANES - B_HEADS)))
        w["out_a"], w["out_b"] = wout[:A_WIDTH], wout[A_WIDTH:]
    else:
        w["od_d"], w["glu"] = full["od_d"][i // 2][None], full["od_w_glu"][i // 2]
    return w


def _full_grads(grads):
    ev, od = [grads[i] for i in range(0, DEPTH, 2)], [grads[i] for i in range(1, DEPTH, 2)]
    out = {
        "norm_mix": jnp.stack([g["norm_mix"] for g in grads]), "norm_ffn": jnp.stack([g["norm_ffn"] for g in grads]),
        "ev_w_in": jnp.stack([jnp.concatenate([g["in_uv"], g["in_qkv"], g["in_f"][:, :B_HEADS]], axis=1) for g in ev]),
        "ev_w_out": jnp.stack([jnp.concatenate([g["out_a"], g["out_b"]], axis=0) for g in ev]),
        "od_w_glu": jnp.stack([g["glu"] for g in od]),
        "ffn_w_up": jnp.stack([jnp.concatenate([g["up_g"], g["up_u"]], axis=1) for g in grads]),
        "ffn_conv_w": jnp.stack([jnp.concatenate([g["cw_g"], g["cw_u"]], axis=1) for g in grads]),
        "ffn_conv_b": jnp.stack([g["conv_b"] for g in grads]),
        "ffn_w_down": jnp.stack([g["down"] for g in grads]),
        "ple_w_proj": jnp.stack([g["ple_proj"] for g in grads]),
        "ple_w_gate": jnp.stack([g["ple_gate"] for g in grads]),
    }
    for n in ("ev_b_fgate", "ev_q_norm", "ev_k_norm", "ev_v_norm", "ev_w_spatial", "ev_b_spatial"):
        out[n] = jnp.stack([g[n] for g in ev])
    for n in ("od_a_re", "od_a_im", "od_log_dt", "od_b_re", "od_b_im", "od_c_re", "od_c_im", "od_d"):
        out[n] = jnp.stack([g[n] for g in od])
    return out


def _pack(arrs, row_multiple):
    flat = jnp.concatenate([a.reshape(-1) for a in arrs])
    rows = -(-flat.shape[0] // (PACK_W * row_multiple)) * row_multiple
    return jnp.pad(flat, (0, rows * PACK_W - flat.shape[0])).reshape(rows, PACK_W)


def _unpack(buf, shapes):
    flat = buf.reshape(-1)
    out, at = [], 0
    for s in shapes:
        n = int(np.prod(s))
        out.append(flat[at:at + n].reshape(s))
        at += n
    return out


def _shard(name, a, k):
    ax = SHARD_AXIS[name]
    n = a.shape[ax] // N_CHIPS
    return lax.slice_in_dim(a, k * n, (k + 1) * n, axis=ax)


_ANY = pl.BlockSpec(memory_space=pl.ANY)


def _mesh_pos():
    return lax.axis_index("x"), lax.axis_index("y"), lax.axis_index("c")


def _other_chips(x, y):
    return [(1 - x, y), (x, 1 - y), (1 - x, 1 - y)]


def _gather_shards(name, shards):
    n = len(shards)

    def body(*refs):
        ins, outs = refs[:n], refs[n:2 * n]
        send_sems, recv_sems, local_sems = refs[2 * n:]
        x, y, c = _mesh_pos()
        sibling = (x, y, 1 - c)
        chips = _other_chips(x, y)

        def part(a, k, hc):
            half = shards[a].shape[0] // 2
            return outs[a].at[k, pl.ds(hc * half, half), :]

        def copy(sem, src, dst, to):
            return pltpu.make_async_remote_copy(src_ref=src, dst_ref=dst, send_sem=send_sems.at[sem],
                                                recv_sem=recv_sems.at[sem], device_id=to, device_id_type=MESH)

        local, sent, passed = [], [], []
        for a in range(n):
            half = shards[a].shape[0] // 2
            local.append(pltpu.make_async_copy(ins[a], outs[a].at[2 * x + y], local_sems.at[a]))
            local[-1].start()
            for j, (cx, cy) in enumerate(chips):
                sent.append(copy(6 * a + j, ins[a].at[pl.ds(c * half, half), :], part(a, 2 * x + y, c), (cx, cy, c)))
                sent[-1].start()
        for a in range(n):
            for j, (cx, cy) in enumerate(chips):
                blk = part(a, 2 * cx + cy, c)
                copy(6 * a + j, blk, blk, (cx, cy, c)).wait_recv()
                passed.append(copy(6 * a + 3 + j, blk, blk, sibling))
                passed[-1].start()
        for a in range(n):
            for j, (cx, cy) in enumerate(chips):
                blk = part(a, 2 * cx + cy, 1 - c)
                copy(6 * a + 3 + j, blk, blk, sibling).wait_recv()
        for cp in sent + passed:
            cp.wait_send()
        for cp in local:
            cp.wait()

    return pl.pallas_call(
        body, out_shape=[jax.ShapeDtypeStruct((N_CHIPS,) + s.shape, s.dtype) for s in shards],
        in_specs=[_ANY] * n, out_specs=[_ANY] * n,
        scratch_shapes=[pltpu.SemaphoreType.DMA((6 * n,)), pltpu.SemaphoreType.DMA((6 * n,)),
                        pltpu.SemaphoreType.DMA((n,))],
        name=name,
    )(*shards)


def _swap_halves(name, arrs):
    n = len(arrs)

    def body(*refs):
        ins, outs = refs[:n], refs[n:2 * n]
        send_sems, recv_sems = refs[2 * n:]
        x, y, c = _mesh_pos()
        cps = []
        for a in range(n):
            half = arrs[a].shape[1] // 2
            cps.append(pltpu.make_async_remote_copy(
                src_ref=ins[a].at[:, pl.ds((1 - c) * half, half), :], dst_ref=outs[a], send_sem=send_sems.at[a],
                recv_sem=recv_sems.at[a], device_id=(x, y, 1 - c), device_id_type=MESH))
            cps[-1].start()
        for cp in cps:
            cp.wait()

    return pl.pallas_call(
        body, out_shape=[jax.ShapeDtypeStruct((a.shape[0], a.shape[1] // 2, a.shape[2]), a.dtype) for a in arrs],
        in_specs=[_ANY] * n, out_specs=[_ANY] * n,
        scratch_shapes=[pltpu.SemaphoreType.DMA((n,)), pltpu.SemaphoreType.DMA((n,))], name=name,
    )(*arrs)


def _send_to_owner_chips(name, arrs):
    n = len(arrs)

    def body(*refs):
        ins, outs = refs[:n], refs[n:2 * n]
        send_sems, recv_sems = refs[2 * n:]
        x, y, c = _mesh_pos()
        cps = []
        for a in range(n):
            for j, (cx, cy) in enumerate(_other_chips(x, y)):
                cps.append(pltpu.make_async_remote_copy(
                    src_ref=ins[a].at[2 * cx + cy], dst_ref=outs[a].at[j], send_sem=send_sems.at[3 * a + j],
                    recv_sem=recv_sems.at[3 * a + j], device_id=(cx, cy, c), device_id_type=MESH))
                cps[-1].start()
        for cp in cps:
            cp.wait()

    return pl.pallas_call(
        body, out_shape=[jax.ShapeDtypeStruct((3,) + a.shape[1:], a.dtype) for a in arrs],
        in_specs=[_ANY] * n, out_specs=[_ANY] * n,
        scratch_shapes=[pltpu.SemaphoreType.DMA((3 * n,)), pltpu.SemaphoreType.DMA((3 * n,))], name=name,
    )(*arrs)


def _swap_with_sibling(name, arrs):
    n = len(arrs)

    def body(*refs):
        ins, outs = refs[:n], refs[n:2 * n]
        send_sems, recv_sems = refs[2 * n:]
        x, y, c = _mesh_pos()
        cps = []
        for a in range(n):
            cps.append(pltpu.make_async_remote_copy(
                src_ref=ins[a], dst_ref=outs[a], send_sem=send_sems.at[a], recv_sem=recv_sems.at[a],
                device_id=(x, y, 1 - c), device_id_type=MESH))
            cps[-1].start()
        for cp in cps:
            cp.wait()

    return pl.pallas_call(
        body, out_shape=[jax.ShapeDtypeStruct(a.shape, a.dtype) for a in arrs],
        in_specs=[_ANY] * n, out_specs=[_ANY] * n,
        scratch_shapes=[pltpu.SemaphoreType.DMA((n,)), pltpu.SemaphoreType.DMA((n,))], name=name,
    )(*arrs)


def _all_gather_devices(name, a):
    rows, w = a.shape

    def body(a_ref, out_ref, send_sems, recv_sems, local_sem):
        x, y, c = _mesh_pos()
        me, sibling = (x, y, c), (x, y, 1 - c)
        chips = _other_chips(x, y)

        def slot(px, py, pc):
            return out_ref.at[4 * px + 2 * py + pc]

        def copy(sem, block, to, src=None):
            return pltpu.make_async_remote_copy(src_ref=slot(*block) if src is None else src, dst_ref=slot(*block),
                                                send_sem=send_sems.at[sem], recv_sem=recv_sems.at[sem], device_id=to,
                                                device_id_type=MESH)

        mine = pltpu.make_async_copy(a_ref, slot(*me), local_sem)
        mine.start()
        first = [copy(0, me, sibling, src=a_ref)]
        first += [copy(1 + j, me, (*chip, c), src=a_ref) for j, chip in enumerate(chips)]
        for cp in first:
            cp.start()
        passed = [copy(4 + j, (*chip, c), sibling) for j, chip in enumerate(chips)]
        for j, chip in enumerate(chips):
            copy(1 + j, (*chip, c), me).wait_recv()
            passed[j].start()
        copy(0, sibling, me).wait_recv()
        for j, chip in enumerate(chips):
            copy(4 + j, (*chip, 1 - c), me).wait_recv()
        for cp in first + passed:
            cp.wait_send()
        mine.wait()

    return pl.pallas_call(
        body, out_shape=jax.ShapeDtypeStruct((8, rows, w), a.dtype), in_specs=[_ANY], out_specs=_ANY,
        scratch_shapes=[pltpu.SemaphoreType.DMA((7,)), pltpu.SemaphoreType.DMA((7,)), pltpu.SemaphoreType.DMA],
        name=name,
    )(a)


_PACK_TILE = 256


def _sum_rows(name, arrs):
    def fn(*vals):
        tot = vals[0]
        for v in vals[1:]:
            tot = tot + v
        return (tot,)

    return _rows(name, fn, list(arrs), [], [(arrs[0].shape[1], F32)], tile=_PACK_TILE)[0]


def _adam_math(wv, gv, mv, vv):
    m2 = ADAM_B1 * mv + (1.0 - ADAM_B1) * gv
    v2 = ADAM_B2 * vv + (1.0 - ADAM_B2) * (gv * gv)
    m_hat = m2 / (1.0 - ADAM_B1 ** ADAM_STEP)
    v_hat = v2 / (1.0 - ADAM_B2 ** ADAM_STEP)
    delta = -ADAM_LR * (m_hat / (jnp.sqrt(v_hat) + ADAM_EPS) + ADAM_WD * wv)
    return delta, m2, v2


def _adamw(name, w, g, m, v):
    return _rows(name, _adam_math, [w, g, m, v], [], [(w.shape[1], F32)] * 3, tile=_PACK_TILE)


_INPUT_ORDER = (["x", "p"] + WEIGHT_ORDER + ["loss_target"] + ["m_" + n for n in WEIGHT_ORDER]
                + ["v_" + n for n in WEIGHT_ORDER])


_SUM_BLOCK_BYTES = 3 * 512 * 1024


def _sum_tile(half, w):
    best = 16
    for t in range(16, half + 1, 16):
        if half % t == 0 and t * w * 4 <= _SUM_BLOCK_BYTES:
            best = t
    return best


def _pair_sum(name, g, got, core):
    nk, rows, w = g.shape
    half = rows // 2
    tile_rows = _sum_tile(half, w)
    nt = half // tile_rows

    def body(c_ref, g_ref, got_ref, o_ref, ob_ref):
        tot = g_ref[...] + got_ref[...]
        o_ref[...] = tot
        ob_ref[...] = tot.astype(BF16)

    spec = pl.BlockSpec((None, tile_rows, w), lambda k, i, c: (k, i, 0))
    grid_spec = pltpu.PrefetchScalarGridSpec(
        num_scalar_prefetch=1, grid=(nk, nt),
        in_specs=[pl.BlockSpec((None, tile_rows, w), lambda k, i, c: (k, c[0] * nt + i, 0)), spec],
        out_specs=[spec, spec])
    return pl.pallas_call(
        body, grid_spec=grid_spec, name=name, compiler_params=_cparams(("parallel", "parallel")),
        out_shape=[jax.ShapeDtypeStruct((nk, half, w), F32), jax.ShapeDtypeStruct((nk, half, w), BF16)])(core, g, got)


def _owner_sum(name, pair, owed, chip):
    _, half, w = pair.shape
    tile_rows = _sum_tile(half, w)

    def body(k_ref, p_ref, a_ref, b_ref, c_ref, o_ref):
        o_ref[...] = ((p_ref[...] + a_ref[...].astype(F32)) + b_ref[...].astype(F32)) + c_ref[...].astype(F32)

    def owed_spec(j):
        return pl.BlockSpec((None, tile_rows, w), lambda i, k: (j, i, 0))

    grid_spec = pltpu.PrefetchScalarGridSpec(
        num_scalar_prefetch=1, grid=(half // tile_rows,),
        in_specs=[pl.BlockSpec((None, tile_rows, w), lambda i, k: (k[0], i, 0)), owed_spec(0), owed_spec(1),
                  owed_spec(2)],
        out_specs=pl.BlockSpec((tile_rows, w), lambda i, k: (i, 0)))
    return pl.pallas_call(body, grid_spec=grid_spec, out_shape=jax.ShapeDtypeStruct((half, w), F32), name=name,
                          compiler_params=_cparams(("parallel",)))(chip, pair, owed, owed, owed)


def _adamw_halves(name, w, mine, other, m, v, core):
    rows, wd = w.shape
    tile_rows = _sum_tile(rows // 2, wd)
    nh = (rows // 2) // tile_rows

    def body(c_ref, w_ref, a_ref, b_ref, m_ref, v_ref, g_ref, d_ref, m2_ref, v2_ref):
        own = (pl.program_id(0) // nh) == c_ref[0]
        g = jnp.where(own, a_ref[...], b_ref[...])
        g_ref[...] = g
        d_ref[...], m2_ref[...], v2_ref[...] = _adam_math(w_ref[...], g, m_ref[...], v_ref[...])

    full = pl.BlockSpec((tile_rows, wd), lambda i, c: (i, 0))
    part = pl.BlockSpec((tile_rows, wd), lambda i, c: (lax.rem(i, nh), 0))
    grid_spec = pltpu.PrefetchScalarGridSpec(num_scalar_prefetch=1, grid=(2 * nh,),
                                             in_specs=[full, part, part, full, full], out_specs=[full] * 4)
    return pl.pallas_call(body, grid_spec=grid_spec, out_shape=[jax.ShapeDtypeStruct((rows, wd), F32)] * 4, name=name,
                          compiler_params=_cparams(("parallel",)))(core, w, mine, other, m, v)


MATRIX_WEIGHTS = [n for n in BIG_WEIGHTS if n not in KEPT_F32]
TINY_SHARDED = [n for n in BIG_WEIGHTS if n in KEPT_F32]


def _as_rows(a):
    return a.reshape(-1, a.shape[-1])


def _owner_major(grads):
    ev, od = [grads[i] for i in range(0, DEPTH, 2)], [grads[i] for i in range(1, DEPTH, 2)]

    def cols(m, k, n):
        w = m.shape[1] // n
        return m[:, k * w:(k + 1) * w]

    def rows(m, k, n):
        r = m.shape[0] // n
        return m[k * r:(k + 1) * r]

    w_in = [jnp.concatenate([g["in_uv"], g["in_qkv"], g["in_f"][:, :B_HEADS]], axis=1) for g in ev]
    per_chip = {
        "ev_w_in": lambda k: [cols(m, k, N_CHIPS) for m in w_in],
        "ev_w_out": lambda k: [rows(g["out_a"] if k < 2 else g["out_b"], k % 2, 2) for g in ev],
        "od_w_glu": lambda k: [cols(g["glu"], k, N_CHIPS) for g in od],
        "ffn_w_up": lambda k: [cols(g["up_g"] if k < 2 else g["up_u"], k % 2, 2) for g in grads],
        "ffn_w_down": lambda k: [rows(g["down"], k, N_CHIPS) for g in grads],
        "ple_w_proj": lambda k: [cols(g["ple_proj"], k, N_CHIPS) for g in grads],
        "ple_w_gate": lambda k: [rows(g["ple_gate"], k, N_CHIPS) for g in grads],
    }
    return {n: jnp.stack([jnp.concatenate(per_chip[n](k), axis=0) for k in range(N_CHIPS)]) for n in MATRIX_WEIGHTS}


def _small_grads(grads):
    ev, od = [grads[i] for i in range(0, DEPTH, 2)], [grads[i] for i in range(1, DEPTH, 2)]
    out = {"norm_mix": jnp.stack([g["norm_mix"] for g in grads]), "norm_ffn": jnp.stack([g["norm_ffn"] for g in grads]),
           "ffn_conv_w": jnp.stack([jnp.concatenate([g["cw_g"], g["cw_u"]], axis=1) for g in grads]),
           "ffn_conv_b": jnp.stack([g["conv_b"] for g in grads])}
    for n in ("ev_b_fgate", "ev_q_norm", "ev_k_norm", "ev_v_norm", "ev_w_spatial", "ev_b_spatial"):
        out[n] = jnp.stack([g[n] for g in ev])
    for n in ("od_a_re", "od_a_im", "od_log_dt", "od_b_re", "od_b_im", "od_c_re", "od_c_im", "od_d"):
        out[n] = jnp.stack([g[n] for g in od])
    return out


def _step(a):
    xi, yi, ci = _mesh_pos()
    chip = 2 * xi + yi
    core_arr, chip_arr = ci.astype(jnp.int32).reshape(1), chip.astype(jnp.int32).reshape(1)
    rp = {n: a[n] for n in SMALL_WEIGHTS}

    tiny = _pack([a[n] for n in TINY_SHARDED], 32)
    gathered = _gather_shards("gather_weights", [_as_rows(a[n]).astype(BF16) for n in MATRIX_WEIGHTS] + [tiny])
    full = {}
    for n, g in zip(MATRIX_WEIGHTS, gathered):
        full[n] = jnp.concatenate([g[k].reshape(a[n].shape) for k in range(N_CHIPS)], axis=SHARD_AXIS[n])
    tiny_parts = [_unpack(gathered[-1][k], [a[n].shape for n in TINY_SHARDED]) for k in range(N_CHIPS)]
    for idx, n in enumerate(TINY_SHARDED):
        full[n] = jnp.concatenate([tiny_parts[k][idx] for k in range(N_CHIPS)], axis=SHARD_AXIS[n])
    lw = [_layer_weights(i, full, rp) for i in range(DEPTH)]

    loss_local, grad_x, grads = _local_step(a["x"][0], a["p"][:, 0], a["loss_target"][0], lw, rp)
    loss = lax.psum(loss_local, ("x", "y", "c"))

    contrib = _owner_major(grads)
    mats = [contrib[n] for n in MATRIX_WEIGHTS]
    got = _swap_halves("grad_pair_swap", mats)
    pair = [_pair_sum(f"grad_pair_sum_{n}", g, h, core_arr) for n, g, h in zip(MATRIX_WEIGHTS, mats, got)]
    owed = _send_to_owner_chips("grad_to_owner", [pb for _, pb in pair])
    mine = [_owner_sum(f"grad_owner_sum_{n}", p, o, chip_arr) for n, (p, _), o in zip(MATRIX_WEIGHTS, pair, owed)]
    theirs = _swap_with_sibling("grad_half_swap", mine)

    small_names = SMALL_WEIGHTS + TINY_SHARDED
    sg = _small_grads(grads)
    everyone = _all_gather_devices("small_grad_gather", _pack([sg[n] for n in small_names], _PACK_TILE))
    g_small = _sum_rows("small_grad_sum", [everyone[d] for d in range(8)])
    small_full = dict(zip(small_names, _unpack(g_small, [sg[n].shape for n in small_names])))

    out = {}
    for n, own_half, other_half in zip(MATRIX_WEIGHTS, mine, theirs):
        shape = a[n].shape
        g2d, delta, m2, v2 = _adamw_halves(f"adamw_{n}", _as_rows(a[n]), own_half, other_half,
                                           _as_rows(a["m_" + n]), _as_rows(a["v_" + n]), core_arr)
        for kind, val in (("grad", g2d), ("delta", delta), ("new_m", m2), ("new_v", v2)):
            out[kind + "_" + n] = val.reshape(shape)
    g_sm = {n: small_full[n] for n in SMALL_WEIGHTS}
    for n in TINY_SHARDED:
        width = a[n].shape[SHARD_AXIS[n]]
        g_sm[n] = lax.dynamic_slice_in_dim(small_full[n], chip * width, width, axis=SHARD_AXIS[n])
    shapes = [a[n].shape for n in small_names]
    w, m, v = (_pack([a[pre + n] for n in small_names], _PACK_TILE) for pre in ("", "m_", "v_"))
    g = _pack([g_sm[n] for n in small_names], _PACK_TILE)
    delta, m2, v2 = _adamw("adamw_small", w, g, m, v)
    for kind, buf in (("delta", delta), ("new_m", m2), ("new_v", v2)):
        for n, val in zip(small_names, _unpack(buf, shapes)):
            out[kind + "_" + n] = val
    for n in small_names:
        out["grad_" + n] = g_sm[n]
    res = [loss, grad_x[None]]
    for kind in ("grad", "delta", "new_m", "new_v"):
        res += [out[kind + "_" + n] for n in WEIGHT_ORDER]
    return tuple(res)


def kernel(x, p, norm_mix, norm_ffn, ev_w_in, ev_b_fgate, ev_q_norm, ev_k_norm, ev_v_norm, ev_w_spatial, ev_b_spatial, ev_w_out, od_a_re, od_a_im, od_log_dt, od_b_re, od_b_im, od_c_re, od_c_im, od_d, od_w_glu, ffn_w_up, ffn_conv_w, ffn_conv_b, ffn_w_down, ple_w_proj, ple_w_gate, loss_target, m_norm_mix, m_norm_ffn, m_ev_w_in, m_ev_b_fgate, m_ev_q_norm, m_ev_k_norm, m_ev_v_norm, m_ev_w_spatial, m_ev_b_spatial, m_ev_w_out, m_od_a_re, m_od_a_im, m_od_log_dt, m_od_b_re, m_od_b_im, m_od_c_re, m_od_c_im, m_od_d, m_od_w_glu, m_ffn_w_up, m_ffn_conv_w, m_ffn_conv_b, m_ffn_w_down, m_ple_w_proj, m_ple_w_gate, v_norm_mix, v_norm_ffn, v_ev_w_in, v_ev_b_fgate, v_ev_q_norm, v_ev_k_norm, v_ev_v_norm, v_ev_w_spatial, v_ev_b_spatial, v_ev_w_out, v_od_a_re, v_od_a_im, v_od_log_dt, v_od_b_re, v_od_b_im, v_od_c_re, v_od_c_im, v_od_d, v_od_w_glu, v_ffn_w_up, v_ffn_conv_w, v_ffn_conv_b, v_ffn_w_down, v_ple_w_proj, v_ple_w_gate):
    args = (x, p, norm_mix, norm_ffn, ev_w_in, ev_b_fgate, ev_q_norm, ev_k_norm, ev_v_norm, ev_w_spatial, ev_b_spatial, ev_w_out, od_a_re, od_a_im, od_log_dt, od_b_re, od_b_im, od_c_re, od_c_im, od_d, od_w_glu, ffn_w_up, ffn_conv_w, ffn_conv_b, ffn_w_down, ple_w_proj, ple_w_gate, loss_target, m_norm_mix, m_norm_ffn, m_ev_w_in, m_ev_b_fgate, m_ev_q_norm, m_ev_k_norm, m_ev_v_norm, m_ev_w_spatial, m_ev_b_spatial, m_ev_w_out, m_od_a_re, m_od_a_im, m_od_log_dt, m_od_b_re, m_od_b_im, m_od_c_re, m_od_c_im, m_od_d, m_od_w_glu, m_ffn_w_up, m_ffn_conv_w, m_ffn_conv_b, m_ffn_w_down, m_ple_w_proj, m_ple_w_gate, v_norm_mix, v_norm_ffn, v_ev_w_in, v_ev_b_fgate, v_ev_q_norm, v_ev_k_norm, v_ev_v_norm, v_ev_w_spatial, v_ev_b_spatial, v_ev_w_out, v_od_a_re, v_od_a_im, v_od_log_dt, v_od_b_re, v_od_b_im, v_od_c_re, v_od_c_im, v_od_d, v_od_w_glu, v_ffn_w_up, v_ffn_conv_w, v_ffn_conv_b, v_ffn_w_down, v_ple_w_proj, v_ple_w_gate)
    return _step(dict(zip(_INPUT_ORDER, args)))
```

```python
import functools
import math

import jax
import jax.numpy as jnp
import numpy as np
from jax import lax
from jax.experimental import pallas as pl
from jax.experimental.pallas import tpu as pltpu

F32 = jnp.float32
BF16 = jnp.bfloat16
MESH = pl.DeviceIdType.MESH

V7X_VMEM_LIMIT_BYTES = 56 * 1024 * 1024
LANES = 128

D_MODEL = 1024
DEPTH = 4
A_GROUPS = 4
A_CHUNK = 128
A_WIDTH = 512
B_HEADS = 8
B_HEAD_DIM = 64
B_WIDTH = 512
S5_GROUP_CH = 16
S5_GROUPS = 64
S5_STATE = 64
S5_BLOCKS = 8
S5_LANES = 512
D_FF = 2816
PLE_DIM = 256
EPS = 1e-6
NEG_INF = -1e30

ADAM_LR = 0.001
ADAM_B1 = 0.9
ADAM_B2 = 0.999
ADAM_EPS = 1e-08
ADAM_WD = 0.01
ADAM_STEP = 10

N_CHIPS = 4
PACK_W = 1024


def _cparams(sem):
    return pltpu.CompilerParams(dimension_semantics=sem, vmem_limit_bytes=V7X_VMEM_LIMIT_BYTES)


def _pick(n, target):
    if n <= target:
        return n
    t = (target // LANES) * LANES
    while t >= LANES:
        if n % t == 0:
            return t
        t -= LANES
    return n


_GELU_K = 0.7978845608028654
_GELU_C = 0.044715


def _gelu(x):
    return x * (0.5 * (1.0 + jnp.tanh(_GELU_K * (x + _GELU_C * (x * x * x)))))


def _gelu_grad(x):
    x2 = x * x
    t = jnp.tanh(_GELU_K * (x + _GELU_C * (x * x2)))
    return 0.5 * (1.0 + t) + (0.5 * x) * (1.0 - t * t) * (_GELU_K * (1.0 + (3.0 * _GELU_C) * x2))


def _sigmoid(x):
    return 0.5 * jnp.tanh(0.5 * x) + 0.5


def _log_sigmoid(x):
    return -(jnp.maximum(-x, 0.0) + jnp.log(1.0 + jnp.exp(-jnp.abs(x))))


def _rstd(x):
    return lax.rsqrt(jnp.mean(x * x, axis=-1, keepdims=True) + EPS)


def _rows(name, fn, row_ins, full_ins, outs, accs=(), tile=256):
    rows = row_ins[0].shape[0]
    r = min(tile, rows)
    n = rows // r
    n_in = len(row_ins) + len(full_ins)
    n_out = len(outs)

    def body(*refs):
        res = fn(*[ref[...] for ref in refs[:n_in]])
        for ref, v in zip(refs[n_in:n_in + n_out], res[:n_out]):
            ref[...] = v.astype(ref.dtype)
        acc_refs = refs[n_in + n_out:]
        if acc_refs:
            @pl.when(pl.program_id(0) == 0)
            def _():
                for ref in acc_refs:
                    ref[...] = jnp.zeros(ref.shape, ref.dtype)

            for ref, v in zip(acc_refs, res[n_out:]):
                ref[...] += v

    in_specs = [pl.BlockSpec((r, a.shape[1]), lambda i: (i, 0)) for a in row_ins]
    in_specs += [pl.BlockSpec(a.shape, lambda i, nd=a.ndim: (0,) * nd) for a in full_ins]
    out_shape = [jax.ShapeDtypeStruct((rows, w), dt) for (w, dt) in outs]
    out_shape += [jax.ShapeDtypeStruct(s, F32) for s in accs]
    out_specs = [pl.BlockSpec((r, w), lambda i: (i, 0)) for (w, dt) in outs]
    out_specs += [pl.BlockSpec(s, lambda i, nd=len(s): (0,) * nd) for s in accs]
    return pl.pallas_call(
        body, grid=(n,), in_specs=in_specs, out_specs=out_specs, out_shape=out_shape, name=name,
        compiler_params=_cparams(("arbitrary",) if accs else ("parallel",)),
    )(*row_ins, *full_ins)


_DOT_DIMS = {"nn": (((1,), (0,)), ((), ())), "nt": (((1,), (1,)), ((), ())), "tn": (((0,), (0,)), ((), ()))}


def _mm(name, a, b, mode="nn", out_dtype=F32, res=None, tm=1024, tn=1024, tk=1024, norm_gain=None):
    if mode == "nn":
        (m, k), (k2, n) = a.shape, b.shape
    elif mode == "nt":
        (m, k), (n, k2) = a.shape, b.shape
    else:
        (k, m), (k2, n) = a.shape, b.shape
    assert k == k2, (name, a.shape, b.shape, mode)
    tm, tn, tk = _pick(m, tm), _pick(n, tn), _pick(k, tk)
    nk = k // tk
    dims = _DOT_DIMS[mode]
    has_res = res is not None
    has_norm = norm_gain is not None
    assert not has_norm or tn == n, (name, tn, n)
    n_in = 2 + has_res + has_norm

    def body(*refs):
        a_ref, b_ref = refs[0], refs[1]
        res_ref = refs[2] if has_res else None
        gain_ref = refs[n_in - 1] if has_norm else None
        o_ref = refs[n_in]
        h_ref = refs[n_in + 1] if has_norm else None

        def finish(tot):
            if has_res:
                tot = res_ref[...] + tot
            o_ref[...] = tot.astype(o_ref.dtype)
            if has_norm:
                h_ref[...] = ((tot * _rstd(tot)) * gain_ref[...]).astype(h_ref.dtype)

        prod = lax.dot_general(a_ref[...].astype(BF16), b_ref[...].astype(BF16), dims, preferred_element_type=F32)
        if nk == 1:
            finish(prod)
            return
        acc = refs[-1]
        kk = pl.program_id(2)

        @pl.when(kk == 0)
        def _():
            acc[...] = prod

        @pl.when(kk > 0)
        def _():
            acc[...] += prod

        @pl.when(kk == nk - 1)
        def _():
            finish(acc[...])

    if mode == "nn":
        a_spec = pl.BlockSpec((tm, tk), lambda i, j, kk: (i, kk))
        b_spec = pl.BlockSpec((tk, tn), lambda i, j, kk: (kk, j))
    elif mode == "nt":
        a_spec = pl.BlockSpec((tm, tk), lambda i, j, kk: (i, kk))
        b_spec = pl.BlockSpec((tn, tk), lambda i, j, kk: (j, kk))
    else:
        a_spec = pl.BlockSpec((tk, tm), lambda i, j, kk: (kk, i))
        b_spec = pl.BlockSpec((tk, tn), lambda i, j, kk: (kk, j))
    o_spec = pl.BlockSpec((tm, tn), lambda i, j, kk: (i, j))
    in_specs = [a_spec, b_spec] + ([o_spec] if has_res else [])
    in_specs += [pl.BlockSpec((1, tn), lambda i, j, kk: (0, j))] if has_norm else []
    args = (a, b) + ((res,) if has_res else ()) + ((norm_gain,) if has_norm else ())
    out_shape = jax.ShapeDtypeStruct((m, n), out_dtype)
    return pl.pallas_call(
        body, grid=(m // tm, n // tn, nk), in_specs=in_specs, out_specs=[o_spec, o_spec] if has_norm else o_spec,
        out_shape=[out_shape, jax.ShapeDtypeStruct((m, n), BF16)] if has_norm else out_shape, name=name,
        scratch_shapes=[pltpu.VMEM((tm, tn), F32)] if nk > 1 else [],
        compiler_params=_cparams(("parallel", "parallel", "arbitrary")),
    )(*args)


def _mm_norm_bwd(name, a, b, mode, x, dres, gain, res=None, tm=512, tk=1024):
    if mode == "nn":
        (m, k), (k2, n) = a.shape, b.shape
    else:
        (m, k), (n, k2) = a.shape, b.shape
    assert k == k2, (name, a.shape, b.shape, mode)
    tm, tk = _pick(m, tm), _pick(k, tk)
    nk = k // tk
    dims = _DOT_DIMS[mode]
    has_res = res is not None
    n_in = 5 + has_res

    def body(*refs):
        a_ref, b_ref = refs[0], refs[1]
        res_ref = refs[2] if has_res else None
        x_ref, dres_ref, gain_ref = refs[n_in - 3:n_in]
        o_ref, dg_ref = refs[n_in], refs[n_in + 1]

        def finish(d):
            if has_res:
                d = res_ref[...] + d
            xv = x_ref[...]
            r = _rstd(xv)
            xh = xv * r
            dyg = d * gain_ref[...]
            o_ref[...] = dres_ref[...] + r * (dyg - xh * jnp.mean(dyg * xh, axis=-1, keepdims=True))
            part = jnp.sum(d * xh, axis=0, keepdims=True)

            @pl.when(pl.program_id(0) == 0)
            def _():
                dg_ref[...] = part

            @pl.when(pl.program_id(0) > 0)
            def _():
                dg_ref[...] += part

        prod = lax.dot_general(a_ref[...].astype(BF16), b_ref[...].astype(BF16), dims, preferred_element_type=F32)
        if nk == 1:
            finish(prod)
            return
        acc = refs[-1]
        kk = pl.program_id(1)

        @pl.when(kk == 0)
        def _():
            acc[...] = prod

        @pl.when(kk > 0)
        def _():
            acc[...] += prod

        @pl.when(kk == nk - 1)
        def _():
            finish(acc[...])

    a_spec = pl.BlockSpec((tm, tk), lambda i, kk: (i, kk))
    b_spec = pl.BlockSpec((tk, n), lambda i, kk: (kk, 0)) if mode == "nn" else pl.BlockSpec((n, tk), lambda i, kk: (0, kk))
    row_spec = pl.BlockSpec((tm, n), lambda i, kk: (i, 0))
    vec_spec = pl.BlockSpec((1, n), lambda i, kk: (0, 0))
    in_specs = [a_spec, b_spec] + ([row_spec] if has_res else []) + [row_spec, row_spec, vec_spec]
    args = (a, b) + ((res,) if has_res else ()) + (x, dres, gain)
    return pl.pallas_call(
        body, grid=(m // tm, nk), in_specs=in_specs, out_specs=[row_spec, vec_spec],
        out_shape=[jax.ShapeDtypeStruct((m, n), F32), jax.ShapeDtypeStruct((1, n), F32)], name=name,
        scratch_shapes=[pltpu.VMEM((tm, n), F32)] if nk > 1 else [],
        compiler_params=_cparams(("arbitrary", "arbitrary")),
    )(*args)


def _rmsnorm_fwd(name, x, g, outs):
    def fn(xv, gv):
        y = (xv * _rstd(xv)) * gv
        return tuple(y for _ in outs)

    return _rows(name, fn, [x], [g], [(x.shape[1], dt) for dt in outs])


_CONV_ROWS = 256
_CONV_COLS = 1408


def _conv_taps(h_ref, halo_ref, first):
    h = h_ref[...]
    rows = h.shape[0]
    row = lax.broadcasted_iota(jnp.int32, (rows, 1), 0)
    keep = jnp.where(first, 0.0, 1.0)
    m1 = halo_ref[7:8, :] * keep
    m2 = halo_ref[6:7, :] * keep
    p1 = jnp.where(row == 0, m1, pltpu.roll(h, 1, 0))
    p2 = jnp.where(row == 0, m2, jnp.where(row == 1, m1, pltpu.roll(h, 2, 0)))
    return h, p1, p2


def _conv_specs(rows, r, cw):
    tile = pl.BlockSpec((r, cw), lambda j, i: (i, j))
    halo = pl.BlockSpec((8, cw), lambda j, i: (jnp.maximum(i * (r // 8) - 1, 0), j))
    vec3 = pl.BlockSpec((3, cw), lambda j, i: (0, j))
    vec1 = pl.BlockSpec((1, cw), lambda j, i: (0, j))
    return tile, halo, vec3, vec1


def _convffn_fwd(name, hg, hu, wg, wu, bg, bu):
    rows, f = hg.shape
    r, cw = min(_CONV_ROWS, rows), _pick(f, _CONV_COLS)

    def body(hg_ref, hgh_ref, hu_ref, huh_ref, wg_ref, wu_ref, bg_ref, bu_ref, o_ref):
        first = pl.program_id(1) == 0
        h, p1, p2 = _conv_taps(hg_ref, hgh_ref, first)
        g = bg_ref[...] + wg_ref[0:1, :] * p2 + wg_ref[1:2, :] * p1 + wg_ref[2:3, :] * h
        h, p1, p2 = _conv_taps(hu_ref, huh_ref, first)
        u = bu_ref[...] + wu_ref[0:1, :] * p2 + wu_ref[1:2, :] * p1 + wu_ref[2:3, :] * h
        o_ref[...] = ((g * _sigmoid(g)) * u).astype(o_ref.dtype)

    tile, halo, vec3, vec1 = _conv_specs(rows, r, cw)
    return pl.pallas_call(
        body, grid=(f // cw, rows // r), in_specs=[tile, halo, tile, halo, vec3, vec3, vec1, vec1], out_specs=tile,
        out_shape=jax.ShapeDtypeStruct((rows, f), BF16), name=name, compiler_params=_cparams(("parallel", "parallel")),
    )(hg, hg, hu, hu, wg, wu, bg, bu)


def _gate_grads(da, g, u):
    sg = _sigmoid(g)
    return da * u * (sg * (1.0 + g * (1.0 - sg))), da * (g * sg)


def _conv_back(dc, dc_next, w_ref, last):
    r = dc.shape[0]
    row = lax.broadcasted_iota(jnp.int32, (r, 1), 0)
    keep = jnp.where(last, 0.0, 1.0)
    n0 = dc_next[0:1, :] * keep
    n1 = dc_next[1:2, :] * keep
    f1 = jnp.where(row == r - 1, n0, pltpu.roll(dc, r - 1, 0))
    f2 = jnp.where(row == r - 1, n1, jnp.where(row == r - 2, n0, pltpu.roll(dc, r - 2, 0)))
    return w_ref[2:3, :] * dc + w_ref[1:2, :] * f1 + w_ref[0:1, :] * f2


def _conv_next_rows(h, nxt_ref, w_ref, b_ref):
    r = h.shape[0]
    hn = nxt_ref[...]
    row = lax.broadcasted_iota(jnp.int32, (8, 1), 0)
    m1, m2 = h[r - 1:r, :], h[r - 2:r - 1, :]
    p1 = jnp.where(row == 0, m1, pltpu.roll(hn, 1, 0))
    p2 = jnp.where(row == 0, m2, jnp.where(row == 1, m1, pltpu.roll(hn, 2, 0)))
    return b_ref[...] + w_ref[0:1, :] * p2 + w_ref[1:2, :] * p1 + w_ref[2:3, :] * hn


def _convffn_bwd(name, da, hg, hu, wg, wu, bg, bu):
    rows, f = hg.shape
    r, cw = min(_CONV_ROWS, rows), _pick(f, _CONV_COLS)
    nrt = rows // r

    def body(da_ref, dan_ref, hg_ref, hgh_ref, hgn_ref, hu_ref, huh_ref, hun_ref, wg_ref, wu_ref, bg_ref, bu_ref,
             dhg_ref, dhu_ref, dwg_ref, dwu_ref, dbg_ref, dbu_ref):
        first = pl.program_id(1) == 0
        last = pl.program_id(1) == nrt - 1
        hgv, g1, g2 = _conv_taps(hg_ref, hgh_ref, first)
        g = bg_ref[...] + wg_ref[0:1, :] * g2 + wg_ref[1:2, :] * g1 + wg_ref[2:3, :] * hgv
        huv, u1, u2 = _conv_taps(hu_ref, huh_ref, first)
        u = bu_ref[...] + wu_ref[0:1, :] * u2 + wu_ref[1:2, :] * u1 + wu_ref[2:3, :] * huv
        dcg, dcu = _gate_grads(da_ref[...], g, u)
        dcg_n, dcu_n = _gate_grads(dan_ref[...], _conv_next_rows(hgv, hgn_ref, wg_ref, bg_ref),
                                   _conv_next_rows(huv, hun_ref, wu_ref, bu_ref))
        dhg_ref[...] = _conv_back(dcg, dcg_n, wg_ref, last).astype(dhg_ref.dtype)
        dhu_ref[...] = _conv_back(dcu, dcu_n, wu_ref, last).astype(dhu_ref.dtype)

        @pl.when(first)
        def _():
            for ref in (dwg_ref, dwu_ref, dbg_ref, dbu_ref):
                ref[...] = jnp.zeros(ref.shape, ref.dtype)

        def colsum(v):
            return jnp.sum(v, axis=0, keepdims=True)

        dwg_ref[0:1, :] += colsum(dcg * g2)
        dwg_ref[1:2, :] += colsum(dcg * g1)
        dwg_ref[2:3, :] += colsum(dcg * hgv)
        dwu_ref[0:1, :] += colsum(dcu * u2)
        dwu_ref[1:2, :] += colsum(dcu * u1)
        dwu_ref[2:3, :] += colsum(dcu * huv)
        dbg_ref[...] += colsum(dcg)
        dbu_ref[...] += colsum(dcu)

    tile, halo, vec3, vec1 = _conv_specs(rows, r, cw)
    nxt = pl.BlockSpec((8, cw), lambda j, i: (jnp.minimum((i + 1) * (r // 8), rows // 8 - 1), j))
    big = jax.ShapeDtypeStruct((rows, f), BF16)
    return pl.pallas_call(
        body, grid=(f // cw, nrt),
        in_specs=[tile, nxt, tile, halo, nxt, tile, halo, nxt, vec3, vec3, vec1, vec1],
        out_specs=[tile, tile, vec3, vec3, vec1, vec1],
        out_shape=[big, big, jax.ShapeDtypeStruct((3, f), F32), jax.ShapeDtypeStruct((3, f), F32),
                   jax.ShapeDtypeStruct((1, f), F32), jax.ShapeDtypeStruct((1, f), F32)],
        name=name, compiler_params=_cparams(("parallel", "arbitrary")),
    )(da, da, hg, hg, hg, hu, hu, hu, wg, wu, bg, bu)


_GMLP_ROWS = 256


def _gmlp_group_norm(vg, gain):
    r = lax.rsqrt(jnp.mean(vg * vg, axis=-1, keepdims=True) + EPS)
    vh = vg * r
    return vh, r, vh * gain


def _gmlp_fwd(name, zuv, v_gain, w_tril, b_exp):
    rows = zuv.shape[0]
    r = min(_GMLP_ROWS, rows)

    def body(z_ref, gain_ref, w_ref, b_ref, o_ref):
        for ch in range(r // A_CHUNK):
            lo = ch * A_CHUNK
            for g in range(A_GROUPS):
                c0 = g * LANES
                u = _gelu(z_ref[lo:lo + A_CHUNK, c0:c0 + LANES])
                v = _gelu(z_ref[lo:lo + A_CHUNK, A_WIDTH + c0:A_WIDTH + c0 + LANES])
                _, _, vn = _gmlp_group_norm(v, gain_ref[:, c0:c0 + LANES])
                sv = jnp.dot(w_ref[g], vn.astype(BF16), preferred_element_type=F32) + b_ref[g]
                o_ref[lo:lo + A_CHUNK, c0:c0 + LANES] = (u * sv).astype(o_ref.dtype)

    return pl.pallas_call(
        body, grid=(rows // r,),
        in_specs=[pl.BlockSpec((r, 2 * A_WIDTH), lambda i: (i, 0)), pl.BlockSpec((1, A_WIDTH), lambda i: (0, 0)),
                  pl.BlockSpec((A_GROUPS, A_CHUNK, A_CHUNK), lambda i: (0, 0, 0)),
                  pl.BlockSpec((A_GROUPS, A_CHUNK, LANES), lambda i: (0, 0, 0))],
        out_specs=pl.BlockSpec((r, A_WIDTH), lambda i: (i, 0)),
        out_shape=jax.ShapeDtypeStruct((rows, A_WIDTH), BF16), name=name, compiler_params=_cparams(("parallel",)),
    )(zuv, v_gain, w_tril, b_exp)


def _gmlp_bwd(name, zuv, dya, v_gain, w_tril, w_tril_t, b_exp):
    rows = zuv.shape[0]
    r = min(_GMLP_ROWS, rows)

    def body(z_ref, dy_ref, gain_ref, w_ref, wt_ref, b_ref, dz_ref, dw_ref, db_ref, dgain_ref):
        @pl.when(pl.program_id(0) == 0)
        def _():
            for ref in (dw_ref, db_ref, dgain_ref):
                ref[...] = jnp.zeros(ref.shape, ref.dtype)

        for ch in range(r // A_CHUNK):
            lo = ch * A_CHUNK
            for g in range(A_GROUPS):
                c0 = g * LANES
                zu = z_ref[lo:lo + A_CHUNK, c0:c0 + LANES]
                zv = z_ref[lo:lo + A_CHUNK, A_WIDTH + c0:A_WIDTH + c0 + LANES]
                gain = gain_ref[:, c0:c0 + LANES]
                u = _gelu(zu)
                v = _gelu(zv)
                vh, rr, vn = _gmlp_group_norm(v, gain)
                vn_b = vn.astype(BF16)
                sv = jnp.dot(w_ref[g], vn_b, preferred_element_type=F32) + b_ref[g]
                dy = dy_ref[lo:lo + A_CHUNK, c0:c0 + LANES]
                dsv = dy * u
                dsv_b = dsv.astype(BF16)
                dz_ref[lo:lo + A_CHUNK, c0:c0 + LANES] = ((dy * sv) * _gelu_grad(zu)).astype(dz_ref.dtype)
                dw_ref[g] += lax.dot_general(dsv_b, vn_b, _DOT_DIMS["nt"], preferred_element_type=F32)
                db_ref[g] += dsv
                dvn = jnp.dot(wt_ref[g], dsv_b, preferred_element_type=F32)
                dgain_ref[:, c0:c0 + LANES] += jnp.sum(dvn * vh, axis=0, keepdims=True)
                dvh = dvn * gain
                dv = rr * (dvh - vh * jnp.mean(dvh * vh, axis=-1, keepdims=True))
                dz_ref[lo:lo + A_CHUNK, A_WIDTH + c0:A_WIDTH + c0 + LANES] = (dv * _gelu_grad(zv)).astype(dz_ref.dtype)

    wspec = pl.BlockSpec((A_GROUPS, A_CHUNK, A_CHUNK), lambda i: (0, 0, 0))
    bspec = pl.BlockSpec((A_GROUPS, A_CHUNK, LANES), lambda i: (0, 0, 0))
    gspec = pl.BlockSpec((1, A_WIDTH), lambda i: (0, 0))
    return pl.pallas_call(
        body, grid=(rows // r,),
        in_specs=[pl.BlockSpec((r, 2 * A_WIDTH), lambda i: (i, 0)), pl.BlockSpec((r, A_WIDTH), lambda i: (i, 0)),
                  gspec, wspec, wspec, bspec],
        out_specs=[pl.BlockSpec((r, 2 * A_WIDTH), lambda i: (i, 0)), wspec, bspec, gspec],
        out_shape=[jax.ShapeDtypeStruct((rows, 2 * A_WIDTH), BF16),
                   jax.ShapeDtypeStruct((A_GROUPS, A_CHUNK, A_CHUNK), F32),
                   jax.ShapeDtypeStruct((A_GROUPS, A_CHUNK, LANES), F32), jax.ShapeDtypeStruct((1, A_WIDTH), F32)],
        name=name, compiler_params=_cparams(("arbitrary",)),
    )(zuv, dya, v_gain, w_tril, w_tril_t, b_exp)


_ATT_T = 512
_Q_SCALE = B_HEAD_DIM ** -0.5


def _head_mean(v, bd):
    hi = v.astype(BF16)
    lo = (v - hi.astype(F32)).astype(BF16)
    tot = jnp.dot(hi, bd, preferred_element_type=F32) + jnp.dot(lo, bd, preferred_element_type=F32)
    return tot * (1.0 / B_HEAD_DIM)


def _qkv_prep_fwd(name, zqkv, zf, qg, kg, bf, bd):
    def fn(z, f, qg_v, kg_v, bf_v, bd_v):
        zq, zk, zv = z[:, :B_WIDTH], z[:, B_WIDTH:2 * B_WIDTH], z[:, 2 * B_WIDTH:]
        q = (zq * lax.rsqrt(_head_mean(zq * zq, bd_v) + EPS)) * qg_v * _Q_SCALE
        k = (zk * lax.rsqrt(_head_mean(zk * zk, bd_v) + EPS)) * kg_v
        return q, k, zv, _log_sigmoid(f + bf_v)

    return _rows(name, fn, [zqkv, zf], [qg, kg, bf, bd],
                 [(B_WIDTH, BF16), (B_WIDTH, BF16), (B_WIDTH, BF16), (LANES, F32)])


def _qkv_prep_bwd(name, zqkv, zf, dq, dk, dv, dls, qg, kg, bf, bd):
    def fn(z, f, dq_v, dk_v, dv_v, dls_v, qg_v, kg_v, bf_v, bd_v):
        zq, zk = z[:, :B_WIDTH], z[:, B_WIDTH:2 * B_WIDTH]

        def norm_bwd(x, dy, gain):
            r = lax.rsqrt(_head_mean(x * x, bd_v) + EPS)
            xh = x * r
            dxh = dy * gain
            dx = r * (dxh - xh * _head_mean(dxh * xh, bd_v))
            return dx, jnp.sum(dy * xh, axis=0, keepdims=True)

        dzq, dqg = norm_bwd(zq, dq_v * _Q_SCALE, qg_v)
        dzk, dkg = norm_bwd(zk, dk_v, kg_v)
        dzf = dls_v * (1.0 - _sigmoid(f + bf_v))
        return jnp.concatenate([dzq, dzk, dv_v], axis=1), dzf, dqg, dkg, jnp.sum(dzf, axis=0, keepdims=True)

    return _rows(name, fn, [zqkv, zf, dq, dk, dv, dls], [qg, kg, bf, bd],
                 [(3 * B_WIDTH, BF16), (LANES, BF16)], accs=[(1, B_WIDTH), (1, B_WIDTH), (1, LANES)])


def _cumsum_rows(name, a, reverse=False, tile=512):
    rows, w = a.shape
    r = min(tile, rows)
    n = rows // r

    def body(a_ref, o_ref, carry):
        @pl.when(pl.program_id(0) == 0)
        def _():
            carry[...] = jnp.zeros(carry.shape, carry.dtype)

        x = a_ref[...]
        row = lax.broadcasted_iota(jnp.int32, (r, 1), 0)
        s = 1
        while s < r:
            if reverse:
                x = x + jnp.where(row < r - s, pltpu.roll(x, r - s, 0), 0.0)
            else:
                x = x + jnp.where(row >= s, pltpu.roll(x, s, 0), 0.0)
            s *= 2
        x = x + carry[0:1, :]
        o_ref[...] = x
        edge = x[0:1, :] if reverse else x[r - 1:r, :]
        carry[...] = jnp.broadcast_to(edge, carry.shape)

    idx = (lambda i: (n - 1 - i, 0)) if reverse else (lambda i: (i, 0))
    return pl.pallas_call(
        body, grid=(n,), in_specs=[pl.BlockSpec((r, w), idx)], out_specs=pl.BlockSpec((r, w), idx),
        out_shape=jax.ShapeDtypeStruct((rows, w), F32), scratch_shapes=[pltpu.VMEM((8, w), F32)], name=name,
        compiler_params=_cparams(("arbitrary",)),
    )(a)


def _head_masks():
    lane = lax.broadcasted_iota(jnp.int32, (1, LANES), 1)
    return [lane < B_HEAD_DIM, lane >= B_HEAD_DIM]


def _causal(t):
    row = lax.broadcasted_iota(jnp.int32, (t, t), 0)
    col = lax.broadcasted_iota(jnp.int32, (t, t), 1)
    return row, col


def _col_from_row(row_vec):
    return jnp.transpose(jnp.broadcast_to(row_vec, (LANES, row_vec.shape[1])))[:, 0:1]


def _row_from_col(col):
    return jnp.transpose(jnp.broadcast_to(col, (col.shape[0], LANES)))[0:1, :]


def _flash_fwd(name, q, k, v, nck_rows):
    rows = q.shape[0]
    t = min(_ATT_T, rows)
    nb = rows // t

    def body(q_ref, k_ref, v_ref, nck_ref, o_ref, lse_ref):
        pair, i = pl.program_id(0), pl.program_id(1)
        q2 = q_ref[...]
        row, col = _causal(t)
        masks = _head_masks()
        qh = [jnp.where(hm, q2, jnp.zeros_like(q2)) for hm in masks]

        def step(j, carry, diag):
            ml, acc = carry
            start = pl.multiple_of(j * t, t)
            kb = k_ref[pl.ds(start, t), :]
            vb = v_ref[pl.ds(start, t), :]
            new_ml = []
            for hh, hm in enumerate(masks):
                m, l = ml[hh]
                s = lax.dot_general(qh[hh], kb, _DOT_DIMS["nt"], preferred_element_type=F32)
                s = s + nck_ref[2 * pair + hh, pl.ds(j, 1), :]
                if diag:
                    s = jnp.where(col <= row, s, NEG_INF)
                m_new = jnp.maximum(m, jnp.max(s, axis=1, keepdims=True))
                p = jnp.exp(s - m_new)
                alpha = jnp.exp(m - m_new)
                new_ml.append((m_new, alpha * l + jnp.sum(p, axis=1, keepdims=True)))
                pv = jnp.dot(p.astype(BF16), jnp.where(hm, vb, jnp.zeros_like(vb)), preferred_element_type=F32)
                acc = acc * jnp.where(hm, alpha, 1.0) + pv
            return tuple(new_ml), acc

        def init_ml():
            return (jnp.full((t, 1), NEG_INF, F32), jnp.zeros((t, 1), F32))

        init = ((init_ml(), init_ml()), jnp.zeros((t, LANES), F32))
        carry = lax.fori_loop(0, i, lambda j, c: step(j, c, False), init)
        ml, acc = step(i, carry, True)
        o_ref[...] = acc / jnp.where(masks[0], ml[0][1], ml[1][1])
        for hh in range(2):
            lse_ref[hh, 0] = _row_from_col(ml[hh][0] + jnp.log(ml[hh][1]))

    return pl.pallas_call(
        body, grid=(B_HEADS // 2, nb),
        in_specs=[pl.BlockSpec((t, LANES), lambda p, i: (i, p)), pl.BlockSpec((rows, LANES), lambda p, i: (0, p)),
                  pl.BlockSpec((rows, LANES), lambda p, i: (0, p)),
                  pl.BlockSpec((B_HEADS, nb, t), lambda p, i: (0, 0, 0))],
        out_specs=[pl.BlockSpec((t, LANES), lambda p, i: (i, p)),
                   pl.BlockSpec((2, 1, 1, t), lambda p, i: (p, i, 0, 0))],
        out_shape=[jax.ShapeDtypeStruct((rows, B_WIDTH), F32), jax.ShapeDtypeStruct((B_HEADS, nb, 1, t), F32)],
        name=name, compiler_params=_cparams(("parallel", "parallel")),
    )(q, k, v, nck_rows)


def _flash_bwd_dq(name, q, k, v, nck_rows, o, do, lse_rows):
    rows = q.shape[0]
    t = min(_ATT_T, rows)
    nb = rows // t

    def body(q_ref, k_ref, v_ref, nck_ref, o_ref, do_ref, lse_ref, dq_ref, delta_ref, dob_ref):
        pair, i = pl.program_id(0), pl.program_id(1)
        q2 = q_ref[...]
        do2 = do_ref[...]
        od = o_ref[...] * do2
        do_b = do2.astype(BF16)
        dob_ref[...] = do_b
        row, col = _causal(t)
        masks = _head_masks()
        qh = [jnp.where(hm, q2, jnp.zeros_like(q2)) for hm in masks]
        doh = [jnp.where(hm, do_b, jnp.zeros_like(do_b)) for hm in masks]
        delta = [jnp.sum(jnp.where(hm, od, 0.0), axis=1, keepdims=True) for hm in masks]
        lse = [_col_from_row(lse_ref[2 * pair + hh, pl.ds(i, 1), :]) for hh in range(2)]

        def step(j, carry, diag):
            acc, rowsum = carry
            start = pl.multiple_of(j * t, t)
            kb = k_ref[pl.ds(start, t), :]
            vb = v_ref[pl.ds(start, t), :]
            new_rowsum = []
            for hh, hm in enumerate(masks):
                s = lax.dot_general(qh[hh], kb, _DOT_DIMS["nt"], preferred_element_type=F32)
                s = s + nck_ref[2 * pair + hh, pl.ds(j, 1), :]
                p = jnp.exp(s - lse[hh])
                if diag:
                    p = jnp.where(col <= row, p, 0.0)
                dp = lax.dot_general(doh[hh], vb, _DOT_DIMS["nt"], preferred_element_type=F32)
                ds = p * (dp - delta[hh])
                new_rowsum.append(rowsum[hh] + jnp.sum(ds, axis=1, keepdims=True))
                acc = acc + jnp.dot(ds.astype(BF16), jnp.where(hm, kb, jnp.zeros_like(kb)),
                                    preferred_element_type=F32)
            return acc, tuple(new_rowsum)

        zcol = jnp.zeros((t, 1), F32)
        carry = lax.fori_loop(0, i, lambda j, c: step(j, c, False), (jnp.zeros((t, LANES), F32), (zcol, zcol)))
        acc, rowsum = step(i, carry, True)
        dq_ref[...] = acc
        for hh in range(2):
            delta_ref[hh, 0] = _row_from_col(delta[hh] + rowsum[hh])

    tile = pl.BlockSpec((t, LANES), lambda p, i: (i, p))
    full = pl.BlockSpec((rows, LANES), lambda p, i: (0, p))
    rowspec = pl.BlockSpec((B_HEADS, nb, t), lambda p, i: (0, 0, 0))
    return pl.pallas_call(
        body, grid=(B_HEADS // 2, nb),
        in_specs=[tile, full, full, rowspec, tile, tile, rowspec],
        out_specs=[tile, pl.BlockSpec((2, 1, 1, t), lambda p, i: (p, i, 0, 0)), tile],
        out_shape=[jax.ShapeDtypeStruct((rows, B_WIDTH), F32), jax.ShapeDtypeStruct((B_HEADS, nb, 1, t), F32),
                   jax.ShapeDtypeStruct((rows, B_WIDTH), BF16)],
        name=name, compiler_params=_cparams(("parallel", "parallel")),
    )(q, k, v, nck_rows, o, do, lse_rows)


def _flash_bwd_dkv(name, q, k, v, nck_rows, do, lse_rows, delta_rows):
    rows = q.shape[0]
    t = min(_ATT_T, rows)
    nb = rows // t

    def body(k_ref, v_ref, q_ref, do_ref, nck_ref, lse_ref, delta_ref, dk_ref, dv_ref, dn_ref):
        pair, j = pl.program_id(0), pl.program_id(1)
        k2 = k_ref[...]
        v2 = v_ref[...]
        row, col = _causal(t)
        masks = _head_masks()
        kh = [jnp.where(hm, k2, jnp.zeros_like(k2)) for hm in masks]
        vh = [jnp.where(hm, v2, jnp.zeros_like(v2)) for hm in masks]
        nck = [_col_from_row(nck_ref[2 * pair + hh, pl.ds(j, 1), :]) for hh in range(2)]

        def step(i, carry, diag):
            dk, dv, dn = carry
            start = pl.multiple_of(i * t, t)
            qb = q_ref[pl.ds(start, t), :]
            dob = do_ref[pl.ds(start, t), :]
            dn_new = []
            for hh, hm in enumerate(masks):
                head = 2 * pair + hh
                st = lax.dot_general(kh[hh], qb, _DOT_DIMS["nt"], preferred_element_type=F32) + nck[hh]
                pt = jnp.exp(st - lse_ref[head, pl.ds(i, 1), :])
                if diag:
                    pt = jnp.where(row <= col, pt, 0.0)
                dpt = lax.dot_general(vh[hh], dob, _DOT_DIMS["nt"], preferred_element_type=F32)
                dst = pt * (dpt - delta_ref[head, pl.ds(i, 1), :])
                dv = dv + jnp.dot(pt.astype(BF16), jnp.where(hm, dob, jnp.zeros_like(dob)),
                                  preferred_element_type=F32)
                dk = dk + jnp.dot(dst.astype(BF16), jnp.where(hm, qb, jnp.zeros_like(qb)),
                                  preferred_element_type=F32)
                dn_new.append(dn[hh] + jnp.sum(dst, axis=1, keepdims=True))
            return dk, dv, tuple(dn_new)

        zero = jnp.zeros((t, LANES), F32)
        zcol = jnp.zeros((t, 1), F32)
        carry = step(j, (zero, zero, (zcol, zcol)), True)
        dk, dv, dn = lax.fori_loop(j + 1, nb, lambda i, c: step(i, c, False), carry)
        dk_ref[...] = dk
        dv_ref[...] = dv
        for hh in range(2):
            dn_ref[hh, 0] = _row_from_col(dn[hh])

    tile = pl.BlockSpec((t, LANES), lambda p, j: (j, p))
    full = pl.BlockSpec((rows, LANES), lambda p, j: (0, p))
    rowspec = pl.BlockSpec((B_HEADS, nb, t), lambda p, j: (0, 0, 0))
    big = jax.ShapeDtypeStruct((rows, B_WIDTH), F32)
    return pl.pallas_call(
        body, grid=(B_HEADS // 2, nb),
        in_specs=[tile, tile, full, full, rowspec, rowspec, rowspec],
        out_specs=[tile, tile, pl.BlockSpec((2, 1, 1, t), lambda p, j: (p, j, 0, 0))],
        out_shape=[big, big, jax.ShapeDtypeStruct((B_HEADS, nb, 1, t), F32)],
        name=name, compiler_params=_cparams(("parallel", "parallel")),
    )(k, v, q, do, nck_rows, lse_rows, delta_rows)


_S5_ROWS = 512


def _s5_discretize(a_re, a_im, log_dt, b_re, b_im):
    dt = jnp.exp(log_dt)[:, None]
    mag = jnp.exp(a_re * dt)
    ab_re, ab_im = mag * jnp.cos(a_im * dt), mag * jnp.sin(a_im * dt)
    den = a_re * a_re + a_im * a_im
    nr, ni = ab_re - 1.0, ab_im
    cr = (nr * a_re + ni * a_im) / den
    ci = (ni * a_re - nr * a_im) / den
    bb_re = cr[..., None] * b_re - ci[..., None] * b_im
    bb_im = cr[..., None] * b_im + ci[..., None] * b_re
    return ab_re, ab_im, bb_re, bb_im


def _s5_block_diag(m):
    g, r, c = m.shape
    mb = m.reshape(S5_BLOCKS, 8, r, c)
    eye = jnp.eye(8, dtype=m.dtype)
    return jnp.einsum("bgrc,gh->bgrhc", mb, eye).reshape(S5_BLOCKS, 8 * r, 8 * c)


def _s5_block_diag_extract(m, r, c):
    mb = m.reshape(S5_BLOCKS, 8, r, 8, c)
    return jnp.einsum("bgrhc,gh->bgrc", mb, jnp.eye(8, dtype=m.dtype)).reshape(S5_GROUPS, r, c)


def _s5_tables(ab_re, ab_im, r):
    ar = jnp.broadcast_to(ab_re.reshape(1, -1), (r, S5_GROUPS * S5_STATE))
    ai = jnp.broadcast_to(ab_im.reshape(1, -1), (r, S5_GROUPS * S5_STATE))

    def mul(x, y):
        return x[0] * y[0] - x[1] * y[1], x[0] * y[1] + x[1] * y[0]

    return lax.associative_scan(mul, (ar, ai), axis=0)


def _scan_step(xr, xi, ar, ai, s, row, up):
    r = xr.shape[0]
    if up:
        ai = -ai
    if s < 8:
        if up:
            sr = jnp.where(row < r - s, pltpu.roll(xr, r - s, 0), 0.0)
            si = jnp.where(row < r - s, pltpu.roll(xi, r - s, 0), 0.0)
        else:
            sr = jnp.where(row >= s, pltpu.roll(xr, s, 0), 0.0)
            si = jnp.where(row >= s, pltpu.roll(xi, s, 0), 0.0)
        return xr + (ar * sr - ai * si), xi + (ar * si + ai * sr)
    if up:
        (dr, di), (sr, si) = (xr[:r - s], xi[:r - s]), (xr[s:], xi[s:])
        nr, ni = dr + (ar * sr - ai * si), di + (ar * si + ai * sr)
        return jnp.concatenate([nr, xr[r - s:]], axis=0), jnp.concatenate([ni, xi[r - s:]], axis=0)
    (dr, di), (sr, si) = (xr[s:], xi[s:]), (xr[:r - s], xi[:r - s])
    nr, ni = dr + (ar * sr - ai * si), di + (ar * si + ai * sr)
    return jnp.concatenate([xr[:s], nr], axis=0), jnp.concatenate([xi[:s], ni], axis=0)


_S5_CHUNK = 16


def _scan_tile(xr, xi, pr_ref, pi_ref, tr_ref, ti_ref, edge_ref, up):
    r, nl = xr.shape
    ch = _S5_CHUNK
    nch = r // ch
    sub = lax.broadcasted_iota(jnp.int32, (r, 1), 0) & (ch - 1)
    s = 1
    while s < ch:
        ar, ai = pr_ref[s - 1:s, :], pi_ref[s - 1:s, :]
        if up:
            ai, keep, shift = -ai, sub < ch - s, r - s
        else:
            keep, shift = sub >= s, s
        sr = jnp.where(keep, pltpu.roll(xr, shift, 0), 0.0)
        si = jnp.where(keep, pltpu.roll(xi, shift, 0), 0.0)
        xr, xi = xr + (ar * sr - ai * si), xi + (ar * si + ai * sr)
        s *= 2
    nb = nl // LANES
    for k in range(nb):
        edge_ref[k] = xr[:, k * LANES:(k + 1) * LANES]
        edge_ref[nb + k] = xi[:, k * LANES:(k + 1) * LANES]
    e0 = 0 if up else ch - 1
    er = jnp.concatenate([edge_ref[k, pl.ds(e0, nch, stride=ch), :] for k in range(nb)], axis=1)
    ei = jnp.concatenate([edge_ref[nb + k, pl.ds(e0, nch, stride=ch), :] for k in range(nb)], axis=1)
    rowc = lax.broadcasted_iota(jnp.int32, (nch, 1), 0)
    s = 1
    while s < nch:
        er, ei = _scan_step(er, ei, pr_ref[ch * s - 1:ch * s, :], pi_ref[ch * s - 1:ch * s, :], s, rowc, up)
        s *= 2
    if up:
        nr = jnp.where(rowc < nch - 1, pltpu.roll(er, nch - 1, 0), 0.0)
        ni = jnp.where(rowc < nch - 1, pltpu.roll(ei, nch - 1, 0), 0.0)
    else:
        nr = jnp.where(rowc >= 1, pltpu.roll(er, 1, 0), 0.0)
        ni = jnp.where(rowc >= 1, pltpu.roll(ei, 1, 0), 0.0)
    br = jnp.concatenate([jnp.broadcast_to(nr[n:n + 1, :], (ch, nl)) for n in range(nch)], axis=0)
    bi = jnp.concatenate([jnp.broadcast_to(ni[n:n + 1, :], (ch, nl)) for n in range(nch)], axis=0)
    tr, ti = tr_ref[...], ti_ref[...]
    if up:
        ti = -ti
    return xr + (tr * br - ti * bi), xi + (tr * bi + ti * br)


def _s5_scan_tile(u_ref, bcat_ref, pr_ref, pi_ref, tr_ref, ti_ref, edge_ref, cin_r, cin_i):
    bu = jnp.dot(u_ref[...], bcat_ref[...], preferred_element_type=F32)
    xr, xi = bu[:, :S5_LANES], bu[:, S5_LANES:]
    ar, ai = pr_ref[0:1, :], pi_ref[0:1, :]
    first = lax.broadcasted_iota(jnp.int32, (8, 1), 0) == 0
    xr = jnp.concatenate([xr[:8] + jnp.where(first, ar * cin_r - ai * cin_i, 0.0), xr[8:]], axis=0)
    xi = jnp.concatenate([xi[:8] + jnp.where(first, ar * cin_i + ai * cin_r, 0.0), xi[8:]], axis=0)
    return _scan_tile(xr, xi, pr_ref, pi_ref, tr_ref, ti_ref, edge_ref, False)


def _s5_fwd(name, u, bcat, ccat, pw_re, pw_im, pt_re, pt_im):
    rows = u.shape[0]
    r = pw_re.shape[0]
    nt = rows // r

    def body(u_ref, bcat_ref, ccat_ref, pr_ref, pi_ref, tr_ref, ti_ref, y_ref, xin_ref, carry, edge):
        @pl.when(pl.program_id(1) == 0)
        def _():
            carry[...] = jnp.zeros(carry.shape, carry.dtype)

        xin_ref[...] = carry[...]
        xr, xi = _s5_scan_tile(u_ref, bcat_ref, pr_ref, pi_ref, tr_ref, ti_ref, edge,
                               carry[0:1, :S5_LANES], carry[0:1, S5_LANES:])
        xcat = jnp.concatenate([xr, xi], axis=1)
        carry[...] = jnp.broadcast_to(xcat[r - 1:r, :], carry.shape)
        y_ref[...] = jnp.dot(xcat.astype(BF16), ccat_ref[...], preferred_element_type=F32)

    tab = pl.BlockSpec((r, S5_LANES), lambda b, i: (0, b))
    return pl.pallas_call(
        body, grid=(S5_BLOCKS, nt),
        in_specs=[pl.BlockSpec((r, LANES), lambda b, i: (i, b)),
                  pl.BlockSpec((None, LANES, 2 * S5_LANES), lambda b, i: (b, 0, 0)),
                  pl.BlockSpec((None, 2 * S5_LANES, LANES), lambda b, i: (b, 0, 0)), tab, tab, tab, tab],
        out_specs=[pl.BlockSpec((r, LANES), lambda b, i: (i, b)),
                   pl.BlockSpec((None, 8, 2 * S5_LANES), lambda b, i: (b, i, 0))],
        out_shape=[jax.ShapeDtypeStruct((rows, D_MODEL), F32),
                   jax.ShapeDtypeStruct((S5_BLOCKS, 8 * nt, 2 * S5_LANES), F32)],
        scratch_shapes=[pltpu.VMEM((8, 2 * S5_LANES), F32), pltpu.VMEM((2 * S5_LANES // LANES, r, LANES), F32)], name=name,
        compiler_params=_cparams(("parallel", "arbitrary")),
    )(u, bcat, ccat, pw_re, pw_im, pt_re, pt_im)


def _s5_bwd(name, u, dy, xin, bcat, ccat, pw_re, pw_im, pt_re, pt_im, ptu_re, ptu_im):
    rows = u.shape[0]
    r = pw_re.shape[0]
    nt = rows // r

    def body(u_ref, dy_ref, xin_ref, bcat_ref, ccat_ref, pr_ref, pi_ref, tr_ref, ti_ref, ur_ref, ui_ref,
             du_ref, db_ref, dc_ref, dar_ref, dai_ref, carry, edge):
        @pl.when(pl.program_id(1) == 0)
        def _():
            carry[...] = jnp.zeros(carry.shape, carry.dtype)
            for ref in (db_ref, dc_ref, dar_ref, dai_ref):
                ref[...] = jnp.zeros(ref.shape, ref.dtype)

        row = lax.broadcasted_iota(jnp.int32, (r, 1), 0)
        cin_r, cin_i = xin_ref[0:1, :S5_LANES], xin_ref[0:1, S5_LANES:]
        xr, xi = _s5_scan_tile(u_ref, bcat_ref, pr_ref, pi_ref, tr_ref, ti_ref, edge, cin_r, cin_i)
        dy_b = dy_ref[...].astype(BF16)
        xcat = jnp.concatenate([xr, xi], axis=1).astype(BF16)
        dc_ref[...] += lax.dot_general(xcat, dy_b, _DOT_DIMS["tn"], preferred_element_type=F32)
        g = lax.dot_general(dy_b, ccat_ref[...], _DOT_DIMS["nt"], preferred_element_type=F32)
        lr, li = g[:, :S5_LANES], g[:, S5_LANES:]
        nr, ni = carry[0:1, :S5_LANES], carry[0:1, S5_LANES:]
        ar, ai = pr_ref[0:1, :], pi_ref[0:1, :]
        final = lax.broadcasted_iota(jnp.int32, (8, 1), 0) == 7
        lr = jnp.concatenate([lr[:r - 8], lr[r - 8:] + jnp.where(final, ar * nr + ai * ni, 0.0)], axis=0)
        li = jnp.concatenate([li[:r - 8], li[r - 8:] + jnp.where(final, ar * ni - ai * nr, 0.0)], axis=0)
        lr, li = _scan_tile(lr, li, pr_ref, pi_ref, ur_ref, ui_ref, edge, True)
        carry[...] = jnp.broadcast_to(jnp.concatenate([lr[0:1, :], li[0:1, :]], axis=1), carry.shape)
        lcat = jnp.concatenate([lr, li], axis=1).astype(BF16)
        du_ref[...] = lax.dot_general(lcat, bcat_ref[...], _DOT_DIMS["nt"], preferred_element_type=F32)
        db_ref[...] += lax.dot_general(u_ref[...], lcat, _DOT_DIMS["tn"], preferred_element_type=F32)
        pxr = jnp.where(row == 0, cin_r, pltpu.roll(xr, 1, 0))
        pxi = jnp.where(row == 0, cin_i, pltpu.roll(xi, 1, 0))
        dar_ref[...] += jnp.sum((lr * pxr + li * pxi).reshape(r // 8, 8, S5_LANES), axis=0)
        dai_ref[...] += jnp.sum((li * pxr - lr * pxi).reshape(r // 8, 8, S5_LANES), axis=0)

    rev = lambda b, i: (nt - 1 - i, b)
    tab = pl.BlockSpec((r, S5_LANES), lambda b, i: (0, b))
    return pl.pallas_call(
        body, grid=(S5_BLOCKS, nt),
        in_specs=[pl.BlockSpec((r, LANES), rev), pl.BlockSpec((r, LANES), rev),
                  pl.BlockSpec((None, 8, 2 * S5_LANES), lambda b, i: (b, nt - 1 - i, 0)),
                  pl.BlockSpec((None, LANES, 2 * S5_LANES), lambda b, i: (b, 0, 0)),
                  pl.BlockSpec((None, 2 * S5_LANES, LANES), lambda b, i: (b, 0, 0)), tab, tab, tab, tab, tab, tab],
        out_specs=[pl.BlockSpec((r, LANES), rev),
                   pl.BlockSpec((None, LANES, 2 * S5_LANES), lambda b, i: (b, 0, 0)),
                   pl.BlockSpec((None, 2 * S5_LANES, LANES), lambda b, i: (b, 0, 0)),
                   pl.BlockSpec((None, 8, S5_LANES), lambda b, i: (b, 0, 0)),
                   pl.BlockSpec((None, 8, S5_LANES), lambda b, i: (b, 0, 0))],
        out_shape=[jax.ShapeDtypeStruct((rows, D_MODEL), F32),
                   jax.ShapeDtypeStruct((S5_BLOCKS, LANES, 2 * S5_LANES), F32),
                   jax.ShapeDtypeStruct((S5_BLOCKS, 2 * S5_LANES, LANES), F32),
                   jax.ShapeDtypeStruct((S5_BLOCKS, 8, S5_LANES), F32),
                   jax.ShapeDtypeStruct((S5_BLOCKS, 8, S5_LANES), F32)],
        scratch_shapes=[pltpu.VMEM((8, 2 * S5_LANES), F32), pltpu.VMEM((2 * S5_LANES // LANES, r, LANES), F32)], name=name,
        compiler_params=_cparams(("parallel", "arbitrary")),
    )(u, dy, xin, bcat, ccat, pw_re, pw_im, pt_re, pt_im, ptu_re, ptu_im)


def _ones_gain():
    return jnp.ones((1, D_MODEL), F32)


def _channel_fwd(i, x1, p_i, w, rp, hn=None, next_norm=None):
    if hn is None:
        hn, = _rmsnorm_fwd(f"ffn_norm_{i}", x1, rp["norm_ffn"][i][None], [BF16])
    hg = _mm(f"ffn_up_g_{i}", hn, w["up_g"], tn=1408)
    hu = _mm(f"ffn_up_u_{i}", hn, w["up_u"], tn=1408)
    a = _convffn_fwd(f"ffn_conv_{i}", hg, hu, w["cw_g"], w["cw_u"], w["cb_g"], w["cb_u"])
    x2, r = _mm(f"ffn_down_{i}", a, w["down"], res=x1, tk=1408, norm_gain=_ones_gain())
    zg = _mm(f"ple_gate_{i}", r, w["ple_gate"])
    pp = _mm(f"ple_proj_{i}", p_i, w["ple_proj"])
    saved = dict(x1=x1, hn=hn, hg=hg, hu=hu, a=a, x2=x2, r=r, zg=zg, pp=pp, p_i=p_i)
    if next_norm is None:
        x3, = _rows(f"ple_out_{i}", lambda xv, zv, pv: (xv + _sigmoid(zv) * pv,), [x2, zg, pp], [], [(D_MODEL, F32)])
        return x3, None, saved
    gain, dtypes = next_norm

    def ple_out_norm(xv, zv, pv, gv):
        x3v = xv + _sigmoid(zv) * pv
        h = (x3v * _rstd(x3v)) * gv
        return (x3v,) + tuple(h for _ in dtypes)

    x3, *h_next = _rows(f"ple_out_{i}", ple_out_norm, [x2, zg, pp], [gain],
                        [(D_MODEL, F32)] + [(D_MODEL, dt) for dt in dtypes])
    return x3, h_next, saved


def _channel_bwd(i, dx3, sv, w, rp):
    def ple_bwd(dv, zv, pv):
        gate = _sigmoid(zv)
        return dv * gate, (dv * pv) * (gate * (1.0 - gate))

    dpp, dzg = _rows(f"ple_out_bwd_{i}", ple_bwd, [dx3, sv["zg"], sv["pp"]], [], [(D_MODEL, BF16), (D_MODEL, BF16)])
    g = {}
    g["ple_proj"] = _mm(f"ple_proj_dw_{i}", sv["p_i"], dpp, "tn")
    g["ple_gate"] = _mm(f"ple_gate_dw_{i}", sv["r"], dzg, "tn")
    dx2, _ = _mm_norm_bwd(f"ple_gate_dx_{i}", dzg, w["ple_gate"], "nt", sv["x2"], dx3, _ones_gain())
    da = _mm(f"ffn_down_dx_{i}", dx2, w["down"], "nt", tn=1408)
    g["down"] = _mm(f"ffn_down_dw_{i}", sv["a"], dx2, "tn", tm=1408)
    dhg, dhu, g["cw_g"], g["cw_u"], dbg, dbu = _convffn_bwd(
        f"ffn_conv_bwd_{i}", da, sv["hg"], sv["hu"], w["cw_g"], w["cw_u"], w["cb_g"], w["cb_u"])
    g["conv_b"] = jnp.concatenate([dbg, dbu], axis=1)[0]
    g["up_g"] = _mm(f"ffn_up_g_dw_{i}", sv["hn"], dhg, "tn", tn=1408)
    g["up_u"] = _mm(f"ffn_up_u_dw_{i}", sv["hn"], dhu, "tn", tn=1408)
    dhn = _mm(f"ffn_up_g_dx_{i}", dhg, w["up_g"], "nt", tk=1408)
    dx1, dgf = _mm_norm_bwd(f"ffn_up_u_dx_{i}", dhu, w["up_u"], "nt", sv["x1"], dx2, rp["norm_ffn"][i][None],
                            res=dhn, tk=1408)
    g["norm_ffn"] = dgf[0]
    return dx1, g


def _even_consts(e, rp):
    tri = jnp.tril(jnp.ones((A_CHUNK, A_CHUNK), dtype=bool))
    w_tril = jnp.where(tri[None], rp["ev_w_spatial"][e], 0.0).astype(BF16)
    b_exp = jnp.broadcast_to(rp["ev_b_spatial"][e][:, :, None], (A_GROUPS, A_CHUNK, LANES))
    seg = np.arange(B_WIDTH) // B_HEAD_DIM
    bd = jnp.asarray((seg[:, None] == seg[None, :]).astype(np.float32)).astype(BF16)
    return dict(
        tri=tri, w_tril=w_tril, w_tril_t=jnp.swapaxes(w_tril, 1, 2), b_exp=b_exp, bd=bd,
        v_gain=rp["ev_v_norm"][e][None], qg=jnp.tile(rp["ev_q_norm"][e], B_HEADS)[None],
        kg=jnp.tile(rp["ev_k_norm"][e], B_HEADS)[None],
        bf=jnp.pad(rp["ev_b_fgate"][e], (0, LANES - B_HEADS))[None])


def _even_fwd(i, x, w, rp, h_in=None):
    e = i // 2
    c = _even_consts(e, rp)
    rows = x.shape[0]
    t = min(_ATT_T, rows)
    h, = h_in if h_in is not None else _rmsnorm_fwd(f"mix_norm_{i}", x, rp["norm_mix"][i][None], [BF16])
    zuv = _mm(f"in_uv_{i}", h, w["in_uv"])
    zqkv = _mm(f"in_qkv_{i}", h, w["in_qkv"], tn=768)
    zf = _mm(f"in_f_{i}", h, w["in_f"])
    ya = _gmlp_fwd(f"gmlp_{i}", zuv, c["v_gain"], c["w_tril"], c["b_exp"])
    q, k, v, ls = _qkv_prep_fwd(f"qkv_prep_{i}", zqkv, zf, c["qg"], c["kg"], c["bf"], c["bd"])
    csum = _cumsum_rows(f"forget_cumsum_{i}", ls)
    nck = -csum[:, :B_HEADS].T
    nck_rows = nck.reshape(B_HEADS, rows // t, t)
    o, lse = _flash_fwd(f"attn_{i}", q, k, v, nck_rows)
    x1 = _mm(f"out_a_{i}", ya, w["out_a"], res=x)
    x1, hn = _mm(f"out_b_{i}", o, w["out_b"], res=x1, norm_gain=rp["norm_ffn"][i][None])
    return x1, hn, dict(x=x, h=h, zuv=zuv, zqkv=zqkv, zf=zf, ya=ya, q=q, k=k, v=v, nck_rows=nck_rows, o=o,
                          lse_rows=lse.reshape(B_HEADS, rows // t, t))


def _even_bwd(i, dx1, sv, w, rp):
    e = i // 2
    c = _even_consts(e, rp)
    rows = dx1.shape[0]
    t = min(_ATT_T, rows)
    nb = rows // t
    g = {}
    dya = _mm(f"out_a_dx_{i}", dx1, w["out_a"], "nt")
    do = _mm(f"out_b_dx_{i}", dx1, w["out_b"], "nt")
    g["out_a"] = _mm(f"out_a_dw_{i}", sv["ya"], dx1, "tn")
    g["out_b"] = _mm(f"out_b_dw_{i}", sv["o"], dx1, "tn")
    dq, delta, do_b = _flash_bwd_dq(f"attn_dq_{i}", sv["q"], sv["k"], sv["v"], sv["nck_rows"], sv["o"], do,
                                    sv["lse_rows"])
    dk, dv, dn = _flash_bwd_dkv(f"attn_dkv_{i}", sv["q"], sv["k"], sv["v"], sv["nck_rows"], do_b, sv["lse_rows"],
                                delta.reshape(B_HEADS, nb, t))
    dcs = jnp.pad(-dn.reshape(B_HEADS, rows).T, ((0, 0), (0, LANES - B_HEADS)))
    dls = _cumsum_rows(f"forget_cumsum_bwd_{i}", dcs, reverse=True)
    dzqkv, dzf, dqg, dkg, dbf = _qkv_prep_bwd(f"qkv_prep_bwd_{i}", sv["zqkv"], sv["zf"], dq, dk, dv, dls,
                                              c["qg"], c["kg"], c["bf"], c["bd"])
    dzuv, dws, dbs, dvg = _gmlp_bwd(f"gmlp_bwd_{i}", sv["zuv"], dya, c["v_gain"], c["w_tril"], c["w_tril_t"],
                                    c["b_exp"])
    g["in_uv"] = _mm(f"in_uv_dw_{i}", sv["h"], dzuv, "tn")
    g["in_qkv"] = _mm(f"in_qkv_dw_{i}", sv["h"], dzqkv, "tn", tn=768)
    g["in_f"] = _mm(f"in_f_dw_{i}", sv["h"], dzf, "tn")
    dh = _mm(f"in_uv_dx_{i}", dzuv, w["in_uv"], "nt")
    dh = _mm(f"in_qkv_dx_{i}", dzqkv, w["in_qkv"], "nt", res=dh, tk=768)
    dx, dgm = _mm_norm_bwd(f"in_f_dx_{i}", dzf, w["in_f"], "nt", sv["x"], dx1, rp["norm_mix"][i][None], res=dh)
    g["norm_mix"] = dgm[0]
    g["ev_b_fgate"] = dbf[0, :B_HEADS]
    g["ev_q_norm"] = dqg.reshape(B_HEADS, B_HEAD_DIM).sum(axis=0)
    g["ev_k_norm"] = dkg.reshape(B_HEADS, B_HEAD_DIM).sum(axis=0)
    g["ev_v_norm"] = dvg[0]
    g["ev_w_spatial"] = jnp.where(c["tri"][None], dws, 0.0)
    g["ev_b_spatial"] = dbs.sum(axis=-1)
    return dx, g


def _s5_consts(o, rp, r):
    prm = (rp["od_a_re"][o], rp["od_a_im"][o], rp["od_log_dt"][o], rp["od_b_re"][o], rp["od_b_im"][o])
    (ab_re, ab_im, bb_re, bb_im), vjp = jax.vjp(_s5_discretize, *prm)
    bcat = jnp.concatenate([_s5_block_diag(bb_re.transpose(0, 2, 1)), _s5_block_diag(bb_im.transpose(0, 2, 1))], axis=2)
    c_re, c_im = rp["od_c_re"][o], rp["od_c_im"][o]
    ccat = jnp.concatenate([_s5_block_diag(c_re.transpose(0, 2, 1)), _s5_block_diag(-c_im.transpose(0, 2, 1))], axis=1)
    pw = tuple(_s5_tables(ab_re, ab_im, r))
    reps = (r // _S5_CHUNK, 1)
    down = tuple(jnp.tile(t[:_S5_CHUNK], reps) for t in pw)
    up = tuple(jnp.tile(jnp.flip(t[:_S5_CHUNK], axis=0), reps) for t in pw)
    return dict(vjp=vjp, bcat=bcat.astype(BF16), ccat=ccat.astype(BF16), fwd_tabs=pw + down, bwd_tabs=pw + down + up)


def _odd_fwd(i, x, w, rp, h_in=None):
    o = i // 2
    rows = x.shape[0]
    c = _s5_consts(o, rp, min(_S5_ROWS, rows))
    hb, hf = h_in if h_in is not None else _rmsnorm_fwd(f"mix_norm_{i}", x, rp["norm_mix"][i][None], [BF16, F32])
    ys, xin = _s5_fwd(f"s5_{i}", hb, c["bcat"], c["ccat"], *c["fwd_tabs"])

    def skip_gelu(yv, hv, dv):
        y = yv + dv * hv
        return y, _gelu(y)

    y, ge = _rows(f"s5_skip_gelu_{i}", skip_gelu, [ys, hf], [w["od_d"]], [(D_MODEL, F32), (D_MODEL, BF16)])
    gl = _mm(f"glu_{i}", ge, w["glu"])

    def glu_out(xv, gv, nv):
        x1v = xv + gv[:, :D_MODEL] * _sigmoid(gv[:, D_MODEL:])
        return x1v, (x1v * _rstd(x1v)) * nv

    x1, hn = _rows(f"glu_out_{i}", glu_out, [x, gl], [rp["norm_ffn"][i][None]], [(D_MODEL, F32), (D_MODEL, BF16)])
    return x1, hn, dict(x=x, hb=hb, hf=hf, xin=xin, y=y, ge=ge, gl=gl, c=c)


def _odd_bwd(i, dx1, sv, w, rp):
    o = i // 2
    c = sv["c"]
    g = {}

    def glu_bwd(dv, gv):
        ga, gb = gv[:, :D_MODEL], gv[:, D_MODEL:]
        sg = _sigmoid(gb)
        return (jnp.concatenate([dv * sg, (dv * ga) * (sg * (1.0 - sg))], axis=1),)

    dgl, = _rows(f"glu_out_bwd_{i}", glu_bwd, [dx1, sv["gl"]], [], [(2 * D_MODEL, BF16)])
    g["glu"] = _mm(f"glu_dw_{i}", sv["ge"], dgl, "tn")
    dge = _mm(f"glu_dx_{i}", dgl, w["glu"], "nt")

    def gelu_bwd(dv, yv, hv):
        dy = dv * _gelu_grad(yv)
        return dy, jnp.sum(dy * hv, axis=0, keepdims=True)

    dy, dd = _rows(f"s5_skip_gelu_bwd_{i}", gelu_bwd, [dge, sv["y"], sv["hf"]], [], [(D_MODEL, F32)],
                   accs=[(1, D_MODEL)])
    g["od_d"] = dd[0]
    du, db, dc, dar, dai = _s5_bwd(f"s5_bwd_{i}", sv["hb"], dy, sv["xin"], c["bcat"], c["ccat"], *c["bwd_tabs"])
    dab_re = dar.sum(axis=1).reshape(S5_GROUPS, S5_STATE)
    dab_im = dai.sum(axis=1).reshape(S5_GROUPS, S5_STATE)
    dbb_re = _s5_block_diag_extract(db[:, :, :S5_LANES], S5_GROUP_CH, S5_STATE).transpose(0, 2, 1)
    dbb_im = _s5_block_diag_extract(db[:, :, S5_LANES:], S5_GROUP_CH, S5_STATE).transpose(0, 2, 1)
    g["od_a_re"], g["od_a_im"], g["od_log_dt"], g["od_b_re"], g["od_b_im"] = c["vjp"]((dab_re, dab_im, dbb_re, dbb_im))
    g["od_c_re"] = _s5_block_diag_extract(dc[:, :S5_LANES, :], S5_STATE, S5_GROUP_CH).transpose(0, 2, 1)
    g["od_c_im"] = -_s5_block_diag_extract(dc[:, S5_LANES:, :], S5_STATE, S5_GROUP_CH).transpose(0, 2, 1)

    def norm_bwd(xv, duv, dyv, drv, gv, dv):
        dh = duv + dv * dyv
        r = _rstd(xv)
        xh = xv * r
        dhg = dh * gv
        dx = drv + r * (dhg - xh * jnp.mean(dhg * xh, axis=-1, keepdims=True))
        return dx, jnp.sum(dh * xh, axis=0, keepdims=True)

    dx, dgm = _rows(f"mix_norm_bwd_{i}", norm_bwd, [sv["x"], du, dy, dx1], [rp["norm_mix"][i][None], w["od_d"]],
                    [(D_MODEL, F32)], accs=[(1, D_MODEL)])
    g["norm_mix"] = dgm[0]
    return dx, g


def _local_step(x, p, target, lw, rp):
    saved = []
    h_next = None
    for i in range(DEPTH):
        x, hn, s_mix = (_even_fwd if i % 2 == 0 else _odd_fwd)(i, x, lw[i], rp, h_next)
        nxt = None
        if i + 1 < DEPTH:
            nxt = (rp["norm_mix"][i + 1][None], [BF16, F32] if (i + 1) % 2 else [BF16])
        x, h_next, s_ch = _channel_fwd(i, x, p[i], lw[i], rp, hn, nxt)
        saved.append((s_mix, s_ch))

    def loss_fn(yv, tv):
        diff = yv - tv
        return diff * (1.0 / D_MODEL), jnp.sum(diff * diff, axis=0, keepdims=True)

    dx, sq = _rows("loss", loss_fn, [x, target], [], [(D_MODEL, F32)], accs=[(1, D_MODEL)])
    loss = 0.5 * jnp.sum(sq) / D_MODEL
    grads = [None] * DEPTH
    for i in reversed(range(DEPTH)):
        s_mix, s_ch = saved[i]
        dx, g_ch = _channel_bwd(i, dx, s_ch, lw[i], rp)
        dx, g_mix = (_even_bwd if i % 2 == 0 else _odd_bwd)(i, dx, s_mix, lw[i], rp)
        grads[i] = {**g_ch, **g_mix}
    return loss, dx, grads


WEIGHT_ORDER = ["norm_mix", "norm_ffn", "ev_w_in", "ev_b_fgate", "ev_q_norm", "ev_k_norm", "ev_v_norm", "ev_w_spatial",
                "ev_b_spatial", "ev_w_out", "od_a_re", "od_a_im", "od_log_dt", "od_b_re", "od_b_im", "od_c_re",
                "od_c_im", "od_d", "od_w_glu", "ffn_w_up", "ffn_conv_w", "ffn_conv_b", "ffn_w_down", "ple_w_proj",
                "ple_w_gate"]
SHARD_AXIS = {"ev_w_in": 2, "ev_w_out": 1, "od_d": 1, "od_w_glu": 2, "ffn_w_up": 2, "ffn_conv_w": 2, "ffn_w_down": 1,
              "ple_w_proj": 2, "ple_w_gate": 1}
BIG_WEIGHTS = [n for n in WEIGHT_ORDER if n in SHARD_AXIS]
SMALL_WEIGHTS = [n for n in WEIGHT_ORDER if n not in SHARD_AXIS]
KEPT_F32 = ("od_d", "ffn_conv_w")
IN_UV, IN_QKV_END, IN_COLS = 2 * A_WIDTH, 2 * A_WIDTH + 3 * B_WIDTH, 2 * A_WIDTH + 3 * B_WIDTH + B_HEADS


def _layer_weights(i, full, rp):
    w = {}
    up, cw, cb = full["ffn_w_up"][i], full["ffn_conv_w"][i], rp["ffn_conv_b"][i][None]
    w["up_g"], w["up_u"] = up[:, :D_FF], up[:, D_FF:]
    w["cw_g"], w["cw_u"] = cw[:, :D_FF], cw[:, D_FF:]
    w["cb_g"], w["cb_u"] = cb[:, :D_FF], cb[:, D_FF:]
    w["down"], w["ple_proj"], w["ple_gate"] = full["ffn_w_down"][i], full["ple_w_proj"][i], full["ple_w_gate"][i]
    if i % 2 == 0:
        win, wout = full["ev_w_in"][i // 2], full["ev_w_out"][i // 2]
        w["in_uv"], w["in_qkv"] = win[:, :IN_UV], win[:, IN_UV:IN_QKV_END]
        w["in_f"] = jnp.pad(win[:, IN_QKV_END:], ((0, 0), (0, LANES - B_HEADS)))
        w["out_a"], w["out_b"] = wout[:A_WIDTH], wout[A_WIDTH:]
    else:
        w["od_d"], w["glu"] = full["od_d"][i // 2][None], full["od_w_glu"][i // 2]
    return w


def _full_grads(grads):
    ev, od = [grads[i] for i in range(0, DEPTH, 2)], [grads[i] for i in range(1, DEPTH, 2)]
    out = {
        "norm_mix": jnp.stack([g["norm_mix"] for g in grads]), "norm_ffn": jnp.stack([g["norm_ffn"] for g in grads]),
        "ev_w_in": jnp.stack([jnp.concatenate([g["in_uv"], g["in_qkv"], g["in_f"][:, :B_HEADS]], axis=1) for g in ev]),
        "ev_w_out": jnp.stack([jnp.concatenate([g["out_a"], g["out_b"]], axis=0) for g in ev]),
        "od_w_glu": jnp.stack([g["glu"] for g in od]),
        "ffn_w_up": jnp.stack([jnp.concatenate([g["up_g"], g["up_u"]], axis=1) for g in grads]),
        "ffn_conv_w": jnp.stack([jnp.concatenate([g["cw_g"], g["cw_u"]], axis=1) for g in grads]),
        "ffn_conv_b": jnp.stack([g["conv_b"] for g in grads]),
        "ffn_w_down": jnp.stack([g["down"] for g in grads]),
        "ple_w_proj": jnp.stack([g["ple_proj"] for g in grads]),
        "ple_w_gate": jnp.stack([g["ple_gate"] for g in grads]),
    }
    for n in ("ev_b_fgate", "ev_q_norm", "ev_k_norm", "ev_v_norm", "ev_w_spatial", "ev_b_spatial"):
        out[n] = jnp.stack([g[n] for g in ev])
    for n in ("od_a_re", "od_a_im", "od_log_dt", "od_b_re", "od_b_im", "od_c_re", "od_c_im", "od_d"):
        out[n] = jnp.stack([g[n] for g in od])
    return out


def _pack(arrs, row_multiple):
    flat = jnp.concatenate([a.reshape(-1) for a in arrs])
    rows = -(-flat.shape[0] // (PACK_W * row_multiple)) * row_multiple
    return jnp.pad(flat, (0, rows * PACK_W - flat.shape[0])).reshape(rows, PACK_W)


def _unpack(buf, shapes):
    flat = buf.reshape(-1)
    out, at = [], 0
    for s in shapes:
        n = int(np.prod(s))
        out.append(flat[at:at + n].reshape(s))
        at += n
    return out


def _shard(name, a, k):
    ax = SHARD_AXIS[name]
    n = a.shape[ax] // N_CHIPS
    return lax.slice_in_dim(a, k * n, (k + 1) * n, axis=ax)


_ANY = pl.BlockSpec(memory_space=pl.ANY)


def _mesh_pos():
    return lax.axis_index("x"), lax.axis_index("y"), lax.axis_index("c")


def _other_chips(x, y):
    return [(1 - x, y), (x, 1 - y), (1 - x, 1 - y)]


def _gather_shards(name, shards):
    n = len(shards)

    def body(*refs):
        ins, outs = refs[:n], refs[n:2 * n]
        send_sems, recv_sems, local_sems = refs[2 * n:]
        x, y, c = _mesh_pos()
        sibling = (x, y, 1 - c)
        chips = _other_chips(x, y)

        def part(a, k, hc):
            half = shards[a].shape[0] // 2
            return outs[a].at[k, pl.ds(hc * half, half), :]

        def copy(sem, src, dst, to):
            return pltpu.make_async_remote_copy(src_ref=src, dst_ref=dst, send_sem=send_sems.at[sem],
                                                recv_sem=recv_sems.at[sem], device_id=to, device_id_type=MESH)

        local, sent, passed = [], [], []
        for a in range(n):
            half = shards[a].shape[0] // 2
            local.append(pltpu.make_async_copy(ins[a], outs[a].at[2 * x + y], local_sems.at[a]))
            local[-1].start()
            for j, (cx, cy) in enumerate(chips):
                sent.append(copy(6 * a + j, ins[a].at[pl.ds(c * half, half), :], part(a, 2 * x + y, c), (cx, cy, c)))
                sent[-1].start()
        for a in range(n):
            for j, (cx, cy) in enumerate(chips):
                blk = part(a, 2 * cx + cy, c)
                copy(6 * a + j, blk, blk, (cx, cy, c)).wait_recv()
                passed.append(copy(6 * a + 3 + j, blk, blk, sibling))
                passed[-1].start()
        for a in range(n):
            for j, (cx, cy) in enumerate(chips):
                blk = part(a, 2 * cx + cy, 1 - c)
                copy(6 * a + 3 + j, blk, blk, sibling).wait_recv()
        for cp in sent + passed:
            cp.wait_send()
        for cp in local:
            cp.wait()

    return pl.pallas_call(
        body, out_shape=[jax.ShapeDtypeStruct((N_CHIPS,) + s.shape, s.dtype) for s in shards],
        in_specs=[_ANY] * n, out_specs=[_ANY] * n,
        scratch_shapes=[pltpu.SemaphoreType.DMA((6 * n,)), pltpu.SemaphoreType.DMA((6 * n,)),
                        pltpu.SemaphoreType.DMA((n,))],
        name=name,
    )(*shards)


def _swap_halves(name, arrs):
    n = len(arrs)

    def body(*refs):
        ins, outs = refs[:n], refs[n:2 * n]
        send_sems, recv_sems = refs[2 * n:]
        x, y, c = _mesh_pos()
        cps = []
        for a in range(n):
            half = arrs[a].shape[1] // 2
            cps.append(pltpu.make_async_remote_copy(
                src_ref=ins[a].at[:, pl.ds((1 - c) * half, half), :], dst_ref=outs[a], send_sem=send_sems.at[a],
                recv_sem=recv_sems.at[a], device_id=(x, y, 1 - c), device_id_type=MESH))
            cps[-1].start()
        for cp in cps:
            cp.wait()

    return pl.pallas_call(
        body, out_shape=[jax.ShapeDtypeStruct((a.shape[0], a.shape[1] // 2, a.shape[2]), a.dtype) for a in arrs],
        in_specs=[_ANY] * n, out_specs=[_ANY] * n,
        scratch_shapes=[pltpu.SemaphoreType.DMA((n,)), pltpu.SemaphoreType.DMA((n,))], name=name,
    )(*arrs)


def _send_to_owner_chips(name, arrs):
    n = len(arrs)

    def body(*refs):
        ins, outs = refs[:n], refs[n:2 * n]
        send_sems, recv_sems = refs[2 * n:]
        x, y, c = _mesh_pos()
        cps = []
        for a in range(n):
            for j, (cx, cy) in enumerate(_other_chips(x, y)):
                cps.append(pltpu.make_async_remote_copy(
                    src_ref=ins[a].at[2 * cx + cy], dst_ref=outs[a].at[j], send_sem=send_sems.at[3 * a + j],
                    recv_sem=recv_sems.at[3 * a + j], device_id=(cx, cy, c), device_id_type=MESH))
                cps[-1].start()
        for cp in cps:
            cp.wait()

    return pl.pallas_call(
        body, out_shape=[jax.ShapeDtypeStruct((3,) + a.shape[1:], a.dtype) for a in arrs],
        in_specs=[_ANY] * n, out_specs=[_ANY] * n,
        scratch_shapes=[pltpu.SemaphoreType.DMA((3 * n,)), pltpu.SemaphoreType.DMA((3 * n,))], name=name,
    )(*arrs)


def _swap_with_sibling(name, arrs):
    n = len(arrs)

    def body(*refs):
        ins, outs = refs[:n], refs[n:2 * n]
        send_sems, recv_sems = refs[2 * n:]
        x, y, c = _mesh_pos()
        cps = []
        for a in range(n):
            cps.append(pltpu.make_async_remote_copy(
                src_ref=ins[a], dst_ref=outs[a], send_sem=send_sems.at[a], recv_sem=recv_sems.at[a],
                device_id=(x, y, 1 - c), device_id_type=MESH))
            cps[-1].start()
        for cp in cps:
            cp.wait()

    return pl.pallas_call(
        body, out_shape=[jax.ShapeDtypeStruct(a.shape, a.dtype) for a in arrs],
        in_specs=[_ANY] * n, out_specs=[_ANY] * n,
        scratch_shapes=[pltpu.SemaphoreType.DMA((n,)), pltpu.SemaphoreType.DMA((n,))], name=name,
    )(*arrs)


def _all_gather_devices(name, a):
    rows, w = a.shape

    def body(a_ref, out_ref, send_sems, recv_sems, local_sem):
        x, y, c = _mesh_pos()
        me, sibling = (x, y, c), (x, y, 1 - c)
        chips = _other_chips(x, y)

        def slot(px, py, pc):
            return out_ref.at[4 * px + 2 * py + pc]

        def copy(sem, block, to, src=None):
            return pltpu.make_async_remote_copy(src_ref=slot(*block) if src is None else src, dst_ref=slot(*block),
                                                send_sem=send_sems.at[sem], recv_sem=recv_sems.at[sem], device_id=to,
                                                device_id_type=MESH)

        mine = pltpu.make_async_copy(a_ref, slot(*me), local_sem)
        mine.start()
        first = [copy(0, me, sibling, src=a_ref)]
        first += [copy(1 + j, me, (*chip, c), src=a_ref) for j, chip in enumerate(chips)]
        for cp in first:
            cp.start()
        passed = [copy(4 + j, (*chip, c), sibling) for j, chip in enumerate(chips)]
        for j, chip in enumerate(chips):
            copy(1 + j, (*chip, c), me).wait_recv()
            passed[j].start()
        copy(0, sibling, me).wait_recv()
        for j, chip in enumerate(chips):
            copy(4 + j, (*chip, 1 - c), me).wait_recv()
        for cp in first + passed:
            cp.wait_send()
        mine.wait()

    return pl.pallas_call(
        body, out_shape=jax.ShapeDtypeStruct((8, rows, w), a.dtype), in_specs=[_ANY], out_specs=_ANY,
        scratch_shapes=[pltpu.SemaphoreType.DMA((7,)), pltpu.SemaphoreType.DMA((7,)), pltpu.SemaphoreType.DMA],
        name=name,
    )(a)


_PACK_TILE = 256


def _sum_rows(name, arrs):
    def fn(*vals):
        tot = vals[0]
        for v in vals[1:]:
            tot = tot + v
        return (tot,)

    return _rows(name, fn, list(arrs), [], [(arrs[0].shape[1], F32)], tile=_PACK_TILE)[0]


def _adam_math(wv, gv, mv, vv):
    m2 = ADAM_B1 * mv + (1.0 - ADAM_B1) * gv
    v2 = ADAM_B2 * vv + (1.0 - ADAM_B2) * (gv * gv)
    m_hat = m2 / (1.0 - ADAM_B1 ** ADAM_STEP)
    v_hat = v2 / (1.0 - ADAM_B2 ** ADAM_STEP)
    delta = -ADAM_LR * (m_hat / (jnp.sqrt(v_hat) + ADAM_EPS) + ADAM_WD * wv)
    return delta, m2, v2


def _adamw(name, w, g, m, v):
    return _rows(name, _adam_math, [w, g, m, v], [], [(w.shape[1], F32)] * 3, tile=_PACK_TILE)


_INPUT_ORDER = (["x", "p"] + WEIGHT_ORDER + ["loss_target"] + ["m_" + n for n in WEIGHT_ORDER]
                + ["v_" + n for n in WEIGHT_ORDER])


_SUM_BLOCK_BYTES = 3 * 512 * 1024


def _sum_tile(half, w):
    best = 16
    for t in range(16, half + 1, 16):
        if half % t == 0 and t * w * 4 <= _SUM_BLOCK_BYTES:
            best = t
    return best


def _pair_sum(name, g, got, core):
    nk, rows, w = g.shape
    half = rows // 2
    tile_rows = _sum_tile(half, w)
    nt = half // tile_rows

    def body(c_ref, g_ref, got_ref, o_ref, ob_ref):
        tot = g_ref[...] + got_ref[...]
        o_ref[...] = tot
        ob_ref[...] = tot.astype(BF16)

    spec = pl.BlockSpec((None, tile_rows, w), lambda k, i, c: (k, i, 0))
    grid_spec = pltpu.PrefetchScalarGridSpec(
        num_scalar_prefetch=1, grid=(nk, nt),
        in_specs=[pl.BlockSpec((None, tile_rows, w), lambda k, i, c: (k, c[0] * nt + i, 0)), spec],
        out_specs=[spec, spec])
    return pl.pallas_call(
        body, grid_spec=grid_spec, name=name, compiler_params=_cparams(("parallel", "parallel")),
        out_shape=[jax.ShapeDtypeStruct((nk, half, w), F32), jax.ShapeDtypeStruct((nk, half, w), BF16)])(core, g, got)


def _owner_sum(name, pair, owed, chip):
    _, half, w = pair.shape
    tile_rows = _sum_tile(half, w)

    def body(k_ref, p_ref, a_ref, b_ref, c_ref, o_ref):
        o_ref[...] = ((p_ref[...] + a_ref[...].astype(F32)) + b_ref[...].astype(F32)) + c_ref[...].astype(F32)

    def owed_spec(j):
        return pl.BlockSpec((None, tile_rows, w), lambda i, k: (j, i, 0))

    grid_spec = pltpu.PrefetchScalarGridSpec(
        num_scalar_prefetch=1, grid=(half // tile_rows,),
        in_specs=[pl.BlockSpec((None, tile_rows, w), lambda i, k: (k[0], i, 0)), owed_spec(0), owed_spec(1),
                  owed_spec(2)],
        out_specs=pl.BlockSpec((tile_rows, w), lambda i, k: (i, 0)))
    return pl.pallas_call(body, grid_spec=grid_spec, out_shape=jax.ShapeDtypeStruct((half, w), F32), name=name,
                          compiler_params=_cparams(("parallel",)))(chip, pair, owed, owed, owed)


def _adamw_halves(name, w, mine, other, m, v, core):
    rows, wd = w.shape
    tile_rows = _sum_tile(rows // 2, wd)
    nh = (rows // 2) // tile_rows

    def body(c_ref, w_ref, a_ref, b_ref, m_ref, v_ref, g_ref, d_ref, m2_ref, v2_ref):
        own = (pl.program_id(0) // nh) == c_ref[0]
        g = jnp.where(own, a_ref[...], b_ref[...])
        g_ref[...] = g
        d_ref[...], m2_ref[...], v2_ref[...] = _adam_math(w_ref[...], g, m_ref[...], v_ref[...])

    full = pl.BlockSpec((tile_rows, wd), lambda i, c: (i, 0))
    part = pl.BlockSpec((tile_rows, wd), lambda i, c: (lax.rem(i, nh), 0))
    grid_spec = pltpu.PrefetchScalarGridSpec(num_scalar_prefetch=1, grid=(2 * nh,),
                                             in_specs=[full, part, part, full, full], out_specs=[full] * 4)
    return pl.pallas_call(body, grid_spec=grid_spec, out_shape=[jax.ShapeDtypeStruct((rows, wd), F32)] * 4, name=name,
                          compiler_params=_cparams(("parallel",)))(core, w, mine, other, m, v)


MATRIX_WEIGHTS = [n for n in BIG_WEIGHTS if n not in KEPT_F32]
TINY_SHARDED = [n for n in BIG_WEIGHTS if n in KEPT_F32]


def _as_rows(a):
    return a.reshape(-1, a.shape[-1])


def _owner_major(grads):
    ev, od = [grads[i] for i in range(0, DEPTH, 2)], [grads[i] for i in range(1, DEPTH, 2)]

    def cols(m, k, n):
        w = m.shape[1] // n
        return m[:, k * w:(k + 1) * w]

    def rows(m, k, n):
        r = m.shape[0] // n
        return m[k * r:(k + 1) * r]

    w_in = [jnp.concatenate([g["in_uv"], g["in_qkv"], g["in_f"][:, :B_HEADS]], axis=1) for g in ev]
    per_chip = {
        "ev_w_in": lambda k: [cols(m, k, N_CHIPS) for m in w_in],
        "ev_w_out": lambda k: [rows(g["out_a"] if k < 2 else g["out_b"], k % 2, 2) for g in ev],
        "od_w_glu": lambda k: [cols(g["glu"], k, N_CHIPS) for g in od],
        "ffn_w_up": lambda k: [cols(g["up_g"] if k < 2 else g["up_u"], k % 2, 2) for g in grads],
        "ffn_w_down": lambda k: [rows(g["down"], k, N_CHIPS) for g in grads],
        "ple_w_proj": lambda k: [cols(g["ple_proj"], k, N_CHIPS) for g in grads],
        "ple_w_gate": lambda k: [rows(g["ple_gate"], k, N_CHIPS) for g in grads],
    }
    return {n: jnp.stack([jnp.concatenate(per_chip[n](k), axis=0) for k in range(N_CHIPS)]) for n in MATRIX_WEIGHTS}


def _small_grads(grads):
    ev, od = [grads[i] for i in range(0, DEPTH, 2)], [grads[i] for i in range(1, DEPTH, 2)]
    out = {"norm_mix": jnp.stack([g["norm_mix"] for g in grads]), "norm_ffn": jnp.stack([g["norm_ffn"] for g in grads]),
           "ffn_conv_w": jnp.stack([jnp.concatenate([g["cw_g"], g["cw_u"]], axis=1) for g in grads]),
           "ffn_conv_b": jnp.stack([g["conv_b"] for g in grads])}
    for n in ("ev_b_fgate", "ev_q_norm", "ev_k_norm", "ev_v_norm", "ev_w_spatial", "ev_b_spatial"):
        out[n] = jnp.stack([g[n] for g in ev])
    for n in ("od_a_re", "od_a_im", "od_log_dt", "od_b_re", "od_b_im", "od_c_re", "od_c_im", "od_d"):
        out[n] = jnp.stack([g[n] for g in od])
    return out


def _step(a):
    xi, yi, ci = _mesh_pos()
    chip = 2 * xi + yi
    core_arr, chip_arr = ci.astype(jnp.int32).reshape(1), chip.astype(jnp.int32).reshape(1)
    rp = {n: a[n] for n in SMALL_WEIGHTS}

    tiny = _pack([a[n] for n in TINY_SHARDED], 32)
    gathered = _gather_shards("gather_weights", [_as_rows(a[n]).astype(BF16) for n in MATRIX_WEIGHTS] + [tiny])
    full = {}
    for n, g in zip(MATRIX_WEIGHTS, gathered):
        full[n] = jnp.concatenate([g[k].reshape(a[n].shape) for k in range(N_CHIPS)], axis=SHARD_AXIS[n])
    tiny_parts = [_unpack(gathered[-1][k], [a[n].shape for n in TINY_SHARDED]) for k in range(N_CHIPS)]
    for idx, n in enumerate(TINY_SHARDED):
        full[n] = jnp.concatenate([tiny_parts[k][idx] for k in range(N_CHIPS)], axis=SHARD_AXIS[n])
    lw = [_layer_weights(i, full, rp) for i in range(DEPTH)]

    loss_local, grad_x, grads = _local_step(a["x"][0], a["p"][:, 0], a["loss_target"][0], lw, rp)
    loss = lax.psum(loss_local, ("x", "y", "c"))

    contrib = _owner_major(grads)
    mats = [contrib[n] for n in MATRIX_WEIGHTS]
    got = _swap_halves("grad_pair_swap", mats)
    pair = [_pair_sum(f"grad_pair_sum_{n}", g, h, core_arr) for n, g, h in zip(MATRIX_WEIGHTS, mats, got)]
    owed = _send_to_owner_chips("grad_to_owner", [pb for _, pb in pair])
    mine = [_owner_sum(f"grad_owner_sum_{n}", p, o, chip_arr) for n, (p, _), o in zip(MATRIX_WEIGHTS, pair, owed)]
    theirs = _swap_with_sibling("grad_half_swap", mine)

    small_names = SMALL_WEIGHTS + TINY_SHARDED
    sg = _small_grads(grads)
    everyone = _all_gather_devices("small_grad_gather", _pack([sg[n] for n in small_names], _PACK_TILE))
    g_small = _sum_rows("small_grad_sum", [everyone[d] for d in range(8)])
    small_full = dict(zip(small_names, _unpack(g_small, [sg[n].shape for n in small_names])))

    out = {}
    for n, own_half, other_half in zip(MATRIX_WEIGHTS, mine, theirs):
        shape = a[n].shape
        g2d, delta, m2, v2 = _adamw_halves(f"adamw_{n}", _as_rows(a[n]), own_half, other_half,
                                           _as_rows(a["m_" + n]), _as_rows(a["v_" + n]), core_arr)
        for kind, val in (("grad", g2d), ("delta", delta), ("new_m", m2), ("new_v", v2)):
            out[kind + "_" + n] = val.reshape(shape)
    g_sm = {n: small_full[n] for n in SMALL_WEIGHTS}
    for n in TINY_SHARDED:
        width = a[n].shape[SHARD_AXIS[n]]
        g_sm[n] = lax.dynamic_slice_in_dim(small_full[n], chip * width, width, axis=SHARD_AXIS[n])
    shapes = [a[n].shape for n in small_names]
    w, m, v = (_pack([a[pre + n] for n in small_names], _PACK_TILE) for pre in ("", "m_", "v_"))
    g = _pack([g_sm[n] for n in small_names], _PACK_TILE)
    delta, m2, v2 = _adamw("adamw_small", w, g, m, v)
    for kind, buf in (("delta", delta), ("new_m", m2), ("new_v", v2)):
        for n, val in zip(small_names, _unpack(buf, shapes)):
            out[kind + "_" + n] = val
    for n in small_names:
        out["grad_" + n] = g_sm[n]
    res = [loss, grad_x[None]]
    for kind in ("grad", "delta", "new_m", "new_v"):
        res += [out[kind + "_" + n] for n in WEIGHT_ORDER]
    return tuple(res)


def kernel(x, p, norm_mix, norm_ffn, ev_w_in, ev_b_fgate, ev_q_norm, ev_k_norm, ev_v_norm, ev_w_spatial, ev_b_spatial, ev_w_out, od_a_re, od_a_im, od_log_dt, od_b_re, od_b_im, od_c_re, od_c_im, od_d, od_w_glu, ffn_w_up, ffn_conv_w, ffn_conv_b, ffn_w_down, ple_w_proj, ple_w_gate, loss_target, m_norm_mix, m_norm_ffn, m_ev_w_in, m_ev_b_fgate, m_ev_q_norm, m_ev_k_norm, m_ev_v_norm, m_ev_w_spatial, m_ev_b_spatial, m_ev_w_out, m_od_a_re, m_od_a_im, m_od_log_dt, m_od_b_re, m_od_b_im, m_od_c_re, m_od_c_im, m_od_d, m_od_w_glu, m_ffn_w_up, m_ffn_conv_w, m_ffn_conv_b, m_ffn_w_down, m_ple_w_proj, m_ple_w_gate, v_norm_mix, v_norm_ffn, v_ev_w_in, v_ev_b_fgate, v_ev_q_norm, v_ev_k_norm, v_ev_v_norm, v_ev_w_spatial, v_ev_b_spatial, v_ev_w_out, v_od_a_re, v_od_a_im, v_od_log_dt, v_od_b_re, v_od_b_im, v_od_c_re, v_od_c_im, v_od_d, v_od_w_glu, v_ffn_w_up, v_ffn_conv_w, v_ffn_conv_b, v_ffn_w_down, v_ple_w_proj, v_ple_w_gate):
    args = (x, p, norm_mix, norm_ffn, ev_w_in, ev_b_fgate, ev_q_norm, ev_k_norm, ev_v_norm, ev_w_spatial, ev_b_spatial, ev_w_out, od_a_re, od_a_im, od_log_dt, od_b_re, od_b_im, od_c_re, od_c_im, od_d, od_w_glu, ffn_w_up, ffn_conv_w, ffn_conv_b, ffn_w_down, ple_w_proj, ple_w_gate, loss_target, m_norm_mix, m_norm_ffn, m_ev_w_in, m_ev_b_fgate, m_ev_q_norm, m_ev_k_norm, m_ev_v_norm, m_ev_w_spatial, m_ev_b_spatial, m_ev_w_out, m_od_a_re, m_od_a_im, m_od_log_dt, m_od_b_re, m_od_b_im, m_od_c_re, m_od_c_im, m_od_d, m_od_w_glu, m_ffn_w_up, m_ffn_conv_w, m_ffn_conv_b, m_ffn_w_down, m_ple_w_proj, m_ple_w_gate, v_norm_mix, v_norm_ffn, v_ev_w_in, v_ev_b_fgate, v_ev_q_norm, v_ev_k_norm, v_ev_v_norm, v_ev_w_spatial, v_ev_b_spatial, v_ev_w_out, v_od_a_re, v_od_a_im, v_od_log_dt, v_od_b_re, v_od_b_im, v_od_c_re, v_od_c_im, v_od_d, v_od_w_glu, v_ffn_w_up, v_ffn_conv_w, v_ffn_conv_b, v_ffn_w_down, v_ple_w_proj, v_ple_w_gate)
    return _step(dict(zip(_INPUT_ORDER, args)))
```
